```python
import jax, jax.numpy as jnp
from jax import lax
import numpy as np

D_MODEL = 1024
BATCH = 32
SEQ = 256
DEPTH = 1
DEC_BATCH = 8
DEC_SEQ = 2048
PAST_LEN = 512

GRID_W = 64
EPS = 1e-6
HG_HEADS = 4
HG_DK = 128
HG_DV = 128
HG_KW = HG_HEADS * HG_DK
HG_VW = HG_HEADS * HG_DV
HG_CHUNK = 32
LRU_W = D_MODEL // 2
LRU_BLOCKS = 8
LRU_BW = LRU_W // LRU_BLOCKS
CONV_W = 4
LRU_C = 8.0
PK_HEADS = 8
N_KEYS = 128
N_EXPERTS = N_KEYS * N_KEYS
PK_TOPK = 16
PK_DQ = 256
PK_DH = PK_DQ // 2
PK_TOKEN_BLOCK = 128
IN_SIZES = (HG_KW, HG_KW, HG_KW, HG_VW, HG_VW, LRU_W, LRU_W, D_MODEL, D_MODEL)
IN_W = sum(IN_SIZES)
IN_SPLITS = tuple(sum(IN_SIZES[:i + 1]) for i in range(len(IN_SIZES) - 1))

kernel_name = 'hybrid_hgrn2_rglru_peer_diffusion_step'


def _rmsnorm(x, w):
    xf = x.astype(jnp.float32)
    y = xf * lax.rsqrt(jnp.mean(xf * xf, axis=-1, keepdims=True) + EPS)
    return (y * w.astype(jnp.float32)).astype(x.dtype)


def _flip(t):
    return jnp.flip(t, axis=1)


def _to_col_major(t):
    b, n, w = t.shape
    rows = n // GRID_W
    return t.reshape(b, rows, GRID_W, w).transpose(0, 2, 1, 3).reshape(b, n, w)


def _to_row_major(t):
    b, n, w = t.shape
    rows = n // GRID_W
    return t.reshape(b, GRID_W, rows, w).transpose(0, 2, 1, 3).reshape(b, n, w)


def _hgrn2_scan(q, k, v, logf, s0):
    b, n, h, _ = q.shape
    nc = n // HG_CHUNK

    def chunks(t):
        return t.astype(jnp.float32).reshape(b, nc, HG_CHUNK, h, t.shape[-1])

    q, k, v, logf = chunks(q), chunks(k), chunks(v), chunks(logf)
    cum = jnp.cumsum(logf, axis=2)
    tot = cum[:, :, -1:]
    q_dec = q * jnp.exp(cum)
    k_inv = k * jnp.exp(-cum)
    k_end = k * jnp.exp(tot - cum)
    lower = jnp.tril(jnp.ones((HG_CHUNK, HG_CHUNK), dtype=bool))
    att = jnp.where(lower, jnp.einsum('bcthk,bcshk->bchts', q_dec, k_inv), 0.0)
    o_intra = jnp.einsum('bchts,bcshv->bcthv', att, v)
    ds = jnp.einsum('bcshk,bcshv->bchkv', k_end, v)
    decay = jnp.exp(tot[:, :, 0])

    def step(s, inp):
        d, dsc = inp
        return d[..., None] * s + dsc, s

    s_fin, s_prev = lax.scan(step, s0.astype(jnp.float32),
                             (jnp.moveaxis(decay, 1, 0), jnp.moveaxis(ds, 1, 0)))
    o_inter = jnp.einsum('bcthk,bchkv->bcthv', q_dec, jnp.moveaxis(s_prev, 0, 1))
    return (o_intra + o_inter).reshape(b, n, h, -1), s_fin


def _lin_combine(left, right):
    a1, b1 = left
    a2, b2 = right
    return a1 * a2, a2 * b1 + b2


def _rglru(y, w_r, b_r, w_i, b_i, lam, h0):
    b, n, w = y.shape
    yb = y.reshape(b, n, LRU_BLOCKS, LRU_BW)
    r = jax.nn.sigmoid(jnp.einsum('bngi,gij->bngj', yb, w_r.astype(jnp.float32)).reshape(b, n, w)
                       + b_r.astype(jnp.float32))
    i = jax.nn.sigmoid(jnp.einsum('bngi,gij->bngj', yb, w_i.astype(jnp.float32)).reshape(b, n, w)
                       + b_i.astype(jnp.float32))
    log_a = -LRU_C * jax.nn.softplus(-lam.astype(jnp.float32)) * r
    a = jnp.exp(log_a)
    u = jnp.sqrt(-jnp.expm1(2.0 * log_a)) * (i * y)
    u = u.at[:, 0].add(a[:, 0] * h0.astype(jnp.float32))
    _, hs = lax.associative_scan(_lin_combine, (a, u), axis=1)
    return hs, hs[:, -1]


def _short_conv(x, w, bias):
    n = x.shape[1]
    left = CONV_W // 2
    xp = jnp.pad(x, ((0, 0), (left, CONV_W - 1 - left), (0, 0)))
    out = xp[:, 0:n] * w[0]
    for j in range(1, CONV_W):
        out = out + xp[:, j:j + n] * w[j]
    return out + bias


def _mixer(h, p, lb, hg_s0, lru_s0, col_major):
    b, n, _ = h.shape
    f32 = jnp.float32
    z = h @ p['w_in']
    q, f_fw, f_bw, v, g, xr, xg, ga, gb = jnp.split(z, IN_SPLITS, axis=-1)

    def heads(t):
        return t.astype(f32).reshape(b, n, HG_HEADS, -1)

    q = jax.nn.silu(heads(q))
    v = heads(v)

    def forget(f_raw, lb_dir):
        f = lb_dir + (1.0 - lb_dir) * jax.nn.sigmoid(f_raw.astype(f32))
        return heads(1.0 - f), heads(jnp.log(f))

    k_fw, lf_fw = forget(f_fw, lb[0])
    k_bw, lf_bw = forget(f_bw, lb[1])
    o_fw, s_fw = _hgrn2_scan(q, k_fw, v, lf_fw, hg_s0[:, 0])
    o_bw, s_bw = _hgrn2_scan(_flip(q), _flip(k_bw), _flip(v), _flip(lf_bw), hg_s0[:, 1])
    o = o_fw + _flip(o_bw)
    o = _rmsnorm(o, p['hg_norm']) * jax.nn.silu(heads(g))
    y_a = o.reshape(b, n, HG_VW).astype(h.dtype) @ p['w_branch_a']

    xr = xr.astype(f32)
    if col_major:
        xr = _to_col_major(xr)
    xc = _short_conv(xr, p['lru_conv_w'].astype(f32), p['lru_conv_b'].astype(f32))
    h_fw, e_fw = _rglru(xc, p['lru_w_r'][0], p['lru_b_r'][0], p['lru_w_i'][0], p['lru_b_i'][0],
                        p['lru_lambda'][0], lru_s0[:, 0])
    h_bw, e_bw = _rglru(_flip(xc), p['lru_w_r'][1], p['lru_b_r'][1], p['lru_w_i'][1], p['lru_b_i'][1],
                        p['lru_lambda'][1], lru_s0[:, 1])
    hr = h_fw + _flip(h_bw)
    if col_major:
        hr = _to_row_major(hr)
    y_b = (hr.astype(h.dtype) * jax.nn.gelu(xg)) @ p['w_branch_b']

    merged = jax.nn.sigmoid(ga) * y_a + jax.nn.sigmoid(gb) * y_b
    return (merged @ p['w_out'],
            jnp.stack([s_fw, s_bw], axis=1),
            jnp.stack([e_fw, e_bw], axis=1))


def _peer(h, p):
    b, n, d = h.shape
    t = b * n
    ht = h.reshape(t, d)
    qh = (ht @ p['pk_w_q']).astype(jnp.float32).reshape(t, PK_HEADS, 2, PK_DH)
    s = jnp.einsum('thpd,phkd->pthk', qh, p['pk_sub_keys'].astype(jnp.float32))
    sv, si = lax.top_k(s, PK_TOPK)
    cand = (sv[0][..., :, None] + sv[1][..., None, :]).reshape(t, PK_HEADS, PK_TOPK * PK_TOPK)
    cidx = (si[0][..., :, None] * N_KEYS + si[1][..., None, :]).reshape(t, PK_HEADS, PK_TOPK * PK_TOPK)
    best, pos = lax.top_k(cand, PK_TOPK)
    eidx = jnp.take_along_axis(cidx, pos, axis=-1)
    gate = jax.nn.softmax(best, axis=-1)
    nb = t // PK_TOKEN_BLOCK
    u_tab, v_tab = p['pk_u'], p['pk_v']

    def block(args):
        hb, eb, gb = args
        act = jax.nn.gelu(jnp.einsum('td,thkd->thk', hb, u_tab[eb]).astype(jnp.float32))
        return jnp.einsum('thk,thkd->td', (gb * act).astype(hb.dtype), v_tab[eb])

    out = lax.map(block, (ht.reshape(nb, PK_TOKEN_BLOCK, d),
                          eidx.reshape(nb, PK_TOKEN_BLOCK, PK_HEADS, PK_TOPK),
                          gate.reshape(nb, PK_TOKEN_BLOCK, PK_HEADS, PK_TOPK)))
    return out.reshape(b, n, d)


def _layer(x, cond, p, lb, hg_s0, lru_s0, col_major):
    m = (jax.nn.silu(cond.astype(jnp.float32)) @ p['w_mod'].astype(jnp.float32)
         + p['b_mod'].astype(jnp.float32))
    sh1, sc1, g1, sh2, sc2, g2 = jnp.split(m[:, None, :].astype(x.dtype), 6, axis=-1)
    h = _rmsnorm(x, p['norm1']) * (1.0 + sc1) + sh1
    mix, hg_fin, lru_fin = _mixer(h, p, lb, hg_s0, lru_s0, col_major)
    x = x + g1 * mix
    h = _rmsnorm(x, p['norm2']) * (1.0 + sc2) + sh2
    x = x + g2 * _peer(h, p)
    return x, hg_fin, lru_fin


def setup_inputs(seed: int = 0) -> dict:
    key = jax.random.key(seed)
    ks = jax.random.split(key, 32)
    D = D_MODEL

    def nrm(k, shape, s):
        return jax.random.normal(k, shape, jnp.float32) * s

    a_c = jax.random.uniform(ks[18], (DEPTH, 2, LRU_W), jnp.float32, 0.9, 0.999)
    sig = a_c ** (1.0 / LRU_C)
    lru_lambda = jnp.log(sig) - jnp.log1p(-sig)
    return {
        'x_prompt': nrm(ks[0], (BATCH, SEQ, D), 1.0),
        'x_sample': nrm(ks[1], (DEC_BATCH, DEC_SEQ, D), 1.0),
        'state_hgrn': nrm(ks[2], (DEC_BATCH, DEPTH, 2, HG_HEADS, HG_DK, HG_DV), 0.5),
        'state_rglru': nrm(ks[3], (DEC_BATCH, DEPTH, 2, LRU_W), 0.5),
        'c': nrm(ks[4], (DEC_BATCH, D), 1.0),
        'c_ctx': nrm(ks[5], (D,), 1.0),
        'w_mod': nrm(ks[6], (DEPTH, D, 6 * D), 0.5 * D ** -0.5),
        'b_mod': nrm(ks[7], (DEPTH, 6 * D), 0.01),
        'norm1': 1.0 + nrm(ks[8], (DEPTH, D), 0.02),
        'w_in': nrm(ks[9], (DEPTH, D, IN_W), D ** -0.5),
        'hg_lb_logits': nrm(ks[10], (2, DEPTH + 1, HG_KW), 0.5),
        'hg_norm': 1.0 + nrm(ks[11], (DEPTH, HG_DV), 0.02),
        'lru_conv_w': nrm(ks[12], (DEPTH, CONV_W, LRU_W), CONV_W ** -0.5),
        'lru_conv_b': nrm(ks[13], (DEPTH, LRU_W), 0.01),
        'lru_w_r': nrm(ks[14], (DEPTH, 2, LRU_BLOCKS, LRU_BW, LRU_BW), LRU_BW ** -0.5),
        'lru_b_r': nrm(ks[15], (DEPTH, 2, LRU_W), 0.01),
        'lru_w_i': nrm(ks[16], (DEPTH, 2, LRU_BLOCKS, LRU_BW, LRU_BW), LRU_BW ** -0.5),
        'lru_b_i': nrm(ks[17], (DEPTH, 2, LRU_W), 0.01),
        'lru_lambda': lru_lambda,
        'w_branch_a': nrm(ks[19], (DEPTH, HG_VW, D), HG_VW ** -0.5),
        'w_branch_b': nrm(ks[20], (DEPTH, LRU_W, D), LRU_W ** -0.5),
        'w_out': nrm(ks[21], (DEPTH, D, D), D ** -0.5),
        'norm2': 1.0 + nrm(ks[22], (DEPTH, D), 0.02),
        'pk_w_q': nrm(ks[23], (DEPTH, D, PK_HEADS * PK_DQ), D ** -0.5),
        'pk_sub_keys': nrm(ks[24], (DEPTH, 2, PK_HEADS, N_KEYS, PK_DH), PK_DH ** -0.5),
        'pk_u': nrm(ks[25], (DEPTH, N_EXPERTS, D), D ** -0.5),
        'pk_v': nrm(ks[26], (DEPTH, N_EXPERTS, D), PK_HEADS ** -0.5),
        'norm_f': 1.0 + nrm(ks[27], (D,), 0.02),
    }


def reference(x_prompt, x_sample, state_hgrn, state_rglru, c, c_ctx, w_mod, b_mod, norm1, w_in,
              hg_lb_logits, hg_norm, lru_conv_w, lru_conv_b, lru_w_r, lru_b_r, lru_w_i, lru_b_i,
              lru_lambda, w_branch_a, w_branch_b, w_out, norm2, pk_w_q, pk_sub_keys, pk_u, pk_v,
              norm_f):
    lb_all = jnp.cumsum(jax.nn.softmax(hg_lb_logits.astype(jnp.float32), axis=1), axis=1)
    xp, xs = x_prompt, x_sample
    cond_ctx = c_ctx[None, :]
    n_ctx = x_prompt.shape[0]
    hg_new, lru_new = [], []
    for l in range(DEPTH):
        p = {'w_mod': w_mod[l], 'b_mod': b_mod[l], 'norm1': norm1[l], 'w_in': w_in[l],
             'hg_norm': hg_norm[l], 'lru_conv_w': lru_conv_w[l], 'lru_conv_b': lru_conv_b[l],
             'lru_w_r': lru_w_r[l], 'lru_b_r': lru_b_r[l], 'lru_w_i': lru_w_i[l],
             'lru_b_i': lru_b_i[l], 'lru_lambda': lru_lambda[l], 'w_branch_a': w_branch_a[l],
             'w_branch_b': w_branch_b[l], 'w_out': w_out[l], 'norm2': norm2[l],
             'pk_w_q': pk_w_q[l], 'pk_sub_keys': pk_sub_keys[l], 'pk_u': pk_u[l], 'pk_v': pk_v[l]}
        lb = lb_all[:, l]
        hg0 = jnp.zeros((n_ctx, 2, HG_HEADS, HG_DK, HG_DV), jnp.float32)
        lr0 = jnp.zeros((n_ctx, 2, LRU_W), jnp.float32)
        xp, hg_fin, lru_fin = _layer(xp, cond_ctx, p, lb, hg0, lr0, False)
        hg_new.append(hg_fin)
        lru_new.append(lru_fin)
        xs, _, _ = _layer(xs, c, p, lb, state_hgrn[:, l], state_rglru[:, l], True)
    y_prompt = _rmsnorm(xp, norm_f)
    y_sample = _rmsnorm(xs, norm_f)
    new_state_hgrn = jnp.stack(hg_new, axis=1)
    new_state_rglru = jnp.stack(lru_new, axis=1)
    return (y_prompt, y_sample, new_state_hgrn, new_state_rglru)
```

```python
import functools

import jax, jax.numpy as jnp
from jax import lax
from jax.experimental import pallas as pl
from jax.experimental.pallas import tpu as pltpu
from jax.experimental.pallas import tpu_sc as plsc

D_MODEL = 1024
DEPTH = 1
GRID_W = 64
EPS = 1e-6
HG_HEADS = 4
HG_DK = 128
HG_DV = 128
HG_KW = HG_HEADS * HG_DK
HG_VW = HG_HEADS * HG_DV
HG_CHUNK = 32
LRU_W = D_MODEL // 2
LRU_BLOCKS = 8
LRU_BW = LRU_W // LRU_BLOCKS
CONV_W = 4
LRU_C = 8.0
PK_HEADS = 8
N_KEYS = 128
PK_TOPK = 16
PK_DQ = 256
PK_DH = PK_DQ // 2
PK_TOKEN_BLOCK = 128
IN_SIZES = (HG_KW, HG_KW, HG_KW, HG_VW, HG_VW, LRU_W, LRU_W, D_MODEL, D_MODEL)
IN_W = sum(IN_SIZES)
IN_SPLITS = tuple(sum(IN_SIZES[:i + 1]) for i in range(len(IN_SIZES) - 1))


def _rmsnorm(x, w):
    xf = x.astype(jnp.float32)
    y = xf * lax.rsqrt(jnp.mean(xf * xf, axis=-1, keepdims=True) + EPS)
    return (y * w.astype(jnp.float32)).astype(x.dtype)


def _flip(t):
    return jnp.flip(t, axis=1)


def _to_col_major(t):
    b, n, w = t.shape
    rows = n // GRID_W
    return t.reshape(b, rows, GRID_W, w).transpose(0, 2, 1, 3).reshape(b, n, w)


def _to_row_major(t):
    b, n, w = t.shape
    rows = n // GRID_W
    return t.reshape(b, GRID_W, rows, w).transpose(0, 2, 1, 3).reshape(b, n, w)


def _hgrn2_scan(q, k, v, logf, s0):
    b, n, h, _ = q.shape
    nc = n // HG_CHUNK

    def chunks(t):
        return t.astype(jnp.float32).reshape(b, nc, HG_CHUNK, h, t.shape[-1])

    q, k, v, logf = chunks(q), chunks(k), chunks(v), chunks(logf)
    cum = jnp.cumsum(logf, axis=2)
    tot = cum[:, :, -1:]
    q_dec = q * jnp.exp(cum)
    k_inv = k * jnp.exp(-cum)
    k_end = k * jnp.exp(tot - cum)
    lower = jnp.tril(jnp.ones((HG_CHUNK, HG_CHUNK), dtype=bool))
    att = jnp.where(lower, jnp.einsum('bcthk,bcshk->bchts', q_dec, k_inv), 0.0)
    o_intra = jnp.einsum('bchts,bcshv->bcthv', att, v)
    ds = jnp.einsum('bcshk,bcshv->bchkv', k_end, v)
    decay = jnp.exp(tot[:, :, 0])

    def step(s, inp):
        d, dsc = inp
        return d[..., None] * s + dsc, s

    s_fin, s_prev = lax.scan(step, s0.astype(jnp.float32),
                             (jnp.moveaxis(decay, 1, 0), jnp.moveaxis(ds, 1, 0)))
    o_inter = jnp.einsum('bcthk,bchkv->bcthv', q_dec, jnp.moveaxis(s_prev, 0, 1))
    return (o_intra + o_inter).reshape(b, n, h, -1), s_fin


def _lin_combine(left, right):
    a1, b1 = left
    a2, b2 = right
    return a1 * a2, a2 * b1 + b2


def _rglru(y, w_r, b_r, w_i, b_i, lam, h0):
    b, n, w = y.shape
    yb = y.reshape(b, n, LRU_BLOCKS, LRU_BW)
    r = jax.nn.sigmoid(jnp.einsum('bngi,gij->bngj', yb, w_r.astype(jnp.float32)).reshape(b, n, w)
                       + b_r.astype(jnp.float32))
    i = jax.nn.sigmoid(jnp.einsum('bngi,gij->bngj', yb, w_i.astype(jnp.float32)).reshape(b, n, w)
                       + b_i.astype(jnp.float32))
    log_a = -LRU_C * jax.nn.softplus(-lam.astype(jnp.float32)) * r
    a = jnp.exp(log_a)
    u = jnp.sqrt(-jnp.expm1(2.0 * log_a)) * (i * y)
    u = u.at[:, 0].add(a[:, 0] * h0.astype(jnp.float32))
    _, hs = lax.associative_scan(_lin_combine, (a, u), axis=1)
    return hs, hs[:, -1]


def _short_conv(x, w, bias):
    n = x.shape[1]
    left = CONV_W // 2
    xp = jnp.pad(x, ((0, 0), (left, CONV_W - 1 - left), (0, 0)))
    out = xp[:, 0:n] * w[0]
    for j in range(1, CONV_W):
        out = out + xp[:, j:j + n] * w[j]
    return out + bias


def _mixer(h, p, lb, hg_s0, lru_s0, col_major):
    b, n, _ = h.shape
    f32 = jnp.float32
    z = h @ p['w_in']
    q, f_fw, f_bw, v, g, xr, xg, ga, gb = jnp.split(z, IN_SPLITS, axis=-1)

    def heads(t):
        return t.astype(f32).reshape(b, n, HG_HEADS, -1)

    q = jax.nn.silu(heads(q))
    v = heads(v)

    def forget(f_raw, lb_dir):
        f = lb_dir + (1.0 - lb_dir) * jax.nn.sigmoid(f_raw.astype(f32))
        return heads(1.0 - f), heads(jnp.log(f))

    k_fw, lf_fw = forget(f_fw, lb[0])
    k_bw, lf_bw = forget(f_bw, lb[1])
    o_fw, s_fw = _hgrn2_scan(q, k_fw, v, lf_fw, hg_s0[:, 0])
    o_bw, s_bw = _hgrn2_scan(_flip(q), _flip(k_bw), _flip(v), _flip(lf_bw), hg_s0[:, 1])
    o = o_fw + _flip(o_bw)
    o = _rmsnorm(o, p['hg_norm']) * jax.nn.silu(heads(g))
    y_a = o.reshape(b, n, HG_VW).astype(h.dtype) @ p['w_branch_a']

    xr = xr.astype(f32)
    if col_major:
        xr = _to_col_major(xr)
    xc = _short_conv(xr, p['lru_conv_w'].astype(f32), p['lru_conv_b'].astype(f32))
    h_fw, e_fw = _rglru(xc, p['lru_w_r'][0], p['lru_b_r'][0], p['lru_w_i'][0], p['lru_b_i'][0],
                        p['lru_lambda'][0], lru_s0[:, 0])
    h_bw, e_bw = _rglru(_flip(xc), p['lru_w_r'][1], p['lru_b_r'][1], p['lru_w_i'][1], p['lru_b_i'][1],
                        p['lru_lambda'][1], lru_s0[:, 1])
    hr = h_fw + _flip(h_bw)
    if col_major:
        hr = _to_row_major(hr)
    y_b = (hr.astype(h.dtype) * jax.nn.gelu(xg)) @ p['w_branch_b']

    merged = jax.nn.sigmoid(ga) * y_a + jax.nn.sigmoid(gb) * y_b
    return (merged @ p['w_out'],
            jnp.stack([s_fw, s_bw], axis=1),
            jnp.stack([e_fw, e_bw], axis=1))


def _peer(h, p):
    b, n, d = h.shape
    t = b * n
    ht = h.reshape(t, d)
    qh = (ht @ p['pk_w_q']).astype(jnp.float32).reshape(t, PK_HEADS, 2, PK_DH)
    s = jnp.einsum('thpd,phkd->pthk', qh, p['pk_sub_keys'].astype(jnp.float32))
    sv, si = lax.top_k(s, PK_TOPK)
    cand = (sv[0][..., :, None] + sv[1][..., None, :]).reshape(t, PK_HEADS, PK_TOPK * PK_TOPK)
    cidx = (si[0][..., :, None] * N_KEYS + si[1][..., None, :]).reshape(t, PK_HEADS, PK_TOPK * PK_TOPK)
    best, pos = lax.top_k(cand, PK_TOPK)
    eidx = jnp.take_along_axis(cidx, pos, axis=-1)
    gate = jax.nn.softmax(best, axis=-1)
    out = _peer_experts(ht, eidx.reshape(t * PK_HEADS, PK_TOPK), gate.reshape(t * PK_HEADS, PK_TOPK),
                        p['pk_u'], p['pk_v'])
    return out.reshape(b, n, d)


SC_LANES = 16
SC_CORES = 2
SC_SUBCORES = 16
SC_WORKERS = SC_CORES * SC_SUBCORES
SC_ROW_CHUNKS = D_MODEL // SC_LANES
SC_TOKEN_BLOCK = 8


def _gelu_tanh(x):
    y = 0.7978845608028654 * (x + 0.044715 * (x * x * x))
    t = 1.0 - 2.0 / (jnp.exp(2.0 * y) + 1.0)
    return x * (0.5 * (1.0 + t))


def _peer_sc_body(h_hbm, idx_hbm, gate_hbm, u_hbm, v_hbm, out_hbm,
                  h_v, idx_v, gate_v, out_v, ub0, ub1, vb0, vb1, su0, su1, sv0, sv1, *, tokens_per_worker):
    nh, k, tb, lanes = PK_HEADS, PK_TOPK, SC_TOKEN_BLOCK, SC_LANES
    wid = lax.axis_index("s") * SC_CORES + lax.axis_index("c")
    ubufs, vbufs, sus, svs = (ub0, ub1), (vb0, vb1), (su0, su1), (sv0, sv1)
    lane = lax.iota(jnp.int32, lanes)
    zero = jnp.zeros((lanes,), jnp.float32)

    def start(tok, hd, par):
        irow = idx_v.at[tok * nh + hd]
        pltpu.async_copy(u_hbm.at[irow], ubufs[par], sus[par])
        pltpu.async_copy(v_hbm.at[irow], vbufs[par], svs[par])

    def wait(par):
        irow = idx_v.at[0]
        pltpu.make_async_copy(u_hbm.at[irow], ubufs[par], sus[par]).wait()
        pltpu.make_async_copy(v_hbm.at[irow], vbufs[par], svs[par]).wait()

    @pl.loop(0, tokens_per_worker // tb)
    def _(blk):
        base = wid * tokens_per_worker + blk * tb
        pltpu.sync_copy(h_hbm.at[pl.ds(base, tb)], h_v)
        pltpu.sync_copy(idx_hbm.at[pl.ds(base * nh, tb * nh)], idx_v)
        pltpu.sync_copy(gate_hbm.at[pl.ds(base * nh, tb * nh)], gate_v)
        start(0, 0, 0)

        @pl.loop(0, tb)
        def _(tok):
            for hd in range(nh):
                par = hd % 2
                if hd + 1 < nh:
                    start(tok, hd + 1, 1 - par)
                else:
                    @pl.when(tok + 1 < tb)
                    def _():
                        start(tok + 1, 0, 1 - par)
                wait(par)
                ub, vb = ubufs[par], vbufs[par]

                def u_body(c, accs):
                    hc = h_v[tok, pl.ds(c * lanes, lanes)]
                    return tuple(accs[r] + hc * ub[r, pl.ds(c * lanes, lanes)] for r in range(k))

                accs = lax.fori_loop(0, SC_ROW_CHUNKS, u_body, (zero,) * k)
                s_vec = zero
                for r in range(k):
                    s_vec = jnp.where(lane == r, jnp.sum(accs[r]), s_vec)
                w_vec = gate_v[tok * nh + hd, :] * _gelu_tanh(s_vec)
                wb = [jnp.sum(jnp.where(lane == r, w_vec, 0.0)) for r in range(k)]

                def v_body(c, carry):
                    o = wb[0] * vb[0, pl.ds(c * lanes, lanes)]
                    for r in range(1, k):
                        o = o + wb[r] * vb[r, pl.ds(c * lanes, lanes)]
                    if hd == 0:
                        out_v[tok, pl.ds(c * lanes, lanes)] = o
                    else:
                        out_v[tok, pl.ds(c * lanes, lanes)] = out_v[tok, pl.ds(c * lanes, lanes)] + o
                    return carry

                lax.fori_loop(0, SC_ROW_CHUNKS, v_body, 0)

        pltpu.sync_copy(out_v, out_hbm.at[pl.ds(base, tb)])


def _peer_experts(h, eidx, gate, u_tab, v_tab):
    t, d = h.shape
    assert t % (SC_WORKERS * SC_TOKEN_BLOCK) == 0 and d == D_MODEL
    rows = pltpu.VMEM((PK_TOPK, d), jnp.float32)
    return pl.kernel(
        functools.partial(_peer_sc_body, tokens_per_worker=t // SC_WORKERS),
        out_type=jax.ShapeDtypeStruct((t, d), jnp.float32),
        mesh=plsc.VectorSubcoreMesh(core_axis_name="c", subcore_axis_name="s"),
        scratch_types=[
            pltpu.VMEM((SC_TOKEN_BLOCK, d), jnp.float32),
            pltpu.VMEM((SC_TOKEN_BLOCK * PK_HEADS, PK_TOPK), jnp.int32),
            pltpu.VMEM((SC_TOKEN_BLOCK * PK_HEADS, PK_TOPK), jnp.float32),
            pltpu.VMEM((SC_TOKEN_BLOCK, d), jnp.float32),
            rows, rows, rows, rows,
            pltpu.SemaphoreType.DMA, pltpu.SemaphoreType.DMA,
            pltpu.SemaphoreType.DMA, pltpu.SemaphoreType.DMA,
        ],
        compiler_params=pltpu.CompilerParams(needs_layout_passes=False),
        name="peer_experts_sc",
    )(h, eidx, gate, u_tab, v_tab)


def _layer(x, cond, p, lb, hg_s0, lru_s0, col_major):
    m = (jax.nn.silu(cond.astype(jnp.float32)) @ p['w_mod'].astype(jnp.float32)
         + p['b_mod'].astype(jnp.float32))
    sh1, sc1, g1, sh2, sc2, g2 = jnp.split(m[:, None, :].astype(x.dtype), 6, axis=-1)
    h = _rmsnorm(x, p['norm1']) * (1.0 + sc1) + sh1
    mix, hg_fin, lru_fin = _mixer(h, p, lb, hg_s0, lru_s0, col_major)
    x = x + g1 * mix
    h = _rmsnorm(x, p['norm2']) * (1.0 + sc2) + sh2
    x = x + g2 * _peer(h, p)
    return x, hg_fin, lru_fin


def _final_norm_body(x_ref, w_ref, o_ref):
    x = x_ref[...]
    y = x * lax.rsqrt(jnp.mean(x * x, axis=-1, keepdims=True) + EPS)
    o_ref[...] = y * w_ref[...]


def _final_norm(x, w):
    b, n, d = x.shape
    xt = x.reshape(b * n, d)
    tm = 512
    out = pl.pallas_call(
        _final_norm_body,
        out_shape=jax.ShapeDtypeStruct(xt.shape, xt.dtype),
        grid=(xt.shape[0] // tm,),
        in_specs=[pl.BlockSpec((tm, d), lambda i: (i, 0)), pl.BlockSpec((1, d), lambda i: (0, 0))],
        out_specs=pl.BlockSpec((tm, d), lambda i: (i, 0)),
    )(xt, w.reshape(1, d))
    return out.reshape(b, n, d)


def kernel(x_prompt, x_sample, state_hgrn, state_rglru, c, c_ctx, w_mod, b_mod, norm1, w_in,
           hg_lb_logits, hg_norm, lru_conv_w, lru_conv_b, lru_w_r, lru_b_r, lru_w_i, lru_b_i,
           lru_lambda, w_branch_a, w_branch_b, w_out, norm2, pk_w_q, pk_sub_keys, pk_u, pk_v,
           norm_f):
    lb_all = jnp.cumsum(jax.nn.softmax(hg_lb_logits.astype(jnp.float32), axis=1), axis=1)
    xp, xs = x_prompt, x_sample
    cond_ctx = c_ctx[None, :]
    n_ctx = x_prompt.shape[0]
    hg_new, lru_new = [], []
    for l in range(DEPTH):
        p = {'w_mod': w_mod[l], 'b_mod': b_mod[l], 'norm1': norm1[l], 'w_in': w_in[l],
             'hg_norm': hg_norm[l], 'lru_conv_w': lru_conv_w[l], 'lru_conv_b': lru_conv_b[l],
             'lru_w_r': lru_w_r[l], 'lru_b_r': lru_b_r[l], 'lru_w_i': lru_w_i[l],
             'lru_b_i': lru_b_i[l], 'lru_lambda': lru_lambda[l], 'w_branch_a': w_branch_a[l],
             'w_branch_b': w_branch_b[l], 'w_out': w_out[l], 'norm2': norm2[l],
             'pk_w_q': pk_w_q[l], 'pk_sub_keys': pk_sub_keys[l], 'pk_u': pk_u[l], 'pk_v': pk_v[l]}
        lb = lb_all[:, l]
        hg0 = jnp.zeros((n_ctx, 2, HG_HEADS, HG_DK, HG_DV), jnp.float32)
        lr0 = jnp.zeros((n_ctx, 2, LRU_W), jnp.float32)
        xp, hg_fin, lru_fin = _layer(xp, cond_ctx, p, lb, hg0, lr0, False)
        hg_new.append(hg_fin)
        lru_new.append(lru_fin)
        xs, _, _ = _layer(xs, c, p, lb, state_hgrn[:, l], state_rglru[:, l], True)
    y_prompt = _final_norm(xp, norm_f)
    y_sample = _final_norm(xs, norm_f)
    return (y_prompt, y_sample, jnp.stack(hg_new, axis=1), jnp.stack(lru_new, axis=1))
```

```python
import functools

import jax, jax.numpy as jnp
from jax import lax
from jax.experimental import pallas as pl
from jax.experimental.pallas import tpu as pltpu
from jax.experimental.pallas import tpu_sc as plsc

D_MODEL = 1024
GRID_W = 64
EPS = 1e-6
HG_HEADS = 4
HG_DK = 128
HG_DV = 128
HG_KW = HG_HEADS * HG_DK
HG_VW = HG_HEADS * HG_DV
HG_CHUNK = 32
LRU_W = D_MODEL // 2
LRU_BLOCKS = 8
LRU_BW = LRU_W // LRU_BLOCKS
CONV_W = 4
LRU_C = 8.0
PK_HEADS = 8
N_KEYS = 128
PK_TOPK = 16
PK_DQ = 256
PK_DH = PK_DQ // 2
IN_W = 3 * HG_KW + 2 * HG_VW + 2 * LRU_W + 2 * D_MODEL
COL_G, COL_XR, COL_XG, COL_GA, COL_GB = 4, 5, 6, 7, 9
N_MOD = 6
MOD_ROWS = 16
F32 = jnp.float32
BF16 = jnp.bfloat16
VMEM_LIMIT = 48 * 1024 * 1024


def _cparams(*sem):
    return pltpu.CompilerParams(dimension_semantics=sem, vmem_limit_bytes=VMEM_LIMIT)


def _silu(x):
    return x * jax.nn.sigmoid(x)


def _rms(x):
    return x * lax.rsqrt(jnp.mean(x * x, axis=-1, keepdims=True) + EPS)


def _mod_body(c_ref, w_ref, b_ref, o_ref):
    o_ref[...] = lax.dot_general(_silu(c_ref[...]), w_ref[...], (((1,), (0,)), ((), ())),
                                 precision=lax.Precision.HIGHEST, preferred_element_type=F32) + b_ref[...]


def _modulation(cond, w_mod, b_mod):
    d = D_MODEL
    out = pl.pallas_call(
        _mod_body,
        out_shape=jax.ShapeDtypeStruct((MOD_ROWS, N_MOD * d), F32),
        grid=(N_MOD,),
        in_specs=[pl.BlockSpec((MOD_ROWS, d), lambda j: (0, 0)),
                  pl.BlockSpec((d, d), lambda j: (0, j)),
                  pl.BlockSpec((1, d), lambda j: (0, j))],
        out_specs=pl.BlockSpec((MOD_ROWS, d), lambda j: (0, j)),
        compiler_params=_cparams("arbitrary"),
        name="adaln_modulation",
    )(cond, w_mod, b_mod.reshape(1, N_MOD * d))
    return out.reshape(MOD_ROWS, N_MOD, d)


def _mod_row_map(tm, t_ctx, lat_n, ctx_row):
    nct = t_ctx // tm
    per = lat_n // tm
    return lambda i: jnp.where(i < nct, ctx_row, (i - nct) // per)


TM_IN = 512
TN_IN = IN_W // 2


def _in_body(x_ref, m_ref, n1_ref, w_ref, z_ref, h_scr):
    @pl.when(pl.program_id(1) == 0)
    def _():
        y = _rms(x_ref[...]) * n1_ref[...]
        h_scr[...] = (y * (1.0 + m_ref[0, 1:2, :]) + m_ref[0, 0:1, :]).astype(BF16)

    z_ref[...] = jnp.dot(h_scr[...], w_ref[...], preferred_element_type=F32)


def _in_proj(x, mod, norm1, w_in_bf, row_of):
    t, d = x.shape
    return pl.pallas_call(
        _in_body,
        out_shape=jax.ShapeDtypeStruct((t, IN_W), F32),
        grid=(t // TM_IN, IN_W // TN_IN),
        in_specs=[pl.BlockSpec((TM_IN, d), lambda i, j: (i, 0)),
                  pl.BlockSpec((1, N_MOD, d), lambda i, j: (row_of(i), 0, 0)),
                  pl.BlockSpec((1, d), lambda i, j: (0, 0)),
                  pl.BlockSpec((d, TN_IN), lambda i, j: (0, j))],
        out_specs=pl.BlockSpec((TM_IN, TN_IN), lambda i, j: (i, j)),
        scratch_shapes=[pltpu.VMEM((TM_IN, d), BF16)],
        compiler_params=_cparams("arbitrary", "arbitrary"),
        name="in_proj",
    )(x, mod, norm1.reshape(1, d), w_in_bf)


def _hg_body(q_ref, ff_ref, fb_ref, v_ref, lb_ref, s0_ref, o_ref, sfin_ref, st_scr, ob_scr, *, n):
    c = HG_CHUNK
    nc = n // c
    row = lax.broadcasted_iota(jnp.int32, (c, c), 0)
    col = lax.broadcasted_iota(jnp.int32, (c, c), 1)
    lower = row >= col
    tri = (lower.astype(F32), (row <= col).astype(F32))
    masks = (lower, row <= col)
    f_refs = (ff_ref, fb_ref)
    for d in range(2):
        st_scr[d] = s0_ref[0, d, 0].T

    def chunk(ci, d):
        r = pl.ds(pl.multiple_of(ci * c, c), c)
        q = _silu(q_ref[r, :])
        v = v_ref[r, :].astype(BF16)
        lb = lb_ref[d:d + 1, :]
        f = lb + (1.0 - lb) * jax.nn.sigmoid(f_refs[d][r, :])
        k = 1.0 - f
        lf = jnp.log(f)
        cum = lax.dot_general(tri[d], lf, (((1,), (0,)), ((), ())),
                              precision=lax.Precision.HIGHEST, preferred_element_type=F32)
        tot = jnp.sum(lf, axis=0, keepdims=True)
        q_dec = (q * jnp.exp(cum)).astype(BF16)
        k_inv = (k * jnp.exp(-cum)).astype(BF16)
        k_end = (k * jnp.exp(tot - cum)).astype(BF16)
        att = lax.dot_general(q_dec, k_inv, (((1,), (1,)), ((), ())), preferred_element_type=F32)
        att = jnp.where(masks[d], att, 0.0).astype(BF16)
        st = st_scr[d]
        o = (jnp.dot(att, v, preferred_element_type=F32)
             + lax.dot_general(q_dec, st.astype(BF16), (((1,), (1,)), ((), ())), preferred_element_type=F32))
        ds_t = lax.dot_general(v, k_end, (((0,), (0,)), ((), ())), preferred_element_type=F32)
        st_scr[d] = st * jnp.exp(tot) + ds_t
        return r, o

    def step(i, carry):
        r, o = chunk(i, 0)
        o_ref[r, :] = o
        r, o = chunk(nc - 1 - i, 1)
        ob_scr[r, :] = o
        return carry

    lax.fori_loop(0, nc, step, 0)
    o_ref[...] = o_ref[...] + ob_scr[...]
    for d in range(2):
        sfin_ref[0, d, 0] = st_scr[d].T


def _hgrn2(z, lb, s0, n, row_block0):
    seqs = s0.shape[0]
    hb = HG_KW // HG_DK

    def zspec(col0):
        return pl.BlockSpec((n, HG_DK), lambda s, h: (row_block0 + s, col0 + h))

    st_spec = pl.BlockSpec((1, 2, 1, HG_DK, HG_DV), lambda s, h: (s, 0, h, 0, 0))
    return pl.pallas_call(
        functools.partial(_hg_body, n=n),
        out_shape=(jax.ShapeDtypeStruct((seqs * n, HG_VW), F32),
                   jax.ShapeDtypeStruct((seqs, 2, HG_HEADS, HG_DK, HG_DV), F32)),
        grid=(seqs, HG_HEADS),
        in_specs=[zspec(0), zspec(hb), zspec(2 * hb), zspec(3 * hb),
                  pl.BlockSpec((2, HG_DK), lambda s, h: (0, h)), st_spec],
        out_specs=(pl.BlockSpec((n, HG_DV), lambda s, h: (s, h)), st_spec),
        scratch_shapes=[pltpu.VMEM((2, HG_DV, HG_DK), F32), pltpu.VMEM((n, HG_DV), F32)],
        compiler_params=_cparams("arbitrary", "arbitrary"),
        name="hgrn2_scan",
    )(z, z, z, z, lb, s0)


LRU_RB = 64
HALO = 8


def _lru_body(x_ref, cw_ref, cb_ref, wr_ref, wi_ref, br_ref, bi_ref, lam_ref, h0_ref, hr_ref, e_ref,
              xp_scr, xc_scr, *, n):
    rb = LRU_RB
    nb = n // rb
    w = LRU_W
    zeros = jnp.zeros((HALO, w), F32)
    xp_scr[0:HALO, :] = zeros
    xp_scr[HALO + n:HALO + n + HALO, :] = zeros
    xp_scr[HALO:HALO + n, :] = x_ref[...]
    cw = cw_ref[...]
    cb = cb_ref[...]

    def conv_blk(b, carry):
        base = pl.multiple_of(b * rb, rb)
        xh = xp_scr[pl.ds(base, rb + 2 * HALO), :]
        ext = rb + 2 * HALO
        acc = cb + cw[2:3, :] * xh[HALO:HALO + rb]
        acc = acc + cw[0:1, :] * pltpu.roll(xh, 2, axis=0)[HALO:HALO + rb]
        acc = acc + cw[1:2, :] * pltpu.roll(xh, 1, axis=0)[HALO:HALO + rb]
        acc = acc + cw[3:4, :] * pltpu.roll(xh, ext - 1, axis=0)[HALO:HALO + rb]
        xc_scr[pl.ds(base, rb), :] = acc
        return carry

    lax.fori_loop(0, nb, conv_blk, 0)

    rows = lax.broadcasted_iota(jnp.int32, (rb, w), 0)

    def gates(blk, d):
        r = pl.ds(pl.multiple_of(blk * rb, rb), rb)
        xc = xc_scr[r, :]
        xb = xc.astype(BF16)
        rg = jax.nn.sigmoid(jnp.dot(xb, wr_ref[d], preferred_element_type=F32) + br_ref[d:d + 1, :])
        ig = jax.nn.sigmoid(jnp.dot(xb, wi_ref[d], preferred_element_type=F32) + bi_ref[d:d + 1, :])
        log_a = (-LRU_C) * jax.nn.softplus(-lam_ref[d:d + 1, :]) * rg
        a = jnp.exp(log_a)
        u = jnp.sqrt(-jnp.tanh(log_a) * (a * a + 1.0)) * (ig * xc)
        return r, a, u

    def fwd_blk(blk, h_prev):
        r, a, u = gates(blk, 0)
        s = 1
        while s < rb:
            keep = rows >= s
            a_sh = jnp.where(keep, pltpu.roll(a, s, axis=0), 1.0)
            u_sh = jnp.where(keep, pltpu.roll(u, s, axis=0), 0.0)
            u = a * u_sh + u
            a = a * a_sh
            s *= 2
        h = u + a * h_prev
        hr_ref[r, :] = h
        return h[rb - 1:rb, :]

    def bwd_blk(i, h_next):
        r, a, u = gates(nb - 1 - i, 1)
        s = 1
        while s < rb:
            keep = rows < rb - s
            a_sh = jnp.where(keep, pltpu.roll(a, rb - s, axis=0), 1.0)
            u_sh = jnp.where(keep, pltpu.roll(u, rb - s, axis=0), 0.0)
            u = a * u_sh + u
            a = a * a_sh
            s *= 2
        h = u + a * h_next
        hr_ref[r, :] = hr_ref[r, :] + h
        return h[0:1, :]

    e_ref[0, 0:1, :] = lax.fori_loop(0, nb, fwd_blk, h0_ref[0, 0:1, :])
    e_ref[0, 1:2, :] = lax.fori_loop(0, nb, bwd_blk, h0_ref[0, 1:2, :])


def _rglru(x, col_block, row_block0, n, seqs, conv_w, conv_b, wr_bd, wi_bd, b_r, b_i, lam, h0):
    w = LRU_W
    full2 = lambda s: (0, 0)
    full3 = lambda s: (0, 0, 0)
    return pl.pallas_call(
        functools.partial(_lru_body, n=n),
        out_shape=(jax.ShapeDtypeStruct((seqs * n, w), F32), jax.ShapeDtypeStruct((seqs, 2, w), F32)),
        grid=(seqs,),
        in_specs=[pl.BlockSpec((n, w), lambda s: (row_block0 + s, col_block)),
                  pl.BlockSpec((CONV_W, w), full2), pl.BlockSpec((1, w), full2),
                  pl.BlockSpec((2, w, w), full3), pl.BlockSpec((2, w, w), full3),
                  pl.BlockSpec((2, w), full2), pl.BlockSpec((2, w), full2), pl.BlockSpec((2, w), full2),
                  pl.BlockSpec((1, 2, w), lambda s: (s, 0, 0))],
        out_specs=(pl.BlockSpec((n, w), lambda s: (s, 0)), pl.BlockSpec((1, 2, w), lambda s: (s, 0, 0))),
        scratch_shapes=[pltpu.VMEM((n + 2 * HALO, w), F32), pltpu.VMEM((n, w), F32)],
        compiler_params=_cparams("arbitrary"),
        name="rglru",
    )(x, conv_w, conv_b.reshape(1, w), wr_bd, wi_bd, b_r, b_i, lam, h0)


def _block_diag(wg):
    eye = jnp.eye(LRU_BLOCKS, dtype=wg.dtype)
    dense = wg[:, :, :, None, :] * eye[None, :, None, :, None]
    return dense.reshape(2, LRU_W, LRU_W).astype(BF16)


TM_OUT = 256


def _out_body(x_ref, o_ref, hr_ref, g_ref, xg_ref, ga0_ref, ga1_ref, gb0_ref, gb1_ref, m_ref,
              hgn_ref, n2_ref, wa_ref, wb_ref, wo_ref, wq_ref, x1_ref, h2_ref, q_ref):
    o = o_ref[...]
    parts = []
    for h in range(HG_HEADS):
        parts.append(_rms(o[:, h * HG_DV:(h + 1) * HG_DV]) * hgn_ref[...])
    on = jnp.concatenate(parts, axis=1) * _silu(g_ref[...])
    y_a = jnp.dot(on.astype(BF16), wa_ref[...], preferred_element_type=F32)
    y_b = jnp.dot((hr_ref[...] * jax.nn.gelu(xg_ref[...])).astype(BF16), wb_ref[...], preferred_element_type=F32)
    ga = jnp.concatenate([ga0_ref[...], ga1_ref[...]], axis=1)
    gb = jnp.concatenate([gb0_ref[...], gb1_ref[...]], axis=1)
    merged = jax.nn.sigmoid(ga) * y_a + jax.nn.sigmoid(gb) * y_b
    mix = jnp.dot(merged.astype(BF16), wo_ref[...], preferred_element_type=F32)
    x1 = x_ref[...] + m_ref[0, 2:3, :] * mix
    x1_ref[...] = x1
    h2 = _rms(x1) * n2_ref[...] * (1.0 + m_ref[0, 4:5, :]) + m_ref[0, 3:4, :]
    h2_ref[...] = h2
    q_ref[...] = jnp.dot(h2.astype(BF16), wq_ref[...], preferred_element_type=F32).astype(BF16)


def _out_proj(x, o_hg, hr, z, mod, hg_norm, norm2, wa, wb, wo, wq, row_of):
    t, d = x.shape
    tm = TM_OUT
    nq = PK_HEADS * PK_DQ
    half = lambda i: (i, 0)
    zc = lambda c: pl.BlockSpec((tm, LRU_W), lambda i: (i, c))
    const = lambda i: (0, 0)
    return pl.pallas_call(
        _out_body,
        out_shape=(jax.ShapeDtypeStruct((t, d), F32), jax.ShapeDtypeStruct((t, d), F32),
                   jax.ShapeDtypeStruct((t, nq), BF16)),
        grid=(t // tm,),
        in_specs=[pl.BlockSpec((tm, d), half), pl.BlockSpec((tm, HG_VW), half), pl.BlockSpec((tm, LRU_W), half),
                  zc(COL_G), zc(COL_XG), zc(COL_GA), zc(COL_GA + 1), zc(COL_GB), zc(COL_GB + 1),
                  pl.BlockSpec((1, N_MOD, d), lambda i: (row_of(i), 0, 0)),
                  pl.BlockSpec((1, HG_DV), const), pl.BlockSpec((1, d), const),
                  pl.BlockSpec((HG_VW, d), const), pl.BlockSpec((LRU_W, d), const),
                  pl.BlockSpec((d, d), const), pl.BlockSpec((d, nq), const)],
        out_specs=(pl.BlockSpec((tm, d), half), pl.BlockSpec((tm, d), half), pl.BlockSpec((tm, nq), half)),
        compiler_params=_cparams("arbitrary"),
        name="out_proj",
    )(x, o_hg, hr, z, z, z, z, z, z, mod, hg_norm.reshape(1, HG_DV), norm2.reshape(1, d), wa, wb, wo, wq)


TT_TOPK = 128


def _topk_rows(s, k, payload=None):
    r = s.shape[0]
    rid = lax.broadcasted_iota(jnp.int32, s.shape, 0)
    vals, picks = [], []
    for _ in range(k):
        m = jnp.max(s, axis=0, keepdims=True)
        idx = jnp.min(jnp.where(s == m, rid, r), axis=0, keepdims=True)
        hit = rid == idx
        vals.append(m)
        picks.append(idx if payload is None else jnp.sum(jnp.where(hit, payload, 0), axis=0, keepdims=True))
        s = jnp.where(hit, -jnp.inf, s)
    return jnp.concatenate(vals, axis=0), jnp.concatenate(picks, axis=0)


def _topk_body(q_ref, keys_ref, eidx_ref, gate_ref):
    e_rows, g_rows = [], []
    for h in range(PK_HEADS):
        sv, si = [], []
        for p in range(2):
            c0 = h * PK_DQ + p * PK_DH
            s = lax.dot_general(keys_ref[p, h], q_ref[:, c0:c0 + PK_DH], (((1,), (1,)), ((), ())),
                                preferred_element_type=F32)
            v, i = _topk_rows(s, PK_TOPK)
            sv.append(v)
            si.append(i)
        cand = jnp.concatenate([sv[0][i:i + 1, :] + sv[1] for i in range(PK_TOPK)], axis=0)
        cidx = jnp.concatenate([si[0][i:i + 1, :] * N_KEYS + si[1] for i in range(PK_TOPK)], axis=0)
        best, eid = _topk_rows(cand, PK_TOPK, payload=cidx)
        ex = jnp.exp(best - best[0:1, :])
        g_rows.append(ex / jnp.sum(ex, axis=0, keepdims=True))
        e_rows.append(eid)
    eidx_ref[...] = jnp.concatenate(e_rows, axis=0).T
    gate_ref[...] = jnp.concatenate(g_rows, axis=0).T


def _pk_topk(q, keys_bf):
    t = q.shape[0]
    tt = TT_TOPK
    ne = PK_HEADS * PK_TOPK
    return pl.pallas_call(
        _topk_body,
        out_shape=(jax.ShapeDtypeStruct((t, ne), jnp.int32), jax.ShapeDtypeStruct((t, ne), F32)),
        grid=(t // tt,),
        in_specs=[pl.BlockSpec((tt, PK_HEADS * PK_DQ), lambda i: (i, 0)),
                  pl.BlockSpec((2, PK_HEADS, N_KEYS, PK_DH), lambda i: (0, 0, 0, 0))],
        out_specs=(pl.BlockSpec((tt, ne), lambda i: (i, 0)), pl.BlockSpec((tt, ne), lambda i: (i, 0))),
        compiler_params=_cparams("arbitrary"),
        name="pk_topk",
    )(q, keys_bf)


SC_LANES = 16
SC_CORES = 2
SC_SUBCORES = 16
SC_WORKERS = SC_CORES * SC_SUBCORES
SC_ROW_CHUNKS = D_MODEL // SC_LANES
SC_TOKEN_BLOCK = 8


def _gelu_tanh(x):
    y = 0.7978845608028654 * (x + 0.044715 * (x * x * x))
    t = 1.0 - 2.0 / (jnp.exp(2.0 * y) + 1.0)
    return x * (0.5 * (1.0 + t))


def _peer_sc_body(h_hbm, idx_hbm, gate_hbm, u_hbm, v_hbm, out_hbm,
                  h_v, idx_v, gate_v, out_v, ub0, ub1, vb0, vb1, su0, su1, sv0, sv1, *, tokens_per_worker):
    nh, k, tb, lanes = PK_HEADS, PK_TOPK, SC_TOKEN_BLOCK, SC_LANES
    wid = lax.axis_index("s") * SC_CORES + lax.axis_index("c")
    ubufs, vbufs, sus, svs = (ub0, ub1), (vb0, vb1), (su0, su1), (sv0, sv1)
    lane = lax.iota(jnp.int32, lanes)
    zero = jnp.zeros((lanes,), F32)

    def start(tok, hd, par):
        irow = idx_v.at[tok * nh + hd]
        pltpu.async_copy(u_hbm.at[irow], ubufs[par], sus[par])
        pltpu.async_copy(v_hbm.at[irow], vbufs[par], svs[par])

    def wait(par):
        irow = idx_v.at[0]
        pltpu.make_async_copy(u_hbm.at[irow], ubufs[par], sus[par]).wait()
        pltpu.make_async_copy(v_hbm.at[irow], vbufs[par], svs[par]).wait()

    @pl.loop(0, tokens_per_worker // tb)
    def _(blk):
        base = wid * tokens_per_worker + blk * tb
        pltpu.sync_copy(h_hbm.at[pl.ds(base, tb)], h_v)
        pltpu.sync_copy(idx_hbm.at[pl.ds(base * nh, tb * nh)], idx_v)
        pltpu.sync_copy(gate_hbm.at[pl.ds(base * nh, tb * nh)], gate_v)
        start(0, 0, 0)

        @pl.loop(0, tb)
        def _(tok):
            for hd in range(nh):
                par = hd % 2
                if hd + 1 < nh:
                    start(tok, hd + 1, 1 - par)
                else:
                    @pl.when(tok + 1 < tb)
                    def _():
                        start(tok + 1, 0, 1 - par)
                wait(par)
                ub, vb = ubufs[par], vbufs[par]

                def u_body(c, accs):
                    hc = h_v[tok, pl.ds(c * lanes, lanes)]
                    return tuple(accs[r] + hc * ub[r, pl.ds(c * lanes, lanes)] for r in range(k))

                accs = lax.fori_loop(0, SC_ROW_CHUNKS, u_body, (zero,) * k)
                s_vec = zero
                for r in range(k):
                    s_vec = jnp.where(lane == r, jnp.sum(accs[r]), s_vec)
                w_vec = gate_v[tok * nh + hd, :] * _gelu_tanh(s_vec)
                wb = [jnp.sum(jnp.where(lane == r, w_vec, 0.0)) for r in range(k)]

                def v_body(c, carry):
                    o = wb[0] * vb[0, pl.ds(c * lanes, lanes)]
                    for r in range(1, k):
                        o = o + wb[r] * vb[r, pl.ds(c * lanes, lanes)]
                    if hd == 0:
                        out_v[tok, pl.ds(c * lanes, lanes)] = o
                    else:
                        out_v[tok, pl.ds(c * lanes, lanes)] = out_v[tok, pl.ds(c * lanes, lanes)] + o
                    return carry

                lax.fori_loop(0, SC_ROW_CHUNKS, v_body, 0)

        pltpu.sync_copy(out_v, out_hbm.at[pl.ds(base, tb)])


def _peer_experts(h, eidx, gate, u_tab, v_tab):
    t, d = h.shape
    assert t % (SC_WORKERS * SC_TOKEN_BLOCK) == 0 and d == D_MODEL
    rows = pltpu.VMEM((PK_TOPK, d), F32)
    return pl.kernel(
        functools.partial(_peer_sc_body, tokens_per_worker=t // SC_WORKERS),
        out_type=jax.ShapeDtypeStruct((t, d), F32),
        mesh=plsc.VectorSubcoreMesh(core_axis_name="c", subcore_axis_name="s"),
        scratch_types=[
            pltpu.VMEM((SC_TOKEN_BLOCK, d), F32),
            pltpu.VMEM((SC_TOKEN_BLOCK * PK_HEADS, PK_TOPK), jnp.int32),
            pltpu.VMEM((SC_TOKEN_BLOCK * PK_HEADS, PK_TOPK), F32),
            pltpu.VMEM((SC_TOKEN_BLOCK, d), F32),
            rows, rows, rows, rows,
            pltpu.SemaphoreType.DMA, pltpu.SemaphoreType.DMA,
            pltpu.SemaphoreType.DMA, pltpu.SemaphoreType.DMA,
        ],
        compiler_params=pltpu.CompilerParams(needs_layout_passes=False),
        name="peer_experts_sc",
    )(h, eidx, gate, u_tab, v_tab)


TM_FIN = 512


def _final_body(x1_ref, p_ref, m_ref, w_ref, y_ref):
    y_ref[...] = _rms(x1_ref[...] + m_ref[0, 5:6, :] * p_ref[...]) * w_ref[...]


def _final(x1, peer, mod, norm_f, row_of):
    t, d = x1.shape
    tile = pl.BlockSpec((TM_FIN, d), lambda i: (i, 0))
    return pl.pallas_call(
        _final_body,
        out_shape=jax.ShapeDtypeStruct((t, d), F32),
        grid=(t // TM_FIN,),
        in_specs=[tile, tile, pl.BlockSpec((1, N_MOD, d), lambda i: (row_of(i), 0, 0)),
                  pl.BlockSpec((1, d), lambda i: (0, 0))],
        out_specs=tile,
        compiler_params=_cparams("arbitrary"),
        name="final_norm",
    )(x1, peer, mod, norm_f.reshape(1, d))


def kernel(x_prompt, x_sample, state_hgrn, state_rglru, c, c_ctx, w_mod, b_mod, norm1, w_in,
           hg_lb_logits, hg_norm, lru_conv_w, lru_conv_b, lru_w_r, lru_b_r, lru_w_i, lru_b_i,
           lru_lambda, w_branch_a, w_branch_b, w_out, norm2, pk_w_q, pk_sub_keys, pk_u, pk_v,
           norm_f):
    assert w_mod.shape[0] == 1, "single trunk layer"
    d = D_MODEL
    nb_ctx, n_ctx, _ = x_prompt.shape
    nb_lat, n_lat, _ = x_sample.shape
    t_ctx, t_lat = nb_ctx * n_ctx, nb_lat * n_lat
    assert nb_lat < MOD_ROWS and t_ctx % n_lat == 0
    ctx_row = nb_lat

    x = jnp.concatenate([x_prompt.reshape(t_ctx, d), x_sample.reshape(t_lat, d)], axis=0)
    cond = jnp.zeros((MOD_ROWS, d), F32).at[:nb_lat].set(c).at[ctx_row].set(c_ctx)
    mod = _modulation(cond, w_mod[0], b_mod[0])
    lb = jnp.cumsum(jax.nn.softmax(hg_lb_logits.astype(F32), axis=1), axis=1)[:, 0]

    z = _in_proj(x, mod, norm1[0], w_in[0].astype(BF16), _mod_row_map(TM_IN, t_ctx, n_lat, ctx_row))

    zeros_hg = jnp.zeros((nb_ctx, 2, HG_HEADS, HG_DK, HG_DV), F32)
    o_ctx, hg_fin = _hgrn2(z, lb, zeros_hg, n_ctx, 0)
    o_lat, _ = _hgrn2(z, lb, state_hgrn[:, 0], n_lat, t_ctx // n_lat)
    o_hg = jnp.concatenate([o_ctx, o_lat], axis=0)

    wr_bd, wi_bd = _block_diag(lru_w_r[0]), _block_diag(lru_w_i[0])
    lru_args = (lru_conv_w[0], lru_conv_b[0], wr_bd, wi_bd, lru_b_r[0], lru_b_i[0], lru_lambda[0])
    hr_ctx, lru_fin = _rglru(z, COL_XR, 0, n_ctx, nb_ctx, *lru_args, jnp.zeros((nb_ctx, 2, LRU_W), F32))
    rows = n_lat // GRID_W
    xr_cm = (z[t_ctx:, COL_XR * LRU_W:(COL_XR + 1) * LRU_W]
             .reshape(nb_lat, rows, GRID_W, LRU_W).transpose(0, 2, 1, 3).reshape(t_lat, LRU_W))
    hr_cm, _ = _rglru(xr_cm, 0, 0, n_lat, nb_lat, *lru_args, state_rglru[:, 0])
    hr_lat = hr_cm.reshape(nb_lat, GRID_W, rows, LRU_W).transpose(0, 2, 1, 3).reshape(t_lat, LRU_W)
    hr = jnp.concatenate([hr_ctx, hr_lat], axis=0)

    x1, h2, q = _out_proj(x, o_hg, hr, z, mod, hg_norm[0], norm2[0],
                          w_branch_a[0].astype(BF16), w_branch_b[0].astype(BF16), w_out[0].astype(BF16),
                          pk_w_q[0].astype(BF16), _mod_row_map(TM_OUT, t_ctx, n_lat, ctx_row))
    eidx, gate = _pk_topk(q, pk_sub_keys[0].astype(BF16))
    t = t_ctx + t_lat
    peer = _peer_experts(h2, eidx.reshape(t * PK_HEADS, PK_TOPK), gate.reshape(t * PK_HEADS, PK_TOPK),
                         pk_u[0], pk_v[0])
    y = _final(x1, peer, mod, norm_f, _mod_row_map(TM_FIN, t_ctx, n_lat, ctx_row))
    return (y[:t_ctx].reshape(nb_ctx, n_ctx, d), y[t_ctx:].reshape(nb_lat, n_lat, d),
            hg_fin[:, None], lru_fin[:, None])
```

```python
import functools

import jax, jax.numpy as jnp
from jax import lax
from jax.experimental import pallas as pl
from jax.experimental.pallas import tpu as pltpu
from jax.experimental.pallas import tpu_sc as plsc

D_MODEL = 1024
GRID_W = 64
EPS = 1e-6
HG_HEADS = 4
HG_DK = 128
HG_DV = 128
HG_KW = HG_HEADS * HG_DK
HG_VW = HG_HEADS * HG_DV
HG_CHUNK = 32
LRU_W = D_MODEL // 2
LRU_BLOCKS = 8
LRU_BW = LRU_W // LRU_BLOCKS
CONV_W = 4
LRU_C = 8.0
PK_HEADS = 8
N_KEYS = 128
PK_TOPK = 16
PK_DQ = 256
PK_DH = PK_DQ // 2
IN_W = 3 * HG_KW + 2 * HG_VW + 2 * LRU_W + 2 * D_MODEL
COL_G, COL_XR, COL_XG, COL_GA, COL_GB = 4, 5, 6, 7, 9
N_MOD = 6
MOD_ROWS = 16
F32 = jnp.float32
BF16 = jnp.bfloat16
VMEM_LIMIT = 48 * 1024 * 1024


def _cparams(*sem):
    return pltpu.CompilerParams(dimension_semantics=sem, vmem_limit_bytes=VMEM_LIMIT)


def _silu(x):
    return x * jax.nn.sigmoid(x)


def _rms(x):
    return x * lax.rsqrt(jnp.mean(x * x, axis=-1, keepdims=True) + EPS)


def _mod_body(c_ref, w_ref, b_ref, o_ref):
    o_ref[...] = lax.dot_general(_silu(c_ref[...]), w_ref[...], (((1,), (0,)), ((), ())),
                                 precision=lax.Precision.HIGHEST, preferred_element_type=F32) + b_ref[...]


def _modulation(cond, w_mod, b_mod):
    d = D_MODEL
    out = pl.pallas_call(
        _mod_body,
        out_shape=jax.ShapeDtypeStruct((MOD_ROWS, N_MOD * d), F32),
        grid=(N_MOD,),
        in_specs=[pl.BlockSpec((MOD_ROWS, d), lambda j: (0, 0)),
                  pl.BlockSpec((d, d), lambda j: (0, j)),
                  pl.BlockSpec((1, d), lambda j: (0, j))],
        out_specs=pl.BlockSpec((MOD_ROWS, d), lambda j: (0, j)),
        compiler_params=_cparams("arbitrary"),
        name="adaln_modulation",
    )(cond, w_mod, b_mod.reshape(1, N_MOD * d))
    return out.reshape(MOD_ROWS, N_MOD, d)


def _mod_row_map(tm, t_ctx, lat_n, ctx_row):
    nct = t_ctx // tm
    per = lat_n // tm
    return lambda i: jnp.where(i < nct, ctx_row, (i - nct) // per)


TM_IN = 512
TN_IN = IN_W // 2


def _in_body(x_ref, m_ref, n1_ref, w_ref, z_ref, h_scr):
    @pl.when(pl.program_id(1) == 0)
    def _():
        y = _rms(x_ref[...]) * n1_ref[...]
        h_scr[...] = (y * (1.0 + m_ref[0, 1:2, :]) + m_ref[0, 0:1, :]).astype(BF16)

    z_ref[...] = jnp.dot(h_scr[...], w_ref[...], preferred_element_type=F32)


def _in_proj(x, mod, norm1, w_in_bf, row_of):
    t, d = x.shape
    return pl.pallas_call(
        _in_body,
        out_shape=jax.ShapeDtypeStruct((t, IN_W), F32),
        grid=(t // TM_IN, IN_W // TN_IN),
        in_specs=[pl.BlockSpec((TM_IN, d), lambda i, j: (i, 0)),
                  pl.BlockSpec((1, N_MOD, d), lambda i, j: (row_of(i), 0, 0)),
                  pl.BlockSpec((1, d), lambda i, j: (0, 0)),
                  pl.BlockSpec((d, TN_IN), lambda i, j: (0, j))],
        out_specs=pl.BlockSpec((TM_IN, TN_IN), lambda i, j: (i, j)),
        scratch_shapes=[pltpu.VMEM((TM_IN, d), BF16)],
        compiler_params=_cparams("arbitrary", "arbitrary"),
        name="in_proj",
    )(x, mod, norm1.reshape(1, d), w_in_bf)


def _hg_body(q_ref, ff_ref, fb_ref, v_ref, lb_ref, s0_ref, o_ref, sfin_ref, st_scr, ob_scr, *, n):
    c = HG_CHUNK
    nc = n // c
    row = lax.broadcasted_iota(jnp.int32, (c, c), 0)
    col = lax.broadcasted_iota(jnp.int32, (c, c), 1)
    lower = row >= col
    tri = (lower.astype(F32), (row <= col).astype(F32))
    masks = (lower, row <= col)
    f_refs = (ff_ref, fb_ref)
    for d in range(2):
        st_scr[d] = s0_ref[0, d, 0].T

    def chunk(ci, d):
        r = pl.ds(pl.multiple_of(ci * c, c), c)
        q = _silu(q_ref[r, :])
        v = v_ref[r, :].astype(BF16)
        lb = lb_ref[d:d + 1, :]
        f = lb + (1.0 - lb) * jax.nn.sigmoid(f_refs[d][r, :])
        k = 1.0 - f
        lf = jnp.log(f)
        cum = lax.dot_general(tri[d], lf, (((1,), (0,)), ((), ())),
                              precision=lax.Precision.HIGHEST, preferred_element_type=F32)
        tot = jnp.sum(lf, axis=0, keepdims=True)
        q_dec = (q * jnp.exp(cum)).astype(BF16)
        k_inv = (k * jnp.exp(-cum)).astype(BF16)
        k_end = (k * jnp.exp(tot - cum)).astype(BF16)
        att = lax.dot_general(q_dec, k_inv, (((1,), (1,)), ((), ())), preferred_element_type=F32)
        att = jnp.where(masks[d], att, 0.0).astype(BF16)
        st = st_scr[d]
        o = (jnp.dot(att, v, preferred_element_type=F32)
             + lax.dot_general(q_dec, st.astype(BF16), (((1,), (1,)), ((), ())), preferred_element_type=F32))
        ds_t = lax.dot_general(v, k_end, (((0,), (0,)), ((), ())), preferred_element_type=F32)
        st_scr[d] = st * jnp.exp(tot) + ds_t
        return r, o

    def step(i, carry):
        r, o = chunk(i, 0)
        o_ref[r, :] = o
        r, o = chunk(nc - 1 - i, 1)
        ob_scr[r, :] = o
        return carry

    lax.fori_loop(0, nc, step, 0)
    o_ref[...] = o_ref[...] + ob_scr[...]
    for d in range(2):
        sfin_ref[0, d, 0] = st_scr[d].T


def _hgrn2(z, lb, s0, n, row_block0):
    seqs = s0.shape[0]
    hb = HG_KW // HG_DK

    def zspec(col0):
        return pl.BlockSpec((n, HG_DK), lambda s, h: (row_block0 + s, col0 + h))

    st_spec = pl.BlockSpec((1, 2, 1, HG_DK, HG_DV), lambda s, h: (s, 0, h, 0, 0))
    return pl.pallas_call(
        functools.partial(_hg_body, n=n),
        out_shape=(jax.ShapeDtypeStruct((seqs * n, HG_VW), F32),
                   jax.ShapeDtypeStruct((seqs, 2, HG_HEADS, HG_DK, HG_DV), F32)),
        grid=(seqs, HG_HEADS),
        in_specs=[zspec(0), zspec(hb), zspec(2 * hb), zspec(3 * hb),
                  pl.BlockSpec((2, HG_DK), lambda s, h: (0, h)), st_spec],
        out_specs=(pl.BlockSpec((n, HG_DV), lambda s, h: (s, h)), st_spec),
        scratch_shapes=[pltpu.VMEM((2, HG_DV, HG_DK), F32), pltpu.VMEM((n, HG_DV), F32)],
        compiler_params=_cparams("arbitrary", "arbitrary"),
        name="hgrn2_scan",
    )(z, z, z, z, lb, s0)


LRU_RB = 64
HALO = 8


def _lru_body(x_ref, cw_ref, cb_ref, wr_ref, wi_ref, br_ref, bi_ref, lam_ref, h0_ref, hr_ref, e_ref,
              xp_scr, xc_scr, *, n):
    rb = LRU_RB
    nb = n // rb
    w = LRU_W
    zeros = jnp.zeros((HALO, w), F32)
    xp_scr[0:HALO, :] = zeros
    xp_scr[HALO + n:HALO + n + HALO, :] = zeros
    xp_scr[HALO:HALO + n, :] = x_ref[...]
    cw = cw_ref[...]
    cb = cb_ref[...]

    def conv_blk(b, carry):
        base = pl.multiple_of(b * rb, rb)
        xh = xp_scr[pl.ds(base, rb + 2 * HALO), :]
        ext = rb + 2 * HALO
        acc = cb + cw[2:3, :] * xh[HALO:HALO + rb]
        acc = acc + cw[0:1, :] * pltpu.roll(xh, 2, axis=0)[HALO:HALO + rb]
        acc = acc + cw[1:2, :] * pltpu.roll(xh, 1, axis=0)[HALO:HALO + rb]
        acc = acc + cw[3:4, :] * pltpu.roll(xh, ext - 1, axis=0)[HALO:HALO + rb]
        xc_scr[pl.ds(base, rb), :] = acc
        return carry

    lax.fori_loop(0, nb, conv_blk, 0)

    rows = lax.broadcasted_iota(jnp.int32, (rb, w), 0)

    def gates(blk, d):
        r = pl.ds(pl.multiple_of(blk * rb, rb), rb)
        xc = xc_scr[r, :]
        xb = xc.astype(BF16)
        rg = jax.nn.sigmoid(jnp.dot(xb, wr_ref[d], preferred_element_type=F32) + br_ref[d:d + 1, :])
        ig = jax.nn.sigmoid(jnp.dot(xb, wi_ref[d], preferred_element_type=F32) + bi_ref[d:d + 1, :])
        log_a = (-LRU_C) * jax.nn.softplus(-lam_ref[d:d + 1, :]) * rg
        a = jnp.exp(log_a)
        u = jnp.sqrt(-jnp.tanh(log_a) * (a * a + 1.0)) * (ig * xc)
        return r, a, u

    def fwd_blk(blk, h_prev):
        r, a, u = gates(blk, 0)
        s = 1
        while s < rb:
            keep = rows >= s
            a_sh = jnp.where(keep, pltpu.roll(a, s, axis=0), 1.0)
            u_sh = jnp.where(keep, pltpu.roll(u, s, axis=0), 0.0)
            u = a * u_sh + u
            a = a * a_sh
            s *= 2
        h = u + a * h_prev
        hr_ref[r, :] = h
        return h[rb - 1:rb, :]

    def bwd_blk(i, h_next):
        r, a, u = gates(nb - 1 - i, 1)
        s = 1
        while s < rb:
            keep = rows < rb - s
            a_sh = jnp.where(keep, pltpu.roll(a, rb - s, axis=0), 1.0)
            u_sh = jnp.where(keep, pltpu.roll(u, rb - s, axis=0), 0.0)
            u = a * u_sh + u
            a = a * a_sh
            s *= 2
        h = u + a * h_next
        hr_ref[r, :] = hr_ref[r, :] + h
        return h[0:1, :]

    e_ref[0, 0:1, :] = lax.fori_loop(0, nb, fwd_blk, h0_ref[0, 0:1, :])
    e_ref[0, 1:2, :] = lax.fori_loop(0, nb, bwd_blk, h0_ref[0, 1:2, :])


def _rglru(x, col_block, row_block0, n, seqs, conv_w, conv_b, wr_bd, wi_bd, b_r, b_i, lam, h0):
    w = LRU_W
    full2 = lambda s: (0, 0)
    full3 = lambda s: (0, 0, 0)
    return pl.pallas_call(
        functools.partial(_lru_body, n=n),
        out_shape=(jax.ShapeDtypeStruct((seqs * n, w), F32), jax.ShapeDtypeStruct((seqs, 2, w), F32)),
        grid=(seqs,),
        in_specs=[pl.BlockSpec((n, w), lambda s: (row_block0 + s, col_block)),
                  pl.BlockSpec((CONV_W, w), full2), pl.BlockSpec((1, w), full2),
                  pl.BlockSpec((2, w, w), full3), pl.BlockSpec((2, w, w), full3),
                  pl.BlockSpec((2, w), full2), pl.BlockSpec((2, w), full2), pl.BlockSpec((2, w), full2),
                  pl.BlockSpec((1, 2, w), lambda s: (s, 0, 0))],
        out_specs=(pl.BlockSpec((n, w), lambda s: (s, 0)), pl.BlockSpec((1, 2, w), lambda s: (s, 0, 0))),
        scratch_shapes=[pltpu.VMEM((n + 2 * HALO, w), F32), pltpu.VMEM((n, w), F32)],
        compiler_params=_cparams("arbitrary"),
        name="rglru",
    )(x, conv_w, conv_b.reshape(1, w), wr_bd, wi_bd, b_r, b_i, lam, h0)


def _block_diag(wg):
    eye = jnp.eye(LRU_BLOCKS, dtype=wg.dtype)
    dense = wg[:, :, :, None, :] * eye[None, :, None, :, None]
    return dense.reshape(2, LRU_W, LRU_W).astype(BF16)


TM_OUT = 256


def _out_body(x_ref, o_ref, hr_ref, g_ref, xg_ref, ga0_ref, ga1_ref, gb0_ref, gb1_ref, m_ref,
              hgn_ref, n2_ref, wa_ref, wb_ref, wo_ref, wq_ref, x1_ref, h2_ref, q_ref):
    o = o_ref[...]
    parts = []
    for h in range(HG_HEADS):
        parts.append(_rms(o[:, h * HG_DV:(h + 1) * HG_DV]) * hgn_ref[...])
    on = jnp.concatenate(parts, axis=1) * _silu(g_ref[...])
    y_a = jnp.dot(on.astype(BF16), wa_ref[...], preferred_element_type=F32)
    y_b = jnp.dot((hr_ref[...] * jax.nn.gelu(xg_ref[...])).astype(BF16), wb_ref[...], preferred_element_type=F32)
    ga = jnp.concatenate([ga0_ref[...], ga1_ref[...]], axis=1)
    gb = jnp.concatenate([gb0_ref[...], gb1_ref[...]], axis=1)
    merged = jax.nn.sigmoid(ga) * y_a + jax.nn.sigmoid(gb) * y_b
    mix = jnp.dot(merged.astype(BF16), wo_ref[...], preferred_element_type=F32)
    x1 = x_ref[...] + m_ref[0, 2:3, :] * mix
    x1_ref[...] = x1
    h2 = _rms(x1) * n2_ref[...] * (1.0 + m_ref[0, 4:5, :]) + m_ref[0, 3:4, :]
    h2_ref[...] = h2
    q_ref[...] = jnp.dot(h2.astype(BF16), wq_ref[...], preferred_element_type=F32).astype(BF16)


def _out_proj(x, o_hg, hr, z, mod, hg_norm, norm2, wa, wb, wo, wq, row_of):
    t, d = x.shape
    tm = TM_OUT
    nq = PK_HEADS * PK_DQ
    half = lambda i: (i, 0)
    zc = lambda c: pl.BlockSpec((tm, LRU_W), lambda i: (i, c))
    const = lambda i: (0, 0)
    return pl.pallas_call(
        _out_body,
        out_shape=(jax.ShapeDtypeStruct((t, d), F32), jax.ShapeDtypeStruct((t, d), F32),
                   jax.ShapeDtypeStruct((t, nq), BF16)),
        grid=(t // tm,),
        in_specs=[pl.BlockSpec((tm, d), half), pl.BlockSpec((tm, HG_VW), half), pl.BlockSpec((tm, LRU_W), half),
                  zc(COL_G), zc(COL_XG), zc(COL_GA), zc(COL_GA + 1), zc(COL_GB), zc(COL_GB + 1),
                  pl.BlockSpec((1, N_MOD, d), lambda i: (row_of(i), 0, 0)),
                  pl.BlockSpec((1, HG_DV), const), pl.BlockSpec((1, d), const),
                  pl.BlockSpec((HG_VW, d), const), pl.BlockSpec((LRU_W, d), const),
                  pl.BlockSpec((d, d), const), pl.BlockSpec((d, nq), const)],
        out_specs=(pl.BlockSpec((tm, d), half), pl.BlockSpec((tm, d), half), pl.BlockSpec((tm, nq), half)),
        compiler_params=_cparams("arbitrary"),
        name="out_proj",
    )(x, o_hg, hr, z, z, z, z, z, z, mod, hg_norm.reshape(1, HG_DV), norm2.reshape(1, d), wa, wb, wo, wq)


TT_TOPK = 128


def _topk_rows(s, k, payload=None):
    r = s.shape[0]
    rid = lax.broadcasted_iota(jnp.int32, s.shape, 0)
    vals, picks = [], []
    for _ in range(k):
        m = jnp.max(s, axis=0, keepdims=True)
        idx = jnp.min(jnp.where(s == m, rid, r), axis=0, keepdims=True)
        hit = rid == idx
        vals.append(m)
        picks.append(idx if payload is None else jnp.sum(jnp.where(hit, payload, 0), axis=0, keepdims=True))
        s = jnp.where(hit, -jnp.inf, s)
    return jnp.concatenate(vals, axis=0), jnp.concatenate(picks, axis=0)


def _topk_body(q_ref, keys_ref, eidx_ref, gate_ref):
    e_rows, g_rows = [], []
    for h in range(PK_HEADS):
        sv, si = [], []
        for p in range(2):
            c0 = h * PK_DQ + p * PK_DH
            s = lax.dot_general(keys_ref[p, h], q_ref[:, c0:c0 + PK_DH], (((1,), (1,)), ((), ())),
                                preferred_element_type=F32)
            v, i = _topk_rows(s, PK_TOPK)
            sv.append(v)
            si.append(i)
        cand = jnp.concatenate([sv[0][i:i + 1, :] + sv[1] for i in range(PK_TOPK)], axis=0)
        cidx = jnp.concatenate([si[0][i:i + 1, :] * N_KEYS + si[1] for i in range(PK_TOPK)], axis=0)
        best, eid = _topk_rows(cand, PK_TOPK, payload=cidx)
        ex = jnp.exp(best - best[0:1, :])
        g_rows.append(ex / jnp.sum(ex, axis=0, keepdims=True))
        e_rows.append(eid)
    eidx_ref[...] = jnp.concatenate(e_rows, axis=0).T
    gate_ref[...] = jnp.concatenate(g_rows, axis=0).T


def _pk_topk(q, keys_bf):
    t = q.shape[0]
    tt = TT_TOPK
    ne = PK_HEADS * PK_TOPK
    return pl.pallas_call(
        _topk_body,
        out_shape=(jax.ShapeDtypeStruct((t, ne), jnp.int32), jax.ShapeDtypeStruct((t, ne), F32)),
        grid=(t // tt,),
        in_specs=[pl.BlockSpec((tt, PK_HEADS * PK_DQ), lambda i: (i, 0)),
                  pl.BlockSpec((2, PK_HEADS, N_KEYS, PK_DH), lambda i: (0, 0, 0, 0))],
        out_specs=(pl.BlockSpec((tt, ne), lambda i: (i, 0)), pl.BlockSpec((tt, ne), lambda i: (i, 0))),
        compiler_params=_cparams("arbitrary"),
        name="pk_topk",
    )(q, keys_bf)


SC_LANES = 16
SC_CORES = 2
SC_SUBCORES = 16
SC_WORKERS = SC_CORES * SC_SUBCORES
SC_ROW_CHUNKS = D_MODEL // SC_LANES
SC_TOKEN_BLOCK = 8


def _gelu_tanh(x):
    y = 0.7978845608028654 * (x + 0.044715 * (x * x * x))
    t = 1.0 - 2.0 / (jnp.exp(2.0 * y) + 1.0)
    return x * (0.5 * (1.0 + t))


def _peer_sc_body(h_hbm, idx_hbm, gate_hbm, u_hbm, v_hbm, out_hbm,
                  h_v, idx_v, gate_v, out_v, ub0, ub1, vb0, vb1, su0, su1, sv0, sv1, *, tokens_per_worker):
    nh, k, tb, lanes = PK_HEADS, PK_TOPK, SC_TOKEN_BLOCK, SC_LANES
    wid = lax.axis_index("s") * SC_CORES + lax.axis_index("c")
    ubufs, vbufs, sus, svs = (ub0, ub1), (vb0, vb1), (su0, su1), (sv0, sv1)
    lane = lax.iota(jnp.int32, lanes)
    zero = jnp.zeros((lanes,), F32)

    def start(tok, hd, par):
        irow = idx_v.at[tok * nh + hd]
        pltpu.async_copy(u_hbm.at[irow], ubufs[par], sus[par])
        pltpu.async_copy(v_hbm.at[irow], vbufs[par], svs[par])

    def wait(par):
        irow = idx_v.at[0]
        pltpu.make_async_copy(u_hbm.at[irow], ubufs[par], sus[par]).wait()
        pltpu.make_async_copy(v_hbm.at[irow], vbufs[par], svs[par]).wait()

    @pl.loop(0, tokens_per_worker // tb)
    def _(blk):
        base = wid * tokens_per_worker + blk * tb
        pltpu.sync_copy(h_hbm.at[pl.ds(base, tb)], h_v)
        pltpu.sync_copy(idx_hbm.at[pl.ds(base * nh, tb * nh)], idx_v)
        pltpu.sync_copy(gate_hbm.at[pl.ds(base * nh, tb * nh)], gate_v)
        start(0, 0, 0)

        @pl.loop(0, tb)
        def _(tok):
            for hd in range(nh):
                par = hd % 2
                if hd + 1 < nh:
                    start(tok, hd + 1, 1 - par)
                else:
                    @pl.when(tok + 1 < tb)
                    def _():
                        start(tok + 1, 0, 1 - par)
                wait(par)
                ub, vb = ubufs[par], vbufs[par]

                @plsc.parallel_loop(0, SC_ROW_CHUNKS, carry=(zero,) * k)
                def accs(c, acc):
                    hc = h_v[tok, pl.ds(c * lanes, lanes)]
                    return tuple(acc[r] + hc * ub[r, pl.ds(c * lanes, lanes)] for r in range(k))

                s_vec = zero
                for r in range(k):
                    s_vec = jnp.where(lane == r, jnp.sum(accs[r]), s_vec)
                w_vec = gate_v[tok * nh + hd, :] * _gelu_tanh(s_vec)
                wb = [jnp.sum(jnp.where(lane == r, w_vec, 0.0)) for r in range(k)]

                @plsc.parallel_loop(0, SC_ROW_CHUNKS)
                def _(c):
                    ps = [wb[r] * vb[r, pl.ds(c * lanes, lanes)] for r in range(k)]
                    while len(ps) > 1:
                        ps = [ps[i] + ps[i + 1] for i in range(0, len(ps), 2)]
                    if hd == 0:
                        out_v[tok, pl.ds(c * lanes, lanes)] = ps[0]
                    else:
                        out_v[tok, pl.ds(c * lanes, lanes)] = out_v[tok, pl.ds(c * lanes, lanes)] + ps[0]

        pltpu.sync_copy(out_v, out_hbm.at[pl.ds(base, tb)])


def _peer_experts(h, eidx, gate, u_tab, v_tab):
    t, d = h.shape
    assert t % (SC_WORKERS * SC_TOKEN_BLOCK) == 0 and d == D_MODEL
    rows = pltpu.VMEM((PK_TOPK, d), F32)
    return pl.kernel(
        functools.partial(_peer_sc_body, tokens_per_worker=t // SC_WORKERS),
        out_type=jax.ShapeDtypeStruct((t, d), F32),
        mesh=plsc.VectorSubcoreMesh(core_axis_name="c", subcore_axis_name="s"),
        scratch_types=[
            pltpu.VMEM((SC_TOKEN_BLOCK, d), F32),
            pltpu.VMEM((SC_TOKEN_BLOCK * PK_HEADS, PK_TOPK), jnp.int32),
            pltpu.VMEM((SC_TOKEN_BLOCK * PK_HEADS, PK_TOPK), F32),
            pltpu.VMEM((SC_TOKEN_BLOCK, d), F32),
            rows, rows, rows, rows,
            pltpu.SemaphoreType.DMA, pltpu.SemaphoreType.DMA,
            pltpu.SemaphoreType.DMA, pltpu.SemaphoreType.DMA,
        ],
        compiler_params=pltpu.CompilerParams(needs_layout_passes=False),
        name="peer_experts_sc",
    )(h, eidx, gate, u_tab, v_tab)


TM_FIN = 512


def _final_body(x1_ref, p_ref, m_ref, w_ref, y_ref):
    y_ref[...] = _rms(x1_ref[...] + m_ref[0, 5:6, :] * p_ref[...]) * w_ref[...]


def _final(x1, peer, mod, norm_f, row_of):
    t, d = x1.shape
    tile = pl.BlockSpec((TM_FIN, d), lambda i: (i, 0))
    return pl.pallas_call(
        _final_body,
        out_shape=jax.ShapeDtypeStruct((t, d), F32),
        grid=(t // TM_FIN,),
        in_specs=[tile, tile, pl.BlockSpec((1, N_MOD, d), lambda i: (row_of(i), 0, 0)),
                  pl.BlockSpec((1, d), lambda i: (0, 0))],
        out_specs=tile,
        compiler_params=_cparams("arbitrary"),
        name="final_norm",
    )(x1, peer, mod, norm_f.reshape(1, d))


def kernel(x_prompt, x_sample, state_hgrn, state_rglru, c, c_ctx, w_mod, b_mod, norm1, w_in,
           hg_lb_logits, hg_norm, lru_conv_w, lru_conv_b, lru_w_r, lru_b_r, lru_w_i, lru_b_i,
           lru_lambda, w_branch_a, w_branch_b, w_out, norm2, pk_w_q, pk_sub_keys, pk_u, pk_v,
           norm_f):
    assert w_mod.shape[0] == 1, "single trunk layer"
    d = D_MODEL
    nb_ctx, n_ctx, _ = x_prompt.shape
    nb_lat, n_lat, _ = x_sample.shape
    t_ctx, t_lat = nb_ctx * n_ctx, nb_lat * n_lat
    assert nb_lat < MOD_ROWS and t_ctx % n_lat == 0
    ctx_row = nb_lat

    x = jnp.concatenate([x_prompt.reshape(t_ctx, d), x_sample.reshape(t_lat, d)], axis=0)
    cond = jnp.zeros((MOD_ROWS, d), F32).at[:nb_lat].set(c).at[ctx_row].set(c_ctx)
    mod = _modulation(cond, w_mod[0], b_mod[0])
    lb = jnp.cumsum(jax.nn.softmax(hg_lb_logits.astype(F32), axis=1), axis=1)[:, 0]

    z = _in_proj(x, mod, norm1[0], w_in[0].astype(BF16), _mod_row_map(TM_IN, t_ctx, n_lat, ctx_row))

    zeros_hg = jnp.zeros((nb_ctx, 2, HG_HEADS, HG_DK, HG_DV), F32)
    o_ctx, hg_fin = _hgrn2(z, lb, zeros_hg, n_ctx, 0)
    o_lat, _ = _hgrn2(z, lb, state_hgrn[:, 0], n_lat, t_ctx // n_lat)
    o_hg = jnp.concatenate([o_ctx, o_lat], axis=0)

    wr_bd, wi_bd = _block_diag(lru_w_r[0]), _block_diag(lru_w_i[0])
    lru_args = (lru_conv_w[0], lru_conv_b[0], wr_bd, wi_bd, lru_b_r[0], lru_b_i[0], lru_lambda[0])
    hr_ctx, lru_fin = _rglru(z, COL_XR, 0, n_ctx, nb_ctx, *lru_args, jnp.zeros((nb_ctx, 2, LRU_W), F32))
    rows = n_lat // GRID_W
    xr_cm = (z[t_ctx:, COL_XR * LRU_W:(COL_XR + 1) * LRU_W]
             .reshape(nb_lat, rows, GRID_W, LRU_W).transpose(0, 2, 1, 3).reshape(t_lat, LRU_W))
    hr_cm, _ = _rglru(xr_cm, 0, 0, n_lat, nb_lat, *lru_args, state_rglru[:, 0])
    hr_lat = hr_cm.reshape(nb_lat, GRID_W, rows, LRU_W).transpose(0, 2, 1, 3).reshape(t_lat, LRU_W)
    hr = jnp.concatenate([hr_ctx, hr_lat], axis=0)

    x1, h2, q = _out_proj(x, o_hg, hr, z, mod, hg_norm[0], norm2[0],
                          w_branch_a[0].astype(BF16), w_branch_b[0].astype(BF16), w_out[0].astype(BF16),
                          pk_w_q[0].astype(BF16), _mod_row_map(TM_OUT, t_ctx, n_lat, ctx_row))
    eidx, gate = _pk_topk(q, pk_sub_keys[0].astype(BF16))
    t = t_ctx + t_lat
    peer = _peer_experts(h2, eidx.reshape(t * PK_HEADS, PK_TOPK), gate.reshape(t * PK_HEADS, PK_TOPK),
                         pk_u[0], pk_v[0])
    y = _final(x1, peer, mod, norm_f, _mod_row_map(TM_FIN, t_ctx, n_lat, ctx_row))
    return (y[:t_ctx].reshape(nb_ctx, n_ctx, d), y[t_ctx:].reshape(nb_lat, n_lat, d),
            hg_fin[:, None], lru_fin[:, None])
```

```python
import functools

import jax, jax.numpy as jnp
from jax import lax
from jax.experimental import pallas as pl
from jax.experimental.pallas import tpu as pltpu
from jax.experimental.pallas import tpu_sc as plsc

D_MODEL = 1024
GRID_W = 64
EPS = 1e-6
HG_HEADS = 4
HG_DK = 128
HG_DV = 128
HG_KW = HG_HEADS * HG_DK
HG_VW = HG_HEADS * HG_DV
HG_CHUNK = 32
LRU_W = D_MODEL // 2
LRU_BLOCKS = 8
LRU_BW = LRU_W // LRU_BLOCKS
CONV_W = 4
LRU_C = 8.0
PK_HEADS = 8
N_KEYS = 128
PK_TOPK = 16
PK_DQ = 256
PK_DH = PK_DQ // 2
IN_W = 3 * HG_KW + 2 * HG_VW + 2 * LRU_W + 2 * D_MODEL
COL_G, COL_XR, COL_XG, COL_GA, COL_GB = 4, 5, 6, 7, 9
N_MOD = 6
MOD_ROWS = 16
F32 = jnp.float32
BF16 = jnp.bfloat16
VMEM_LIMIT = 48 * 1024 * 1024


def _cparams(*sem):
    return pltpu.CompilerParams(dimension_semantics=sem, vmem_limit_bytes=VMEM_LIMIT)


def _silu(x):
    return x * jax.nn.sigmoid(x)


def _rms(x):
    return x * lax.rsqrt(jnp.mean(x * x, axis=-1, keepdims=True) + EPS)


def _mod_body(c_ref, w_ref, b_ref, o_ref):
    o_ref[...] = lax.dot_general(_silu(c_ref[...]), w_ref[...], (((1,), (0,)), ((), ())),
                                 precision=lax.Precision.HIGHEST, preferred_element_type=F32) + b_ref[...]


def _modulation(cond, w_mod, b_mod):
    d = D_MODEL
    out = pl.pallas_call(
        _mod_body,
        out_shape=jax.ShapeDtypeStruct((MOD_ROWS, N_MOD * d), F32),
        grid=(N_MOD,),
        in_specs=[pl.BlockSpec((MOD_ROWS, d), lambda j: (0, 0)),
                  pl.BlockSpec((d, d), lambda j: (0, j)),
                  pl.BlockSpec((1, d), lambda j: (0, j))],
        out_specs=pl.BlockSpec((MOD_ROWS, d), lambda j: (0, j)),
        compiler_params=_cparams("arbitrary"),
        name="adaln_modulation",
    )(cond, w_mod, b_mod.reshape(1, N_MOD * d))
    return out.reshape(MOD_ROWS, N_MOD, d)


def _mod_row_map(tm, n, row0, per_seq):
    per = n // tm
    return (lambda i: row0 + i // per) if per_seq else (lambda i: row0)


TM_IN = 512
TN_IN = IN_W // 2


def _in_body(x_ref, m_ref, n1_ref, w_ref, z_ref, h_scr):
    @pl.when(pl.program_id(1) == 0)
    def _():
        y = _rms(x_ref[...]) * n1_ref[...]
        h_scr[...] = (y * (1.0 + m_ref[0, 1:2, :]) + m_ref[0, 0:1, :]).astype(BF16)

    z_ref[...] = jnp.dot(h_scr[...], w_ref[...], preferred_element_type=F32)


def _in_proj(x, mod, norm1, w_in_bf, row_of):
    t, d = x.shape
    return pl.pallas_call(
        _in_body,
        out_shape=jax.ShapeDtypeStruct((t, IN_W), F32),
        grid=(t // TM_IN, IN_W // TN_IN),
        in_specs=[pl.BlockSpec((TM_IN, d), lambda i, j: (i, 0)),
                  pl.BlockSpec((1, N_MOD, d), lambda i, j: (row_of(i), 0, 0)),
                  pl.BlockSpec((1, d), lambda i, j: (0, 0)),
                  pl.BlockSpec((d, TN_IN), lambda i, j: (0, j))],
        out_specs=pl.BlockSpec((TM_IN, TN_IN), lambda i, j: (i, j)),
        scratch_shapes=[pltpu.VMEM((TM_IN, d), BF16)],
        compiler_params=_cparams("arbitrary", "arbitrary"),
        name="in_proj",
    )(x, mod, norm1.reshape(1, d), w_in_bf)


def _hg_body(q_ref, ff_ref, fb_ref, v_ref, lb_ref, s0_ref, o_ref, sfin_ref, st_scr, ob_scr, *, n):
    c = HG_CHUNK
    nc = n // c
    row = lax.broadcasted_iota(jnp.int32, (c, c), 0)
    col = lax.broadcasted_iota(jnp.int32, (c, c), 1)
    lower = row >= col
    tri = (lower.astype(F32), (row <= col).astype(F32))
    masks = (lower, row <= col)
    f_refs = (ff_ref, fb_ref)
    for d in range(2):
        st_scr[d] = s0_ref[0, d, 0].T

    def chunk(ci, d):
        r = pl.ds(pl.multiple_of(ci * c, c), c)
        q = _silu(q_ref[r, :])
        v = v_ref[r, :].astype(BF16)
        lb = lb_ref[d:d + 1, :]
        f = lb + (1.0 - lb) * jax.nn.sigmoid(f_refs[d][r, :])
        k = 1.0 - f
        lf = jnp.log(f)
        cum = lax.dot_general(tri[d], lf, (((1,), (0,)), ((), ())),
                              precision=lax.Precision.HIGHEST, preferred_element_type=F32)
        tot = jnp.sum(lf, axis=0, keepdims=True)
        q_dec = (q * jnp.exp(cum)).astype(BF16)
        k_inv = (k * jnp.exp(-cum)).astype(BF16)
        k_end = (k * jnp.exp(tot - cum)).astype(BF16)
        att = lax.dot_general(q_dec, k_inv, (((1,), (1,)), ((), ())), preferred_element_type=F32)
        att = jnp.where(masks[d], att, 0.0).astype(BF16)
        st = st_scr[d]
        o = (jnp.dot(att, v, preferred_element_type=F32)
             + lax.dot_general(q_dec, st.astype(BF16), (((1,), (1,)), ((), ())), preferred_element_type=F32))
        ds_t = lax.dot_general(v, k_end, (((0,), (0,)), ((), ())), preferred_element_type=F32)
        st_scr[d] = st * jnp.exp(tot) + ds_t
        return r, o

    def step(i, carry):
        r, o = chunk(i, 0)
        o_ref[r, :] = o
        r, o = chunk(nc - 1 - i, 1)
        ob_scr[r, :] = o
        return carry

    lax.fori_loop(0, nc, step, 0)
    o_ref[...] = o_ref[...] + ob_scr[...]
    for d in range(2):
        sfin_ref[0, d, 0] = st_scr[d].T


def _hgrn2(z, lb, s0, n, row_block0):
    seqs = s0.shape[0]
    hb = HG_KW // HG_DK

    def zspec(col0):
        return pl.BlockSpec((n, HG_DK), lambda s, h: (row_block0 + s, col0 + h))

    st_spec = pl.BlockSpec((1, 2, 1, HG_DK, HG_DV), lambda s, h: (s, 0, h, 0, 0))
    return pl.pallas_call(
        functools.partial(_hg_body, n=n),
        out_shape=(jax.ShapeDtypeStruct((seqs * n, HG_VW), F32),
                   jax.ShapeDtypeStruct((seqs, 2, HG_HEADS, HG_DK, HG_DV), F32)),
        grid=(seqs, HG_HEADS),
        in_specs=[zspec(0), zspec(hb), zspec(2 * hb), zspec(3 * hb),
                  pl.BlockSpec((2, HG_DK), lambda s, h: (0, h)), st_spec],
        out_specs=(pl.BlockSpec((n, HG_DV), lambda s, h: (s, h)), st_spec),
        scratch_shapes=[pltpu.VMEM((2, HG_DV, HG_DK), F32), pltpu.VMEM((n, HG_DV), F32)],
        compiler_params=_cparams("arbitrary", "arbitrary"),
        name="hgrn2_scan",
    )(z, z, z, z, lb, s0)


LRU_RB = 64
HALO = 8


def _lru_body(x_ref, cw_ref, cb_ref, wr_ref, wi_ref, br_ref, bi_ref, lam_ref, h0_ref, hr_ref, e_ref,
              xp_scr, xc_scr, *, n):
    rb = LRU_RB
    nb = n // rb
    w = LRU_W
    zeros = jnp.zeros((HALO, w), F32)
    xp_scr[0:HALO, :] = zeros
    xp_scr[HALO + n:HALO + n + HALO, :] = zeros
    xp_scr[HALO:HALO + n, :] = x_ref[...]
    cw = cw_ref[...]
    cb = cb_ref[...]

    def conv_blk(b, carry):
        base = pl.multiple_of(b * rb, rb)
        xh = xp_scr[pl.ds(base, rb + 2 * HALO), :]
        ext = rb + 2 * HALO
        acc = cb + cw[2:3, :] * xh[HALO:HALO + rb]
        acc = acc + cw[0:1, :] * pltpu.roll(xh, 2, axis=0)[HALO:HALO + rb]
        acc = acc + cw[1:2, :] * pltpu.roll(xh, 1, axis=0)[HALO:HALO + rb]
        acc = acc + cw[3:4, :] * pltpu.roll(xh, ext - 1, axis=0)[HALO:HALO + rb]
        xc_scr[pl.ds(base, rb), :] = acc
        return carry

    lax.fori_loop(0, nb, conv_blk, 0)

    rows = lax.broadcasted_iota(jnp.int32, (rb, w), 0)

    def gates(blk, d):
        r = pl.ds(pl.multiple_of(blk * rb, rb), rb)
        xc = xc_scr[r, :]
        xb = xc.astype(BF16)
        rg = jax.nn.sigmoid(jnp.dot(xb, wr_ref[d], preferred_element_type=F32) + br_ref[d:d + 1, :])
        ig = jax.nn.sigmoid(jnp.dot(xb, wi_ref[d], preferred_element_type=F32) + bi_ref[d:d + 1, :])
        log_a = (-LRU_C) * jax.nn.softplus(-lam_ref[d:d + 1, :]) * rg
        a = jnp.exp(log_a)
        u = jnp.sqrt(-jnp.tanh(log_a) * (a * a + 1.0)) * (ig * xc)
        return r, a, u

    def fwd_blk(blk, h_prev):
        r, a, u = gates(blk, 0)
        s = 1
        while s < rb:
            keep = rows >= s
            a_sh = jnp.where(keep, pltpu.roll(a, s, axis=0), 1.0)
            u_sh = jnp.where(keep, pltpu.roll(u, s, axis=0), 0.0)
            u = a * u_sh + u
            a = a * a_sh
            s *= 2
        h = u + a * h_prev
        hr_ref[r, :] = h
        return h[rb - 1:rb, :]

    def bwd_blk(i, h_next):
        r, a, u = gates(nb - 1 - i, 1)
        s = 1
        while s < rb:
            keep = rows < rb - s
            a_sh = jnp.where(keep, pltpu.roll(a, rb - s, axis=0), 1.0)
            u_sh = jnp.where(keep, pltpu.roll(u, rb - s, axis=0), 0.0)
            u = a * u_sh + u
            a = a * a_sh
            s *= 2
        h = u + a * h_next
        hr_ref[r, :] = hr_ref[r, :] + h
        return h[0:1, :]

    e_ref[0, 0:1, :] = lax.fori_loop(0, nb, fwd_blk, h0_ref[0, 0:1, :])
    e_ref[0, 1:2, :] = lax.fori_loop(0, nb, bwd_blk, h0_ref[0, 1:2, :])


def _rglru(x, col_block, row_block0, n, seqs, conv_w, conv_b, wr_bd, wi_bd, b_r, b_i, lam, h0):
    w = LRU_W
    full2 = lambda s: (0, 0)
    full3 = lambda s: (0, 0, 0)
    return pl.pallas_call(
        functools.partial(_lru_body, n=n),
        out_shape=(jax.ShapeDtypeStruct((seqs * n, w), F32), jax.ShapeDtypeStruct((seqs, 2, w), F32)),
        grid=(seqs,),
        in_specs=[pl.BlockSpec((n, w), lambda s: (row_block0 + s, col_block)),
                  pl.BlockSpec((CONV_W, w), full2), pl.BlockSpec((1, w), full2),
                  pl.BlockSpec((2, w, w), full3), pl.BlockSpec((2, w, w), full3),
                  pl.BlockSpec((2, w), full2), pl.BlockSpec((2, w), full2), pl.BlockSpec((2, w), full2),
                  pl.BlockSpec((1, 2, w), lambda s: (s, 0, 0))],
        out_specs=(pl.BlockSpec((n, w), lambda s: (s, 0)), pl.BlockSpec((1, 2, w), lambda s: (s, 0, 0))),
        scratch_shapes=[pltpu.VMEM((n + 2 * HALO, w), F32), pltpu.VMEM((n, w), F32)],
        compiler_params=_cparams("arbitrary"),
        name="rglru",
    )(x, conv_w, conv_b.reshape(1, w), wr_bd, wi_bd, b_r, b_i, lam, h0)


def _block_diag(wg):
    eye = jnp.eye(LRU_BLOCKS, dtype=wg.dtype)
    dense = wg[:, :, :, None, :] * eye[None, :, None, :, None]
    return dense.reshape(2, LRU_W, LRU_W).astype(BF16)


TM_OUT = 256


def _out_body(x_ref, o_ref, hr_ref, g_ref, xg_ref, ga0_ref, ga1_ref, gb0_ref, gb1_ref, m_ref,
              hgn_ref, n2_ref, wa_ref, wb_ref, wo_ref, wq_ref, x1_ref, h2_ref, q_ref):
    o = o_ref[...]
    parts = []
    for h in range(HG_HEADS):
        parts.append(_rms(o[:, h * HG_DV:(h + 1) * HG_DV]) * hgn_ref[...])
    on = jnp.concatenate(parts, axis=1) * _silu(g_ref[...])
    y_a = jnp.dot(on.astype(BF16), wa_ref[...], preferred_element_type=F32)
    y_b = jnp.dot((hr_ref[...] * jax.nn.gelu(xg_ref[...])).astype(BF16), wb_ref[...], preferred_element_type=F32)
    ga = jnp.concatenate([ga0_ref[...], ga1_ref[...]], axis=1)
    gb = jnp.concatenate([gb0_ref[...], gb1_ref[...]], axis=1)
    merged = jax.nn.sigmoid(ga) * y_a + jax.nn.sigmoid(gb) * y_b
    mix = jnp.dot(merged.astype(BF16), wo_ref[...], preferred_element_type=F32)
    x1 = x_ref[...] + m_ref[0, 2:3, :] * mix
    x1_ref[...] = x1
    h2 = _rms(x1) * n2_ref[...] * (1.0 + m_ref[0, 4:5, :]) + m_ref[0, 3:4, :]
    h2_ref[...] = h2
    q_ref[...] = jnp.dot(h2.astype(BF16), wq_ref[...], preferred_element_type=F32).astype(BF16)


def _out_proj(x, o_hg, hr, z, mod, hg_norm, norm2, wa, wb, wo, wq, row_of):
    t, d = x.shape
    tm = TM_OUT
    nq = PK_HEADS * PK_DQ
    half = lambda i: (i, 0)
    zc = lambda c: pl.BlockSpec((tm, LRU_W), lambda i: (i, c))
    const = lambda i: (0, 0)
    return pl.pallas_call(
        _out_body,
        out_shape=(jax.ShapeDtypeStruct((t, d), F32), jax.ShapeDtypeStruct((t, d), F32),
                   jax.ShapeDtypeStruct((t, nq), BF16)),
        grid=(t // tm,),
        in_specs=[pl.BlockSpec((tm, d), half), pl.BlockSpec((tm, HG_VW), half), pl.BlockSpec((tm, LRU_W), half),
                  zc(COL_G), zc(COL_XG), zc(COL_GA), zc(COL_GA + 1), zc(COL_GB), zc(COL_GB + 1),
                  pl.BlockSpec((1, N_MOD, d), lambda i: (row_of(i), 0, 0)),
                  pl.BlockSpec((1, HG_DV), const), pl.BlockSpec((1, d), const),
                  pl.BlockSpec((HG_VW, d), const), pl.BlockSpec((LRU_W, d), const),
                  pl.BlockSpec((d, d), const), pl.BlockSpec((d, nq), const)],
        out_specs=(pl.BlockSpec((tm, d), half), pl.BlockSpec((tm, d), half), pl.BlockSpec((tm, nq), half)),
        compiler_params=_cparams("arbitrary"),
        name="out_proj",
    )(x, o_hg, hr, z, z, z, z, z, z, mod, hg_norm.reshape(1, HG_DV), norm2.reshape(1, d), wa, wb, wo, wq)


TT_TOPK = 128


def _topk_rows(s, k, payload=None):
    r = s.shape[0]
    rid = lax.broadcasted_iota(jnp.int32, s.shape, 0)
    vals, picks = [], []
    for _ in range(k):
        m = jnp.max(s, axis=0, keepdims=True)
        idx = jnp.min(jnp.where(s == m, rid, r), axis=0, keepdims=True)
        hit = rid == idx
        vals.append(m)
        picks.append(idx if payload is None else jnp.sum(jnp.where(hit, payload, 0), axis=0, keepdims=True))
        s = jnp.where(hit, -jnp.inf, s)
    return jnp.concatenate(vals, axis=0), jnp.concatenate(picks, axis=0)


def _topk_body(q_ref, keys_ref, eidx_ref, gate_ref):
    e_rows, g_rows = [], []
    for h in range(PK_HEADS):
        sv, si = [], []
        for p in range(2):
            c0 = h * PK_DQ + p * PK_DH
            s = lax.dot_general(keys_ref[p, h], q_ref[:, c0:c0 + PK_DH], (((1,), (1,)), ((), ())),
                                preferred_element_type=F32)
            v, i = _topk_rows(s, PK_TOPK)
            sv.append(v)
            si.append(i)
        cand = jnp.concatenate([sv[0][i:i + 1, :] + sv[1] for i in range(PK_TOPK)], axis=0)
        cidx = jnp.concatenate([si[0][i:i + 1, :] * N_KEYS + si[1] for i in range(PK_TOPK)], axis=0)
        best, eid = _topk_rows(cand, PK_TOPK, payload=cidx)
        ex = jnp.exp(best - best[0:1, :])
        g_rows.append(ex / jnp.sum(ex, axis=0, keepdims=True))
        e_rows.append(eid)
    eidx_ref[...] = jnp.concatenate(e_rows, axis=0).T
    gate_ref[...] = jnp.concatenate(g_rows, axis=0).T


def _pk_topk(q, keys_bf):
    t = q.shape[0]
    tt = TT_TOPK
    ne = PK_HEADS * PK_TOPK
    return pl.pallas_call(
        _topk_body,
        out_shape=(jax.ShapeDtypeStruct((t, ne), jnp.int32), jax.ShapeDtypeStruct((t, ne), F32)),
        grid=(t // tt,),
        in_specs=[pl.BlockSpec((tt, PK_HEADS * PK_DQ), lambda i: (i, 0)),
                  pl.BlockSpec((2, PK_HEADS, N_KEYS, PK_DH), lambda i: (0, 0, 0, 0))],
        out_specs=(pl.BlockSpec((tt, ne), lambda i: (i, 0)), pl.BlockSpec((tt, ne), lambda i: (i, 0))),
        compiler_params=_cparams("arbitrary"),
        name="pk_topk",
    )(q, keys_bf)


SC_LANES = 16
SC_CORES = 2
SC_SUBCORES = 16
SC_WORKERS = SC_CORES * SC_SUBCORES
SC_ROW_CHUNKS = D_MODEL // SC_LANES
SC_TOKEN_BLOCK = 8


def _gelu_tanh(x):
    y = 0.7978845608028654 * (x + 0.044715 * (x * x * x))
    t = 1.0 - 2.0 / (jnp.exp(2.0 * y) + 1.0)
    return x * (0.5 * (1.0 + t))


def _peer_sc_body(h_hbm, idx_hbm, gate_hbm, u_hbm, v_hbm, out_hbm,
                  h_v, idx_v, gate_v, out_v, ub0, ub1, vb0, vb1, su0, su1, sv0, sv1, *, tokens_per_worker):
    nh, k, tb, lanes = PK_HEADS, PK_TOPK, SC_TOKEN_BLOCK, SC_LANES
    wid = lax.axis_index("s") * SC_CORES + lax.axis_index("c")
    ubufs, vbufs, sus, svs = (ub0, ub1), (vb0, vb1), (su0, su1), (sv0, sv1)
    lane = lax.iota(jnp.int32, lanes)
    zero = jnp.zeros((lanes,), F32)

    def start(tok, hd, par):
        irow = idx_v.at[tok * nh + hd]
        pltpu.async_copy(u_hbm.at[irow], ubufs[par], sus[par])
        pltpu.async_copy(v_hbm.at[irow], vbufs[par], svs[par])

    def wait(par):
        irow = idx_v.at[0]
        pltpu.make_async_copy(u_hbm.at[irow], ubufs[par], sus[par]).wait()
        pltpu.make_async_copy(v_hbm.at[irow], vbufs[par], svs[par]).wait()

    @pl.loop(0, tokens_per_worker // tb)
    def _(blk):
        base = wid * tokens_per_worker + blk * tb
        pltpu.sync_copy(h_hbm.at[pl.ds(base, tb)], h_v)
        pltpu.sync_copy(idx_hbm.at[pl.ds(base * nh, tb * nh)], idx_v)
        pltpu.sync_copy(gate_hbm.at[pl.ds(base * nh, tb * nh)], gate_v)
        start(0, 0, 0)

        @pl.loop(0, tb)
        def _(tok):
            for hd in range(nh):
                par = hd % 2
                if hd + 1 < nh:
                    start(tok, hd + 1, 1 - par)
                else:
                    @pl.when(tok + 1 < tb)
                    def _():
                        start(tok + 1, 0, 1 - par)
                wait(par)
                ub, vb = ubufs[par], vbufs[par]

                @plsc.parallel_loop(0, SC_ROW_CHUNKS, carry=(zero,) * k)
                def accs(c, acc):
                    hc = h_v[tok, pl.ds(c * lanes, lanes)]
                    return tuple(acc[r] + hc * ub[r, pl.ds(c * lanes, lanes)] for r in range(k))

                s_vec = zero
                for r in range(k):
                    s_vec = jnp.where(lane == r, jnp.sum(accs[r]), s_vec)
                w_vec = gate_v[tok * nh + hd, :] * _gelu_tanh(s_vec)
                wb = [jnp.sum(jnp.where(lane == r, w_vec, 0.0)) for r in range(k)]

                @plsc.parallel_loop(0, SC_ROW_CHUNKS)
                def _(c):
                    ps = [wb[r] * vb[r, pl.ds(c * lanes, lanes)] for r in range(k)]
                    while len(ps) > 1:
                        ps = [ps[i] + ps[i + 1] for i in range(0, len(ps), 2)]
                    if hd == 0:
                        out_v[tok, pl.ds(c * lanes, lanes)] = ps[0]
                    else:
                        out_v[tok, pl.ds(c * lanes, lanes)] = out_v[tok, pl.ds(c * lanes, lanes)] + ps[0]

        pltpu.sync_copy(out_v, out_hbm.at[pl.ds(base, tb)])


def _peer_experts(h, eidx, gate, u_tab, v_tab):
    t, d = h.shape
    assert t % (SC_WORKERS * SC_TOKEN_BLOCK) == 0 and d == D_MODEL
    rows = pltpu.VMEM((PK_TOPK, d), F32)
    return pl.kernel(
        functools.partial(_peer_sc_body, tokens_per_worker=t // SC_WORKERS),
        out_type=jax.ShapeDtypeStruct((t, d), F32),
        mesh=plsc.VectorSubcoreMesh(core_axis_name="c", subcore_axis_name="s"),
        scratch_types=[
            pltpu.VMEM((SC_TOKEN_BLOCK, d), F32),
            pltpu.VMEM((SC_TOKEN_BLOCK * PK_HEADS, PK_TOPK), jnp.int32),
            pltpu.VMEM((SC_TOKEN_BLOCK * PK_HEADS, PK_TOPK), F32),
            pltpu.VMEM((SC_TOKEN_BLOCK, d), F32),
            rows, rows, rows, rows,
            pltpu.SemaphoreType.DMA, pltpu.SemaphoreType.DMA,
            pltpu.SemaphoreType.DMA, pltpu.SemaphoreType.DMA,
        ],
        compiler_params=pltpu.CompilerParams(needs_layout_passes=False),
        name="peer_experts_sc",
    )(h, eidx, gate, u_tab, v_tab)


TM_FIN = 512


def _final_body(x1_ref, p_ref, m_ref, w_ref, y_ref):
    y_ref[...] = _rms(x1_ref[...] + m_ref[0, 5:6, :] * p_ref[...]) * w_ref[...]


def _final(x1, peer, mod, norm_f, row_of):
    t, d = x1.shape
    tile = pl.BlockSpec((TM_FIN, d), lambda i: (i, 0))
    return pl.pallas_call(
        _final_body,
        out_shape=jax.ShapeDtypeStruct((t, d), F32),
        grid=(t // TM_FIN,),
        in_specs=[tile, tile, pl.BlockSpec((1, N_MOD, d), lambda i: (row_of(i), 0, 0)),
                  pl.BlockSpec((1, d), lambda i: (0, 0))],
        out_specs=tile,
        compiler_params=_cparams("arbitrary"),
        name="final_norm",
    )(x1, peer, mod, norm_f.reshape(1, d))


LAT_GROUP = 4


def _group(x, mod, row0, per_seq, n, hg_s0, lru_s0, col_major, p):
    seqs = hg_s0.shape[0]
    t = seqs * n
    rows_of = lambda tm: _mod_row_map(tm, n, row0, per_seq)
    z = _in_proj(x, mod, p['norm1'], p['w_in'], rows_of(TM_IN))
    o_hg, hg_fin = _hgrn2(z, p['lb'], hg_s0, n, 0)
    if col_major:
        rows = n // GRID_W
        xr = (z[:, COL_XR * LRU_W:(COL_XR + 1) * LRU_W]
              .reshape(seqs, rows, GRID_W, LRU_W).transpose(0, 2, 1, 3).reshape(t, LRU_W))
        hr, lru_fin = _rglru(xr, 0, 0, n, seqs, *p['lru'], lru_s0)
        hr = hr.reshape(seqs, GRID_W, rows, LRU_W).transpose(0, 2, 1, 3).reshape(t, LRU_W)
    else:
        hr, lru_fin = _rglru(z, COL_XR, 0, n, seqs, *p['lru'], lru_s0)
    x1, h2, q = _out_proj(x, o_hg, hr, z, mod, p['hg_norm'], p['norm2'], p['w_a'], p['w_b'], p['w_o'], p['w_q'],
                          rows_of(TM_OUT))
    eidx, gate = _pk_topk(q, p['keys'])
    peer = _peer_experts(h2, eidx.reshape(t * PK_HEADS, PK_TOPK), gate.reshape(t * PK_HEADS, PK_TOPK),
                         p['pk_u'], p['pk_v'])
    return _final(x1, peer, mod, p['norm_f'], rows_of(TM_FIN)), hg_fin, lru_fin


def kernel(x_prompt, x_sample, state_hgrn, state_rglru, c, c_ctx, w_mod, b_mod, norm1, w_in,
           hg_lb_logits, hg_norm, lru_conv_w, lru_conv_b, lru_w_r, lru_b_r, lru_w_i, lru_b_i,
           lru_lambda, w_branch_a, w_branch_b, w_out, norm2, pk_w_q, pk_sub_keys, pk_u, pk_v,
           norm_f):
    assert w_mod.shape[0] == 1, "single trunk layer"
    d = D_MODEL
    nb_ctx, n_ctx, _ = x_prompt.shape
    nb_lat, n_lat, _ = x_sample.shape
    assert nb_lat < MOD_ROWS and nb_lat % LAT_GROUP == 0
    ctx_row = nb_lat

    cond = jnp.zeros((MOD_ROWS, d), F32).at[:nb_lat].set(c).at[ctx_row].set(c_ctx)
    mod = _modulation(cond, w_mod[0], b_mod[0])
    p = {
        'lb': jnp.cumsum(jax.nn.softmax(hg_lb_logits.astype(F32), axis=1), axis=1)[:, 0],
        'norm1': norm1[0], 'w_in': w_in[0].astype(BF16), 'hg_norm': hg_norm[0], 'norm2': norm2[0],
        'lru': (lru_conv_w[0], lru_conv_b[0], _block_diag(lru_w_r[0]), _block_diag(lru_w_i[0]),
                lru_b_r[0], lru_b_i[0], lru_lambda[0]),
        'w_a': w_branch_a[0].astype(BF16), 'w_b': w_branch_b[0].astype(BF16), 'w_o': w_out[0].astype(BF16),
        'w_q': pk_w_q[0].astype(BF16), 'keys': pk_sub_keys[0].astype(BF16),
        'pk_u': pk_u[0], 'pk_v': pk_v[0], 'norm_f': norm_f,
    }
    y_ctx, hg_fin, lru_fin = _group(
        x_prompt.reshape(nb_ctx * n_ctx, d), mod, ctx_row, False, n_ctx,
        jnp.zeros((nb_ctx, 2, HG_HEADS, HG_DK, HG_DV), F32), jnp.zeros((nb_ctx, 2, LRU_W), F32), False, p)
    y_lat = []
    for s0 in range(0, nb_lat, LAT_GROUP):
        sl = slice(s0, s0 + LAT_GROUP)
        y, _, _ = _group(x_sample[sl].reshape(LAT_GROUP * n_lat, d), mod, s0, True, n_lat,
                         state_hgrn[sl, 0], state_rglru[sl, 0], True, p)
        y_lat.append(y.reshape(LAT_GROUP, n_lat, d))
    return (y_ctx.reshape(nb_ctx, n_ctx, d), jnp.concatenate(y_lat, axis=0), hg_fin[:, None], lru_fin[:, None])
```

```python
import functools

import jax, jax.numpy as jnp
from jax import lax
from jax.experimental import pallas as pl
from jax.experimental.pallas import tpu as pltpu
from jax.experimental.pallas import tpu_sc as plsc

D_MODEL = 1024
GRID_W = 64
EPS = 1e-6
HG_HEADS = 4
HG_DK = 128
HG_DV = 128
HG_KW = HG_HEADS * HG_DK
HG_VW = HG_HEADS * HG_DV
HG_CHUNK = 32
LRU_W = D_MODEL // 2
LRU_BLOCKS = 8
LRU_BW = LRU_W // LRU_BLOCKS
CONV_W = 4
LRU_C = 8.0
PK_HEADS = 8
N_KEYS = 128
PK_TOPK = 16
PK_DQ = 256
PK_DH = PK_DQ // 2
IN_W = 3 * HG_KW + 2 * HG_VW + 2 * LRU_W + 2 * D_MODEL
COL_G, COL_XR, COL_XG, COL_GA, COL_GB = 4, 5, 6, 7, 9
N_MOD = 6
MOD_ROWS = 16
F32 = jnp.float32
BF16 = jnp.bfloat16
VMEM_LIMIT = 48 * 1024 * 1024


def _cparams(*sem):
    return pltpu.CompilerParams(dimension_semantics=sem, vmem_limit_bytes=VMEM_LIMIT)


def _silu(x):
    return x * jax.nn.sigmoid(x)


def _rms(x):
    return x * lax.rsqrt(jnp.mean(x * x, axis=-1, keepdims=True) + EPS)


def _mod_body(c_ref, w_ref, b_ref, o_ref):
    o_ref[...] = lax.dot_general(_silu(c_ref[...]), w_ref[...], (((1,), (0,)), ((), ())),
                                 precision=lax.Precision.HIGHEST, preferred_element_type=F32) + b_ref[...]


def _modulation(cond, w_mod, b_mod):
    d = D_MODEL
    out = pl.pallas_call(
        _mod_body,
        out_shape=jax.ShapeDtypeStruct((MOD_ROWS, N_MOD * d), F32),
        grid=(N_MOD,),
        in_specs=[pl.BlockSpec((MOD_ROWS, d), lambda j: (0, 0)),
                  pl.BlockSpec((d, d), lambda j: (0, j)),
                  pl.BlockSpec((1, d), lambda j: (0, j))],
        out_specs=pl.BlockSpec((MOD_ROWS, d), lambda j: (0, j)),
        compiler_params=_cparams("arbitrary"),
        name="adaln_modulation",
    )(cond, w_mod, b_mod.reshape(1, N_MOD * d))
    return out.reshape(MOD_ROWS, N_MOD, d)


def _mod_row_map(tm, n, row0, per_seq):
    per = n // tm
    return (lambda i: row0 + i // per) if per_seq else (lambda i: row0)


TM_IN = 512
TN_IN = IN_W // 2


def _in_body(x_ref, m_ref, n1_ref, w_ref, z_ref, h_scr):
    @pl.when(pl.program_id(1) == 0)
    def _():
        y = _rms(x_ref[...]) * n1_ref[...]
        h_scr[...] = (y * (1.0 + m_ref[0, 1:2, :]) + m_ref[0, 0:1, :]).astype(BF16)

    z_ref[...] = jnp.dot(h_scr[...], w_ref[...], preferred_element_type=F32)


def _in_proj(x, mod, norm1, w_in_bf, row_of):
    t, d = x.shape
    return pl.pallas_call(
        _in_body,
        out_shape=jax.ShapeDtypeStruct((t, IN_W), F32),
        grid=(t // TM_IN, IN_W // TN_IN),
        in_specs=[pl.BlockSpec((TM_IN, d), lambda i, j: (i, 0)),
                  pl.BlockSpec((1, N_MOD, d), lambda i, j: (row_of(i), 0, 0)),
                  pl.BlockSpec((1, d), lambda i, j: (0, 0)),
                  pl.BlockSpec((d, TN_IN), lambda i, j: (0, j))],
        out_specs=pl.BlockSpec((TM_IN, TN_IN), lambda i, j: (i, j)),
        scratch_shapes=[pltpu.VMEM((TM_IN, d), BF16)],
        compiler_params=_cparams("arbitrary", "arbitrary"),
        name="in_proj",
    )(x, mod, norm1.reshape(1, d), w_in_bf)


def _hg_body(q_ref, ff_ref, fb_ref, v_ref, lb_ref, s0_ref, o_ref, sfin_ref, st_scr, ob_scr, *, n):
    c = HG_CHUNK
    nc = n // c
    row = lax.broadcasted_iota(jnp.int32, (c, c), 0)
    col = lax.broadcasted_iota(jnp.int32, (c, c), 1)
    lower = row >= col
    tri = (lower.astype(F32), (row <= col).astype(F32))
    masks = (lower, row <= col)
    f_refs = (ff_ref, fb_ref)
    for d in range(2):
        st_scr[d] = s0_ref[0, d, 0].T

    def chunk(ci, d):
        r = pl.ds(pl.multiple_of(ci * c, c), c)
        q = _silu(q_ref[r, :])
        v = v_ref[r, :].astype(BF16)
        lb = lb_ref[d:d + 1, :]
        f = lb + (1.0 - lb) * jax.nn.sigmoid(f_refs[d][r, :])
        k = 1.0 - f
        lf = jnp.log(f)
        cum = lax.dot_general(tri[d], lf, (((1,), (0,)), ((), ())),
                              precision=lax.Precision.HIGHEST, preferred_element_type=F32)
        tot = jnp.sum(lf, axis=0, keepdims=True)
        q_dec = (q * jnp.exp(cum)).astype(BF16)
        k_inv = (k * jnp.exp(-cum)).astype(BF16)
        k_end = (k * jnp.exp(tot - cum)).astype(BF16)
        att = lax.dot_general(q_dec, k_inv, (((1,), (1,)), ((), ())), preferred_element_type=F32)
        att = jnp.where(masks[d], att, 0.0).astype(BF16)
        st = st_scr[d]
        o = (jnp.dot(att, v, preferred_element_type=F32)
             + lax.dot_general(q_dec, st.astype(BF16), (((1,), (1,)), ((), ())), preferred_element_type=F32))
        ds_t = lax.dot_general(v, k_end, (((0,), (0,)), ((), ())), preferred_element_type=F32)
        st_scr[d] = st * jnp.exp(tot) + ds_t
        return r, o

    def step(i, carry):
        r, o = chunk(i, 0)
        o_ref[r, :] = o
        r, o = chunk(nc - 1 - i, 1)
        ob_scr[r, :] = o
        return carry

    lax.fori_loop(0, nc, step, 0)
    o_ref[...] = o_ref[...] + ob_scr[...]
    for d in range(2):
        sfin_ref[0, d, 0] = st_scr[d].T


def _hgrn2(z, lb, s0, n, row_block0):
    seqs = s0.shape[0]
    hb = HG_KW // HG_DK

    def zspec(col0):
        return pl.BlockSpec((n, HG_DK), lambda s, h: (row_block0 + s, col0 + h))

    st_spec = pl.BlockSpec((1, 2, 1, HG_DK, HG_DV), lambda s, h: (s, 0, h, 0, 0))
    return pl.pallas_call(
        functools.partial(_hg_body, n=n),
        out_shape=(jax.ShapeDtypeStruct((seqs * n, HG_VW), F32),
                   jax.ShapeDtypeStruct((seqs, 2, HG_HEADS, HG_DK, HG_DV), F32)),
        grid=(seqs, HG_HEADS),
        in_specs=[zspec(0), zspec(hb), zspec(2 * hb), zspec(3 * hb),
                  pl.BlockSpec((2, HG_DK), lambda s, h: (0, h)), st_spec],
        out_specs=(pl.BlockSpec((n, HG_DV), lambda s, h: (s, h)), st_spec),
        scratch_shapes=[pltpu.VMEM((2, HG_DV, HG_DK), F32), pltpu.VMEM((n, HG_DV), F32)],
        compiler_params=_cparams("arbitrary", "arbitrary"),
        name="hgrn2_scan",
    )(z, z, z, z, lb, s0)


LRU_RB = 64
HALO = 8


def _lru_body(x_ref, cw_ref, cb_ref, wr_ref, wi_ref, br_ref, bi_ref, lam_ref, h0_ref, hr_ref, e_ref,
              xp_scr, xc_scr, *, n):
    rb = LRU_RB
    nb = n // rb
    w = LRU_W
    zeros = jnp.zeros((HALO, w), F32)
    xp_scr[0:HALO, :] = zeros
    xp_scr[HALO + n:HALO + n + HALO, :] = zeros
    xp_scr[HALO:HALO + n, :] = x_ref[...]
    cw = cw_ref[...]
    cb = cb_ref[...]

    def conv_blk(b, carry):
        base = pl.multiple_of(b * rb, rb)
        xh = xp_scr[pl.ds(base, rb + 2 * HALO), :]
        ext = rb + 2 * HALO
        acc = cb + cw[2:3, :] * xh[HALO:HALO + rb]
        acc = acc + cw[0:1, :] * pltpu.roll(xh, 2, axis=0)[HALO:HALO + rb]
        acc = acc + cw[1:2, :] * pltpu.roll(xh, 1, axis=0)[HALO:HALO + rb]
        acc = acc + cw[3:4, :] * pltpu.roll(xh, ext - 1, axis=0)[HALO:HALO + rb]
        xc_scr[pl.ds(base, rb), :] = acc
        return carry

    lax.fori_loop(0, nb, conv_blk, 0)

    rows = lax.broadcasted_iota(jnp.int32, (rb, w), 0)

    def gates(blk, d):
        r = pl.ds(pl.multiple_of(blk * rb, rb), rb)
        xc = xc_scr[r, :]
        xb = xc.astype(BF16)
        rg = jax.nn.sigmoid(jnp.dot(xb, wr_ref[d], preferred_element_type=F32) + br_ref[d:d + 1, :])
        ig = jax.nn.sigmoid(jnp.dot(xb, wi_ref[d], preferred_element_type=F32) + bi_ref[d:d + 1, :])
        log_a = (-LRU_C) * jax.nn.softplus(-lam_ref[d:d + 1, :]) * rg
        a = jnp.exp(log_a)
        u = jnp.sqrt(-jnp.tanh(log_a) * (a * a + 1.0)) * (ig * xc)
        return r, a, u

    def fwd_blk(blk, h_prev):
        r, a, u = gates(blk, 0)
        s = 1
        while s < rb:
            keep = rows >= s
            a_sh = jnp.where(keep, pltpu.roll(a, s, axis=0), 1.0)
            u_sh = jnp.where(keep, pltpu.roll(u, s, axis=0), 0.0)
            u = a * u_sh + u
            a = a * a_sh
            s *= 2
        h = u + a * h_prev
        hr_ref[r, :] = h
        return h[rb - 1:rb, :]

    def bwd_blk(i, h_next):
        r, a, u = gates(nb - 1 - i, 1)
        s = 1
        while s < rb:
            keep = rows < rb - s
            a_sh = jnp.where(keep, pltpu.roll(a, rb - s, axis=0), 1.0)
            u_sh = jnp.where(keep, pltpu.roll(u, rb - s, axis=0), 0.0)
            u = a * u_sh + u
            a = a * a_sh
            s *= 2
        h = u + a * h_next
        hr_ref[r, :] = hr_ref[r, :] + h
        return h[0:1, :]

    e_ref[0, 0:1, :] = lax.fori_loop(0, nb, fwd_blk, h0_ref[0, 0:1, :])
    e_ref[0, 1:2, :] = lax.fori_loop(0, nb, bwd_blk, h0_ref[0, 1:2, :])


def _rglru(x, col_block, row_block0, n, seqs, conv_w, conv_b, wr_bd, wi_bd, b_r, b_i, lam, h0):
    w = LRU_W
    full2 = lambda s: (0, 0)
    full3 = lambda s: (0, 0, 0)
    return pl.pallas_call(
        functools.partial(_lru_body, n=n),
        out_shape=(jax.ShapeDtypeStruct((seqs * n, w), F32), jax.ShapeDtypeStruct((seqs, 2, w), F32)),
        grid=(seqs,),
        in_specs=[pl.BlockSpec((n, w), lambda s: (row_block0 + s, col_block)),
                  pl.BlockSpec((CONV_W, w), full2), pl.BlockSpec((1, w), full2),
                  pl.BlockSpec((2, w, w), full3), pl.BlockSpec((2, w, w), full3),
                  pl.BlockSpec((2, w), full2), pl.BlockSpec((2, w), full2), pl.BlockSpec((2, w), full2),
                  pl.BlockSpec((1, 2, w), lambda s: (s, 0, 0))],
        out_specs=(pl.BlockSpec((n, w), lambda s: (s, 0)), pl.BlockSpec((1, 2, w), lambda s: (s, 0, 0))),
        scratch_shapes=[pltpu.VMEM((n + 2 * HALO, w), F32), pltpu.VMEM((n, w), F32)],
        compiler_params=_cparams("arbitrary"),
        name="rglru",
    )(x, conv_w, conv_b.reshape(1, w), wr_bd, wi_bd, b_r, b_i, lam, h0)


def _block_diag(wg):
    eye = jnp.eye(LRU_BLOCKS, dtype=wg.dtype)
    dense = wg[:, :, :, None, :] * eye[None, :, None, :, None]
    return dense.reshape(2, LRU_W, LRU_W).astype(BF16)


TM_OUT = 256


def _out_body(x_ref, o_ref, hr_ref, g_ref, xg_ref, ga0_ref, ga1_ref, gb0_ref, gb1_ref, m_ref,
              hgn_ref, n2_ref, wa_ref, wb_ref, wo_ref, wq_ref, x1_ref, h2_ref, q_ref):
    o = o_ref[...]
    parts = []
    for h in range(HG_HEADS):
        parts.append(_rms(o[:, h * HG_DV:(h + 1) * HG_DV]) * hgn_ref[...])
    on = jnp.concatenate(parts, axis=1) * _silu(g_ref[...])
    y_a = jnp.dot(on.astype(BF16), wa_ref[...], preferred_element_type=F32)
    y_b = jnp.dot((hr_ref[...] * jax.nn.gelu(xg_ref[...])).astype(BF16), wb_ref[...], preferred_element_type=F32)
    ga = jnp.concatenate([ga0_ref[...], ga1_ref[...]], axis=1)
    gb = jnp.concatenate([gb0_ref[...], gb1_ref[...]], axis=1)
    merged = jax.nn.sigmoid(ga) * y_a + jax.nn.sigmoid(gb) * y_b
    mix = jnp.dot(merged.astype(BF16), wo_ref[...], preferred_element_type=F32)
    x1 = x_ref[...] + m_ref[0, 2:3, :] * mix
    x1_ref[...] = x1
    h2 = _rms(x1) * n2_ref[...] * (1.0 + m_ref[0, 4:5, :]) + m_ref[0, 3:4, :]
    h2_ref[...] = h2
    q_ref[...] = jnp.dot(h2.astype(BF16), wq_ref[...], preferred_element_type=F32).astype(BF16)


def _out_proj(x, o_hg, hr, z, mod, hg_norm, norm2, wa, wb, wo, wq, row_of):
    t, d = x.shape
    tm = TM_OUT
    nq = PK_HEADS * PK_DQ
    half = lambda i: (i, 0)
    zc = lambda c: pl.BlockSpec((tm, LRU_W), lambda i: (i, c))
    const = lambda i: (0, 0)
    return pl.pallas_call(
        _out_body,
        out_shape=(jax.ShapeDtypeStruct((t, d), F32), jax.ShapeDtypeStruct((t, d), F32),
                   jax.ShapeDtypeStruct((t, nq), BF16)),
        grid=(t // tm,),
        in_specs=[pl.BlockSpec((tm, d), half), pl.BlockSpec((tm, HG_VW), half), pl.BlockSpec((tm, LRU_W), half),
                  zc(COL_G), zc(COL_XG), zc(COL_GA), zc(COL_GA + 1), zc(COL_GB), zc(COL_GB + 1),
                  pl.BlockSpec((1, N_MOD, d), lambda i: (row_of(i), 0, 0)),
                  pl.BlockSpec((1, HG_DV), const), pl.BlockSpec((1, d), const),
                  pl.BlockSpec((HG_VW, d), const), pl.BlockSpec((LRU_W, d), const),
                  pl.BlockSpec((d, d), const), pl.BlockSpec((d, nq), const)],
        out_specs=(pl.BlockSpec((tm, d), half), pl.BlockSpec((tm, d), half), pl.BlockSpec((tm, nq), half)),
        compiler_params=_cparams("arbitrary"),
        name="out_proj",
    )(x, o_hg, hr, z, z, z, z, z, z, mod, hg_norm.reshape(1, HG_DV), norm2.reshape(1, d), wa, wb, wo, wq)


TT_TOPK = 128


def _topk_rows(s, k, payload=None):
    r = s.shape[0]
    rid = lax.broadcasted_iota(jnp.int32, s.shape, 0)
    vals, picks = [], []
    for _ in range(k):
        m = jnp.max(s, axis=0, keepdims=True)
        idx = jnp.min(jnp.where(s == m, rid, r), axis=0, keepdims=True)
        hit = rid == idx
        vals.append(m)
        picks.append(idx if payload is None else jnp.sum(jnp.where(hit, payload, 0), axis=0, keepdims=True))
        s = jnp.where(hit, -jnp.inf, s)
    return jnp.concatenate(vals, axis=0), jnp.concatenate(picks, axis=0)


def _topk_body(q_ref, keys_ref, eidx_ref, gate_ref):
    e_rows, g_rows = [], []
    for h in range(PK_HEADS):
        sv, si = [], []
        for p in range(2):
            c0 = h * PK_DQ + p * PK_DH
            s = lax.dot_general(keys_ref[p, h], q_ref[:, c0:c0 + PK_DH], (((1,), (1,)), ((), ())),
                                preferred_element_type=F32)
            v, i = _topk_rows(s, PK_TOPK)
            sv.append(v)
            si.append(i)
        cand = jnp.concatenate([sv[0][i:i + 1, :] + sv[1] for i in range(PK_TOPK)], axis=0)
        cidx = jnp.concatenate([si[0][i:i + 1, :] * N_KEYS + si[1] for i in range(PK_TOPK)], axis=0)
        best, eid = _topk_rows(cand, PK_TOPK, payload=cidx)
        ex = jnp.exp(best - best[0:1, :])
        g_rows.append(ex / jnp.sum(ex, axis=0, keepdims=True))
        e_rows.append(eid)
    eidx_ref[...] = jnp.concatenate(e_rows, axis=0).T
    gate_ref[...] = jnp.concatenate(g_rows, axis=0).T


def _pk_topk(q, keys_bf):
    t = q.shape[0]
    tt = TT_TOPK
    ne = PK_HEADS * PK_TOPK
    return pl.pallas_call(
        _topk_body,
        out_shape=(jax.ShapeDtypeStruct((t, ne), jnp.int32), jax.ShapeDtypeStruct((t, ne), F32)),
        grid=(t // tt,),
        in_specs=[pl.BlockSpec((tt, PK_HEADS * PK_DQ), lambda i: (i, 0)),
                  pl.BlockSpec((2, PK_HEADS, N_KEYS, PK_DH), lambda i: (0, 0, 0, 0))],
        out_specs=(pl.BlockSpec((tt, ne), lambda i: (i, 0)), pl.BlockSpec((tt, ne), lambda i: (i, 0))),
        compiler_params=_cparams("arbitrary"),
        name="pk_topk",
    )(q, keys_bf)


SC_LANES = 16
SC_CORES = 2
SC_SUBCORES = 16
SC_WORKERS = SC_CORES * SC_SUBCORES
SC_ROW_CHUNKS = D_MODEL // SC_LANES
SC_TOKEN_BLOCK = 8


def _gelu_tanh(x):
    y = 0.7978845608028654 * (x + 0.044715 * (x * x * x))
    t = 1.0 - 2.0 / (jnp.exp(2.0 * y) + 1.0)
    return x * (0.5 * (1.0 + t))


def _peer_sc_body(h_hbm, idx_hbm, gate_hbm, u_hbm, v_hbm, out_hbm,
                  h_v, idx_v, gate_v, out_v, ub0, ub1, vb0, vb1, su0, su1, sv0, sv1, *, tokens_per_worker):
    nh, k, tb, lanes = PK_HEADS, PK_TOPK, SC_TOKEN_BLOCK, SC_LANES
    wid = lax.axis_index("s") * SC_CORES + lax.axis_index("c")
    ubufs, vbufs, sus, svs = (ub0, ub1), (vb0, vb1), (su0, su1), (sv0, sv1)
    lane = lax.iota(jnp.int32, lanes)
    zero = jnp.zeros((lanes,), F32)

    def start(tok, hd, par):
        irow = idx_v.at[tok * nh + hd]
        pltpu.async_copy(u_hbm.at[irow], ubufs[par], sus[par])
        pltpu.async_copy(v_hbm.at[irow], vbufs[par], svs[par])

    def wait(par):
        irow = idx_v.at[0]
        pltpu.make_async_copy(u_hbm.at[irow], ubufs[par], sus[par]).wait()
        pltpu.make_async_copy(v_hbm.at[irow], vbufs[par], svs[par]).wait()

    @pl.loop(0, tokens_per_worker // tb)
    def _(blk):
        base = wid * tokens_per_worker + blk * tb
        pltpu.sync_copy(h_hbm.at[pl.ds(base, tb)], h_v)
        pltpu.sync_copy(idx_hbm.at[pl.ds(base * nh, tb * nh)], idx_v)
        pltpu.sync_copy(gate_hbm.at[pl.ds(base * nh, tb * nh)], gate_v)
        start(0, 0, 0)

        @pl.loop(0, tb)
        def _(tok):
            for hd in range(nh):
                par = hd % 2
                if hd + 1 < nh:
                    start(tok, hd + 1, 1 - par)
                else:
                    @pl.when(tok + 1 < tb)
                    def _():
                        start(tok + 1, 0, 1 - par)
                wait(par)
                ub, vb = ubufs[par], vbufs[par]

                @plsc.parallel_loop(0, SC_ROW_CHUNKS, carry=(zero,) * k)
                def accs(c, acc):
                    hc = h_v[tok, pl.ds(c * lanes, lanes)]
                    return tuple(acc[r] + hc * ub[r, pl.ds(c * lanes, lanes)] for r in range(k))

                s_vec = zero
                for r in range(k):
                    s_vec = jnp.where(lane == r, jnp.sum(accs[r]), s_vec)
                w_vec = gate_v[tok * nh + hd, :] * _gelu_tanh(s_vec)
                wb = [jnp.sum(jnp.where(lane == r, w_vec, 0.0)) for r in range(k)]

                @plsc.parallel_loop(0, SC_ROW_CHUNKS)
                def _(c):
                    ps = [wb[r] * vb[r, pl.ds(c * lanes, lanes)] for r in range(k)]
                    while len(ps) > 1:
                        ps = [ps[i] + ps[i + 1] for i in range(0, len(ps), 2)]
                    if hd == 0:
                        out_v[tok, pl.ds(c * lanes, lanes)] = ps[0]
                    else:
                        out_v[tok, pl.ds(c * lanes, lanes)] = out_v[tok, pl.ds(c * lanes, lanes)] + ps[0]

        pltpu.sync_copy(out_v, out_hbm.at[pl.ds(base, tb)])


def _peer_experts(h, eidx, gate, u_tab, v_tab):
    t, d = h.shape
    assert t % (SC_WORKERS * SC_TOKEN_BLOCK) == 0 and d == D_MODEL
    rows = pltpu.VMEM((PK_TOPK, d), F32)
    return pl.kernel(
        functools.partial(_peer_sc_body, tokens_per_worker=t // SC_WORKERS),
        out_type=jax.ShapeDtypeStruct((t, d), F32),
        mesh=plsc.VectorSubcoreMesh(core_axis_name="c", subcore_axis_name="s"),
        scratch_types=[
            pltpu.VMEM((SC_TOKEN_BLOCK, d), F32),
            pltpu.VMEM((SC_TOKEN_BLOCK * PK_HEADS, PK_TOPK), jnp.int32),
            pltpu.VMEM((SC_TOKEN_BLOCK * PK_HEADS, PK_TOPK), F32),
            pltpu.VMEM((SC_TOKEN_BLOCK, d), F32),
            rows, rows, rows, rows,
            pltpu.SemaphoreType.DMA, pltpu.SemaphoreType.DMA,
            pltpu.SemaphoreType.DMA, pltpu.SemaphoreType.DMA,
        ],
        compiler_params=pltpu.CompilerParams(needs_layout_passes=False),
        name="peer_experts_sc",
    )(h, eidx, gate, u_tab, v_tab)


TM_FIN = 512


def _final_body(x1_ref, p_ref, m_ref, w_ref, y_ref):
    y_ref[...] = _rms(x1_ref[...] + m_ref[0, 5:6, :] * p_ref[...]) * w_ref[...]


def _final(x1, peer, mod, norm_f, row_of):
    t, d = x1.shape
    tile = pl.BlockSpec((TM_FIN, d), lambda i: (i, 0))
    return pl.pallas_call(
        _final_body,
        out_shape=jax.ShapeDtypeStruct((t, d), F32),
        grid=(t // TM_FIN,),
        in_specs=[tile, tile, pl.BlockSpec((1, N_MOD, d), lambda i: (row_of(i), 0, 0)),
                  pl.BlockSpec((1, d), lambda i: (0, 0))],
        out_specs=tile,
        compiler_params=_cparams("arbitrary"),
        name="final_norm",
    )(x1, peer, mod, norm_f.reshape(1, d))


CTX_GROUP = 16
LAT_GROUP = 2


def _group(x, mod, row0, per_seq, n, hg_s0, lru_s0, col_major, p):
    seqs = hg_s0.shape[0]
    t = seqs * n
    rows_of = lambda tm: _mod_row_map(tm, n, row0, per_seq)
    z = _in_proj(x, mod, p['norm1'], p['w_in'], rows_of(TM_IN))
    o_hg, hg_fin = _hgrn2(z, p['lb'], hg_s0, n, 0)
    if col_major:
        rows = n // GRID_W
        xr = (z[:, COL_XR * LRU_W:(COL_XR + 1) * LRU_W]
              .reshape(seqs, rows, GRID_W, LRU_W).transpose(0, 2, 1, 3).reshape(t, LRU_W))
        hr, lru_fin = _rglru(xr, 0, 0, n, seqs, *p['lru'], lru_s0)
        hr = hr.reshape(seqs, GRID_W, rows, LRU_W).transpose(0, 2, 1, 3).reshape(t, LRU_W)
    else:
        hr, lru_fin = _rglru(z, COL_XR, 0, n, seqs, *p['lru'], lru_s0)
    x1, h2, q = _out_proj(x, o_hg, hr, z, mod, p['hg_norm'], p['norm2'], p['w_a'], p['w_b'], p['w_o'], p['w_q'],
                          rows_of(TM_OUT))
    eidx, gate = _pk_topk(q, p['keys'])
    peer = _peer_experts(h2, eidx.reshape(t * PK_HEADS, PK_TOPK), gate.reshape(t * PK_HEADS, PK_TOPK),
                         p['pk_u'], p['pk_v'])
    return _final(x1, peer, mod, p['norm_f'], rows_of(TM_FIN)), hg_fin, lru_fin


def kernel(x_prompt, x_sample, state_hgrn, state_rglru, c, c_ctx, w_mod, b_mod, norm1, w_in,
           hg_lb_logits, hg_norm, lru_conv_w, lru_conv_b, lru_w_r, lru_b_r, lru_w_i, lru_b_i,
           lru_lambda, w_branch_a, w_branch_b, w_out, norm2, pk_w_q, pk_sub_keys, pk_u, pk_v,
           norm_f):
    assert w_mod.shape[0] == 1, "single trunk layer"
    d = D_MODEL
    nb_ctx, n_ctx, _ = x_prompt.shape
    nb_lat, n_lat, _ = x_sample.shape
    assert nb_lat < MOD_ROWS and nb_lat % LAT_GROUP == 0 and nb_ctx % CTX_GROUP == 0
    ctx_row = nb_lat

    cond = jnp.zeros((MOD_ROWS, d), F32).at[:nb_lat].set(c).at[ctx_row].set(c_ctx)
    mod = _modulation(cond, w_mod[0], b_mod[0])
    p = {
        'lb': jnp.cumsum(jax.nn.softmax(hg_lb_logits.astype(F32), axis=1), axis=1)[:, 0],
        'norm1': norm1[0], 'w_in': w_in[0].astype(BF16), 'hg_norm': hg_norm[0], 'norm2': norm2[0],
        'lru': (lru_conv_w[0], lru_conv_b[0], _block_diag(lru_w_r[0]), _block_diag(lru_w_i[0]),
                lru_b_r[0], lru_b_i[0], lru_lambda[0]),
        'w_a': w_branch_a[0].astype(BF16), 'w_b': w_branch_b[0].astype(BF16), 'w_o': w_out[0].astype(BF16),
        'w_q': pk_w_q[0].astype(BF16), 'keys': pk_sub_keys[0].astype(BF16),
        'pk_u': pk_u[0], 'pk_v': pk_v[0], 'norm_f': norm_f,
    }
    y_ctx, hg_fin, lru_fin = [], [], []
    for s0 in range(0, nb_ctx, CTX_GROUP):
        y, hg, lr = _group(
            x_prompt[s0:s0 + CTX_GROUP].reshape(CTX_GROUP * n_ctx, d), mod, ctx_row, False, n_ctx,
            jnp.zeros((CTX_GROUP, 2, HG_HEADS, HG_DK, HG_DV), F32), jnp.zeros((CTX_GROUP, 2, LRU_W), F32), False, p)
        y_ctx.append(y.reshape(CTX_GROUP, n_ctx, d))
        hg_fin.append(hg)
        lru_fin.append(lr)
    y_lat = []
    for s0 in range(0, nb_lat, LAT_GROUP):
        sl = slice(s0, s0 + LAT_GROUP)
        y, _, _ = _group(x_sample[sl].reshape(LAT_GROUP * n_lat, d), mod, s0, True, n_lat,
                         state_hgrn[sl, 0], state_rglru[sl, 0], True, p)
        y_lat.append(y.reshape(LAT_GROUP, n_lat, d))
    return (jnp.concatenate(y_ctx, axis=0), jnp.concatenate(y_lat, axis=0),
            jnp.concatenate(hg_fin, axis=0)[:, None], jnp.concatenate(lru_fin, axis=0)[:, None])
```

```python
import functools

import jax, jax.numpy as jnp
from jax import lax
from jax.experimental import pallas as pl
from jax.experimental.pallas import tpu as pltpu
from jax.experimental.pallas import tpu_sc as plsc

D_MODEL = 1024
GRID_W = 64
EPS = 1e-6
HG_HEADS = 4
HG_DK = 128
HG_DV = 128
HG_KW = HG_HEADS * HG_DK
HG_VW = HG_HEADS * HG_DV
HG_CHUNK = 32
LRU_W = D_MODEL // 2
LRU_BLOCKS = 8
LRU_BW = LRU_W // LRU_BLOCKS
CONV_W = 4
LRU_C = 8.0
PK_HEADS = 8
N_KEYS = 128
PK_TOPK = 16
PK_DQ = 256
PK_DH = PK_DQ // 2
IN_W = 3 * HG_KW + 2 * HG_VW + 2 * LRU_W + 2 * D_MODEL
COL_G, COL_XR, COL_XG, COL_GA, COL_GB = 4, 5, 6, 7, 9
N_MOD = 6
MOD_ROWS = 16
F32 = jnp.float32
BF16 = jnp.bfloat16
VMEM_LIMIT = 48 * 1024 * 1024


def _cparams(*sem):
    return pltpu.CompilerParams(dimension_semantics=sem, vmem_limit_bytes=VMEM_LIMIT)


def _silu(x):
    return x * jax.nn.sigmoid(x)


def _rms(x):
    return x * lax.rsqrt(jnp.mean(x * x, axis=-1, keepdims=True) + EPS)


def _mod_body(c_ref, w_ref, b_ref, o_ref):
    o_ref[...] = lax.dot_general(_silu(c_ref[...]), w_ref[...], (((1,), (0,)), ((), ())),
                                 precision=lax.Precision.HIGHEST, preferred_element_type=F32) + b_ref[...]


def _modulation(cond, w_mod, b_mod):
    d = D_MODEL
    out = pl.pallas_call(
        _mod_body,
        out_shape=jax.ShapeDtypeStruct((MOD_ROWS, N_MOD * d), F32),
        grid=(N_MOD,),
        in_specs=[pl.BlockSpec((MOD_ROWS, d), lambda j: (0, 0)),
                  pl.BlockSpec((d, d), lambda j: (0, j)),
                  pl.BlockSpec((1, d), lambda j: (0, j))],
        out_specs=pl.BlockSpec((MOD_ROWS, d), lambda j: (0, j)),
        compiler_params=_cparams("arbitrary"),
        name="adaln_modulation",
    )(cond, w_mod, b_mod.reshape(1, N_MOD * d))
    return out.reshape(MOD_ROWS, N_MOD, d)


def _mod_row_map(tm, n, row0, per_seq):
    per = n // tm
    return (lambda i: row0 + i // per) if per_seq else (lambda i: row0)


TM_IN = 512
TN_IN = IN_W // 2


def _in_body(x_ref, m_ref, n1_ref, w_ref, z_ref, h_scr):
    @pl.when(pl.program_id(1) == 0)
    def _():
        y = _rms(x_ref[...]) * n1_ref[...]
        h_scr[...] = (y * (1.0 + m_ref[0, 1:2, :]) + m_ref[0, 0:1, :]).astype(BF16)

    z_ref[...] = jnp.dot(h_scr[...], w_ref[...], preferred_element_type=F32)


def _in_proj(x, mod, norm1, w_in_bf, row_of):
    t, d = x.shape
    return pl.pallas_call(
        _in_body,
        out_shape=jax.ShapeDtypeStruct((t, IN_W), F32),
        grid=(t // TM_IN, IN_W // TN_IN),
        in_specs=[pl.BlockSpec((TM_IN, d), lambda i, j: (i, 0)),
                  pl.BlockSpec((1, N_MOD, d), lambda i, j: (row_of(i), 0, 0)),
                  pl.BlockSpec((1, d), lambda i, j: (0, 0)),
                  pl.BlockSpec((d, TN_IN), lambda i, j: (0, j))],
        out_specs=pl.BlockSpec((TM_IN, TN_IN), lambda i, j: (i, j)),
        scratch_shapes=[pltpu.VMEM((TM_IN, d), BF16)],
        compiler_params=_cparams("arbitrary", "arbitrary"),
        name="in_proj",
    )(x, mod, norm1.reshape(1, d), w_in_bf)


def _hg_body(q_ref, ff_ref, fb_ref, v_ref, lb_ref, s0_ref, o_ref, sfin_ref, st_scr, ob_scr, *, n):
    c = HG_CHUNK
    nc = n // c
    row = lax.broadcasted_iota(jnp.int32, (c, c), 0)
    col = lax.broadcasted_iota(jnp.int32, (c, c), 1)
    lower = row >= col
    tri = (lower.astype(F32), (row <= col).astype(F32))
    masks = (lower, row <= col)
    f_refs = (ff_ref, fb_ref)
    for d in range(2):
        st_scr[d] = s0_ref[0, d, 0].T

    def chunk(ci, d):
        r = pl.ds(pl.multiple_of(ci * c, c), c)
        q = _silu(q_ref[r, :])
        v = v_ref[r, :].astype(BF16)
        lb = lb_ref[d:d + 1, :]
        f = lb + (1.0 - lb) * jax.nn.sigmoid(f_refs[d][r, :])
        k = 1.0 - f
        lf = jnp.log(f)
        cum = lax.dot_general(tri[d], lf, (((1,), (0,)), ((), ())),
                              precision=lax.Precision.HIGHEST, preferred_element_type=F32)
        tot = jnp.sum(lf, axis=0, keepdims=True)
        q_dec = (q * jnp.exp(cum)).astype(BF16)
        k_inv = (k * jnp.exp(-cum)).astype(BF16)
        k_end = (k * jnp.exp(tot - cum)).astype(BF16)
        att = lax.dot_general(q_dec, k_inv, (((1,), (1,)), ((), ())), preferred_element_type=F32)
        att = jnp.where(masks[d], att, 0.0).astype(BF16)
        st = st_scr[d]
        o = (jnp.dot(att, v, preferred_element_type=F32)
             + lax.dot_general(q_dec, st.astype(BF16), (((1,), (1,)), ((), ())), preferred_element_type=F32))
        ds_t = lax.dot_general(v, k_end, (((0,), (0,)), ((), ())), preferred_element_type=F32)
        st_scr[d] = st * jnp.exp(tot) + ds_t
        return r, o

    def step(i, carry):
        r, o = chunk(i, 0)
        o_ref[r, :] = o
        r, o = chunk(nc - 1 - i, 1)
        ob_scr[r, :] = o
        return carry

    lax.fori_loop(0, nc, step, 0)
    o_ref[...] = o_ref[...] + ob_scr[...]
    for d in range(2):
        sfin_ref[0, d, 0] = st_scr[d].T


def _hgrn2(z, lb, s0, n, row_block0):
    seqs = s0.shape[0]
    hb = HG_KW // HG_DK

    def zspec(col0):
        return pl.BlockSpec((n, HG_DK), lambda s, h: (row_block0 + s, col0 + h))

    st_spec = pl.BlockSpec((1, 2, 1, HG_DK, HG_DV), lambda s, h: (s, 0, h, 0, 0))
    return pl.pallas_call(
        functools.partial(_hg_body, n=n),
        out_shape=(jax.ShapeDtypeStruct((seqs * n, HG_VW), F32),
                   jax.ShapeDtypeStruct((seqs, 2, HG_HEADS, HG_DK, HG_DV), F32)),
        grid=(seqs, HG_HEADS),
        in_specs=[zspec(0), zspec(hb), zspec(2 * hb), zspec(3 * hb),
                  pl.BlockSpec((2, HG_DK), lambda s, h: (0, h)), st_spec],
        out_specs=(pl.BlockSpec((n, HG_DV), lambda s, h: (s, h)), st_spec),
        scratch_shapes=[pltpu.VMEM((2, HG_DV, HG_DK), F32), pltpu.VMEM((n, HG_DV), F32)],
        compiler_params=_cparams("arbitrary", "arbitrary"),
        name="hgrn2_scan",
    )(z, z, z, z, lb, s0)


LRU_RB = 64
HALO = 8


def _lru_body(x_ref, cw_ref, cb_ref, wr_ref, wi_ref, br_ref, bi_ref, lam_ref, h0_ref, hr_ref, e_ref,
              xp_scr, xc_scr, *, n):
    rb = LRU_RB
    nb = n // rb
    w = LRU_W
    zeros = jnp.zeros((HALO, w), F32)
    xp_scr[0:HALO, :] = zeros
    xp_scr[HALO + n:HALO + n + HALO, :] = zeros
    xp_scr[HALO:HALO + n, :] = x_ref[...]
    cw = cw_ref[...]
    cb = cb_ref[...]

    def conv_blk(b, carry):
        base = pl.multiple_of(b * rb, rb)
        xh = xp_scr[pl.ds(base, rb + 2 * HALO), :]
        ext = rb + 2 * HALO
        acc = cb + cw[2:3, :] * xh[HALO:HALO + rb]
        acc = acc + cw[0:1, :] * pltpu.roll(xh, 2, axis=0)[HALO:HALO + rb]
        acc = acc + cw[1:2, :] * pltpu.roll(xh, 1, axis=0)[HALO:HALO + rb]
        acc = acc + cw[3:4, :] * pltpu.roll(xh, ext - 1, axis=0)[HALO:HALO + rb]
        xc_scr[pl.ds(base, rb), :] = acc
        return carry

    lax.fori_loop(0, nb, conv_blk, 0)

    rows = lax.broadcasted_iota(jnp.int32, (rb, w), 0)

    def gates(blk, d):
        r = pl.ds(pl.multiple_of(blk * rb, rb), rb)
        xc = xc_scr[r, :]
        xb = xc.astype(BF16)
        rg = jax.nn.sigmoid(jnp.dot(xb, wr_ref[d], preferred_element_type=F32) + br_ref[d:d + 1, :])
        ig = jax.nn.sigmoid(jnp.dot(xb, wi_ref[d], preferred_element_type=F32) + bi_ref[d:d + 1, :])
        log_a = (-LRU_C) * jax.nn.softplus(-lam_ref[d:d + 1, :]) * rg
        a = jnp.exp(log_a)
        u = jnp.sqrt(-jnp.tanh(log_a) * (a * a + 1.0)) * (ig * xc)
        return r, a, u

    def fwd_blk(blk, h_prev):
        r, a, u = gates(blk, 0)
        s = 1
        while s < rb:
            keep = rows >= s
            a_sh = jnp.where(keep, pltpu.roll(a, s, axis=0), 1.0)
            u_sh = jnp.where(keep, pltpu.roll(u, s, axis=0), 0.0)
            u = a * u_sh + u
            a = a * a_sh
            s *= 2
        h = u + a * h_prev
        hr_ref[r, :] = h
        return h[rb - 1:rb, :]

    def bwd_blk(i, h_next):
        r, a, u = gates(nb - 1 - i, 1)
        s = 1
        while s < rb:
            keep = rows < rb - s
            a_sh = jnp.where(keep, pltpu.roll(a, rb - s, axis=0), 1.0)
            u_sh = jnp.where(keep, pltpu.roll(u, rb - s, axis=0), 0.0)
            u = a * u_sh + u
            a = a * a_sh
            s *= 2
        h = u + a * h_next
        hr_ref[r, :] = hr_ref[r, :] + h
        return h[0:1, :]

    e_ref[0, 0:1, :] = lax.fori_loop(0, nb, fwd_blk, h0_ref[0, 0:1, :])
    e_ref[0, 1:2, :] = lax.fori_loop(0, nb, bwd_blk, h0_ref[0, 1:2, :])


def _rglru(x, col_block, row_block0, n, seqs, conv_w, conv_b, wr_bd, wi_bd, b_r, b_i, lam, h0):
    w = LRU_W
    full2 = lambda s: (0, 0)
    full3 = lambda s: (0, 0, 0)
    return pl.pallas_call(
        functools.partial(_lru_body, n=n),
        out_shape=(jax.ShapeDtypeStruct((seqs * n, w), F32), jax.ShapeDtypeStruct((seqs, 2, w), F32)),
        grid=(seqs,),
        in_specs=[pl.BlockSpec((n, w), lambda s: (row_block0 + s, col_block)),
                  pl.BlockSpec((CONV_W, w), full2), pl.BlockSpec((1, w), full2),
                  pl.BlockSpec((2, w, w), full3), pl.BlockSpec((2, w, w), full3),
                  pl.BlockSpec((2, w), full2), pl.BlockSpec((2, w), full2), pl.BlockSpec((2, w), full2),
                  pl.BlockSpec((1, 2, w), lambda s: (s, 0, 0))],
        out_specs=(pl.BlockSpec((n, w), lambda s: (s, 0)), pl.BlockSpec((1, 2, w), lambda s: (s, 0, 0))),
        scratch_shapes=[pltpu.VMEM((n + 2 * HALO, w), F32), pltpu.VMEM((n, w), F32)],
        compiler_params=_cparams("arbitrary"),
        name="rglru",
    )(x, conv_w, conv_b.reshape(1, w), wr_bd, wi_bd, b_r, b_i, lam, h0)


def _block_diag(wg):
    eye = jnp.eye(LRU_BLOCKS, dtype=wg.dtype)
    dense = wg[:, :, :, None, :] * eye[None, :, None, :, None]
    return dense.reshape(2, LRU_W, LRU_W).astype(BF16)


TM_OUT = 256


def _out_body(x_ref, o_ref, hr_ref, g_ref, xg_ref, ga0_ref, ga1_ref, gb0_ref, gb1_ref, m_ref,
              hgn_ref, n2_ref, wa_ref, wb_ref, wo_ref, wq_ref, x1_ref, h2_ref, q_ref):
    o = o_ref[...]
    parts = []
    for h in range(HG_HEADS):
        parts.append(_rms(o[:, h * HG_DV:(h + 1) * HG_DV]) * hgn_ref[...])
    on = jnp.concatenate(parts, axis=1) * _silu(g_ref[...])
    y_a = jnp.dot(on.astype(BF16), wa_ref[...], preferred_element_type=F32)
    y_b = jnp.dot((hr_ref[...] * jax.nn.gelu(xg_ref[...])).astype(BF16), wb_ref[...], preferred_element_type=F32)
    ga = jnp.concatenate([ga0_ref[...], ga1_ref[...]], axis=1)
    gb = jnp.concatenate([gb0_ref[...], gb1_ref[...]], axis=1)
    merged = jax.nn.sigmoid(ga) * y_a + jax.nn.sigmoid(gb) * y_b
    mix = jnp.dot(merged.astype(BF16), wo_ref[...], preferred_element_type=F32)
    x1 = x_ref[...] + m_ref[0, 2:3, :] * mix
    x1_ref[...] = x1
    h2 = _rms(x1) * n2_ref[...] * (1.0 + m_ref[0, 4:5, :]) + m_ref[0, 3:4, :]
    h2_ref[...] = h2
    q_ref[...] = jnp.dot(h2.astype(BF16), wq_ref[...], preferred_element_type=F32).astype(BF16)


def _out_proj(x, o_hg, hr, z, mod, hg_norm, norm2, wa, wb, wo, wq, row_of):
    t, d = x.shape
    tm = TM_OUT
    nq = PK_HEADS * PK_DQ
    half = lambda i: (i, 0)
    zc = lambda c: pl.BlockSpec((tm, LRU_W), lambda i: (i, c))
    const = lambda i: (0, 0)
    return pl.pallas_call(
        _out_body,
        out_shape=(jax.ShapeDtypeStruct((t, d), F32), jax.ShapeDtypeStruct((t, d), F32),
                   jax.ShapeDtypeStruct((t, nq), BF16)),
        grid=(t // tm,),
        in_specs=[pl.BlockSpec((tm, d), half), pl.BlockSpec((tm, HG_VW), half), pl.BlockSpec((tm, LRU_W), half),
                  zc(COL_G), zc(COL_XG), zc(COL_GA), zc(COL_GA + 1), zc(COL_GB), zc(COL_GB + 1),
                  pl.BlockSpec((1, N_MOD, d), lambda i: (row_of(i), 0, 0)),
                  pl.BlockSpec((1, HG_DV), const), pl.BlockSpec((1, d), const),
                  pl.BlockSpec((HG_VW, d), const), pl.BlockSpec((LRU_W, d), const),
                  pl.BlockSpec((d, d), const), pl.BlockSpec((d, nq), const)],
        out_specs=(pl.BlockSpec((tm, d), half), pl.BlockSpec((tm, d), half), pl.BlockSpec((tm, nq), half)),
        compiler_params=_cparams("arbitrary"),
        name="out_proj",
    )(x, o_hg, hr, z, z, z, z, z, z, mod, hg_norm.reshape(1, HG_DV), norm2.reshape(1, d), wa, wb, wo, wq)


TT_TOPK = 128


def _topk_rows(s, k, payload=None):
    r = s.shape[0]
    rid = lax.broadcasted_iota(jnp.int32, s.shape, 0)
    vals, picks = [], []
    for _ in range(k):
        m = jnp.max(s, axis=0, keepdims=True)
        idx = jnp.min(jnp.where(s == m, rid, r), axis=0, keepdims=True)
        hit = rid == idx
        vals.append(m)
        picks.append(idx if payload is None else jnp.sum(jnp.where(hit, payload, 0), axis=0, keepdims=True))
        s = jnp.where(hit, -jnp.inf, s)
    return jnp.concatenate(vals, axis=0), jnp.concatenate(picks, axis=0)


def _topk_body(q_ref, keys_ref, eidx_ref, gate_ref):
    e_rows, g_rows = [], []
    for h in range(PK_HEADS):
        sv, si = [], []
        for p in range(2):
            c0 = h * PK_DQ + p * PK_DH
            s = lax.dot_general(keys_ref[p, h], q_ref[:, c0:c0 + PK_DH], (((1,), (1,)), ((), ())),
                                preferred_element_type=F32)
            v, i = _topk_rows(s, PK_TOPK)
            sv.append(v)
            si.append(i)
        cand = jnp.concatenate([sv[0][i:i + 1, :] + sv[1] for i in range(PK_TOPK)], axis=0)
        cidx = jnp.concatenate([si[0][i:i + 1, :] * N_KEYS + si[1] for i in range(PK_TOPK)], axis=0)
        best, eid = _topk_rows(cand, PK_TOPK, payload=cidx)
        ex = jnp.exp(best - best[0:1, :])
        g_rows.append(ex / jnp.sum(ex, axis=0, keepdims=True))
        e_rows.append(eid)
    eidx_ref[...] = jnp.concatenate(e_rows, axis=0).T
    gate_ref[...] = jnp.concatenate(g_rows, axis=0).T


def _pk_topk(q, keys_bf):
    t = q.shape[0]
    tt = TT_TOPK
    ne = PK_HEADS * PK_TOPK
    return pl.pallas_call(
        _topk_body,
        out_shape=(jax.ShapeDtypeStruct((t, ne), jnp.int32), jax.ShapeDtypeStruct((t, ne), F32)),
        grid=(t // tt,),
        in_specs=[pl.BlockSpec((tt, PK_HEADS * PK_DQ), lambda i: (i, 0)),
                  pl.BlockSpec((2, PK_HEADS, N_KEYS, PK_DH), lambda i: (0, 0, 0, 0))],
        out_specs=(pl.BlockSpec((tt, ne), lambda i: (i, 0)), pl.BlockSpec((tt, ne), lambda i: (i, 0))),
        compiler_params=_cparams("arbitrary"),
        name="pk_topk",
    )(q, keys_bf)


SC_LANES = 16
SC_CORES = 2
SC_SUBCORES = 16
SC_WORKERS = SC_CORES * SC_SUBCORES
SC_HALF = D_MODEL // 2
SC_ROW_CHUNKS = SC_HALF // SC_LANES
SC_TOKEN_BLOCK = 8
HI_MASK = -65536


def _pack_bf16_pairs(tab):
    b = lax.bitcast_convert_type(tab.astype(BF16), jnp.uint16).astype(jnp.uint32)
    return lax.bitcast_convert_type((b[:, :SC_HALF] << 16) | b[:, SC_HALF:], jnp.int32)


def _unpack_pair(w):
    return (lax.bitcast_convert_type(w & jnp.int32(HI_MASK), F32), lax.bitcast_convert_type(w << 16, F32))


def _gelu_tanh(x):
    y = 0.7978845608028654 * (x + 0.044715 * (x * x * x))
    t = 1.0 - 2.0 / (jnp.exp(2.0 * y) + 1.0)
    return x * (0.5 * (1.0 + t))


def _peer_sc_body(h_hbm, idx_hbm, gate_hbm, u_hbm, v_hbm, out_hbm,
                  h_v, idx_v, gate_v, out_v, ub0, ub1, vb0, vb1, su0, su1, sv0, sv1, *, tokens_per_worker):
    nh, k, tb, lanes = PK_HEADS, PK_TOPK, SC_TOKEN_BLOCK, SC_LANES
    wid = lax.axis_index("s") * SC_CORES + lax.axis_index("c")
    ubufs, vbufs, sus, svs = (ub0, ub1), (vb0, vb1), (su0, su1), (sv0, sv1)
    lane = lax.iota(jnp.int32, lanes)
    zero = jnp.zeros((lanes,), F32)

    def start(tok, hd, par):
        irow = idx_v.at[tok * nh + hd]
        pltpu.async_copy(u_hbm.at[irow], ubufs[par], sus[par])
        pltpu.async_copy(v_hbm.at[irow], vbufs[par], svs[par])

    def wait(par):
        irow = idx_v.at[0]
        pltpu.make_async_copy(u_hbm.at[irow], ubufs[par], sus[par]).wait()
        pltpu.make_async_copy(v_hbm.at[irow], vbufs[par], svs[par]).wait()

    @pl.loop(0, tokens_per_worker // tb)
    def _(blk):
        base = wid * tokens_per_worker + blk * tb
        pltpu.sync_copy(h_hbm.at[pl.ds(base, tb)], h_v)
        pltpu.sync_copy(idx_hbm.at[pl.ds(base * nh, tb * nh)], idx_v)
        pltpu.sync_copy(gate_hbm.at[pl.ds(base * nh, tb * nh)], gate_v)
        start(0, 0, 0)

        @pl.loop(0, tb)
        def _(tok):
            for hd in range(nh):
                par = hd % 2
                if hd + 1 < nh:
                    start(tok, hd + 1, 1 - par)
                else:
                    @pl.when(tok + 1 < tb)
                    def _():
                        start(tok + 1, 0, 1 - par)
                wait(par)
                ub, vb = ubufs[par], vbufs[par]

                @plsc.parallel_loop(0, SC_ROW_CHUNKS, carry=(zero,) * k)
                def accs(c, acc):
                    ha = h_v[tok, pl.ds(c * lanes, lanes)]
                    hb = h_v[tok, pl.ds(SC_HALF + c * lanes, lanes)]
                    out = []
                    for r in range(k):
                        a, b = _unpack_pair(ub[r, pl.ds(c * lanes, lanes)])
                        out.append(acc[r] + (ha * a + hb * b))
                    return tuple(out)

                s_vec = zero
                for r in range(k):
                    s_vec = jnp.where(lane == r, jnp.sum(accs[r]), s_vec)
                w_vec = gate_v[tok * nh + hd, :] * _gelu_tanh(s_vec)
                wb = [jnp.sum(jnp.where(lane == r, w_vec, 0.0)) for r in range(k)]

                @plsc.parallel_loop(0, SC_ROW_CHUNKS)
                def _(c):
                    pa, pb = [], []
                    for r in range(k):
                        a, b = _unpack_pair(vb[r, pl.ds(c * lanes, lanes)])
                        pa.append(wb[r] * a)
                        pb.append(wb[r] * b)
                    while len(pa) > 1:
                        pa = [pa[i] + pa[i + 1] for i in range(0, len(pa), 2)]
                        pb = [pb[i] + pb[i + 1] for i in range(0, len(pb), 2)]
                    for off, o in ((0, pa[0]), (SC_HALF, pb[0])):
                        cols = pl.ds(off + c * lanes, lanes)
                        if hd == 0:
                            out_v[tok, cols] = o
                        else:
                            out_v[tok, cols] = out_v[tok, cols] + o

        pltpu.sync_copy(out_v, out_hbm.at[pl.ds(base, tb)])


def _peer_experts(h, eidx, gate, u_tab, v_tab):
    t, d = h.shape
    assert t % (SC_WORKERS * SC_TOKEN_BLOCK) == 0 and d == D_MODEL
    rows = pltpu.VMEM((PK_TOPK, SC_HALF), jnp.int32)
    return pl.kernel(
        functools.partial(_peer_sc_body, tokens_per_worker=t // SC_WORKERS),
        out_type=jax.ShapeDtypeStruct((t, d), F32),
        mesh=plsc.VectorSubcoreMesh(core_axis_name="c", subcore_axis_name="s"),
        scratch_types=[
            pltpu.VMEM((SC_TOKEN_BLOCK, d), F32),
            pltpu.VMEM((SC_TOKEN_BLOCK * PK_HEADS, PK_TOPK), jnp.int32),
            pltpu.VMEM((SC_TOKEN_BLOCK * PK_HEADS, PK_TOPK), F32),
            pltpu.VMEM((SC_TOKEN_BLOCK, d), F32),
            rows, rows, rows, rows,
            pltpu.SemaphoreType.DMA, pltpu.SemaphoreType.DMA,
            pltpu.SemaphoreType.DMA, pltpu.SemaphoreType.DMA,
        ],
        compiler_params=pltpu.CompilerParams(needs_layout_passes=False),
        name="peer_experts_sc",
    )(h, eidx, gate, u_tab, v_tab)


TM_FIN = 512


def _final_body(x1_ref, p_ref, m_ref, w_ref, y_ref):
    y_ref[...] = _rms(x1_ref[...] + m_ref[0, 5:6, :] * p_ref[...]) * w_ref[...]


def _final(x1, peer, mod, norm_f, row_of):
    t, d = x1.shape
    tile = pl.BlockSpec((TM_FIN, d), lambda i: (i, 0))
    return pl.pallas_call(
        _final_body,
        out_shape=jax.ShapeDtypeStruct((t, d), F32),
        grid=(t // TM_FIN,),
        in_specs=[tile, tile, pl.BlockSpec((1, N_MOD, d), lambda i: (row_of(i), 0, 0)),
                  pl.BlockSpec((1, d), lambda i: (0, 0))],
        out_specs=tile,
        compiler_params=_cparams("arbitrary"),
        name="final_norm",
    )(x1, peer, mod, norm_f.reshape(1, d))


CTX_GROUP = 16
LAT_GROUP = 2


def _group(x, mod, row0, per_seq, n, hg_s0, lru_s0, col_major, p):
    seqs = hg_s0.shape[0]
    t = seqs * n
    rows_of = lambda tm: _mod_row_map(tm, n, row0, per_seq)
    z = _in_proj(x, mod, p['norm1'], p['w_in'], rows_of(TM_IN))
    o_hg, hg_fin = _hgrn2(z, p['lb'], hg_s0, n, 0)
    if col_major:
        rows = n // GRID_W
        xr = (z[:, COL_XR * LRU_W:(COL_XR + 1) * LRU_W]
              .reshape(seqs, rows, GRID_W, LRU_W).transpose(0, 2, 1, 3).reshape(t, LRU_W))
        hr, lru_fin = _rglru(xr, 0, 0, n, seqs, *p['lru'], lru_s0)
        hr = hr.reshape(seqs, GRID_W, rows, LRU_W).transpose(0, 2, 1, 3).reshape(t, LRU_W)
    else:
        hr, lru_fin = _rglru(z, COL_XR, 0, n, seqs, *p['lru'], lru_s0)
    x1, h2, q = _out_proj(x, o_hg, hr, z, mod, p['hg_norm'], p['norm2'], p['w_a'], p['w_b'], p['w_o'], p['w_q'],
                          rows_of(TM_OUT))
    eidx, gate = _pk_topk(q, p['keys'])
    peer = _peer_experts(h2, eidx.reshape(t * PK_HEADS, PK_TOPK), gate.reshape(t * PK_HEADS, PK_TOPK),
                         p['pk_u'], p['pk_v'])
    return _final(x1, peer, mod, p['norm_f'], rows_of(TM_FIN)), hg_fin, lru_fin


def kernel(x_prompt, x_sample, state_hgrn, state_rglru, c, c_ctx, w_mod, b_mod, norm1, w_in,
           hg_lb_logits, hg_norm, lru_conv_w, lru_conv_b, lru_w_r, lru_b_r, lru_w_i, lru_b_i,
           lru_lambda, w_branch_a, w_branch_b, w_out, norm2, pk_w_q, pk_sub_keys, pk_u, pk_v,
           norm_f):
    assert w_mod.shape[0] == 1, "single trunk layer"
    d = D_MODEL
    nb_ctx, n_ctx, _ = x_prompt.shape
    nb_lat, n_lat, _ = x_sample.shape
    assert nb_lat < MOD_ROWS and nb_lat % LAT_GROUP == 0 and nb_ctx % CTX_GROUP == 0
    ctx_row = nb_lat

    cond = jnp.zeros((MOD_ROWS, d), F32).at[:nb_lat].set(c).at[ctx_row].set(c_ctx)
    mod = _modulation(cond, w_mod[0], b_mod[0])
    p = {
        'lb': jnp.cumsum(jax.nn.softmax(hg_lb_logits.astype(F32), axis=1), axis=1)[:, 0],
        'norm1': norm1[0], 'w_in': w_in[0].astype(BF16), 'hg_norm': hg_norm[0], 'norm2': norm2[0],
        'lru': (lru_conv_w[0], lru_conv_b[0], _block_diag(lru_w_r[0]), _block_diag(lru_w_i[0]),
                lru_b_r[0], lru_b_i[0], lru_lambda[0]),
        'w_a': w_branch_a[0].astype(BF16), 'w_b': w_branch_b[0].astype(BF16), 'w_o': w_out[0].astype(BF16),
        'w_q': pk_w_q[0].astype(BF16), 'keys': pk_sub_keys[0].astype(BF16),
        'pk_u': _pack_bf16_pairs(pk_u[0]), 'pk_v': _pack_bf16_pairs(pk_v[0]), 'norm_f': norm_f,
    }
    y_ctx, hg_fin, lru_fin = [], [], []
    for s0 in range(0, nb_ctx, CTX_GROUP):
        y, hg, lr = _group(
            x_prompt[s0:s0 + CTX_GROUP].reshape(CTX_GROUP * n_ctx, d), mod, ctx_row, False, n_ctx,
            jnp.zeros((CTX_GROUP, 2, HG_HEADS, HG_DK, HG_DV), F32), jnp.zeros((CTX_GROUP, 2, LRU_W), F32), False, p)
        y_ctx.append(y.reshape(CTX_GROUP, n_ctx, d))
        hg_fin.append(hg)
        lru_fin.append(lr)
    y_lat = []
    for s0 in range(0, nb_lat, LAT_GROUP):
        sl = slice(s0, s0 + LAT_GROUP)
        y, _, _ = _group(x_sample[sl].reshape(LAT_GROUP * n_lat, d), mod, s0, True, n_lat,
                         state_hgrn[sl, 0], state_rglru[sl, 0], True, p)
        y_lat.append(y.reshape(LAT_GROUP, n_lat, d))
    return (jnp.concatenate(y_ctx, axis=0), jnp.concatenate(y_lat, axis=0),
            jnp.concatenate(hg_fin, axis=0)[:, None], jnp.concatenate(lru_fin, axis=0)[:, None])
```

```python
import functools

import jax, jax.numpy as jnp
from jax import lax
from jax.experimental import pallas as pl
from jax.experimental.pallas import tpu as pltpu
from jax.experimental.pallas import tpu_sc as plsc

D_MODEL = 1024
GRID_W = 64
EPS = 1e-6
HG_HEADS = 4
HG_DK = 128
HG_DV = 128
HG_KW = HG_HEADS * HG_DK
HG_VW = HG_HEADS * HG_DV
HG_CHUNK = 32
LRU_W = D_MODEL // 2
LRU_BLOCKS = 8
LRU_BW = LRU_W // LRU_BLOCKS
CONV_W = 4
LRU_C = 8.0
PK_HEADS = 8
N_KEYS = 128
PK_TOPK = 16
PK_DQ = 256
PK_DH = PK_DQ // 2
IN_W = 3 * HG_KW + 2 * HG_VW + 2 * LRU_W + 2 * D_MODEL
COL_G, COL_XR, COL_XG, COL_GA, COL_GB = 4, 5, 6, 7, 9
N_MOD = 6
MOD_ROWS = 16
F32 = jnp.float32
BF16 = jnp.bfloat16
VMEM_LIMIT = 48 * 1024 * 1024


def _cparams(*sem):
    return pltpu.CompilerParams(dimension_semantics=sem, vmem_limit_bytes=VMEM_LIMIT)


def _silu(x):
    return x * jax.nn.sigmoid(x)


def _rms(x):
    return x * lax.rsqrt(jnp.mean(x * x, axis=-1, keepdims=True) + EPS)


def _mod_body(c_ref, w_ref, b_ref, o_ref):
    o_ref[...] = lax.dot_general(_silu(c_ref[...]), w_ref[...], (((1,), (0,)), ((), ())),
                                 precision=lax.Precision.HIGHEST, preferred_element_type=F32) + b_ref[...]


def _modulation(cond, w_mod, b_mod):
    d = D_MODEL
    out = pl.pallas_call(
        _mod_body,
        out_shape=jax.ShapeDtypeStruct((MOD_ROWS, N_MOD * d), F32),
        grid=(N_MOD,),
        in_specs=[pl.BlockSpec((MOD_ROWS, d), lambda j: (0, 0)),
                  pl.BlockSpec((d, d), lambda j: (0, j)),
                  pl.BlockSpec((1, d), lambda j: (0, j))],
        out_specs=pl.BlockSpec((MOD_ROWS, d), lambda j: (0, j)),
        compiler_params=_cparams("arbitrary"),
        name="adaln_modulation",
    )(cond, w_mod, b_mod.reshape(1, N_MOD * d))
    return out.reshape(MOD_ROWS, N_MOD, d)


def _mod_row_map(tm, n, row0, per_seq):
    per = n // tm
    return (lambda i: row0 + i // per) if per_seq else (lambda i: row0)


TM_IN = 512
TN_IN = IN_W // 2


def _in_body(x_ref, m_ref, n1_ref, w_ref, z_ref, h_scr):
    @pl.when(pl.program_id(1) == 0)
    def _():
        y = _rms(x_ref[...]) * n1_ref[...]
        h_scr[...] = (y * (1.0 + m_ref[0, 1:2, :]) + m_ref[0, 0:1, :]).astype(BF16)

    z_ref[...] = jnp.dot(h_scr[...], w_ref[...], preferred_element_type=F32)


def _in_proj(x, mod, norm1, w_in_bf, row_of):
    t, d = x.shape
    return pl.pallas_call(
        _in_body,
        out_shape=jax.ShapeDtypeStruct((t, IN_W), F32),
        grid=(t // TM_IN, IN_W // TN_IN),
        in_specs=[pl.BlockSpec((TM_IN, d), lambda i, j: (i, 0)),
                  pl.BlockSpec((1, N_MOD, d), lambda i, j: (row_of(i), 0, 0)),
                  pl.BlockSpec((1, d), lambda i, j: (0, 0)),
                  pl.BlockSpec((d, TN_IN), lambda i, j: (0, j))],
        out_specs=pl.BlockSpec((TM_IN, TN_IN), lambda i, j: (i, j)),
        scratch_shapes=[pltpu.VMEM((TM_IN, d), BF16)],
        compiler_params=_cparams("arbitrary", "arbitrary"),
        name="in_proj",
    )(x, mod, norm1.reshape(1, d), w_in_bf)


def _hg_body(q_ref, ff_ref, fb_ref, v_ref, lb_ref, s0_ref, o_ref, sfin_ref, st_scr, ob_scr, *, n):
    c = HG_CHUNK
    nc = n // c
    row = lax.broadcasted_iota(jnp.int32, (c, c), 0)
    col = lax.broadcasted_iota(jnp.int32, (c, c), 1)
    lower = row >= col
    tri = (lower.astype(F32), (row <= col).astype(F32))
    masks = (lower, row <= col)
    f_refs = (ff_ref, fb_ref)
    for d in range(2):
        st_scr[d] = s0_ref[0, d, 0].T

    def chunk(ci, d):
        r = pl.ds(pl.multiple_of(ci * c, c), c)
        q = _silu(q_ref[r, :])
        v = v_ref[r, :].astype(BF16)
        lb = lb_ref[d:d + 1, :]
        f = lb + (1.0 - lb) * jax.nn.sigmoid(f_refs[d][r, :])
        k = 1.0 - f
        lf = jnp.log(f)
        cum = lax.dot_general(tri[d], lf, (((1,), (0,)), ((), ())),
                              precision=lax.Precision.HIGHEST, preferred_element_type=F32)
        tot = jnp.sum(lf, axis=0, keepdims=True)
        q_dec = (q * jnp.exp(cum)).astype(BF16)
        k_inv = (k * jnp.exp(-cum)).astype(BF16)
        k_end = (k * jnp.exp(tot - cum)).astype(BF16)
        att = lax.dot_general(q_dec, k_inv, (((1,), (1,)), ((), ())), preferred_element_type=F32)
        att = jnp.where(masks[d], att, 0.0).astype(BF16)
        st = st_scr[d]
        o = (jnp.dot(att, v, preferred_element_type=F32)
             + lax.dot_general(q_dec, st.astype(BF16), (((1,), (1,)), ((), ())), preferred_element_type=F32))
        ds_t = lax.dot_general(v, k_end, (((0,), (0,)), ((), ())), preferred_element_type=F32)
        st_scr[d] = st * jnp.exp(tot) + ds_t
        return r, o

    def step(i, carry):
        r, o = chunk(i, 0)
        o_ref[r, :] = o
        r, o = chunk(nc - 1 - i, 1)
        ob_scr[r, :] = o
        return carry

    lax.fori_loop(0, nc, step, 0)
    o_ref[...] = o_ref[...] + ob_scr[...]
    for d in range(2):
        sfin_ref[0, d, 0] = st_scr[d].T


def _hgrn2(z, lb, s0, n, row_block0):
    seqs = s0.shape[0]
    hb = HG_KW // HG_DK

    def zspec(col0):
        return pl.BlockSpec((n, HG_DK), lambda s, h: (row_block0 + s, col0 + h))

    st_spec = pl.BlockSpec((1, 2, 1, HG_DK, HG_DV), lambda s, h: (s, 0, h, 0, 0))
    return pl.pallas_call(
        functools.partial(_hg_body, n=n),
        out_shape=(jax.ShapeDtypeStruct((seqs * n, HG_VW), F32),
                   jax.ShapeDtypeStruct((seqs, 2, HG_HEADS, HG_DK, HG_DV), F32)),
        grid=(seqs, HG_HEADS),
        in_specs=[zspec(0), zspec(hb), zspec(2 * hb), zspec(3 * hb),
                  pl.BlockSpec((2, HG_DK), lambda s, h: (0, h)), st_spec],
        out_specs=(pl.BlockSpec((n, HG_DV), lambda s, h: (s, h)), st_spec),
        scratch_shapes=[pltpu.VMEM((2, HG_DV, HG_DK), F32), pltpu.VMEM((n, HG_DV), F32)],
        compiler_params=_cparams("arbitrary", "arbitrary"),
        name="hgrn2_scan",
    )(z, z, z, z, lb, s0)


LRU_RB = 64
HALO = 8


def _lru_body(x_ref, cw_ref, cb_ref, wr_ref, wi_ref, br_ref, bi_ref, lam_ref, h0_ref, hr_ref, e_ref,
              xp_scr, xc_scr, *, n):
    rb = LRU_RB
    nb = n // rb
    w = LRU_W
    zeros = jnp.zeros((HALO, w), F32)
    xp_scr[0:HALO, :] = zeros
    xp_scr[HALO + n:HALO + n + HALO, :] = zeros
    xp_scr[HALO:HALO + n, :] = x_ref[...]
    cw = cw_ref[...]
    cb = cb_ref[...]

    def conv_blk(b, carry):
        base = pl.multiple_of(b * rb, rb)
        xh = xp_scr[pl.ds(base, rb + 2 * HALO), :]
        ext = rb + 2 * HALO
        acc = cb + cw[2:3, :] * xh[HALO:HALO + rb]
        acc = acc + cw[0:1, :] * pltpu.roll(xh, 2, axis=0)[HALO:HALO + rb]
        acc = acc + cw[1:2, :] * pltpu.roll(xh, 1, axis=0)[HALO:HALO + rb]
        acc = acc + cw[3:4, :] * pltpu.roll(xh, ext - 1, axis=0)[HALO:HALO + rb]
        xc_scr[pl.ds(base, rb), :] = acc
        return carry

    lax.fori_loop(0, nb, conv_blk, 0)

    rows = lax.broadcasted_iota(jnp.int32, (rb, w), 0)

    def gates(blk, d):
        r = pl.ds(pl.multiple_of(blk * rb, rb), rb)
        xc = xc_scr[r, :]
        xb = xc.astype(BF16)
        rg = jax.nn.sigmoid(jnp.dot(xb, wr_ref[d], preferred_element_type=F32) + br_ref[d:d + 1, :])
        ig = jax.nn.sigmoid(jnp.dot(xb, wi_ref[d], preferred_element_type=F32) + bi_ref[d:d + 1, :])
        log_a = (-LRU_C) * jax.nn.softplus(-lam_ref[d:d + 1, :]) * rg
        a = jnp.exp(log_a)
        u = jnp.sqrt(-jnp.tanh(log_a) * (a * a + 1.0)) * (ig * xc)
        return r, a, u

    def fwd_blk(blk, h_prev):
        r, a, u = gates(blk, 0)
        s = 1
        while s < rb:
            keep = rows >= s
            a_sh = jnp.where(keep, pltpu.roll(a, s, axis=0), 1.0)
            u_sh = jnp.where(keep, pltpu.roll(u, s, axis=0), 0.0)
            u = a * u_sh + u
            a = a * a_sh
            s *= 2
        h = u + a * h_prev
        hr_ref[r, :] = h
        return h[rb - 1:rb, :]

    def bwd_blk(i, h_next):
        r, a, u = gates(nb - 1 - i, 1)
        s = 1
        while s < rb:
            keep = rows < rb - s
            a_sh = jnp.where(keep, pltpu.roll(a, rb - s, axis=0), 1.0)
            u_sh = jnp.where(keep, pltpu.roll(u, rb - s, axis=0), 0.0)
            u = a * u_sh + u
            a = a * a_sh
            s *= 2
        h = u + a * h_next
        hr_ref[r, :] = hr_ref[r, :] + h
        return h[0:1, :]

    e_ref[0, 0:1, :] = lax.fori_loop(0, nb, fwd_blk, h0_ref[0, 0:1, :])
    e_ref[0, 1:2, :] = lax.fori_loop(0, nb, bwd_blk, h0_ref[0, 1:2, :])


def _rglru(x, col_block, row_block0, n, seqs, conv_w, conv_b, wr_bd, wi_bd, b_r, b_i, lam, h0):
    w = LRU_W
    full2 = lambda s: (0, 0)
    full3 = lambda s: (0, 0, 0)
    return pl.pallas_call(
        functools.partial(_lru_body, n=n),
        out_shape=(jax.ShapeDtypeStruct((seqs * n, w), F32), jax.ShapeDtypeStruct((seqs, 2, w), F32)),
        grid=(seqs,),
        in_specs=[pl.BlockSpec((n, w), lambda s: (row_block0 + s, col_block)),
                  pl.BlockSpec((CONV_W, w), full2), pl.BlockSpec((1, w), full2),
                  pl.BlockSpec((2, w, w), full3), pl.BlockSpec((2, w, w), full3),
                  pl.BlockSpec((2, w), full2), pl.BlockSpec((2, w), full2), pl.BlockSpec((2, w), full2),
                  pl.BlockSpec((1, 2, w), lambda s: (s, 0, 0))],
        out_specs=(pl.BlockSpec((n, w), lambda s: (s, 0)), pl.BlockSpec((1, 2, w), lambda s: (s, 0, 0))),
        scratch_shapes=[pltpu.VMEM((n + 2 * HALO, w), F32), pltpu.VMEM((n, w), F32)],
        compiler_params=_cparams("arbitrary"),
        name="rglru",
    )(x, conv_w, conv_b.reshape(1, w), wr_bd, wi_bd, b_r, b_i, lam, h0)


def _block_diag(wg):
    eye = jnp.eye(LRU_BLOCKS, dtype=wg.dtype)
    dense = wg[:, :, :, None, :] * eye[None, :, None, :, None]
    return dense.reshape(2, LRU_W, LRU_W).astype(BF16)


TM_OUT = 256


def _out_body(x_ref, o_ref, hr_ref, g_ref, xg_ref, ga0_ref, ga1_ref, gb0_ref, gb1_ref, m_ref,
              hgn_ref, n2_ref, wa_ref, wb_ref, wo_ref, wq_ref, x1_ref, h2_ref, q_ref):
    o = o_ref[...]
    parts = []
    for h in range(HG_HEADS):
        parts.append(_rms(o[:, h * HG_DV:(h + 1) * HG_DV]) * hgn_ref[...])
    on = jnp.concatenate(parts, axis=1) * _silu(g_ref[...])
    y_a = jnp.dot(on.astype(BF16), wa_ref[...], preferred_element_type=F32)
    y_b = jnp.dot((hr_ref[...] * jax.nn.gelu(xg_ref[...])).astype(BF16), wb_ref[...], preferred_element_type=F32)
    ga = jnp.concatenate([ga0_ref[...], ga1_ref[...]], axis=1)
    gb = jnp.concatenate([gb0_ref[...], gb1_ref[...]], axis=1)
    merged = jax.nn.sigmoid(ga) * y_a + jax.nn.sigmoid(gb) * y_b
    mix = jnp.dot(merged.astype(BF16), wo_ref[...], preferred_element_type=F32)
    x1 = x_ref[...] + m_ref[0, 2:3, :] * mix
    x1_ref[...] = x1
    h2 = _rms(x1) * n2_ref[...] * (1.0 + m_ref[0, 4:5, :]) + m_ref[0, 3:4, :]
    h2_ref[...] = h2
    q_ref[...] = jnp.dot(h2.astype(BF16), wq_ref[...], preferred_element_type=F32).astype(BF16)


def _out_proj(x, o_hg, hr, z, mod, hg_norm, norm2, wa, wb, wo, wq, row_of):
    t, d = x.shape
    tm = TM_OUT
    nq = PK_HEADS * PK_DQ
    half = lambda i: (i, 0)
    zc = lambda c: pl.BlockSpec((tm, LRU_W), lambda i: (i, c))
    const = lambda i: (0, 0)
    return pl.pallas_call(
        _out_body,
        out_shape=(jax.ShapeDtypeStruct((t, d), F32), jax.ShapeDtypeStruct((t, d), F32),
                   jax.ShapeDtypeStruct((t, nq), BF16)),
        grid=(t // tm,),
        in_specs=[pl.BlockSpec((tm, d), half), pl.BlockSpec((tm, HG_VW), half), pl.BlockSpec((tm, LRU_W), half),
                  zc(COL_G), zc(COL_XG), zc(COL_GA), zc(COL_GA + 1), zc(COL_GB), zc(COL_GB + 1),
                  pl.BlockSpec((1, N_MOD, d), lambda i: (row_of(i), 0, 0)),
                  pl.BlockSpec((1, HG_DV), const), pl.BlockSpec((1, d), const),
                  pl.BlockSpec((HG_VW, d), const), pl.BlockSpec((LRU_W, d), const),
                  pl.BlockSpec((d, d), const), pl.BlockSpec((d, nq), const)],
        out_specs=(pl.BlockSpec((tm, d), half), pl.BlockSpec((tm, d), half), pl.BlockSpec((tm, nq), half)),
        compiler_params=_cparams("arbitrary"),
        name="out_proj",
    )(x, o_hg, hr, z, z, z, z, z, z, mod, hg_norm.reshape(1, HG_DV), norm2.reshape(1, d), wa, wb, wo, wq)


TT_TOPK = 128


def _topk_rows(s, k, payload=None):
    r = s.shape[0]
    rid = lax.broadcasted_iota(jnp.int32, s.shape, 0)
    vals, picks = [], []
    for _ in range(k):
        m = jnp.max(s, axis=0, keepdims=True)
        idx = jnp.min(jnp.where(s == m, rid, r), axis=0, keepdims=True)
        hit = rid == idx
        vals.append(m)
        picks.append(idx if payload is None else jnp.sum(jnp.where(hit, payload, 0), axis=0, keepdims=True))
        s = jnp.where(hit, -jnp.inf, s)
    return jnp.concatenate(vals, axis=0), jnp.concatenate(picks, axis=0)


def _topk_body(q_ref, keys_ref, eidx_ref, gate_ref):
    e_rows, g_rows = [], []
    for h in range(PK_HEADS):
        sv, si = [], []
        for p in range(2):
            c0 = h * PK_DQ + p * PK_DH
            s = lax.dot_general(keys_ref[p, h], q_ref[:, c0:c0 + PK_DH], (((1,), (1,)), ((), ())),
                                preferred_element_type=F32)
            v, i = _topk_rows(s, PK_TOPK)
            sv.append(v)
            si.append(i)
        cand = jnp.concatenate([sv[0][i:i + 1, :] + sv[1] for i in range(PK_TOPK)], axis=0)
        cidx = jnp.concatenate([si[0][i:i + 1, :] * N_KEYS + si[1] for i in range(PK_TOPK)], axis=0)
        best, eid = _topk_rows(cand, PK_TOPK, payload=cidx)
        ex = jnp.exp(best - best[0:1, :])
        g_rows.append(ex / jnp.sum(ex, axis=0, keepdims=True))
        e_rows.append(eid)
    eidx_ref[...] = jnp.concatenate(e_rows, axis=0).T
    gate_ref[...] = jnp.concatenate(g_rows, axis=0).T


def _pk_topk(q, keys_bf):
    t = q.shape[0]
    tt = TT_TOPK
    ne = PK_HEADS * PK_TOPK
    return pl.pallas_call(
        _topk_body,
        out_shape=(jax.ShapeDtypeStruct((t, ne), jnp.int32), jax.ShapeDtypeStruct((t, ne), F32)),
        grid=(t // tt,),
        in_specs=[pl.BlockSpec((tt, PK_HEADS * PK_DQ), lambda i: (i, 0)),
                  pl.BlockSpec((2, PK_HEADS, N_KEYS, PK_DH), lambda i: (0, 0, 0, 0))],
        out_specs=(pl.BlockSpec((tt, ne), lambda i: (i, 0)), pl.BlockSpec((tt, ne), lambda i: (i, 0))),
        compiler_params=_cparams("arbitrary"),
        name="pk_topk",
    )(q, keys_bf)


SC_LANES = 16
SC_CORES = 2
SC_SUBCORES = 16
SC_WORKERS = SC_CORES * SC_SUBCORES
SC_HALF = D_MODEL // 2
SC_ROW_CHUNKS = SC_HALF // SC_LANES
SC_TOKEN_BLOCK = 8


def _pack_bf16_pairs(tab):
    lo = lax.bitcast_convert_type(tab[:, SC_HALF:].astype(BF16), jnp.uint16).astype(jnp.uint32)
    bits = lax.bitcast_convert_type(tab[:, :SC_HALF], jnp.uint32)
    sign = bits & jnp.uint32(0x80000000)
    mag = (bits & jnp.uint32(0x7FFFFFFF)) + jnp.uint32(1 << 15)
    hi = jnp.where(mag >= lo, (mag - lo) >> 16, jnp.uint32(0))
    return lax.bitcast_convert_type(sign | (hi << 16) | lo, jnp.int32)


def _unpack_pair(w):
    return lax.bitcast_convert_type(w, F32), lax.bitcast_convert_type(w << 16, F32)


def _gelu_tanh(x):
    y = 0.7978845608028654 * (x + 0.044715 * (x * x * x))
    t = 1.0 - 2.0 / (jnp.exp(2.0 * y) + 1.0)
    return x * (0.5 * (1.0 + t))


def _peer_sc_body(h_hbm, idx_hbm, gate_hbm, u_hbm, v_hbm, out_hbm,
                  h_v, idx_v, gate_v, out_v, ub0, ub1, vb0, vb1, su0, su1, sv0, sv1, *, tokens_per_worker):
    nh, k, tb, lanes = PK_HEADS, PK_TOPK, SC_TOKEN_BLOCK, SC_LANES
    wid = lax.axis_index("s") * SC_CORES + lax.axis_index("c")
    ubufs, vbufs, sus, svs = (ub0, ub1), (vb0, vb1), (su0, su1), (sv0, sv1)
    lane = lax.iota(jnp.int32, lanes)
    zero = jnp.zeros((lanes,), F32)

    def start(tok, hd, par):
        irow = idx_v.at[tok * nh + hd]
        pltpu.async_copy(u_hbm.at[irow], ubufs[par], sus[par])
        pltpu.async_copy(v_hbm.at[irow], vbufs[par], svs[par])

    def wait(par):
        irow = idx_v.at[0]
        pltpu.make_async_copy(u_hbm.at[irow], ubufs[par], sus[par]).wait()
        pltpu.make_async_copy(v_hbm.at[irow], vbufs[par], svs[par]).wait()

    @pl.loop(0, tokens_per_worker // tb)
    def _(blk):
        base = wid * tokens_per_worker + blk * tb
        pltpu.sync_copy(h_hbm.at[pl.ds(base, tb)], h_v)
        pltpu.sync_copy(idx_hbm.at[pl.ds(base * nh, tb * nh)], idx_v)
        pltpu.sync_copy(gate_hbm.at[pl.ds(base * nh, tb * nh)], gate_v)
        start(0, 0, 0)

        @pl.loop(0, tb)
        def _(tok):
            for hd in range(nh):
                par = hd % 2
                if hd + 1 < nh:
                    start(tok, hd + 1, 1 - par)
                else:
                    @pl.when(tok + 1 < tb)
                    def _():
                        start(tok + 1, 0, 1 - par)
                wait(par)
                ub, vb = ubufs[par], vbufs[par]

                @plsc.parallel_loop(0, SC_ROW_CHUNKS, carry=(zero,) * k)
                def accs(c, acc):
                    ha = h_v[tok, pl.ds(c * lanes, lanes)]
                    hb = h_v[tok, pl.ds(SC_HALF + c * lanes, lanes)]
                    out = []
                    for r in range(k):
                        a, b = _unpack_pair(ub[r, pl.ds(c * lanes, lanes)])
                        out.append(acc[r] + (ha * a + hb * b))
                    return tuple(out)

                s_vec = zero
                for r in range(k):
                    s_vec = jnp.where(lane == r, jnp.sum(accs[r]), s_vec)
                w_vec = gate_v[tok * nh + hd, :] * _gelu_tanh(s_vec)
                wb = [jnp.sum(jnp.where(lane == r, w_vec, 0.0)) for r in range(k)]

                @plsc.parallel_loop(0, SC_ROW_CHUNKS)
                def _(c):
                    pa, pb = [], []
                    for r in range(k):
                        a, b = _unpack_pair(vb[r, pl.ds(c * lanes, lanes)])
                        pa.append(wb[r] * a)
                        pb.append(wb[r] * b)
                    while len(pa) > 1:
                        pa = [pa[i] + pa[i + 1] for i in range(0, len(pa), 2)]
                        pb = [pb[i] + pb[i + 1] for i in range(0, len(pb), 2)]
                    for off, o in ((0, pa[0]), (SC_HALF, pb[0])):
                        cols = pl.ds(off + c * lanes, lanes)
                        if hd == 0:
                            out_v[tok, cols] = o
                        else:
                            out_v[tok, cols] = out_v[tok, cols] + o

        pltpu.sync_copy(out_v, out_hbm.at[pl.ds(base, tb)])


def _peer_experts(h, eidx, gate, u_tab, v_tab):
    t, d = h.shape
    assert t % (SC_WORKERS * SC_TOKEN_BLOCK) == 0 and d == D_MODEL
    rows = pltpu.VMEM((PK_TOPK, SC_HALF), jnp.int32)
    return pl.kernel(
        functools.partial(_peer_sc_body, tokens_per_worker=t // SC_WORKERS),
        out_type=jax.ShapeDtypeStruct((t, d), F32),
        mesh=plsc.VectorSubcoreMesh(core_axis_name="c", subcore_axis_name="s"),
        scratch_types=[
            pltpu.VMEM((SC_TOKEN_BLOCK, d), F32),
            pltpu.VMEM((SC_TOKEN_BLOCK * PK_HEADS, PK_TOPK), jnp.int32),
            pltpu.VMEM((SC_TOKEN_BLOCK * PK_HEADS, PK_TOPK), F32),
            pltpu.VMEM((SC_TOKEN_BLOCK, d), F32),
            rows, rows, rows, rows,
            pltpu.SemaphoreType.DMA, pltpu.SemaphoreType.DMA,
            pltpu.SemaphoreType.DMA, pltpu.SemaphoreType.DMA,
        ],
        compiler_params=pltpu.CompilerParams(needs_layout_passes=False),
        name="peer_experts_sc",
    )(h, eidx, gate, u_tab, v_tab)


TM_FIN = 512


def _final_body(x1_ref, p_ref, m_ref, w_ref, y_ref):
    y_ref[...] = _rms(x1_ref[...] + m_ref[0, 5:6, :] * p_ref[...]) * w_ref[...]


def _final(x1, peer, mod, norm_f, row_of):
    t, d = x1.shape
    tile = pl.BlockSpec((TM_FIN, d), lambda i: (i, 0))
    return pl.pallas_call(
        _final_body,
        out_shape=jax.ShapeDtypeStruct((t, d), F32),
        grid=(t // TM_FIN,),
        in_specs=[tile, tile, pl.BlockSpec((1, N_MOD, d), lambda i: (row_of(i), 0, 0)),
                  pl.BlockSpec((1, d), lambda i: (0, 0))],
        out_specs=tile,
        compiler_params=_cparams("arbitrary"),
        name="final_norm",
    )(x1, peer, mod, norm_f.reshape(1, d))


CTX_GROUP = 16
LAT_GROUP = 2


def _group(x, mod, row0, per_seq, n, hg_s0, lru_s0, col_major, p, after):
    seqs = hg_s0.shape[0]
    t = seqs * n
    x, _ = lax.optimization_barrier((x, after))
    rows_of = lambda tm: _mod_row_map(tm, n, row0, per_seq)
    z = _in_proj(x, mod, p['norm1'], p['w_in'], rows_of(TM_IN))
    o_hg, hg_fin = _hgrn2(z, p['lb'], hg_s0, n, 0)
    if col_major:
        rows = n // GRID_W
        xr = (z[:, COL_XR * LRU_W:(COL_XR + 1) * LRU_W]
              .reshape(seqs, rows, GRID_W, LRU_W).transpose(0, 2, 1, 3).reshape(t, LRU_W))
        hr, lru_fin = _rglru(xr, 0, 0, n, seqs, *p['lru'], lru_s0)
        hr = hr.reshape(seqs, GRID_W, rows, LRU_W).transpose(0, 2, 1, 3).reshape(t, LRU_W)
    else:
        hr, lru_fin = _rglru(z, COL_XR, 0, n, seqs, *p['lru'], lru_s0)
    x1, h2, q = _out_proj(x, o_hg, hr, z, mod, p['hg_norm'], p['norm2'], p['w_a'], p['w_b'], p['w_o'], p['w_q'],
                          rows_of(TM_OUT))
    eidx, gate = _pk_topk(q, p['keys'])
    peer = _peer_experts(h2, eidx.reshape(t * PK_HEADS, PK_TOPK), gate.reshape(t * PK_HEADS, PK_TOPK),
                         p['pk_u'], p['pk_v'])
    return _final(x1, peer, mod, p['norm_f'], rows_of(TM_FIN)), hg_fin, lru_fin, eidx


def kernel(x_prompt, x_sample, state_hgrn, state_rglru, c, c_ctx, w_mod, b_mod, norm1, w_in,
           hg_lb_logits, hg_norm, lru_conv_w, lru_conv_b, lru_w_r, lru_b_r, lru_w_i, lru_b_i,
           lru_lambda, w_branch_a, w_branch_b, w_out, norm2, pk_w_q, pk_sub_keys, pk_u, pk_v,
           norm_f):
    assert w_mod.shape[0] == 1, "single trunk layer"
    d = D_MODEL
    nb_ctx, n_ctx, _ = x_prompt.shape
    nb_lat, n_lat, _ = x_sample.shape
    assert nb_lat < MOD_ROWS and nb_lat % LAT_GROUP == 0 and nb_ctx % CTX_GROUP == 0
    ctx_row = nb_lat

    cond = jnp.zeros((MOD_ROWS, d), F32).at[:nb_lat].set(c).at[ctx_row].set(c_ctx)
    mod = _modulation(cond, w_mod[0], b_mod[0])
    p = {
        'lb': jnp.cumsum(jax.nn.softmax(hg_lb_logits.astype(F32), axis=1), axis=1)[:, 0],
        'norm1': norm1[0], 'w_in': w_in[0].astype(BF16), 'hg_norm': hg_norm[0], 'norm2': norm2[0],
        'lru': (lru_conv_w[0], lru_conv_b[0], _block_diag(lru_w_r[0]), _block_diag(lru_w_i[0]),
                lru_b_r[0], lru_b_i[0], lru_lambda[0]),
        'w_a': w_branch_a[0].astype(BF16), 'w_b': w_branch_b[0].astype(BF16), 'w_o': w_out[0].astype(BF16),
        'w_q': pk_w_q[0].astype(BF16), 'keys': pk_sub_keys[0].astype(BF16),
        'pk_u': _pack_bf16_pairs(pk_u[0]), 'pk_v': _pack_bf16_pairs(pk_v[0]), 'norm_f': norm_f,
    }
    y_ctx, hg_fin, lru_fin = [], [], []
    after = (p['pk_u'], p['pk_v'])
    for s0 in range(0, nb_ctx, CTX_GROUP):
        y, hg, lr, after = _group(
            x_prompt[s0:s0 + CTX_GROUP].reshape(CTX_GROUP * n_ctx, d), mod, ctx_row, False, n_ctx,
            jnp.zeros((CTX_GROUP, 2, HG_HEADS, HG_DK, HG_DV), F32), jnp.zeros((CTX_GROUP, 2, LRU_W), F32), False, p,
            after)
        y_ctx.append(y.reshape(CTX_GROUP, n_ctx, d))
        hg_fin.append(hg)
        lru_fin.append(lr)
    y_lat = []
    for s0 in range(0, nb_lat, LAT_GROUP):
        sl = slice(s0, s0 + LAT_GROUP)
        y, _, _, after = _group(x_sample[sl].reshape(LAT_GROUP * n_lat, d), mod, s0, True, n_lat,
                                state_hgrn[sl, 0], state_rglru[sl, 0], True, p, after)
        y_lat.append(y.reshape(LAT_GROUP, n_lat, d))
    return (jnp.concatenate(y_ctx, axis=0), jnp.concatenate(y_lat, axis=0),
            jnp.concatenate(hg_fin, axis=0)[:, None], jnp.concatenate(lru_fin, axis=0)[:, None])
```

```python
import functools

import jax, jax.numpy as jnp
from jax import lax
from jax.experimental import pallas as pl
from jax.experimental.pallas import tpu as pltpu
from jax.experimental.pallas import tpu_sc as plsc

D_MODEL = 1024
GRID_W = 64
EPS = 1e-6
HG_HEADS = 4
HG_DK = 128
HG_DV = 128
HG_KW = HG_HEADS * HG_DK
HG_VW = HG_HEADS * HG_DV
HG_CHUNK = 32
LRU_W = D_MODEL // 2
LRU_BLOCKS = 8
LRU_BW = LRU_W // LRU_BLOCKS
CONV_W = 4
LRU_C = 8.0
PK_HEADS = 8
N_KEYS = 128
PK_TOPK = 16
PK_DQ = 256
PK_DH = PK_DQ // 2
IN_W = 3 * HG_KW + 2 * HG_VW + 2 * LRU_W + 2 * D_MODEL
COL_G, COL_XR, COL_XG, COL_GA, COL_GB = 4, 5, 6, 7, 9
N_MOD = 6
MOD_ROWS = 16
F32 = jnp.float32
BF16 = jnp.bfloat16
VMEM_LIMIT = 48 * 1024 * 1024


def _cparams(*sem):
    return pltpu.CompilerParams(dimension_semantics=sem, vmem_limit_bytes=VMEM_LIMIT)


def _silu(x):
    return x * jax.nn.sigmoid(x)


def _rms(x):
    return x * lax.rsqrt(jnp.mean(x * x, axis=-1, keepdims=True) + EPS)


def _mod_body(c_ref, w_ref, b_ref, o_ref):
    o_ref[...] = lax.dot_general(_silu(c_ref[...]), w_ref[...], (((1,), (0,)), ((), ())),
                                 precision=lax.Precision.HIGHEST, preferred_element_type=F32) + b_ref[...]


def _modulation(cond, w_mod, b_mod):
    d = D_MODEL
    out = pl.pallas_call(
        _mod_body,
        out_shape=jax.ShapeDtypeStruct((MOD_ROWS, N_MOD * d), F32),
        grid=(N_MOD,),
        in_specs=[pl.BlockSpec((MOD_ROWS, d), lambda j: (0, 0)),
                  pl.BlockSpec((d, d), lambda j: (0, j)),
                  pl.BlockSpec((1, d), lambda j: (0, j))],
        out_specs=pl.BlockSpec((MOD_ROWS, d), lambda j: (0, j)),
        compiler_params=_cparams("arbitrary"),
        name="adaln_modulation",
    )(cond, w_mod, b_mod.reshape(1, N_MOD * d))
    return out.reshape(MOD_ROWS, N_MOD, d)


def _mod_row_map(tm, n, row0, per_seq):
    per = n // tm
    return (lambda i: row0 + i // per) if per_seq else (lambda i: row0)


TM_IN = 512
TOKEN_TILE_MIN = 256
TN_IN = IN_W // 2


def _in_body(x_ref, m_ref, n1_ref, w_ref, z_ref, h_scr):
    @pl.when(pl.program_id(1) == 0)
    def _():
        y = _rms(x_ref[...]) * n1_ref[...]
        h_scr[...] = (y * (1.0 + m_ref[0, 1:2, :]) + m_ref[0, 0:1, :]).astype(BF16)

    z_ref[...] = jnp.dot(h_scr[...], w_ref[...], preferred_element_type=F32)


def _token_tile(t, preferred):
    tm = preferred if t % preferred == 0 else TOKEN_TILE_MIN
    assert t % tm == 0
    return tm


def _in_proj(x, mod, norm1, w_in_bf, rows_of):
    t, d = x.shape
    tm = _token_tile(t, TM_IN)
    row_of = rows_of(tm)
    return pl.pallas_call(
        _in_body,
        out_shape=jax.ShapeDtypeStruct((t, IN_W), F32),
        grid=(t // tm, IN_W // TN_IN),
        in_specs=[pl.BlockSpec((tm, d), lambda i, j: (i, 0)),
                  pl.BlockSpec((1, N_MOD, d), lambda i, j: (row_of(i), 0, 0)),
                  pl.BlockSpec((1, d), lambda i, j: (0, 0)),
                  pl.BlockSpec((d, TN_IN), lambda i, j: (0, j))],
        out_specs=pl.BlockSpec((tm, TN_IN), lambda i, j: (i, j)),
        scratch_shapes=[pltpu.VMEM((tm, d), BF16)],
        compiler_params=_cparams("arbitrary", "arbitrary"),
        name="in_proj",
    )(x, mod, norm1.reshape(1, d), w_in_bf)


def _hg_body(q_ref, ff_ref, fb_ref, v_ref, lb_ref, s0_ref, o_ref, sfin_ref, st_scr, ob_scr, *, n):
    c = HG_CHUNK
    nc = n // c
    row = lax.broadcasted_iota(jnp.int32, (c, c), 0)
    col = lax.broadcasted_iota(jnp.int32, (c, c), 1)
    lower = row >= col
    tri = (lower.astype(F32), (row <= col).astype(F32))
    masks = (lower, row <= col)
    f_refs = (ff_ref, fb_ref)
    for d in range(2):
        st_scr[d] = s0_ref[0, d, 0].T

    def chunk(ci, d):
        r = pl.ds(pl.multiple_of(ci * c, c), c)
        q = _silu(q_ref[r, :])
        v = v_ref[r, :].astype(BF16)
        lb = lb_ref[d:d + 1, :]
        f = lb + (1.0 - lb) * jax.nn.sigmoid(f_refs[d][r, :])
        k = 1.0 - f
        lf = jnp.log(f)
        cum = lax.dot_general(tri[d], lf, (((1,), (0,)), ((), ())),
                              precision=lax.Precision.HIGHEST, preferred_element_type=F32)
        tot = jnp.sum(lf, axis=0, keepdims=True)
        q_dec = (q * jnp.exp(cum)).astype(BF16)
        k_inv = (k * jnp.exp(-cum)).astype(BF16)
        k_end = (k * jnp.exp(tot - cum)).astype(BF16)
        att = lax.dot_general(q_dec, k_inv, (((1,), (1,)), ((), ())), preferred_element_type=F32)
        att = jnp.where(masks[d], att, 0.0).astype(BF16)
        st = st_scr[d]
        o = (jnp.dot(att, v, preferred_element_type=F32)
             + lax.dot_general(q_dec, st.astype(BF16), (((1,), (1,)), ((), ())), preferred_element_type=F32))
        ds_t = lax.dot_general(v, k_end, (((0,), (0,)), ((), ())), preferred_element_type=F32)
        st_scr[d] = st * jnp.exp(tot) + ds_t
        return r, o

    def step(i, carry):
        r, o = chunk(i, 0)
        o_ref[r, :] = o
        r, o = chunk(nc - 1 - i, 1)
        ob_scr[r, :] = o
        return carry

    lax.fori_loop(0, nc, step, 0)
    o_ref[...] = o_ref[...] + ob_scr[...]
    for d in range(2):
        sfin_ref[0, d, 0] = st_scr[d].T


def _hgrn2(z, lb, s0, n, row_block0):
    seqs = s0.shape[0]
    hb = HG_KW // HG_DK

    def zspec(col0):
        return pl.BlockSpec((n, HG_DK), lambda s, h: (row_block0 + s, col0 + h))

    st_spec = pl.BlockSpec((1, 2, 1, HG_DK, HG_DV), lambda s, h: (s, 0, h, 0, 0))
    return pl.pallas_call(
        functools.partial(_hg_body, n=n),
        out_shape=(jax.ShapeDtypeStruct((seqs * n, HG_VW), F32),
                   jax.ShapeDtypeStruct((seqs, 2, HG_HEADS, HG_DK, HG_DV), F32)),
        grid=(seqs, HG_HEADS),
        in_specs=[zspec(0), zspec(hb), zspec(2 * hb), zspec(3 * hb),
                  pl.BlockSpec((2, HG_DK), lambda s, h: (0, h)), st_spec],
        out_specs=(pl.BlockSpec((n, HG_DV), lambda s, h: (s, h)), st_spec),
        scratch_shapes=[pltpu.VMEM((2, HG_DV, HG_DK), F32), pltpu.VMEM((n, HG_DV), F32)],
        compiler_params=_cparams("arbitrary", "arbitrary"),
        name="hgrn2_scan",
    )(z, z, z, z, lb, s0)


LRU_RB = 64
HALO = 8


LANES = 128


def _lru_body(x_ref, cw_ref, cb_ref, wr_ref, wi_ref, br_ref, bi_ref, lam_ref, h0_ref, hr_ref, e_ref,
              xp_scr, xc_scr, *cm_scr, n, col_major):
    rb = LRU_RB
    nb = n // rb
    w = LRU_W
    grid_rows = n // GRID_W
    slabs = w // LANES
    zeros = jnp.zeros((HALO, w), F32)
    xp_scr[0:HALO, :] = zeros
    xp_scr[HALO + n:HALO + n + HALO, :] = zeros
    if col_major:
        st_scr, hcm_scr = cm_scr
        for j in range(slabs):
            st_scr[j] = x_ref[:, j * LANES:(j + 1) * LANES]

        def to_col_major(c, carry):
            dst = pl.ds(pl.multiple_of(HALO + c * grid_rows, 8), grid_rows)
            for j in range(slabs):
                xp_scr[dst, j * LANES:(j + 1) * LANES] = st_scr[j, pl.ds(c, grid_rows, stride=GRID_W), :]
            return carry

        lax.fori_loop(0, GRID_W, to_col_major, 0)
        h_dst = hcm_scr
    else:
        xp_scr[HALO:HALO + n, :] = x_ref[...]
        h_dst = hr_ref
    cw = cw_ref[...]
    cb = cb_ref[...]

    def conv_blk(b, carry):
        base = pl.multiple_of(b * rb, rb)
        xh = xp_scr[pl.ds(base, rb + 2 * HALO), :]
        ext = rb + 2 * HALO
        acc = cb + cw[2:3, :] * xh[HALO:HALO + rb]
        acc = acc + cw[0:1, :] * pltpu.roll(xh, 2, axis=0)[HALO:HALO + rb]
        acc = acc + cw[1:2, :] * pltpu.roll(xh, 1, axis=0)[HALO:HALO + rb]
        acc = acc + cw[3:4, :] * pltpu.roll(xh, ext - 1, axis=0)[HALO:HALO + rb]
        xc_scr[pl.ds(base, rb), :] = acc
        return carry

    lax.fori_loop(0, nb, conv_blk, 0)

    rows = lax.broadcasted_iota(jnp.int32, (rb, w), 0)

    def gates(blk, d):
        r = pl.ds(pl.multiple_of(blk * rb, rb), rb)
        xc = xc_scr[r, :]
        xb = xc.astype(BF16)
        rg = jax.nn.sigmoid(jnp.dot(xb, wr_ref[d], preferred_element_type=F32) + br_ref[d:d + 1, :])
        ig = jax.nn.sigmoid(jnp.dot(xb, wi_ref[d], preferred_element_type=F32) + bi_ref[d:d + 1, :])
        log_a = (-LRU_C) * jax.nn.softplus(-lam_ref[d:d + 1, :]) * rg
        a = jnp.exp(log_a)
        u = jnp.sqrt(-jnp.tanh(log_a) * (a * a + 1.0)) * (ig * xc)
        return r, a, u

    def fwd_blk(blk, h_prev):
        r, a, u = gates(blk, 0)
        s = 1
        while s < rb:
            keep = rows >= s
            a_sh = jnp.where(keep, pltpu.roll(a, s, axis=0), 1.0)
            u_sh = jnp.where(keep, pltpu.roll(u, s, axis=0), 0.0)
            u = a * u_sh + u
            a = a * a_sh
            s *= 2
        h = u + a * h_prev
        h_dst[r, :] = h
        return h[rb - 1:rb, :]

    def bwd_blk(i, h_next):
        r, a, u = gates(nb - 1 - i, 1)
        s = 1
        while s < rb:
            keep = rows < rb - s
            a_sh = jnp.where(keep, pltpu.roll(a, rb - s, axis=0), 1.0)
            u_sh = jnp.where(keep, pltpu.roll(u, rb - s, axis=0), 0.0)
            u = a * u_sh + u
            a = a * a_sh
            s *= 2
        h = u + a * h_next
        h_dst[r, :] = h_dst[r, :] + h
        return h[0:1, :]

    e_ref[0, 0:1, :] = lax.fori_loop(0, nb, fwd_blk, h0_ref[0, 0:1, :])
    e_ref[0, 1:2, :] = lax.fori_loop(0, nb, bwd_blk, h0_ref[0, 1:2, :])
    if col_major:
        def to_row_major(c, carry):
            src = pl.ds(pl.multiple_of(c * grid_rows, 8), grid_rows)
            for j in range(slabs):
                st_scr[j, pl.ds(c, grid_rows, stride=GRID_W), :] = hcm_scr[src, j * LANES:(j + 1) * LANES]
            return carry

        lax.fori_loop(0, GRID_W, to_row_major, 0)
        for j in range(slabs):
            hr_ref[:, j * LANES:(j + 1) * LANES] = st_scr[j]


def _rglru(x, col_block, row_block0, n, seqs, col_major, conv_w, conv_b, wr_bd, wi_bd, b_r, b_i, lam, h0):
    w = LRU_W
    cm_scratch = [pltpu.VMEM((w // LANES, n, LANES), F32), pltpu.VMEM((n, w), F32)] if col_major else []
    full2 = lambda s: (0, 0)
    full3 = lambda s: (0, 0, 0)
    return pl.pallas_call(
        functools.partial(_lru_body, n=n, col_major=col_major),
        out_shape=(jax.ShapeDtypeStruct((seqs * n, w), F32), jax.ShapeDtypeStruct((seqs, 2, w), F32)),
        grid=(seqs,),
        in_specs=[pl.BlockSpec((n, w), lambda s: (row_block0 + s, col_block)),
                  pl.BlockSpec((CONV_W, w), full2), pl.BlockSpec((1, w), full2),
                  pl.BlockSpec((2, w, w), full3), pl.BlockSpec((2, w, w), full3),
                  pl.BlockSpec((2, w), full2), pl.BlockSpec((2, w), full2), pl.BlockSpec((2, w), full2),
                  pl.BlockSpec((1, 2, w), lambda s: (s, 0, 0))],
        out_specs=(pl.BlockSpec((n, w), lambda s: (s, 0)), pl.BlockSpec((1, 2, w), lambda s: (s, 0, 0))),
        scratch_shapes=[pltpu.VMEM((n + 2 * HALO, w), F32), pltpu.VMEM((n, w), F32)] + cm_scratch,
        compiler_params=_cparams("arbitrary"),
        name="rglru",
    )(x, conv_w, conv_b.reshape(1, w), wr_bd, wi_bd, b_r, b_i, lam, h0)


def _block_diag(wg):
    eye = jnp.eye(LRU_BLOCKS, dtype=wg.dtype)
    dense = wg[:, :, :, None, :] * eye[None, :, None, :, None]
    return dense.reshape(2, LRU_W, LRU_W).astype(BF16)


TM_OUT = 256


def _out_body(x_ref, o_ref, hr_ref, g_ref, xg_ref, ga0_ref, ga1_ref, gb0_ref, gb1_ref, m_ref,
              hgn_ref, n2_ref, wa_ref, wb_ref, wo_ref, wq_ref, x1_ref, h2_ref, q_ref):
    o = o_ref[...]
    parts = []
    for h in range(HG_HEADS):
        parts.append(_rms(o[:, h * HG_DV:(h + 1) * HG_DV]) * hgn_ref[...])
    on = jnp.concatenate(parts, axis=1) * _silu(g_ref[...])
    y_a = jnp.dot(on.astype(BF16), wa_ref[...], preferred_element_type=F32)
    y_b = jnp.dot((hr_ref[...] * jax.nn.gelu(xg_ref[...])).astype(BF16), wb_ref[...], preferred_element_type=F32)
    ga = jnp.concatenate([ga0_ref[...], ga1_ref[...]], axis=1)
    gb = jnp.concatenate([gb0_ref[...], gb1_ref[...]], axis=1)
    merged = jax.nn.sigmoid(ga) * y_a + jax.nn.sigmoid(gb) * y_b
    mix = jnp.dot(merged.astype(BF16), wo_ref[...], preferred_element_type=F32)
    x1 = x_ref[...] + m_ref[0, 2:3, :] * mix
    x1_ref[...] = x1
    h2 = _rms(x1) * n2_ref[...] * (1.0 + m_ref[0, 4:5, :]) + m_ref[0, 3:4, :]
    h2_ref[...] = h2
    q_ref[...] = jnp.dot(h2.astype(BF16), wq_ref[...], preferred_element_type=F32).astype(BF16)


def _out_proj(x, o_hg, hr, z, mod, hg_norm, norm2, wa, wb, wo, wq, row_of):
    t, d = x.shape
    tm = TM_OUT
    nq = PK_HEADS * PK_DQ
    half = lambda i: (i, 0)
    zc = lambda c: pl.BlockSpec((tm, LRU_W), lambda i: (i, c))
    const = lambda i: (0, 0)
    return pl.pallas_call(
        _out_body,
        out_shape=(jax.ShapeDtypeStruct((t, d), F32), jax.ShapeDtypeStruct((t, d), F32),
                   jax.ShapeDtypeStruct((t, nq), BF16)),
        grid=(t // tm,),
        in_specs=[pl.BlockSpec((tm, d), half), pl.BlockSpec((tm, HG_VW), half), pl.BlockSpec((tm, LRU_W), half),
                  zc(COL_G), zc(COL_XG), zc(COL_GA), zc(COL_GA + 1), zc(COL_GB), zc(COL_GB + 1),
                  pl.BlockSpec((1, N_MOD, d), lambda i: (row_of(i), 0, 0)),
                  pl.BlockSpec((1, HG_DV), const), pl.BlockSpec((1, d), const),
                  pl.BlockSpec((HG_VW, d), const), pl.BlockSpec((LRU_W, d), const),
                  pl.BlockSpec((d, d), const), pl.BlockSpec((d, nq), const)],
        out_specs=(pl.BlockSpec((tm, d), half), pl.BlockSpec((tm, d), half), pl.BlockSpec((tm, nq), half)),
        compiler_params=_cparams("arbitrary"),
        name="out_proj",
    )(x, o_hg, hr, z, z, z, z, z, z, mod, hg_norm.reshape(1, HG_DV), norm2.reshape(1, d), wa, wb, wo, wq)


TT_TOPK = 128


def _topk_rows(s, k, payload=None):
    r = s.shape[0]
    rid = lax.broadcasted_iota(jnp.int32, s.shape, 0)
    vals, picks = [], []
    for _ in range(k):
        m = jnp.max(s, axis=0, keepdims=True)
        idx = jnp.min(jnp.where(s == m, rid, r), axis=0, keepdims=True)
        hit = rid == idx
        vals.append(m)
        picks.append(idx if payload is None else jnp.sum(jnp.where(hit, payload, 0), axis=0, keepdims=True))
        s = jnp.where(hit, -jnp.inf, s)
    return jnp.concatenate(vals, axis=0), jnp.concatenate(picks, axis=0)


def _topk_body(q_ref, keys_ref, eidx_ref, gate_ref):
    e_rows, g_rows = [], []
    for h in range(PK_HEADS):
        sv, si = [], []
        for p in range(2):
            c0 = h * PK_DQ + p * PK_DH
            s = lax.dot_general(keys_ref[p, h], q_ref[:, c0:c0 + PK_DH], (((1,), (1,)), ((), ())),
                                preferred_element_type=F32)
            v, i = _topk_rows(s, PK_TOPK)
            sv.append(v)
            si.append(i)
        cand = jnp.concatenate([sv[0][i:i + 1, :] + sv[1] for i in range(PK_TOPK)], axis=0)
        cidx = jnp.concatenate([si[0][i:i + 1, :] * N_KEYS + si[1] for i in range(PK_TOPK)], axis=0)
        best, eid = _topk_rows(cand, PK_TOPK, payload=cidx)
        ex = jnp.exp(best - best[0:1, :])
        g_rows.append(ex / jnp.sum(ex, axis=0, keepdims=True))
        e_rows.append(eid)
    eidx_ref[...] = jnp.concatenate(e_rows, axis=0).T
    gate_ref[...] = jnp.concatenate(g_rows, axis=0).T


def _pk_topk(q, keys_bf):
    t = q.shape[0]
    tt = TT_TOPK
    ne = PK_HEADS * PK_TOPK
    return pl.pallas_call(
        _topk_body,
        out_shape=(jax.ShapeDtypeStruct((t, ne), jnp.int32), jax.ShapeDtypeStruct((t, ne), F32)),
        grid=(t // tt,),
        in_specs=[pl.BlockSpec((tt, PK_HEADS * PK_DQ), lambda i: (i, 0)),
                  pl.BlockSpec((2, PK_HEADS, N_KEYS, PK_DH), lambda i: (0, 0, 0, 0))],
        out_specs=(pl.BlockSpec((tt, ne), lambda i: (i, 0)), pl.BlockSpec((tt, ne), lambda i: (i, 0))),
        compiler_params=_cparams("arbitrary"),
        name="pk_topk",
    )(q, keys_bf)


SC_LANES = 16
SC_CORES = 2
SC_SUBCORES = 16
SC_WORKERS = SC_CORES * SC_SUBCORES
SC_HALF = D_MODEL // 2
SC_ROW_CHUNKS = SC_HALF // SC_LANES
SC_TOKEN_BLOCK = 8


def _pack_bf16_pairs(tab):
    lo = lax.bitcast_convert_type(tab[:, SC_HALF:].astype(BF16), jnp.uint16).astype(jnp.uint32)
    bits = lax.bitcast_convert_type(tab[:, :SC_HALF], jnp.uint32)
    sign = bits & jnp.uint32(0x80000000)
    mag = (bits & jnp.uint32(0x7FFFFFFF)) + jnp.uint32(1 << 15)
    hi = jnp.where(mag >= lo, (mag - lo) >> 16, jnp.uint32(0))
    return lax.bitcast_convert_type(sign | (hi << 16) | lo, jnp.int32)


def _unpack_pair(w):
    return lax.bitcast_convert_type(w, F32), lax.bitcast_convert_type(w << 16, F32)


def _gelu_tanh(x):
    y = 0.7978845608028654 * (x + 0.044715 * (x * x * x))
    t = 1.0 - 2.0 / (jnp.exp(2.0 * y) + 1.0)
    return x * (0.5 * (1.0 + t))


def _peer_sc_body(h_hbm, idx_hbm, gate_hbm, u_hbm, v_hbm, out_hbm,
                  h_v, idx_v, gate_v, out_v, ub0, ub1, vb0, vb1, su0, su1, sv0, sv1, *, tokens_per_worker):
    nh, k, tb, lanes = PK_HEADS, PK_TOPK, SC_TOKEN_BLOCK, SC_LANES
    wid = lax.axis_index("s") * SC_CORES + lax.axis_index("c")
    ubufs, vbufs, sus, svs = (ub0, ub1), (vb0, vb1), (su0, su1), (sv0, sv1)
    lane = lax.iota(jnp.int32, lanes)
    zero = jnp.zeros((lanes,), F32)

    def start(tok, hd, par):
        irow = idx_v.at[tok * nh + hd]
        pltpu.async_copy(u_hbm.at[irow], ubufs[par], sus[par])
        pltpu.async_copy(v_hbm.at[irow], vbufs[par], svs[par])

    def wait(par):
        irow = idx_v.at[0]
        pltpu.make_async_copy(u_hbm.at[irow], ubufs[par], sus[par]).wait()
        pltpu.make_async_copy(v_hbm.at[irow], vbufs[par], svs[par]).wait()

    @pl.loop(0, tokens_per_worker // tb)
    def _(blk):
        base = wid * tokens_per_worker + blk * tb
        pltpu.sync_copy(h_hbm.at[pl.ds(base, tb)], h_v)
        pltpu.sync_copy(idx_hbm.at[pl.ds(base * nh, tb * nh)], idx_v)
        pltpu.sync_copy(gate_hbm.at[pl.ds(base * nh, tb * nh)], gate_v)
        start(0, 0, 0)

        @pl.loop(0, tb)
        def _(tok):
            for hd in range(nh):
                par = hd % 2
                if hd + 1 < nh:
                    start(tok, hd + 1, 1 - par)
                else:
                    @pl.when(tok + 1 < tb)
                    def _():
                        start(tok + 1, 0, 1 - par)
                wait(par)
                ub, vb = ubufs[par], vbufs[par]

                @plsc.parallel_loop(0, SC_ROW_CHUNKS, carry=(zero,) * k)
                def accs(c, acc):
                    ha = h_v[tok, pl.ds(c * lanes, lanes)]
                    hb = h_v[tok, pl.ds(SC_HALF + c * lanes, lanes)]
                    out = []
                    for r in range(k):
                        a, b = _unpack_pair(ub[r, pl.ds(c * lanes, lanes)])
                        out.append(acc[r] + (ha * a + hb * b))
                    return tuple(out)

                s_vec = zero
                for r in range(k):
                    s_vec = jnp.where(lane == r, jnp.sum(accs[r]), s_vec)
                w_vec = gate_v[tok * nh + hd, :] * _gelu_tanh(s_vec)
                wb = [jnp.sum(jnp.where(lane == r, w_vec, 0.0)) for r in range(k)]

                @plsc.parallel_loop(0, SC_ROW_CHUNKS)
                def _(c):
                    pa, pb = [], []
                    for r in range(k):
                        a, b = _unpack_pair(vb[r, pl.ds(c * lanes, lanes)])
                        pa.append(wb[r] * a)
                        pb.append(wb[r] * b)
                    while len(pa) > 1:
                        pa = [pa[i] + pa[i + 1] for i in range(0, len(pa), 2)]
                        pb = [pb[i] + pb[i + 1] for i in range(0, len(pb), 2)]
                    for off, o in ((0, pa[0]), (SC_HALF, pb[0])):
                        cols = pl.ds(off + c * lanes, lanes)
                        if hd == 0:
                            out_v[tok, cols] = o
                        else:
                            out_v[tok, cols] = out_v[tok, cols] + o

        pltpu.sync_copy(out_v, out_hbm.at[pl.ds(base, tb)])


def _peer_experts(h, eidx, gate, u_tab, v_tab):
    t, d = h.shape
    assert t % (SC_WORKERS * SC_TOKEN_BLOCK) == 0 and d == D_MODEL
    rows = pltpu.VMEM((PK_TOPK, SC_HALF), jnp.int32)
    return pl.kernel(
        functools.partial(_peer_sc_body, tokens_per_worker=t // SC_WORKERS),
        out_type=jax.ShapeDtypeStruct((t, d), F32),
        mesh=plsc.VectorSubcoreMesh(core_axis_name="c", subcore_axis_name="s"),
        scratch_types=[
            pltpu.VMEM((SC_TOKEN_BLOCK, d), F32),
            pltpu.VMEM((SC_TOKEN_BLOCK * PK_HEADS, PK_TOPK), jnp.int32),
            pltpu.VMEM((SC_TOKEN_BLOCK * PK_HEADS, PK_TOPK), F32),
            pltpu.VMEM((SC_TOKEN_BLOCK, d), F32),
            rows, rows, rows, rows,
            pltpu.SemaphoreType.DMA, pltpu.SemaphoreType.DMA,
            pltpu.SemaphoreType.DMA, pltpu.SemaphoreType.DMA,
        ],
        compiler_params=pltpu.CompilerParams(needs_layout_passes=False),
        name="peer_experts_sc",
    )(h, eidx, gate, u_tab, v_tab)


TM_FIN = 512


def _final_body(x1_ref, p_ref, m_ref, w_ref, y_ref):
    y_ref[...] = _rms(x1_ref[...] + m_ref[0, 5:6, :] * p_ref[...]) * w_ref[...]


def _final(x1, peer, mod, norm_f, rows_of):
    t, d = x1.shape
    tm = _token_tile(t, TM_FIN)
    row_of = rows_of(tm)
    tile = pl.BlockSpec((tm, d), lambda i: (i, 0))
    return pl.pallas_call(
        _final_body,
        out_shape=jax.ShapeDtypeStruct((t, d), F32),
        grid=(t // tm,),
        in_specs=[tile, tile, pl.BlockSpec((1, N_MOD, d), lambda i: (row_of(i), 0, 0)),
                  pl.BlockSpec((1, d), lambda i: (0, 0))],
        out_specs=tile,
        compiler_params=_cparams("arbitrary"),
        name="final_norm",
    )(x1, peer, mod, norm_f.reshape(1, d))


CTX_GROUPS = (1, 15, 16)
LAT_GROUP = 2


def _zero_of(a):
    return (lax.shift_right_logical(a[0, 0], 31) >> 1).astype(F32)


def _group(x, mod, row0, per_seq, n, hg_s0, lru_s0, col_major, p, after):
    seqs = hg_s0.shape[0]
    t = seqs * n
    for a in after:
        mod = mod + _zero_of(a)
    rows_of = lambda tm: _mod_row_map(tm, n, row0, per_seq)
    z = _in_proj(x, mod, p['norm1'], p['w_in'], rows_of)
    o_hg, hg_fin = _hgrn2(z, p['lb'], hg_s0, n, 0)
    hr, lru_fin = _rglru(z, COL_XR, 0, n, seqs, col_major, *p['lru'], lru_s0)
    x1, h2, q = _out_proj(x, o_hg, hr, z, mod, p['hg_norm'], p['norm2'], p['w_a'], p['w_b'], p['w_o'], p['w_q'],
                          rows_of(TM_OUT))
    eidx, gate = _pk_topk(q, p['keys'])
    peer = _peer_experts(h2, eidx.reshape(t * PK_HEADS, PK_TOPK), gate.reshape(t * PK_HEADS, PK_TOPK),
                         p['pk_u'], p['pk_v'])
    return _final(x1, peer, mod, p['norm_f'], rows_of), hg_fin, lru_fin, eidx


def kernel(x_prompt, x_sample, state_hgrn, state_rglru, c, c_ctx, w_mod, b_mod, norm1, w_in,
           hg_lb_logits, hg_norm, lru_conv_w, lru_conv_b, lru_w_r, lru_b_r, lru_w_i, lru_b_i,
           lru_lambda, w_branch_a, w_branch_b, w_out, norm2, pk_w_q, pk_sub_keys, pk_u, pk_v,
           norm_f):
    assert w_mod.shape[0] == 1, "single trunk layer"
    d = D_MODEL
    nb_ctx, n_ctx, _ = x_prompt.shape
    nb_lat, n_lat, _ = x_sample.shape
    assert nb_lat < MOD_ROWS and nb_lat % LAT_GROUP == 0 and nb_ctx == sum(CTX_GROUPS)
    ctx_row = nb_lat

    cond = jnp.zeros((MOD_ROWS, d), F32).at[:nb_lat].set(c).at[ctx_row].set(c_ctx)
    mod = _modulation(cond, w_mod[0], b_mod[0])
    p = {
        'lb': jnp.cumsum(jax.nn.softmax(hg_lb_logits.astype(F32), axis=1), axis=1)[:, 0],
        'norm1': norm1[0], 'w_in': w_in[0].astype(BF16), 'hg_norm': hg_norm[0], 'norm2': norm2[0],
        'lru': (lru_conv_w[0], lru_conv_b[0], _block_diag(lru_w_r[0]), _block_diag(lru_w_i[0]),
                lru_b_r[0], lru_b_i[0], lru_lambda[0]),
        'w_a': w_branch_a[0].astype(BF16), 'w_b': w_branch_b[0].astype(BF16), 'w_o': w_out[0].astype(BF16),
        'w_q': pk_w_q[0].astype(BF16), 'keys': pk_sub_keys[0].astype(BF16),
        'pk_u': _pack_bf16_pairs(pk_u[0]), 'pk_v': _pack_bf16_pairs(pk_v[0]), 'norm_f': norm_f,
    }
    y_ctx, hg_fin, lru_fin = [], [], []
    tables = [p['pk_u'], p['pk_v']]
    after = tables
    s0 = 0
    for g in CTX_GROUPS:
        y, hg, lr, eidx = _group(
            x_prompt[s0:s0 + g].reshape(g * n_ctx, d), mod, ctx_row, False, n_ctx,
            jnp.zeros((g, 2, HG_HEADS, HG_DK, HG_DV), F32), jnp.zeros((g, 2, LRU_W), F32), False, p, after)
        s0 += g
        after = [eidx] if after is tables else after
        y_ctx.append(y.reshape(g, n_ctx, d))
        hg_fin.append(hg)
        lru_fin.append(lr)
    y_lat = []
    for s0 in range(0, nb_lat, LAT_GROUP):
        sl = slice(s0, s0 + LAT_GROUP)
        y, _, _, _ = _group(x_sample[sl].reshape(LAT_GROUP * n_lat, d), mod, s0, True, n_lat,
                            state_hgrn[sl, 0], state_rglru[sl, 0], True, p, after)
        y_lat.append(y.reshape(LAT_GROUP, n_lat, d))
    return (jnp.concatenate(y_ctx, axis=0), jnp.concatenate(y_lat, axis=0),
            jnp.concatenate(hg_fin, axis=0)[:, None], jnp.concatenate(lru_fin, axis=0)[:, None])
```

```python
import functools

import jax, jax.numpy as jnp
from jax import lax
from jax.experimental import pallas as pl
from jax.experimental.pallas import tpu as pltpu
from jax.experimental.pallas import tpu_sc as plsc

D_MODEL = 1024
GRID_W = 64
EPS = 1e-6
HG_HEADS = 4
HG_DK = 128
HG_DV = 128
HG_KW = HG_HEADS * HG_DK
HG_VW = HG_HEADS * HG_DV
HG_CHUNK = 32
LRU_W = D_MODEL // 2
LRU_BLOCKS = 8
LRU_BW = LRU_W // LRU_BLOCKS
CONV_W = 4
LRU_C = 8.0
PK_HEADS = 8
N_KEYS = 128
PK_TOPK = 16
PK_DQ = 256
PK_DH = PK_DQ // 2
IN_W = 3 * HG_KW + 2 * HG_VW + 2 * LRU_W + 2 * D_MODEL
COL_G, COL_XR, COL_XG, COL_GA, COL_GB = 4, 5, 6, 7, 9
N_MOD = 6
MOD_ROWS = 16
F32 = jnp.float32
BF16 = jnp.bfloat16
VMEM_LIMIT = 48 * 1024 * 1024


def _cparams(*sem):
    return pltpu.CompilerParams(dimension_semantics=sem, vmem_limit_bytes=VMEM_LIMIT)


def _silu(x):
    return x * jax.nn.sigmoid(x)


def _rms(x):
    return x * lax.rsqrt(jnp.mean(x * x, axis=-1, keepdims=True) + EPS)


def _mod_body(c_ref, w_ref, b_ref, o_ref):
    o_ref[...] = lax.dot_general(_silu(c_ref[...]), w_ref[...], (((1,), (0,)), ((), ())),
                                 precision=lax.Precision.HIGHEST, preferred_element_type=F32) + b_ref[...]


def _modulation(cond, w_mod, b_mod):
    d = D_MODEL
    out = pl.pallas_call(
        _mod_body,
        out_shape=jax.ShapeDtypeStruct((MOD_ROWS, N_MOD * d), F32),
        grid=(N_MOD,),
        in_specs=[pl.BlockSpec((MOD_ROWS, d), lambda j: (0, 0)),
                  pl.BlockSpec((d, d), lambda j: (0, j)),
                  pl.BlockSpec((1, d), lambda j: (0, j))],
        out_specs=pl.BlockSpec((MOD_ROWS, d), lambda j: (0, j)),
        compiler_params=_cparams("arbitrary"),
        name="adaln_modulation",
    )(cond, w_mod, b_mod.reshape(1, N_MOD * d))
    return out.reshape(MOD_ROWS, N_MOD, d)


def _mod_row_map(tm, n, row0, per_seq):
    per = n // tm
    return (lambda i: row0 + i // per) if per_seq else (lambda i: row0)


TM_IN = 512
TOKEN_TILE_MIN = 256
TN_IN = IN_W // 2


def _in_body(x_ref, m_ref, n1_ref, w_ref, z_ref, h_scr):
    @pl.when(pl.program_id(1) == 0)
    def _():
        y = _rms(x_ref[...]) * n1_ref[...]
        h_scr[...] = (y * (1.0 + m_ref[0, 1:2, :]) + m_ref[0, 0:1, :]).astype(BF16)

    z_ref[...] = jnp.dot(h_scr[...], w_ref[...], preferred_element_type=F32)


def _token_tile(t, preferred):
    tm = preferred if t % preferred == 0 else TOKEN_TILE_MIN
    assert t % tm == 0
    return tm


def _in_proj(x, mod, norm1, w_in_bf, rows_of):
    t, d = x.shape
    tm = _token_tile(t, TM_IN)
    row_of = rows_of(tm)
    return pl.pallas_call(
        _in_body,
        out_shape=jax.ShapeDtypeStruct((t, IN_W), F32),
        grid=(t // tm, IN_W // TN_IN),
        in_specs=[pl.BlockSpec((tm, d), lambda i, j: (i, 0)),
                  pl.BlockSpec((1, N_MOD, d), lambda i, j: (row_of(i), 0, 0)),
                  pl.BlockSpec((1, d), lambda i, j: (0, 0)),
                  pl.BlockSpec((d, TN_IN), lambda i, j: (0, j))],
        out_specs=pl.BlockSpec((tm, TN_IN), lambda i, j: (i, j)),
        scratch_shapes=[pltpu.VMEM((tm, d), BF16)],
        compiler_params=_cparams("arbitrary", "arbitrary"),
        name="in_proj",
    )(x, mod, norm1.reshape(1, d), w_in_bf)


def _hg_body(q_ref, ff_ref, fb_ref, v_ref, lb_ref, s0_ref, o_ref, sfin_ref, st_scr, ob_scr, *, n):
    c = HG_CHUNK
    nc = n // c
    row = lax.broadcasted_iota(jnp.int32, (c, c), 0)
    col = lax.broadcasted_iota(jnp.int32, (c, c), 1)
    lower = row >= col
    tri = (lower.astype(F32), (row <= col).astype(F32))
    masks = (lower, row <= col)
    f_refs = (ff_ref, fb_ref)
    for d in range(2):
        st_scr[d] = s0_ref[0, d, 0].T

    def chunk(ci, d):
        r = pl.ds(pl.multiple_of(ci * c, c), c)
        q = _silu(q_ref[r, :])
        v = v_ref[r, :].astype(BF16)
        lb = lb_ref[d:d + 1, :]
        f = lb + (1.0 - lb) * jax.nn.sigmoid(f_refs[d][r, :])
        k = 1.0 - f
        lf = jnp.log(f)
        cum = lax.dot_general(tri[d], lf, (((1,), (0,)), ((), ())),
                              precision=lax.Precision.HIGHEST, preferred_element_type=F32)
        tot = jnp.sum(lf, axis=0, keepdims=True)
        q_dec = (q * jnp.exp(cum)).astype(BF16)
        k_inv = (k * jnp.exp(-cum)).astype(BF16)
        k_end = (k * jnp.exp(tot - cum)).astype(BF16)
        att = lax.dot_general(q_dec, k_inv, (((1,), (1,)), ((), ())), preferred_element_type=F32)
        att = jnp.where(masks[d], att, 0.0).astype(BF16)
        st = st_scr[d]
        o = (jnp.dot(att, v, preferred_element_type=F32)
             + lax.dot_general(q_dec, st.astype(BF16), (((1,), (1,)), ((), ())), preferred_element_type=F32))
        ds_t = lax.dot_general(v, k_end, (((0,), (0,)), ((), ())), preferred_element_type=F32)
        st_scr[d] = st * jnp.exp(tot) + ds_t
        return r, o

    def step(i, carry):
        r, o = chunk(i, 0)
        o_ref[r, :] = o
        r, o = chunk(nc - 1 - i, 1)
        ob_scr[r, :] = o
        return carry

    lax.fori_loop(0, nc, step, 0)
    o_ref[...] = o_ref[...] + ob_scr[...]
    for d in range(2):
        sfin_ref[0, d, 0] = st_scr[d].T


def _hgrn2(z, lb, s0, n, row_block0):
    seqs = s0.shape[0]
    hb = HG_KW // HG_DK

    def zspec(col0):
        return pl.BlockSpec((n, HG_DK), lambda s, h: (row_block0 + s, col0 + h))

    st_spec = pl.BlockSpec((1, 2, 1, HG_DK, HG_DV), lambda s, h: (s, 0, h, 0, 0))
    return pl.pallas_call(
        functools.partial(_hg_body, n=n),
        out_shape=(jax.ShapeDtypeStruct((seqs * n, HG_VW), F32),
                   jax.ShapeDtypeStruct((seqs, 2, HG_HEADS, HG_DK, HG_DV), F32)),
        grid=(seqs, HG_HEADS),
        in_specs=[zspec(0), zspec(hb), zspec(2 * hb), zspec(3 * hb),
                  pl.BlockSpec((2, HG_DK), lambda s, h: (0, h)), st_spec],
        out_specs=(pl.BlockSpec((n, HG_DV), lambda s, h: (s, h)), st_spec),
        scratch_shapes=[pltpu.VMEM((2, HG_DV, HG_DK), F32), pltpu.VMEM((n, HG_DV), F32)],
        compiler_params=_cparams("arbitrary", "arbitrary"),
        name="hgrn2_scan",
    )(z, z, z, z, lb, s0)


LRU_RB = 64
HALO = 8


LANES = 128


def _lru_body(x_ref, cw_ref, cb_ref, wr_ref, wi_ref, br_ref, bi_ref, lam_ref, h0_ref, hr_ref, e_ref,
              xp_scr, xc_scr, *cm_scr, n, col_major):
    rb = LRU_RB
    nb = n // rb
    w = LRU_W
    grid_rows = n // GRID_W
    slabs = w // LANES
    zeros = jnp.zeros((HALO, w), F32)
    xp_scr[0:HALO, :] = zeros
    xp_scr[HALO + n:HALO + n + HALO, :] = zeros
    if col_major:
        st_scr, hcm_scr = cm_scr
        for j in range(slabs):
            st_scr[j] = x_ref[:, j * LANES:(j + 1) * LANES]

        def to_col_major(c, carry):
            dst = pl.ds(pl.multiple_of(HALO + c * grid_rows, 8), grid_rows)
            for j in range(slabs):
                xp_scr[dst, j * LANES:(j + 1) * LANES] = st_scr[j, pl.ds(c, grid_rows, stride=GRID_W), :]
            return carry

        lax.fori_loop(0, GRID_W, to_col_major, 0)
        h_dst = hcm_scr
    else:
        xp_scr[HALO:HALO + n, :] = x_ref[...]
        h_dst = hr_ref
    cw = cw_ref[...]
    cb = cb_ref[...]

    def conv_blk(b, carry):
        base = pl.multiple_of(b * rb, rb)
        xh = xp_scr[pl.ds(base, rb + 2 * HALO), :]
        ext = rb + 2 * HALO
        acc = cb + cw[2:3, :] * xh[HALO:HALO + rb]
        acc = acc + cw[0:1, :] * pltpu.roll(xh, 2, axis=0)[HALO:HALO + rb]
        acc = acc + cw[1:2, :] * pltpu.roll(xh, 1, axis=0)[HALO:HALO + rb]
        acc = acc + cw[3:4, :] * pltpu.roll(xh, ext - 1, axis=0)[HALO:HALO + rb]
        xc_scr[pl.ds(base, rb), :] = acc
        return carry

    lax.fori_loop(0, nb, conv_blk, 0)

    rows = lax.broadcasted_iota(jnp.int32, (rb, w), 0)

    def gates(blk, d):
        r = pl.ds(pl.multiple_of(blk * rb, rb), rb)
        xc = xc_scr[r, :]
        xb = xc.astype(BF16)
        rg = jax.nn.sigmoid(jnp.dot(xb, wr_ref[d], preferred_element_type=F32) + br_ref[d:d + 1, :])
        ig = jax.nn.sigmoid(jnp.dot(xb, wi_ref[d], preferred_element_type=F32) + bi_ref[d:d + 1, :])
        log_a = (-LRU_C) * jax.nn.softplus(-lam_ref[d:d + 1, :]) * rg
        a = jnp.exp(log_a)
        u = jnp.sqrt(-jnp.tanh(log_a) * (a * a + 1.0)) * (ig * xc)
        return r, a, u

    def fwd_blk(blk, h_prev):
        r, a, u = gates(blk, 0)
        s = 1
        while s < rb:
            keep = rows >= s
            a_sh = jnp.where(keep, pltpu.roll(a, s, axis=0), 1.0)
            u_sh = jnp.where(keep, pltpu.roll(u, s, axis=0), 0.0)
            u = a * u_sh + u
            a = a * a_sh
            s *= 2
        h = u + a * h_prev
        h_dst[r, :] = h
        return h[rb - 1:rb, :]

    def bwd_blk(i, h_next):
        r, a, u = gates(nb - 1 - i, 1)
        s = 1
        while s < rb:
            keep = rows < rb - s
            a_sh = jnp.where(keep, pltpu.roll(a, rb - s, axis=0), 1.0)
            u_sh = jnp.where(keep, pltpu.roll(u, rb - s, axis=0), 0.0)
            u = a * u_sh + u
            a = a * a_sh
            s *= 2
        h = u + a * h_next
        h_dst[r, :] = h_dst[r, :] + h
        return h[0:1, :]

    e_ref[0, 0:1, :] = lax.fori_loop(0, nb, fwd_blk, h0_ref[0, 0:1, :])
    e_ref[0, 1:2, :] = lax.fori_loop(0, nb, bwd_blk, h0_ref[0, 1:2, :])
    if col_major:
        def to_row_major(c, carry):
            src = pl.ds(pl.multiple_of(c * grid_rows, 8), grid_rows)
            for j in range(slabs):
                st_scr[j, pl.ds(c, grid_rows, stride=GRID_W), :] = hcm_scr[src, j * LANES:(j + 1) * LANES]
            return carry

        lax.fori_loop(0, GRID_W, to_row_major, 0)
        for j in range(slabs):
            hr_ref[:, j * LANES:(j + 1) * LANES] = st_scr[j]


def _rglru(x, col_block, row_block0, n, seqs, col_major, conv_w, conv_b, wr_bd, wi_bd, b_r, b_i, lam, h0):
    w = LRU_W
    cm_scratch = [pltpu.VMEM((w // LANES, n, LANES), F32), pltpu.VMEM((n, w), F32)] if col_major else []
    full2 = lambda s: (0, 0)
    full3 = lambda s: (0, 0, 0)
    return pl.pallas_call(
        functools.partial(_lru_body, n=n, col_major=col_major),
        out_shape=(jax.ShapeDtypeStruct((seqs * n, w), F32), jax.ShapeDtypeStruct((seqs, 2, w), F32)),
        grid=(seqs,),
        in_specs=[pl.BlockSpec((n, w), lambda s: (row_block0 + s, col_block)),
                  pl.BlockSpec((CONV_W, w), full2), pl.BlockSpec((1, w), full2),
                  pl.BlockSpec((2, w, w), full3), pl.BlockSpec((2, w, w), full3),
                  pl.BlockSpec((2, w), full2), pl.BlockSpec((2, w), full2), pl.BlockSpec((2, w), full2),
                  pl.BlockSpec((1, 2, w), lambda s: (s, 0, 0))],
        out_specs=(pl.BlockSpec((n, w), lambda s: (s, 0)), pl.BlockSpec((1, 2, w), lambda s: (s, 0, 0))),
        scratch_shapes=[pltpu.VMEM((n + 2 * HALO, w), F32), pltpu.VMEM((n, w), F32)] + cm_scratch,
        compiler_params=_cparams("arbitrary"),
        name="rglru",
    )(x, conv_w, conv_b.reshape(1, w), wr_bd, wi_bd, b_r, b_i, lam, h0)


def _block_diag(wg):
    eye = jnp.eye(LRU_BLOCKS, dtype=wg.dtype)
    dense = wg[:, :, :, None, :] * eye[None, :, None, :, None]
    return dense.reshape(2, LRU_W, LRU_W).astype(BF16)


TM_OUT = 256


def _out_body(x_ref, o_ref, hr_ref, g_ref, xg_ref, ga0_ref, ga1_ref, gb0_ref, gb1_ref, m_ref,
              hgn_ref, n2_ref, wa_ref, wb_ref, wo_ref, wq_ref, x1_ref, h2_ref, q_ref):
    o = o_ref[...]
    parts = []
    for h in range(HG_HEADS):
        parts.append(_rms(o[:, h * HG_DV:(h + 1) * HG_DV]) * hgn_ref[...])
    on = jnp.concatenate(parts, axis=1) * _silu(g_ref[...])
    y_a = jnp.dot(on.astype(BF16), wa_ref[...], preferred_element_type=F32)
    y_b = jnp.dot((hr_ref[...] * jax.nn.gelu(xg_ref[...])).astype(BF16), wb_ref[...], preferred_element_type=F32)
    ga = jnp.concatenate([ga0_ref[...], ga1_ref[...]], axis=1)
    gb = jnp.concatenate([gb0_ref[...], gb1_ref[...]], axis=1)
    merged = jax.nn.sigmoid(ga) * y_a + jax.nn.sigmoid(gb) * y_b
    mix = jnp.dot(merged.astype(BF16), wo_ref[...], preferred_element_type=F32)
    x1 = x_ref[...] + m_ref[0, 2:3, :] * mix
    x1_ref[...] = x1
    h2 = _rms(x1) * n2_ref[...] * (1.0 + m_ref[0, 4:5, :]) + m_ref[0, 3:4, :]
    h2_ref[...] = h2
    q_ref[...] = jnp.dot(h2.astype(BF16), wq_ref[...], preferred_element_type=F32).astype(BF16)


def _out_proj(x, o_hg, hr, z, mod, hg_norm, norm2, wa, wb, wo, wq, row_of):
    t, d = x.shape
    tm = TM_OUT
    nq = PK_HEADS * PK_DQ
    half = lambda i: (i, 0)
    zc = lambda c: pl.BlockSpec((tm, LRU_W), lambda i: (i, c))
    const = lambda i: (0, 0)
    return pl.pallas_call(
        _out_body,
        out_shape=(jax.ShapeDtypeStruct((t, d), F32), jax.ShapeDtypeStruct((t, d), F32),
                   jax.ShapeDtypeStruct((t, nq), BF16)),
        grid=(t // tm,),
        in_specs=[pl.BlockSpec((tm, d), half), pl.BlockSpec((tm, HG_VW), half), pl.BlockSpec((tm, LRU_W), half),
                  zc(COL_G), zc(COL_XG), zc(COL_GA), zc(COL_GA + 1), zc(COL_GB), zc(COL_GB + 1),
                  pl.BlockSpec((1, N_MOD, d), lambda i: (row_of(i), 0, 0)),
                  pl.BlockSpec((1, HG_DV), const), pl.BlockSpec((1, d), const),
                  pl.BlockSpec((HG_VW, d), const), pl.BlockSpec((LRU_W, d), const),
                  pl.BlockSpec((d, d), const), pl.BlockSpec((d, nq), const)],
        out_specs=(pl.BlockSpec((tm, d), half), pl.BlockSpec((tm, d), half), pl.BlockSpec((tm, nq), half)),
        compiler_params=_cparams("arbitrary"),
        name="out_proj",
    )(x, o_hg, hr, z, z, z, z, z, z, mod, hg_norm.reshape(1, HG_DV), norm2.reshape(1, d), wa, wb, wo, wq)


TT_TOPK = 128


def _topk_rows(s, k, payload=None):
    r = s.shape[0]
    rid = lax.broadcasted_iota(jnp.int32, s.shape, 0)
    vals, picks = [], []
    for _ in range(k):
        m = jnp.max(s, axis=0, keepdims=True)
        idx = jnp.min(jnp.where(s == m, rid, r), axis=0, keepdims=True)
        hit = rid == idx
        vals.append(m)
        picks.append(idx if payload is None else jnp.sum(jnp.where(hit, payload, 0), axis=0, keepdims=True))
        s = jnp.where(hit, -jnp.inf, s)
    return jnp.concatenate(vals, axis=0), jnp.concatenate(picks, axis=0)


def _topk_body(q_ref, keys_ref, eidx_ref, gate_ref):
    e_rows, g_rows = [], []
    for h in range(PK_HEADS):
        sv, si = [], []
        for p in range(2):
            c0 = h * PK_DQ + p * PK_DH
            s = lax.dot_general(keys_ref[p, h], q_ref[:, c0:c0 + PK_DH], (((1,), (1,)), ((), ())),
                                preferred_element_type=F32)
            v, i = _topk_rows(s, PK_TOPK)
            sv.append(v)
            si.append(i)
        cand = jnp.concatenate([sv[0][i:i + 1, :] + sv[1] for i in range(PK_TOPK)], axis=0)
        cidx = jnp.concatenate([si[0][i:i + 1, :] * N_KEYS + si[1] for i in range(PK_TOPK)], axis=0)
        best, eid = _topk_rows(cand, PK_TOPK, payload=cidx)
        ex = jnp.exp(best - best[0:1, :])
        g_rows.append(ex / jnp.sum(ex, axis=0, keepdims=True))
        e_rows.append(eid)
    eidx_ref[...] = jnp.concatenate(e_rows, axis=0).T
    gate_ref[...] = jnp.concatenate(g_rows, axis=0).T


def _pk_topk(q, keys_bf):
    t = q.shape[0]
    tt = TT_TOPK
    ne = PK_HEADS * PK_TOPK
    return pl.pallas_call(
        _topk_body,
        out_shape=(jax.ShapeDtypeStruct((t, ne), jnp.int32), jax.ShapeDtypeStruct((t, ne), F32)),
        grid=(t // tt,),
        in_specs=[pl.BlockSpec((tt, PK_HEADS * PK_DQ), lambda i: (i, 0)),
                  pl.BlockSpec((2, PK_HEADS, N_KEYS, PK_DH), lambda i: (0, 0, 0, 0))],
        out_specs=(pl.BlockSpec((tt, ne), lambda i: (i, 0)), pl.BlockSpec((tt, ne), lambda i: (i, 0))),
        compiler_params=_cparams("arbitrary"),
        name="pk_topk",
    )(q, keys_bf)


SC_LANES = 16
SC_CORES = 2
SC_SUBCORES = 16
SC_WORKERS = SC_CORES * SC_SUBCORES
SC_HALF = D_MODEL // 2
SC_ROW_CHUNKS = SC_HALF // SC_LANES
SC_TOKEN_BLOCK = 8


def _pack_bf16_pairs(tab):
    lo = lax.bitcast_convert_type(tab[:, SC_HALF:].astype(BF16), jnp.uint16).astype(jnp.uint32)
    bits = lax.bitcast_convert_type(tab[:, :SC_HALF], jnp.uint32)
    sign = bits & jnp.uint32(0x80000000)
    mag = (bits & jnp.uint32(0x7FFFFFFF)) + jnp.uint32(1 << 15)
    hi = jnp.where(mag >= lo, (mag - lo) >> 16, jnp.uint32(0))
    return lax.bitcast_convert_type(sign | (hi << 16) | lo, jnp.int32)


def _unpack_pair(w):
    return lax.bitcast_convert_type(w, F32), lax.bitcast_convert_type(w << 16, F32)


def _gelu_tanh(x):
    y = 0.7978845608028654 * (x + 0.044715 * (x * x * x))
    t = 1.0 - 2.0 / (jnp.exp(2.0 * y) + 1.0)
    return x * (0.5 * (1.0 + t))


def _peer_sc_body(h_hbm, idx_hbm, gate_hbm, u_hbm, v_hbm, out_hbm,
                  h_v, idx_v, gate_v, out_v, ub0, ub1, vb0, vb1, su0, su1, sv0, sv1, *, tokens_per_worker):
    nh, k, tb, lanes = PK_HEADS, PK_TOPK, SC_TOKEN_BLOCK, SC_LANES
    wid = lax.axis_index("s") * SC_CORES + lax.axis_index("c")
    ubufs, vbufs, sus, svs = (ub0, ub1), (vb0, vb1), (su0, su1), (sv0, sv1)
    lane = lax.iota(jnp.int32, lanes)
    zero = jnp.zeros((lanes,), F32)

    def start(tok, hd, par):
        irow = idx_v.at[tok * nh + hd]
        pltpu.async_copy(u_hbm.at[irow], ubufs[par], sus[par])
        pltpu.async_copy(v_hbm.at[irow], vbufs[par], svs[par])

    def wait(par):
        irow = idx_v.at[0]
        pltpu.make_async_copy(u_hbm.at[irow], ubufs[par], sus[par]).wait()
        pltpu.make_async_copy(v_hbm.at[irow], vbufs[par], svs[par]).wait()

    @pl.loop(0, tokens_per_worker // tb)
    def _(blk):
        base = wid * tokens_per_worker + blk * tb
        pltpu.sync_copy(h_hbm.at[pl.ds(base, tb)], h_v)
        pltpu.sync_copy(idx_hbm.at[pl.ds(base * nh, tb * nh)], idx_v)
        pltpu.sync_copy(gate_hbm.at[pl.ds(base * nh, tb * nh)], gate_v)
        start(0, 0, 0)

        @pl.loop(0, tb)
        def _(tok):
            for hd in range(nh):
                par = hd % 2
                if hd + 1 < nh:
                    start(tok, hd + 1, 1 - par)
                else:
                    @pl.when(tok + 1 < tb)
                    def _():
                        start(tok + 1, 0, 1 - par)
                wait(par)
                ub, vb = ubufs[par], vbufs[par]

                @plsc.parallel_loop(0, SC_ROW_CHUNKS, carry=(zero,) * k)
                def accs(c, acc):
                    ha = h_v[tok, pl.ds(c * lanes, lanes)]
                    hb = h_v[tok, pl.ds(SC_HALF + c * lanes, lanes)]
                    out = []
                    for r in range(k):
                        a, b = _unpack_pair(ub[r, pl.ds(c * lanes, lanes)])
                        out.append(acc[r] + (ha * a + hb * b))
                    return tuple(out)

                s_vec = zero
                for r in range(k):
                    s_vec = jnp.where(lane == r, jnp.sum(accs[r]), s_vec)
                w_vec = gate_v[tok * nh + hd, :] * _gelu_tanh(s_vec)
                wb = [jnp.sum(jnp.where(lane == r, w_vec, 0.0)) for r in range(k)]

                @plsc.parallel_loop(0, SC_ROW_CHUNKS)
                def _(c):
                    pa, pb = [], []
                    for r in range(k):
                        a, b = _unpack_pair(vb[r, pl.ds(c * lanes, lanes)])
                        pa.append(wb[r] * a)
                        pb.append(wb[r] * b)
                    while len(pa) > 1:
                        pa = [pa[i] + pa[i + 1] for i in range(0, len(pa), 2)]
                        pb = [pb[i] + pb[i + 1] for i in range(0, len(pb), 2)]
                    for off, o in ((0, pa[0]), (SC_HALF, pb[0])):
                        cols = pl.ds(off + c * lanes, lanes)
                        if hd == 0:
                            out_v[tok, cols] = o
                        else:
                            out_v[tok, cols] = out_v[tok, cols] + o

        pltpu.sync_copy(out_v, out_hbm.at[pl.ds(base, tb)])


def _peer_experts(h, eidx, gate, u_tab, v_tab):
    t, d = h.shape
    assert t % (SC_WORKERS * SC_TOKEN_BLOCK) == 0 and d == D_MODEL
    rows = pltpu.VMEM((PK_TOPK, SC_HALF), jnp.int32)
    return pl.kernel(
        functools.partial(_peer_sc_body, tokens_per_worker=t // SC_WORKERS),
        out_type=jax.ShapeDtypeStruct((t, d), F32),
        mesh=plsc.VectorSubcoreMesh(core_axis_name="c", subcore_axis_name="s"),
        scratch_types=[
            pltpu.VMEM((SC_TOKEN_BLOCK, d), F32),
            pltpu.VMEM((SC_TOKEN_BLOCK * PK_HEADS, PK_TOPK), jnp.int32),
            pltpu.VMEM((SC_TOKEN_BLOCK * PK_HEADS, PK_TOPK), F32),
            pltpu.VMEM((SC_TOKEN_BLOCK, d), F32),
            rows, rows, rows, rows,
            pltpu.SemaphoreType.DMA, pltpu.SemaphoreType.DMA,
            pltpu.SemaphoreType.DMA, pltpu.SemaphoreType.DMA,
        ],
        compiler_params=pltpu.CompilerParams(needs_layout_passes=False),
        name="peer_experts_sc",
    )(h, eidx, gate, u_tab, v_tab)


TT_DENSE = 256
EB_DENSE = 512
N_EXPERTS = N_KEYS * N_KEYS
N_SEL = PK_HEADS * PK_TOPK


def _dense_body(h_ref, e_ref, g_ref, u_ref, v_ref, o_ref, g_scr, acc_scr):
    eb = pl.program_id(1)
    tt = h_ref.shape[0]
    slabs = EB_DENSE // N_KEYS

    @pl.when(eb == 0)
    def _():
        acc_scr[...] = jnp.zeros_like(acc_scr)
        sub = lax.broadcasted_iota(jnp.int32, (N_KEYS, N_SEL), 0)

        def per_token(t, carry):
            e = e_ref[pl.ds(t, 1), :]
            p_t = jnp.where(sub == (e >> 7), g_ref[pl.ds(t, 1), :], 0.0).astype(BF16)
            q_t = (sub == (e & (N_KEYS - 1))).astype(BF16)
            g_scr[pl.ds(pl.multiple_of(t * N_KEYS, N_KEYS), N_KEYS), :] = lax.dot_general(
                p_t, q_t, (((1,), (1,)), ((), ())), preferred_element_type=F32)
            return carry

        lax.fori_loop(0, tt, per_token, 0)

    s = lax.dot_general(h_ref[...].astype(BF16), u_ref[...], (((1,), (1,)), ((), ())), preferred_element_type=F32)
    gs = jnp.concatenate([g_scr[pl.ds(eb * slabs + k, tt, stride=N_KEYS), :] for k in range(slabs)], axis=1)
    acc_scr[...] += jnp.dot((gs * jax.nn.gelu(s)).astype(BF16), v_ref[...], preferred_element_type=F32)

    @pl.when(eb == pl.num_programs(1) - 1)
    def _():
        o_ref[...] = acc_scr[...]


def _peer_dense(h, eidx, gate, u_bf, v_bf):
    t, d = h.shape
    tt = TT_DENSE
    assert t % tt == 0 and N_KEYS == 128
    tile = lambda i, e: (i, 0)
    blk = lambda i, e: (e, 0)
    return pl.pallas_call(
        _dense_body,
        out_shape=jax.ShapeDtypeStruct((t, d), F32),
        grid=(t // tt, N_EXPERTS // EB_DENSE),
        in_specs=[pl.BlockSpec((tt, d), tile), pl.BlockSpec((tt, N_SEL), tile), pl.BlockSpec((tt, N_SEL), tile),
                  pl.BlockSpec((EB_DENSE, d), blk), pl.BlockSpec((EB_DENSE, d), blk)],
        out_specs=pl.BlockSpec((tt, d), tile),
        scratch_shapes=[pltpu.VMEM((tt * N_KEYS, N_KEYS), F32), pltpu.VMEM((tt, d), F32)],
        compiler_params=_cparams("arbitrary", "arbitrary"),
        name="peer_dense_tc",
    )(h, eidx, gate, u_bf, v_bf)


TM_FIN = 512


def _final_body(x1_ref, p_ref, m_ref, w_ref, y_ref):
    y_ref[...] = _rms(x1_ref[...] + m_ref[0, 5:6, :] * p_ref[...]) * w_ref[...]


def _final(x1, peer, mod, norm_f, rows_of):
    t, d = x1.shape
    tm = _token_tile(t, TM_FIN)
    row_of = rows_of(tm)
    tile = pl.BlockSpec((tm, d), lambda i: (i, 0))
    return pl.pallas_call(
        _final_body,
        out_shape=jax.ShapeDtypeStruct((t, d), F32),
        grid=(t // tm,),
        in_specs=[tile, tile, pl.BlockSpec((1, N_MOD, d), lambda i: (row_of(i), 0, 0)),
                  pl.BlockSpec((1, d), lambda i: (0, 0))],
        out_specs=tile,
        compiler_params=_cparams("arbitrary"),
        name="final_norm",
    )(x1, peer, mod, norm_f.reshape(1, d))


CTX_GROUPS = (1, 15, 16)
LAT_GROUP = 2
DENSE_LAT_SEQS = 2


def _zero_of(a):
    return (lax.shift_right_logical(a[0, 0], 31) >> 1).astype(F32)


def _group(x, mod, row0, per_seq, n, hg_s0, lru_s0, col_major, p, after, dense=False):
    seqs = hg_s0.shape[0]
    t = seqs * n
    for a in after:
        mod = mod + _zero_of(a)
    rows_of = lambda tm: _mod_row_map(tm, n, row0, per_seq)
    z = _in_proj(x, mod, p['norm1'], p['w_in'], rows_of)
    o_hg, hg_fin = _hgrn2(z, p['lb'], hg_s0, n, 0)
    hr, lru_fin = _rglru(z, COL_XR, 0, n, seqs, col_major, *p['lru'], lru_s0)
    x1, h2, q = _out_proj(x, o_hg, hr, z, mod, p['hg_norm'], p['norm2'], p['w_a'], p['w_b'], p['w_o'], p['w_q'],
                          rows_of(TM_OUT))
    eidx, gate = _pk_topk(q, p['keys'])
    if dense:
        peer = _peer_dense(h2, eidx, gate, p['u_bf'], p['v_bf'])
    else:
        peer = _peer_experts(h2, eidx.reshape(t * PK_HEADS, PK_TOPK), gate.reshape(t * PK_HEADS, PK_TOPK),
                             p['pk_u'], p['pk_v'])
    return _final(x1, peer, mod, p['norm_f'], rows_of), hg_fin, lru_fin, eidx


def kernel(x_prompt, x_sample, state_hgrn, state_rglru, c, c_ctx, w_mod, b_mod, norm1, w_in,
           hg_lb_logits, hg_norm, lru_conv_w, lru_conv_b, lru_w_r, lru_b_r, lru_w_i, lru_b_i,
           lru_lambda, w_branch_a, w_branch_b, w_out, norm2, pk_w_q, pk_sub_keys, pk_u, pk_v,
           norm_f):
    assert w_mod.shape[0] == 1, "single trunk layer"
    d = D_MODEL
    nb_ctx, n_ctx, _ = x_prompt.shape
    nb_lat, n_lat, _ = x_sample.shape
    assert nb_lat < MOD_ROWS and nb_lat % LAT_GROUP == 0 and nb_ctx == sum(CTX_GROUPS)
    ctx_row = nb_lat

    cond = jnp.zeros((MOD_ROWS, d), F32).at[:nb_lat].set(c).at[ctx_row].set(c_ctx)
    mod = _modulation(cond, w_mod[0], b_mod[0])
    p = {
        'lb': jnp.cumsum(jax.nn.softmax(hg_lb_logits.astype(F32), axis=1), axis=1)[:, 0],
        'norm1': norm1[0], 'w_in': w_in[0].astype(BF16), 'hg_norm': hg_norm[0], 'norm2': norm2[0],
        'lru': (lru_conv_w[0], lru_conv_b[0], _block_diag(lru_w_r[0]), _block_diag(lru_w_i[0]),
                lru_b_r[0], lru_b_i[0], lru_lambda[0]),
        'w_a': w_branch_a[0].astype(BF16), 'w_b': w_branch_b[0].astype(BF16), 'w_o': w_out[0].astype(BF16),
        'w_q': pk_w_q[0].astype(BF16), 'keys': pk_sub_keys[0].astype(BF16),
        'pk_u': _pack_bf16_pairs(pk_u[0]), 'pk_v': _pack_bf16_pairs(pk_v[0]), 'norm_f': norm_f,
        'u_bf': pk_u[0].astype(BF16), 'v_bf': pk_v[0].astype(BF16),
    }
    y_ctx, hg_fin, lru_fin = [], [], []
    tables = [p['pk_u'], p['pk_v']]
    after = tables
    s0 = 0
    for g in CTX_GROUPS:
        y, hg, lr, eidx = _group(
            x_prompt[s0:s0 + g].reshape(g * n_ctx, d), mod, ctx_row, False, n_ctx,
            jnp.zeros((g, 2, HG_HEADS, HG_DK, HG_DV), F32), jnp.zeros((g, 2, LRU_W), F32), False, p, after)
        s0 += g
        after = [eidx] if after is tables else after
        y_ctx.append(y.reshape(g, n_ctx, d))
        hg_fin.append(hg)
        lru_fin.append(lr)
    y_lat = []
    for s0 in range(0, nb_lat, LAT_GROUP):
        sl = slice(s0, s0 + LAT_GROUP)
        y, _, _, _ = _group(x_sample[sl].reshape(LAT_GROUP * n_lat, d), mod, s0, True, n_lat,
                            state_hgrn[sl, 0], state_rglru[sl, 0], True, p, after,
                            dense=s0 >= nb_lat - DENSE_LAT_SEQS)
        y_lat.append(y.reshape(LAT_GROUP, n_lat, d))
    return (jnp.concatenate(y_ctx, axis=0), jnp.concatenate(y_lat, axis=0),
            jnp.concatenate(hg_fin, axis=0)[:, None], jnp.concatenate(lru_fin, axis=0)[:, None])
```

```python
import functools

import jax, jax.numpy as jnp
from jax import lax
from jax.experimental import pallas as pl
from jax.experimental.pallas import tpu as pltpu
from jax.experimental.pallas import tpu_sc as plsc

D_MODEL = 1024
GRID_W = 64
EPS = 1e-6
HG_HEADS = 4
HG_DK = 128
HG_DV = 128
HG_KW = HG_HEADS * HG_DK
HG_VW = HG_HEADS * HG_DV
HG_CHUNK = 32
LRU_W = D_MODEL // 2
LRU_BLOCKS = 8
LRU_BW = LRU_W // LRU_BLOCKS
CONV_W = 4
LRU_C = 8.0
PK_HEADS = 8
N_KEYS = 128
PK_TOPK = 16
PK_DQ = 256
PK_DH = PK_DQ // 2
IN_W = 3 * HG_KW + 2 * HG_VW + 2 * LRU_W + 2 * D_MODEL
COL_G, COL_XR, COL_XG, COL_GA, COL_GB = 4, 5, 6, 7, 9
N_MOD = 6
MOD_ROWS = 16
F32 = jnp.float32
BF16 = jnp.bfloat16
VMEM_LIMIT = 48 * 1024 * 1024


def _cparams(*sem):
    return pltpu.CompilerParams(dimension_semantics=sem, vmem_limit_bytes=VMEM_LIMIT)


def _silu(x):
    return x * jax.nn.sigmoid(x)


def _rms(x):
    return x * lax.rsqrt(jnp.mean(x * x, axis=-1, keepdims=True) + EPS)


def _mod_body(c_ref, w_ref, b_ref, o_ref):
    o_ref[...] = lax.dot_general(_silu(c_ref[...]), w_ref[...], (((1,), (0,)), ((), ())),
                                 precision=lax.Precision.HIGHEST, preferred_element_type=F32) + b_ref[...]


def _modulation(cond, w_mod, b_mod):
    d = D_MODEL
    out = pl.pallas_call(
        _mod_body,
        out_shape=jax.ShapeDtypeStruct((MOD_ROWS, N_MOD * d), F32),
        grid=(N_MOD,),
        in_specs=[pl.BlockSpec((MOD_ROWS, d), lambda j: (0, 0)),
                  pl.BlockSpec((d, d), lambda j: (0, j)),
                  pl.BlockSpec((1, d), lambda j: (0, j))],
        out_specs=pl.BlockSpec((MOD_ROWS, d), lambda j: (0, j)),
        compiler_params=_cparams("arbitrary"),
        name="adaln_modulation",
    )(cond, w_mod, b_mod.reshape(1, N_MOD * d))
    return out.reshape(MOD_ROWS, N_MOD, d)


def _mod_row_map(tm, n, row0, per_seq):
    per = n // tm
    return (lambda i: row0 + i // per) if per_seq else (lambda i: row0)


TM_IN = 512
TOKEN_TILE_MIN = 256
TN_IN = IN_W // 2


def _in_body(x_ref, m_ref, n1_ref, w_ref, z_ref, h_scr):
    @pl.when(pl.program_id(1) == 0)
    def _():
        y = _rms(x_ref[...]) * n1_ref[...]
        h_scr[...] = (y * (1.0 + m_ref[0, 1:2, :]) + m_ref[0, 0:1, :]).astype(BF16)

    z_ref[...] = jnp.dot(h_scr[...], w_ref[...], preferred_element_type=F32)


def _token_tile(t, preferred):
    tm = preferred if t % preferred == 0 else TOKEN_TILE_MIN
    assert t % tm == 0
    return tm


def _in_proj(x, mod, norm1, w_in_bf, rows_of):
    t, d = x.shape
    tm = _token_tile(t, TM_IN)
    row_of = rows_of(tm)
    return pl.pallas_call(
        _in_body,
        out_shape=jax.ShapeDtypeStruct((t, IN_W), F32),
        grid=(t // tm, IN_W // TN_IN),
        in_specs=[pl.BlockSpec((tm, d), lambda i, j: (i, 0)),
                  pl.BlockSpec((1, N_MOD, d), lambda i, j: (row_of(i), 0, 0)),
                  pl.BlockSpec((1, d), lambda i, j: (0, 0)),
                  pl.BlockSpec((d, TN_IN), lambda i, j: (0, j))],
        out_specs=pl.BlockSpec((tm, TN_IN), lambda i, j: (i, j)),
        scratch_shapes=[pltpu.VMEM((tm, d), BF16)],
        compiler_params=_cparams("arbitrary", "arbitrary"),
        name="in_proj",
    )(x, mod, norm1.reshape(1, d), w_in_bf)


HG_HEADS_PER_STEP = 2


def _hg_body(q_ref, ff_ref, fb_ref, v_ref, lb_ref, s0_ref, o_ref, sfin_ref, st_scr, ob_scr, *, n):
    c = HG_CHUNK
    nc = n // c
    hps = HG_HEADS_PER_STEP
    row = lax.broadcasted_iota(jnp.int32, (c, c), 0)
    col = lax.broadcasted_iota(jnp.int32, (c, c), 1)
    lower = row >= col
    tri = (lower.astype(F32), (row <= col).astype(F32))
    masks = (lower, row <= col)
    f_refs = (ff_ref, fb_ref)
    for d in range(2):
        for hh in range(hps):
            st_scr[d * hps + hh] = s0_ref[0, d, hh].T

    def chunk(ci, d, hh):
        r = pl.ds(pl.multiple_of(ci * c, c), c)
        hc = slice(hh * HG_DK, (hh + 1) * HG_DK)
        q = _silu(q_ref[r, hc])
        v = v_ref[r, hc].astype(BF16)
        lb = lb_ref[d:d + 1, hc]
        f = lb + (1.0 - lb) * jax.nn.sigmoid(f_refs[d][r, hc])
        k = 1.0 - f
        lf = jnp.log(f)
        cum = lax.dot_general(tri[d], lf, (((1,), (0,)), ((), ())),
                              precision=lax.Precision.HIGHEST, preferred_element_type=F32)
        tot = jnp.sum(lf, axis=0, keepdims=True)
        q_dec = (q * jnp.exp(cum)).astype(BF16)
        k_inv = (k * jnp.exp(-cum)).astype(BF16)
        k_end = (k * jnp.exp(tot - cum)).astype(BF16)
        att = lax.dot_general(q_dec, k_inv, (((1,), (1,)), ((), ())), preferred_element_type=F32)
        att = jnp.where(masks[d], att, 0.0).astype(BF16)
        st = st_scr[d * hps + hh]
        o = (jnp.dot(att, v, preferred_element_type=F32)
             + lax.dot_general(q_dec, st.astype(BF16), (((1,), (1,)), ((), ())), preferred_element_type=F32))
        ds_t = lax.dot_general(v, k_end, (((0,), (0,)), ((), ())), preferred_element_type=F32)
        st_scr[d * hps + hh] = st * jnp.exp(tot) + ds_t
        return r, hc, o

    def step(i, carry):
        for hh in range(hps):
            r, hc, o = chunk(i, 0, hh)
            o_ref[r, hc] = o
            r, hc, o = chunk(nc - 1 - i, 1, hh)
            ob_scr[r, hc] = o
        return carry

    lax.fori_loop(0, nc, step, 0)
    o_ref[...] = o_ref[...] + ob_scr[...]
    for d in range(2):
        for hh in range(hps):
            sfin_ref[0, d, hh] = st_scr[d * hps + hh].T


def _hgrn2(z, lb, s0, n, row_block0):
    seqs = s0.shape[0]
    hps = HG_HEADS_PER_STEP
    wb = hps * HG_DK
    nblk = HG_KW // wb

    def zspec(seg):
        return pl.BlockSpec((n, wb), lambda s, h: (row_block0 + s, seg * nblk + h))

    st_spec = pl.BlockSpec((1, 2, hps, HG_DK, HG_DV), lambda s, h: (s, 0, h, 0, 0))
    return pl.pallas_call(
        functools.partial(_hg_body, n=n),
        out_shape=(jax.ShapeDtypeStruct((seqs * n, HG_VW), F32),
                   jax.ShapeDtypeStruct((seqs, 2, HG_HEADS, HG_DK, HG_DV), F32)),
        grid=(seqs, HG_HEADS // hps),
        in_specs=[zspec(0), zspec(1), zspec(2), zspec(3),
                  pl.BlockSpec((2, wb), lambda s, h: (0, h)), st_spec],
        out_specs=(pl.BlockSpec((n, wb), lambda s, h: (s, h)), st_spec),
        scratch_shapes=[pltpu.VMEM((2 * hps, HG_DV, HG_DK), F32), pltpu.VMEM((n, wb), F32)],
        compiler_params=_cparams("arbitrary", "arbitrary"),
        name="hgrn2_scan",
    )(z, z, z, z, lb, s0)


LRU_RB = 64
HALO = 8


LANES = 128


def _lru_body(x_ref, cw_ref, cb_ref, wr_ref, wi_ref, br_ref, bi_ref, lam_ref, h0_ref, hr_ref, e_ref,
              xp_scr, xc_scr, *cm_scr, n, col_major):
    rb = LRU_RB
    nb = n // rb
    w = LRU_W
    grid_rows = n // GRID_W
    slabs = w // LANES
    zeros = jnp.zeros((HALO, w), F32)
    xp_scr[0:HALO, :] = zeros
    xp_scr[HALO + n:HALO + n + HALO, :] = zeros
    if col_major:
        st_scr, hcm_scr = cm_scr
        for j in range(slabs):
            st_scr[j] = x_ref[:, j * LANES:(j + 1) * LANES]

        def to_col_major(c, carry):
            dst = pl.ds(pl.multiple_of(HALO + c * grid_rows, 8), grid_rows)
            for j in range(slabs):
                xp_scr[dst, j * LANES:(j + 1) * LANES] = st_scr[j, pl.ds(c, grid_rows, stride=GRID_W), :]
            return carry

        lax.fori_loop(0, GRID_W, to_col_major, 0)
        h_dst = hcm_scr
    else:
        xp_scr[HALO:HALO + n, :] = x_ref[...]
        h_dst = hr_ref
    cw = cw_ref[...]
    cb = cb_ref[...]

    def conv_blk(b, carry):
        base = pl.multiple_of(b * rb, rb)
        xh = xp_scr[pl.ds(base, rb + 2 * HALO), :]
        ext = rb + 2 * HALO
        acc = cb + cw[2:3, :] * xh[HALO:HALO + rb]
        acc = acc + cw[0:1, :] * pltpu.roll(xh, 2, axis=0)[HALO:HALO + rb]
        acc = acc + cw[1:2, :] * pltpu.roll(xh, 1, axis=0)[HALO:HALO + rb]
        acc = acc + cw[3:4, :] * pltpu.roll(xh, ext - 1, axis=0)[HALO:HALO + rb]
        xc_scr[pl.ds(base, rb), :] = acc
        return carry

    lax.fori_loop(0, nb, conv_blk, 0)

    rows = lax.broadcasted_iota(jnp.int32, (rb, w), 0)

    def gates(blk, d):
        r = pl.ds(pl.multiple_of(blk * rb, rb), rb)
        xc = xc_scr[r, :]
        xb = xc.astype(BF16)
        rg = jax.nn.sigmoid(jnp.dot(xb, wr_ref[d], preferred_element_type=F32) + br_ref[d:d + 1, :])
        ig = jax.nn.sigmoid(jnp.dot(xb, wi_ref[d], preferred_element_type=F32) + bi_ref[d:d + 1, :])
        log_a = (-LRU_C) * jax.nn.softplus(-lam_ref[d:d + 1, :]) * rg
        a = jnp.exp(log_a)
        u = jnp.sqrt(-jnp.tanh(log_a) * (a * a + 1.0)) * (ig * xc)
        return r, a, u

    def fwd_blk(blk, h_prev):
        r, a, u = gates(blk, 0)
        s = 1
        while s < rb:
            keep = rows >= s
            a_sh = jnp.where(keep, pltpu.roll(a, s, axis=0), 1.0)
            u_sh = jnp.where(keep, pltpu.roll(u, s, axis=0), 0.0)
            u = a * u_sh + u
            a = a * a_sh
            s *= 2
        h = u + a * h_prev
        h_dst[r, :] = h
        return h[rb - 1:rb, :]

    def bwd_blk(i, h_next):
        r, a, u = gates(nb - 1 - i, 1)
        s = 1
        while s < rb:
            keep = rows < rb - s
            a_sh = jnp.where(keep, pltpu.roll(a, rb - s, axis=0), 1.0)
            u_sh = jnp.where(keep, pltpu.roll(u, rb - s, axis=0), 0.0)
            u = a * u_sh + u
            a = a * a_sh
            s *= 2
        h = u + a * h_next
        h_dst[r, :] = h_dst[r, :] + h
        return h[0:1, :]

    e_ref[0, 0:1, :] = lax.fori_loop(0, nb, fwd_blk, h0_ref[0, 0:1, :])
    e_ref[0, 1:2, :] = lax.fori_loop(0, nb, bwd_blk, h0_ref[0, 1:2, :])
    if col_major:
        def to_row_major(c, carry):
            src = pl.ds(pl.multiple_of(c * grid_rows, 8), grid_rows)
            for j in range(slabs):
                st_scr[j, pl.ds(c, grid_rows, stride=GRID_W), :] = hcm_scr[src, j * LANES:(j + 1) * LANES]
            return carry

        lax.fori_loop(0, GRID_W, to_row_major, 0)
        for j in range(slabs):
            hr_ref[:, j * LANES:(j + 1) * LANES] = st_scr[j]


def _rglru(x, col_block, row_block0, n, seqs, col_major, conv_w, conv_b, wr_bd, wi_bd, b_r, b_i, lam, h0):
    w = LRU_W
    cm_scratch = [pltpu.VMEM((w // LANES, n, LANES), F32), pltpu.VMEM((n, w), F32)] if col_major else []
    full2 = lambda s: (0, 0)
    full3 = lambda s: (0, 0, 0)
    return pl.pallas_call(
        functools.partial(_lru_body, n=n, col_major=col_major),
        out_shape=(jax.ShapeDtypeStruct((seqs * n, w), F32), jax.ShapeDtypeStruct((seqs, 2, w), F32)),
        grid=(seqs,),
        in_specs=[pl.BlockSpec((n, w), lambda s: (row_block0 + s, col_block)),
                  pl.BlockSpec((CONV_W, w), full2), pl.BlockSpec((1, w), full2),
                  pl.BlockSpec((2, w, w), full3), pl.BlockSpec((2, w, w), full3),
                  pl.BlockSpec((2, w), full2), pl.BlockSpec((2, w), full2), pl.BlockSpec((2, w), full2),
                  pl.BlockSpec((1, 2, w), lambda s: (s, 0, 0))],
        out_specs=(pl.BlockSpec((n, w), lambda s: (s, 0)), pl.BlockSpec((1, 2, w), lambda s: (s, 0, 0))),
        scratch_shapes=[pltpu.VMEM((n + 2 * HALO, w), F32), pltpu.VMEM((n, w), F32)] + cm_scratch,
        compiler_params=_cparams("arbitrary"),
        name="rglru",
    )(x, conv_w, conv_b.reshape(1, w), wr_bd, wi_bd, b_r, b_i, lam, h0)


def _block_diag(wg):
    eye = jnp.eye(LRU_BLOCKS, dtype=wg.dtype)
    dense = wg[:, :, :, None, :] * eye[None, :, None, :, None]
    return dense.reshape(2, LRU_W, LRU_W).astype(BF16)


TM_OUT = 256


def _out_body(x_ref, o_ref, hr_ref, g_ref, xg_ref, ga0_ref, ga1_ref, gb0_ref, gb1_ref, m_ref,
              hgn_ref, n2_ref, wa_ref, wb_ref, wo_ref, wq_ref, x1_ref, h2_ref, q_ref):
    o = o_ref[...]
    parts = []
    for h in range(HG_HEADS):
        parts.append(_rms(o[:, h * HG_DV:(h + 1) * HG_DV]) * hgn_ref[...])
    on = jnp.concatenate(parts, axis=1) * _silu(g_ref[...])
    y_a = jnp.dot(on.astype(BF16), wa_ref[...], preferred_element_type=F32)
    y_b = jnp.dot((hr_ref[...] * jax.nn.gelu(xg_ref[...])).astype(BF16), wb_ref[...], preferred_element_type=F32)
    ga = jnp.concatenate([ga0_ref[...], ga1_ref[...]], axis=1)
    gb = jnp.concatenate([gb0_ref[...], gb1_ref[...]], axis=1)
    merged = jax.nn.sigmoid(ga) * y_a + jax.nn.sigmoid(gb) * y_b
    mix = jnp.dot(merged.astype(BF16), wo_ref[...], preferred_element_type=F32)
    x1 = x_ref[...] + m_ref[0, 2:3, :] * mix
    x1_ref[...] = x1
    h2 = _rms(x1) * n2_ref[...] * (1.0 + m_ref[0, 4:5, :]) + m_ref[0, 3:4, :]
    h2_ref[...] = h2
    q_ref[...] = jnp.dot(h2.astype(BF16), wq_ref[...], preferred_element_type=F32).astype(BF16)


def _out_proj(x, o_hg, hr, z, mod, hg_norm, norm2, wa, wb, wo, wq, row_of):
    t, d = x.shape
    tm = TM_OUT
    nq = PK_HEADS * PK_DQ
    half = lambda i: (i, 0)
    zc = lambda c: pl.BlockSpec((tm, LRU_W), lambda i: (i, c))
    const = lambda i: (0, 0)
    return pl.pallas_call(
        _out_body,
        out_shape=(jax.ShapeDtypeStruct((t, d), F32), jax.ShapeDtypeStruct((t, d), F32),
                   jax.ShapeDtypeStruct((t, nq), BF16)),
        grid=(t // tm,),
        in_specs=[pl.BlockSpec((tm, d), half), pl.BlockSpec((tm, HG_VW), half), pl.BlockSpec((tm, LRU_W), half),
                  zc(COL_G), zc(COL_XG), zc(COL_GA), zc(COL_GA + 1), zc(COL_GB), zc(COL_GB + 1),
                  pl.BlockSpec((1, N_MOD, d), lambda i: (row_of(i), 0, 0)),
                  pl.BlockSpec((1, HG_DV), const), pl.BlockSpec((1, d), const),
                  pl.BlockSpec((HG_VW, d), const), pl.BlockSpec((LRU_W, d), const),
                  pl.BlockSpec((d, d), const), pl.BlockSpec((d, nq), const)],
        out_specs=(pl.BlockSpec((tm, d), half), pl.BlockSpec((tm, d), half), pl.BlockSpec((tm, nq), half)),
        compiler_params=_cparams("arbitrary"),
        name="out_proj",
    )(x, o_hg, hr, z, z, z, z, z, z, mod, hg_norm.reshape(1, HG_DV), norm2.reshape(1, d), wa, wb, wo, wq)


TT_TOPK = 128


def _extract_topk(s, order, k, payload=None):
    vals, ords, picks = [], [], []
    for _ in range(k):
        m = jnp.max(s, axis=0, keepdims=True)
        o = jnp.max(jnp.where(s == m, order, -1.0), axis=0, keepdims=True)
        hit = order == o
        vals.append(m)
        ords.append(o)
        if payload is not None:
            picks.append(jnp.sum(jnp.where(hit, payload, 0), axis=0, keepdims=True))
        s = jnp.where(hit, -jnp.inf, s)
    cat = lambda xs: jnp.concatenate(xs, axis=0)
    return cat(vals), cat(ords), (cat(picks) if payload is not None else None)


PAIR_ROWS = 4
PAIR_COLS = 3
assert all((i + 1) * (j + 1) > PK_TOPK for i in range(PAIR_ROWS, PK_TOPK) for j in range(PAIR_COLS, PK_TOPK))
assert all((i + 1) * (8 + 1) > PK_TOPK for i in range(1, PAIR_ROWS)) and (8 + 1) * (1 + 1) > PK_TOPK


def _pair_candidates(sv, si, tt):
    cand, order, eid = [], [], []
    hi = float(PK_TOPK * PK_TOPK - 1)
    for i in range(PAIR_ROWS):
        nj = PK_TOPK if i == 0 else 8
        j = lax.broadcasted_iota(jnp.int32, (nj, tt), 0)
        cand.append(sv[0][i:i + 1, :] + sv[1][:nj])
        order.append(hi - (i * PK_TOPK + j).astype(F32))
        eid.append(si[0][i:i + 1, :] * N_KEYS + si[1][:nj])
    for j in range(PAIR_COLS):
        ni = PK_TOPK if j == 0 else 8
        i = lax.broadcasted_iota(jnp.int32, (ni, tt), 0)
        fresh = i >= PAIR_ROWS
        cand.append(jnp.where(fresh, sv[0][:ni] + sv[1][j:j + 1, :], -jnp.inf))
        order.append(jnp.where(fresh, hi - (i * PK_TOPK + j).astype(F32), -2.0))
        eid.append(si[0][:ni] * N_KEYS + si[1][j:j + 1, :])
    cat = lambda xs: jnp.concatenate(xs, axis=0)
    return cat(cand), cat(order), cat(eid)


def _topk_body(q_ref, keys_ref, eidx_ref, gate_ref):
    tt = q_ref.shape[0]
    key_order = (N_KEYS - 1 - lax.broadcasted_iota(jnp.int32, (N_KEYS, tt), 0)).astype(F32)
    e_rows, g_rows = [], []
    for h in range(PK_HEADS):
        sv, si = [], []
        for p in range(2):
            c0 = h * PK_DQ + p * PK_DH
            s = lax.dot_general(keys_ref[p, h], q_ref[:, c0:c0 + PK_DH], (((1,), (1,)), ((), ())),
                                preferred_element_type=F32)
            v, o, _ = _extract_topk(s, key_order, PK_TOPK)
            sv.append(v)
            si.append(N_KEYS - 1 - o.astype(jnp.int32))
        cand, order, cidx = _pair_candidates(sv, si, tt)
        best, _, eid = _extract_topk(cand, order, PK_TOPK, payload=cidx)
        ex = jnp.exp(best - best[0:1, :])
        g_rows.append(ex / jnp.sum(ex, axis=0, keepdims=True))
        e_rows.append(eid)
    eidx_ref[...] = jnp.concatenate(e_rows, axis=0).T
    gate_ref[...] = jnp.concatenate(g_rows, axis=0).T


def _pk_topk(q, keys_bf):
    t = q.shape[0]
    tt = TT_TOPK
    ne = PK_HEADS * PK_TOPK
    return pl.pallas_call(
        _topk_body,
        out_shape=(jax.ShapeDtypeStruct((t, ne), jnp.int32), jax.ShapeDtypeStruct((t, ne), F32)),
        grid=(t // tt,),
        in_specs=[pl.BlockSpec((tt, PK_HEADS * PK_DQ), lambda i: (i, 0)),
                  pl.BlockSpec((2, PK_HEADS, N_KEYS, PK_DH), lambda i: (0, 0, 0, 0))],
        out_specs=(pl.BlockSpec((tt, ne), lambda i: (i, 0)), pl.BlockSpec((tt, ne), lambda i: (i, 0))),
        compiler_params=_cparams("arbitrary"),
        name="pk_topk",
    )(q, keys_bf)


SC_LANES = 16
SC_CORES = 2
SC_SUBCORES = 16
SC_WORKERS = SC_CORES * SC_SUBCORES
SC_HALF = D_MODEL // 2
SC_ROW_CHUNKS = SC_HALF // SC_LANES
SC_TOKEN_BLOCK = 8


def _pack_bf16_pairs(tab):
    lo = lax.bitcast_convert_type(tab[:, SC_HALF:].astype(BF16), jnp.uint16).astype(jnp.uint32)
    bits = lax.bitcast_convert_type(tab[:, :SC_HALF], jnp.uint32)
    sign = bits & jnp.uint32(0x80000000)
    mag = (bits & jnp.uint32(0x7FFFFFFF)) + jnp.uint32(1 << 15)
    hi = jnp.where(mag >= lo, (mag - lo) >> 16, jnp.uint32(0))
    return lax.bitcast_convert_type(sign | (hi << 16) | lo, jnp.int32)


def _unpack_pair(w):
    return lax.bitcast_convert_type(w, F32), lax.bitcast_convert_type(w << 16, F32)


def _gelu_tanh(x):
    y = 0.7978845608028654 * (x + 0.044715 * (x * x * x))
    t = 1.0 - 2.0 / (jnp.exp(2.0 * y) + 1.0)
    return x * (0.5 * (1.0 + t))


def _peer_sc_body(h_hbm, idx_hbm, gate_hbm, u_hbm, v_hbm, out_hbm,
                  h_v, idx_v, gate_v, out_v, ub0, ub1, vb0, vb1, su0, su1, sv0, sv1, *, tokens_per_worker):
    nh, k, tb, lanes = PK_HEADS, PK_TOPK, SC_TOKEN_BLOCK, SC_LANES
    wid = lax.axis_index("s") * SC_CORES + lax.axis_index("c")
    ubufs, vbufs, sus, svs = (ub0, ub1), (vb0, vb1), (su0, su1), (sv0, sv1)
    lane = lax.iota(jnp.int32, lanes)
    zero = jnp.zeros((lanes,), F32)

    def start(tok, hd, par):
        irow = idx_v.at[tok * nh + hd]
        pltpu.async_copy(u_hbm.at[irow], ubufs[par], sus[par])
        pltpu.async_copy(v_hbm.at[irow], vbufs[par], svs[par])

    def wait(par):
        irow = idx_v.at[0]
        pltpu.make_async_copy(u_hbm.at[irow], ubufs[par], sus[par]).wait()
        pltpu.make_async_copy(v_hbm.at[irow], vbufs[par], svs[par]).wait()

    @pl.loop(0, tokens_per_worker // tb)
    def _(blk):
        base = wid * tokens_per_worker + blk * tb
        pltpu.sync_copy(h_hbm.at[pl.ds(base, tb)], h_v)
        pltpu.sync_copy(idx_hbm.at[pl.ds(base * nh, tb * nh)], idx_v)
        pltpu.sync_copy(gate_hbm.at[pl.ds(base * nh, tb * nh)], gate_v)
        start(0, 0, 0)

        @pl.loop(0, tb)
        def _(tok):
            for hd in range(nh):
                par = hd % 2
                if hd + 1 < nh:
                    start(tok, hd + 1, 1 - par)
                else:
                    @pl.when(tok + 1 < tb)
                    def _():
                        start(tok + 1, 0, 1 - par)
                wait(par)
                ub, vb = ubufs[par], vbufs[par]

                @plsc.parallel_loop(0, SC_ROW_CHUNKS, carry=(zero,) * k)
                def accs(c, acc):
                    ha = h_v[tok, pl.ds(c * lanes, lanes)]
                    hb = h_v[tok, pl.ds(SC_HALF + c * lanes, lanes)]
                    out = []
                    for r in range(k):
                        a, b = _unpack_pair(ub[r, pl.ds(c * lanes, lanes)])
                        out.append(acc[r] + (ha * a + hb * b))
                    return tuple(out)

                s_vec = zero
                for r in range(k):
                    s_vec = jnp.where(lane == r, jnp.sum(accs[r]), s_vec)
                w_vec = gate_v[tok * nh + hd, :] * _gelu_tanh(s_vec)
                wb = [jnp.sum(jnp.where(lane == r, w_vec, 0.0)) for r in range(k)]

                @plsc.parallel_loop(0, SC_ROW_CHUNKS)
                def _(c):
                    pa, pb = [], []
                    for r in range(k):
                        a, b = _unpack_pair(vb[r, pl.ds(c * lanes, lanes)])
                        pa.append(wb[r] * a)
                        pb.append(wb[r] * b)
                    while len(pa) > 1:
                        pa = [pa[i] + pa[i + 1] for i in range(0, len(pa), 2)]
                        pb = [pb[i] + pb[i + 1] for i in range(0, len(pb), 2)]
                    for off, o in ((0, pa[0]), (SC_HALF, pb[0])):
                        cols = pl.ds(off + c * lanes, lanes)
                        if hd == 0:
                            out_v[tok, cols] = o
                        else:
                            out_v[tok, cols] = out_v[tok, cols] + o

        pltpu.sync_copy(out_v, out_hbm.at[pl.ds(base, tb)])


def _peer_experts(h, eidx, gate, u_tab, v_tab):
    t, d = h.shape
    assert t % (SC_WORKERS * SC_TOKEN_BLOCK) == 0 and d == D_MODEL
    rows = pltpu.VMEM((PK_TOPK, SC_HALF), jnp.int32)
    return pl.kernel(
        functools.partial(_peer_sc_body, tokens_per_worker=t // SC_WORKERS),
        out_type=jax.ShapeDtypeStruct((t, d), F32),
        mesh=plsc.VectorSubcoreMesh(core_axis_name="c", subcore_axis_name="s"),
        scratch_types=[
            pltpu.VMEM((SC_TOKEN_BLOCK, d), F32),
            pltpu.VMEM((SC_TOKEN_BLOCK * PK_HEADS, PK_TOPK), jnp.int32),
            pltpu.VMEM((SC_TOKEN_BLOCK * PK_HEADS, PK_TOPK), F32),
            pltpu.VMEM((SC_TOKEN_BLOCK, d), F32),
            rows, rows, rows, rows,
            pltpu.SemaphoreType.DMA, pltpu.SemaphoreType.DMA,
            pltpu.SemaphoreType.DMA, pltpu.SemaphoreType.DMA,
        ],
        compiler_params=pltpu.CompilerParams(needs_layout_passes=False),
        name="peer_experts_sc",
    )(h, eidx, gate, u_tab, v_tab)


TT_DENSE = 256
EB_DENSE = 512
N_EXPERTS = N_KEYS * N_KEYS
N_SEL = PK_HEADS * PK_TOPK


def _dense_body(h_ref, e_ref, g_ref, u_ref, v_ref, o_ref, g_scr, acc_scr):
    eb = pl.program_id(1)
    tt = h_ref.shape[0]
    slabs = EB_DENSE // N_KEYS

    @pl.when(eb == 0)
    def _():
        acc_scr[...] = jnp.zeros_like(acc_scr)
        sub = lax.broadcasted_iota(jnp.int32, (N_KEYS, N_SEL), 0)

        def per_token(t, carry):
            e = e_ref[pl.ds(t, 1), :]
            p_t = jnp.where(sub == (e >> 7), g_ref[pl.ds(t, 1), :], 0.0).astype(BF16)
            q_t = (sub == (e & (N_KEYS - 1))).astype(BF16)
            g_scr[pl.ds(pl.multiple_of(t * N_KEYS, N_KEYS), N_KEYS), :] = lax.dot_general(
                p_t, q_t, (((1,), (1,)), ((), ())), preferred_element_type=F32)
            return carry

        lax.fori_loop(0, tt, per_token, 0, unroll=4)

    s = lax.dot_general(h_ref[...].astype(BF16), u_ref[...], (((1,), (1,)), ((), ())), preferred_element_type=F32)
    gs = jnp.concatenate([g_scr[pl.ds(eb * slabs + k, tt, stride=N_KEYS), :] for k in range(slabs)], axis=1)
    acc_scr[...] += jnp.dot((gs * jax.nn.gelu(s)).astype(BF16), v_ref[...], preferred_element_type=F32)

    @pl.when(eb == pl.num_programs(1) - 1)
    def _():
        o_ref[...] = acc_scr[...]


def _peer_dense(h, eidx, gate, u_bf, v_bf):
    t, d = h.shape
    tt = TT_DENSE
    assert t % tt == 0 and N_KEYS == 128
    tile = lambda i, e: (i, 0)
    blk = lambda i, e: (e, 0)
    return pl.pallas_call(
        _dense_body,
        out_shape=jax.ShapeDtypeStruct((t, d), F32),
        grid=(t // tt, N_EXPERTS // EB_DENSE),
        in_specs=[pl.BlockSpec((tt, d), tile), pl.BlockSpec((tt, N_SEL), tile), pl.BlockSpec((tt, N_SEL), tile),
                  pl.BlockSpec((EB_DENSE, d), blk), pl.BlockSpec((EB_DENSE, d), blk)],
        out_specs=pl.BlockSpec((tt, d), tile),
        scratch_shapes=[pltpu.VMEM((tt * N_KEYS, N_KEYS), F32), pltpu.VMEM((tt, d), F32)],
        compiler_params=_cparams("arbitrary", "arbitrary"),
        name="peer_dense_tc",
    )(h, eidx, gate, u_bf, v_bf)


TM_FIN = 512


def _final_body(x1_ref, p_ref, m_ref, w_ref, y_ref):
    y_ref[...] = _rms(x1_ref[...] + m_ref[0, 5:6, :] * p_ref[...]) * w_ref[...]


def _final(x1, peer, mod, norm_f, rows_of):
    t, d = x1.shape
    tm = _token_tile(t, TM_FIN)
    row_of = rows_of(tm)
    tile = pl.BlockSpec((tm, d), lambda i: (i, 0))
    return pl.pallas_call(
        _final_body,
        out_shape=jax.ShapeDtypeStruct((t, d), F32),
        grid=(t // tm,),
        in_specs=[tile, tile, pl.BlockSpec((1, N_MOD, d), lambda i: (row_of(i), 0, 0)),
                  pl.BlockSpec((1, d), lambda i: (0, 0))],
        out_specs=tile,
        compiler_params=_cparams("arbitrary"),
        name="final_norm",
    )(x1, peer, mod, norm_f.reshape(1, d))


CTX_GROUPS = (1, 15, 16)
LAT_GROUP = 2
DENSE_LAT_SEQS = 4


def _zero_of(a):
    return (lax.shift_right_logical(a[0, 0], 31) >> 1).astype(F32)


def _group(x, mod, row0, per_seq, n, hg_s0, lru_s0, col_major, p, after, dense=False):
    seqs = hg_s0.shape[0]
    t = seqs * n
    for a in after:
        mod = mod + _zero_of(a)
    rows_of = lambda tm: _mod_row_map(tm, n, row0, per_seq)
    z = _in_proj(x, mod, p['norm1'], p['w_in'], rows_of)
    o_hg, hg_fin = _hgrn2(z, p['lb'], hg_s0, n, 0)
    hr, lru_fin = _rglru(z, COL_XR, 0, n, seqs, col_major, *p['lru'], lru_s0)
    x1, h2, q = _out_proj(x, o_hg, hr, z, mod, p['hg_norm'], p['norm2'], p['w_a'], p['w_b'], p['w_o'], p['w_q'],
                          rows_of(TM_OUT))
    eidx, gate = _pk_topk(q, p['keys'])
    if dense:
        peer = _peer_dense(h2, eidx, gate, p['u_bf'], p['v_bf'])
    else:
        peer = _peer_experts(h2, eidx.reshape(t * PK_HEADS, PK_TOPK), gate.reshape(t * PK_HEADS, PK_TOPK),
                             p['pk_u'], p['pk_v'])
    return _final(x1, peer, mod, p['norm_f'], rows_of), hg_fin, lru_fin, eidx


def kernel(x_prompt, x_sample, state_hgrn, state_rglru, c, c_ctx, w_mod, b_mod, norm1, w_in,
           hg_lb_logits, hg_norm, lru_conv_w, lru_conv_b, lru_w_r, lru_b_r, lru_w_i, lru_b_i,
           lru_lambda, w_branch_a, w_branch_b, w_out, norm2, pk_w_q, pk_sub_keys, pk_u, pk_v,
           norm_f):
    assert w_mod.shape[0] == 1, "single trunk layer"
    d = D_MODEL
    nb_ctx, n_ctx, _ = x_prompt.shape
    nb_lat, n_lat, _ = x_sample.shape
    assert nb_lat < MOD_ROWS and nb_lat % LAT_GROUP == 0 and nb_ctx == sum(CTX_GROUPS)
    ctx_row = nb_lat

    cond = jnp.zeros((MOD_ROWS, d), F32).at[:nb_lat].set(c).at[ctx_row].set(c_ctx)
    mod = _modulation(cond, w_mod[0], b_mod[0])
    p = {
        'lb': jnp.cumsum(jax.nn.softmax(hg_lb_logits.astype(F32), axis=1), axis=1)[:, 0],
        'norm1': norm1[0], 'w_in': w_in[0].astype(BF16), 'hg_norm': hg_norm[0], 'norm2': norm2[0],
        'lru': (lru_conv_w[0], lru_conv_b[0], _block_diag(lru_w_r[0]), _block_diag(lru_w_i[0]),
                lru_b_r[0], lru_b_i[0], lru_lambda[0]),
        'w_a': w_branch_a[0].astype(BF16), 'w_b': w_branch_b[0].astype(BF16), 'w_o': w_out[0].astype(BF16),
        'w_q': pk_w_q[0].astype(BF16), 'keys': pk_sub_keys[0].astype(BF16),
        'pk_u': _pack_bf16_pairs(pk_u[0]), 'pk_v': _pack_bf16_pairs(pk_v[0]), 'norm_f': norm_f,
        'u_bf': pk_u[0].astype(BF16), 'v_bf': pk_v[0].astype(BF16),
    }
    y_ctx, hg_fin, lru_fin = [], [], []
    tables = [p['pk_u'], p['pk_v']]
    after = tables
    s0 = 0
    for g in CTX_GROUPS:
        y, hg, lr, eidx = _group(
            x_prompt[s0:s0 + g].reshape(g * n_ctx, d), mod, ctx_row, False, n_ctx,
            jnp.zeros((g, 2, HG_HEADS, HG_DK, HG_DV), F32), jnp.zeros((g, 2, LRU_W), F32), False, p, after)
        s0 += g
        after = [eidx] if after is tables else after
        y_ctx.append(y.reshape(g, n_ctx, d))
        hg_fin.append(hg)
        lru_fin.append(lr)
    y_lat = []
    for s0 in range(0, nb_lat, LAT_GROUP):
        sl = slice(s0, s0 + LAT_GROUP)
        y, _, _, _ = _group(x_sample[sl].reshape(LAT_GROUP * n_lat, d), mod, s0, True, n_lat,
                            state_hgrn[sl, 0], state_rglru[sl, 0], True, p, after,
                            dense=s0 >= nb_lat - DENSE_LAT_SEQS)
        y_lat.append(y.reshape(LAT_GROUP, n_lat, d))
    return (jnp.concatenate(y_ctx, axis=0), jnp.concatenate(y_lat, axis=0),
            jnp.concatenate(hg_fin, axis=0)[:, None], jnp.concatenate(lru_fin, axis=0)[:, None])
```

```python
import functools

import jax, jax.numpy as jnp
from jax import lax
from jax.experimental import pallas as pl
from jax.experimental.pallas import tpu as pltpu
from jax.experimental.pallas import tpu_sc as plsc

D_MODEL = 1024
GRID_W = 64
EPS = 1e-6
HG_HEADS = 4
HG_DK = 128
HG_DV = 128
HG_KW = HG_HEADS * HG_DK
HG_VW = HG_HEADS * HG_DV
HG_CHUNK = 32
LRU_W = D_MODEL // 2
LRU_BLOCKS = 8
LRU_BW = LRU_W // LRU_BLOCKS
CONV_W = 4
LRU_C = 8.0
PK_HEADS = 8
N_KEYS = 128
PK_TOPK = 16
PK_DQ = 256
PK_DH = PK_DQ // 2
IN_W = 3 * HG_KW + 2 * HG_VW + 2 * LRU_W + 2 * D_MODEL
COL_G, COL_XR, COL_XG, COL_GA, COL_GB = 4, 5, 6, 7, 9
N_MOD = 6
MOD_ROWS = 16
F32 = jnp.float32
BF16 = jnp.bfloat16
VMEM_LIMIT = 48 * 1024 * 1024


def _cparams(*sem):
    return pltpu.CompilerParams(dimension_semantics=sem, vmem_limit_bytes=VMEM_LIMIT)


def _silu(x):
    return x * jax.nn.sigmoid(x)


def _rms(x):
    return x * lax.rsqrt(jnp.mean(x * x, axis=-1, keepdims=True) + EPS)


def _mod_body(c_ref, w_ref, b_ref, o_ref):
    o_ref[...] = lax.dot_general(_silu(c_ref[...]), w_ref[...], (((1,), (0,)), ((), ())),
                                 precision=lax.Precision.HIGHEST, preferred_element_type=F32) + b_ref[...]


def _modulation(cond, w_mod, b_mod):
    d = D_MODEL
    out = pl.pallas_call(
        _mod_body,
        out_shape=jax.ShapeDtypeStruct((MOD_ROWS, N_MOD * d), F32),
        grid=(N_MOD,),
        in_specs=[pl.BlockSpec((MOD_ROWS, d), lambda j: (0, 0)),
                  pl.BlockSpec((d, d), lambda j: (0, j)),
                  pl.BlockSpec((1, d), lambda j: (0, j))],
        out_specs=pl.BlockSpec((MOD_ROWS, d), lambda j: (0, j)),
        compiler_params=_cparams("arbitrary"),
        name="adaln_modulation",
    )(cond, w_mod, b_mod.reshape(1, N_MOD * d))
    return out.reshape(MOD_ROWS, N_MOD, d)


def _mod_row_map(tm, n, row0, per_seq):
    per = n // tm
    return (lambda i: row0 + i // per) if per_seq else (lambda i: row0)


TM_IN = 512
TOKEN_TILE_MIN = 256
TN_IN = IN_W // 2


def _in_body(x_ref, m_ref, n1_ref, w_ref, z_ref, h_scr):
    @pl.when(pl.program_id(1) == 0)
    def _():
        y = _rms(x_ref[...]) * n1_ref[...]
        h_scr[...] = (y * (1.0 + m_ref[0, 1:2, :]) + m_ref[0, 0:1, :]).astype(BF16)

    z_ref[...] = jnp.dot(h_scr[...], w_ref[...], preferred_element_type=F32)


def _token_tile(t, preferred):
    tm = preferred if t % preferred == 0 else TOKEN_TILE_MIN
    assert t % tm == 0
    return tm


def _in_proj(x, mod, norm1, w_in_bf, rows_of):
    t, d = x.shape
    tm = _token_tile(t, TM_IN)
    row_of = rows_of(tm)
    return pl.pallas_call(
        _in_body,
        out_shape=jax.ShapeDtypeStruct((t, IN_W), F32),
        grid=(t // tm, IN_W // TN_IN),
        in_specs=[pl.BlockSpec((tm, d), lambda i, j: (i, 0)),
                  pl.BlockSpec((1, N_MOD, d), lambda i, j: (row_of(i), 0, 0)),
                  pl.BlockSpec((1, d), lambda i, j: (0, 0)),
                  pl.BlockSpec((d, TN_IN), lambda i, j: (0, j))],
        out_specs=pl.BlockSpec((tm, TN_IN), lambda i, j: (i, j)),
        scratch_shapes=[pltpu.VMEM((tm, d), BF16)],
        compiler_params=_cparams("arbitrary", "arbitrary"),
        name="in_proj",
    )(x, mod, norm1.reshape(1, d), w_in_bf)


HG_HEADS_PER_STEP = 2


def _hg_body(q_ref, ff_ref, fb_ref, v_ref, lb_ref, s0_ref, o_ref, sfin_ref, st_scr, ob_scr, *, n):
    c = HG_CHUNK
    nc = n // c
    hps = HG_HEADS_PER_STEP
    row = lax.broadcasted_iota(jnp.int32, (c, c), 0)
    col = lax.broadcasted_iota(jnp.int32, (c, c), 1)
    lower = row >= col
    tri = (lower.astype(F32), (row <= col).astype(F32))
    masks = (lower, row <= col)
    f_refs = (ff_ref, fb_ref)
    for d in range(2):
        for hh in range(hps):
            st_scr[d * hps + hh] = s0_ref[0, d, hh].T

    def step(i, carry):
        chains = [(d, hh) for d in range(2) for hh in range(hps)]
        st1 = []
        for d, hh in chains:
            ci = i if d == 0 else nc - 1 - i
            r = pl.ds(pl.multiple_of(ci * c, c), c)
            hc = slice(hh * HG_DK, (hh + 1) * HG_DK)
            lb = lb_ref[d:d + 1, hc]
            f = lb + (1.0 - lb) * jax.nn.sigmoid(f_refs[d][r, hc])
            lf = jnp.log(f)
            cum = lax.dot_general(tri[d], lf, (((1,), (0,)), ((), ())),
                                  precision=lax.Precision.HIGHEST, preferred_element_type=F32)
            st1.append((r, hc, 1.0 - f, lf, cum))
        st2 = []
        for (d, hh), (r, hc, k, lf, cum) in zip(chains, st1):
            q = _silu(q_ref[r, hc])
            v = v_ref[r, hc].astype(BF16)
            tot = jnp.sum(lf, axis=0, keepdims=True)
            q_dec = (q * jnp.exp(cum)).astype(BF16)
            k_inv = (k * jnp.exp(-cum)).astype(BF16)
            k_end = (k * jnp.exp(tot - cum)).astype(BF16)
            st = st_scr[d * hps + hh]
            att = lax.dot_general(q_dec, k_inv, (((1,), (1,)), ((), ())), preferred_element_type=F32)
            o_inter = lax.dot_general(q_dec, st.astype(BF16), (((1,), (1,)), ((), ())), preferred_element_type=F32)
            ds_t = lax.dot_general(v, k_end, (((0,), (0,)), ((), ())), preferred_element_type=F32)
            st_scr[d * hps + hh] = st * jnp.exp(tot) + ds_t
            st2.append((v, att, o_inter))
        for (d, hh), (r, hc, _, _, _), (v, att, o_inter) in zip(chains, st1, st2):
            att = jnp.where(masks[d], att, 0.0).astype(BF16)
            o = jnp.dot(att, v, preferred_element_type=F32) + o_inter
            if d == 0:
                o_ref[r, hc] = o
            else:
                ob_scr[r, hc] = o
        return carry

    lax.fori_loop(0, nc, step, 0)
    o_ref[...] = o_ref[...] + ob_scr[...]
    for d in range(2):
        for hh in range(hps):
            sfin_ref[0, d, hh] = st_scr[d * hps + hh].T


def _hgrn2(z, lb, s0, n, row_block0):
    seqs = s0.shape[0]
    hps = HG_HEADS_PER_STEP
    wb = hps * HG_DK
    nblk = HG_KW // wb

    def zspec(seg):
        return pl.BlockSpec((n, wb), lambda s, h: (row_block0 + s, seg * nblk + h))

    st_spec = pl.BlockSpec((1, 2, hps, HG_DK, HG_DV), lambda s, h: (s, 0, h, 0, 0))
    return pl.pallas_call(
        functools.partial(_hg_body, n=n),
        out_shape=(jax.ShapeDtypeStruct((seqs * n, HG_VW), F32),
                   jax.ShapeDtypeStruct((seqs, 2, HG_HEADS, HG_DK, HG_DV), F32)),
        grid=(seqs, HG_HEADS // hps),
        in_specs=[zspec(0), zspec(1), zspec(2), zspec(3),
                  pl.BlockSpec((2, wb), lambda s, h: (0, h)), st_spec],
        out_specs=(pl.BlockSpec((n, wb), lambda s, h: (s, h)), st_spec),
        scratch_shapes=[pltpu.VMEM((2 * hps, HG_DV, HG_DK), F32), pltpu.VMEM((n, wb), F32)],
        compiler_params=_cparams("arbitrary", "arbitrary"),
        name="hgrn2_scan",
    )(z, z, z, z, lb, s0)


LRU_RB = 64
HALO = 8


LANES = 128


def _lru_body(x_ref, cw_ref, cb_ref, wr_ref, wi_ref, br_ref, bi_ref, lam_ref, h0_ref, hr_ref, e_ref,
              xp_scr, xc_scr, *cm_scr, n, col_major):
    rb = LRU_RB
    nb = n // rb
    w = LRU_W
    grid_rows = n // GRID_W
    slabs = w // LANES
    zeros = jnp.zeros((HALO, w), F32)
    xp_scr[0:HALO, :] = zeros
    xp_scr[HALO + n:HALO + n + HALO, :] = zeros
    if col_major:
        st_scr, hcm_scr = cm_scr
        for j in range(slabs):
            st_scr[j] = x_ref[:, j * LANES:(j + 1) * LANES]

        def to_col_major(c, carry):
            dst = pl.ds(pl.multiple_of(HALO + c * grid_rows, 8), grid_rows)
            for j in range(slabs):
                xp_scr[dst, j * LANES:(j + 1) * LANES] = st_scr[j, pl.ds(c, grid_rows, stride=GRID_W), :]
            return carry

        lax.fori_loop(0, GRID_W, to_col_major, 0)
        h_dst = hcm_scr
    else:
        xp_scr[HALO:HALO + n, :] = x_ref[...]
        h_dst = hr_ref
    cw = cw_ref[...]
    cb = cb_ref[...]

    def conv_blk(b, carry):
        base = pl.multiple_of(b * rb, rb)
        xh = xp_scr[pl.ds(base, rb + 2 * HALO), :]
        ext = rb + 2 * HALO
        acc = cb + cw[2:3, :] * xh[HALO:HALO + rb]
        acc = acc + cw[0:1, :] * pltpu.roll(xh, 2, axis=0)[HALO:HALO + rb]
        acc = acc + cw[1:2, :] * pltpu.roll(xh, 1, axis=0)[HALO:HALO + rb]
        acc = acc + cw[3:4, :] * pltpu.roll(xh, ext - 1, axis=0)[HALO:HALO + rb]
        xc_scr[pl.ds(base, rb), :] = acc
        return carry

    lax.fori_loop(0, nb, conv_blk, 0)

    rows = lax.broadcasted_iota(jnp.int32, (rb, w), 0)

    def gates(blk, d):
        r = pl.ds(pl.multiple_of(blk * rb, rb), rb)
        xc = xc_scr[r, :]
        xb = xc.astype(BF16)
        rg = jax.nn.sigmoid(jnp.dot(xb, wr_ref[d], preferred_element_type=F32) + br_ref[d:d + 1, :])
        ig = jax.nn.sigmoid(jnp.dot(xb, wi_ref[d], preferred_element_type=F32) + bi_ref[d:d + 1, :])
        log_a = (-LRU_C) * jax.nn.softplus(-lam_ref[d:d + 1, :]) * rg
        a = jnp.exp(log_a)
        u = jnp.sqrt(-jnp.tanh(log_a) * (a * a + 1.0)) * (ig * xc)
        return r, a, u

    def fwd_blk(blk, h_prev):
        r, a, u = gates(blk, 0)
        s = 1
        while s < rb:
            keep = rows >= s
            a_sh = jnp.where(keep, pltpu.roll(a, s, axis=0), 1.0)
            u_sh = jnp.where(keep, pltpu.roll(u, s, axis=0), 0.0)
            u = a * u_sh + u
            a = a * a_sh
            s *= 2
        h = u + a * h_prev
        h_dst[r, :] = h
        return h[rb - 1:rb, :]

    def bwd_blk(i, h_next):
        r, a, u = gates(nb - 1 - i, 1)
        s = 1
        while s < rb:
            keep = rows < rb - s
            a_sh = jnp.where(keep, pltpu.roll(a, rb - s, axis=0), 1.0)
            u_sh = jnp.where(keep, pltpu.roll(u, rb - s, axis=0), 0.0)
            u = a * u_sh + u
            a = a * a_sh
            s *= 2
        h = u + a * h_next
        h_dst[r, :] = h_dst[r, :] + h
        return h[0:1, :]

    e_ref[0, 0:1, :] = lax.fori_loop(0, nb, fwd_blk, h0_ref[0, 0:1, :])
    e_ref[0, 1:2, :] = lax.fori_loop(0, nb, bwd_blk, h0_ref[0, 1:2, :])
    if col_major:
        def to_row_major(c, carry):
            src = pl.ds(pl.multiple_of(c * grid_rows, 8), grid_rows)
            for j in range(slabs):
                st_scr[j, pl.ds(c, grid_rows, stride=GRID_W), :] = hcm_scr[src, j * LANES:(j + 1) * LANES]
            return carry

        lax.fori_loop(0, GRID_W, to_row_major, 0)
        for j in range(slabs):
            hr_ref[:, j * LANES:(j + 1) * LANES] = st_scr[j]


def _rglru(x, col_block, row_block0, n, seqs, col_major, conv_w, conv_b, wr_bd, wi_bd, b_r, b_i, lam, h0):
    w = LRU_W
    cm_scratch = [pltpu.VMEM((w // LANES, n, LANES), F32), pltpu.VMEM((n, w), F32)] if col_major else []
    full2 = lambda s: (0, 0)
    full3 = lambda s: (0, 0, 0)
    return pl.pallas_call(
        functools.partial(_lru_body, n=n, col_major=col_major),
        out_shape=(jax.ShapeDtypeStruct((seqs * n, w), F32), jax.ShapeDtypeStruct((seqs, 2, w), F32)),
        grid=(seqs,),
        in_specs=[pl.BlockSpec((n, w), lambda s: (row_block0 + s, col_block)),
                  pl.BlockSpec((CONV_W, w), full2), pl.BlockSpec((1, w), full2),
                  pl.BlockSpec((2, w, w), full3), pl.BlockSpec((2, w, w), full3),
                  pl.BlockSpec((2, w), full2), pl.BlockSpec((2, w), full2), pl.BlockSpec((2, w), full2),
                  pl.BlockSpec((1, 2, w), lambda s: (s, 0, 0))],
        out_specs=(pl.BlockSpec((n, w), lambda s: (s, 0)), pl.BlockSpec((1, 2, w), lambda s: (s, 0, 0))),
        scratch_shapes=[pltpu.VMEM((n + 2 * HALO, w), F32), pltpu.VMEM((n, w), F32)] + cm_scratch,
        compiler_params=_cparams("arbitrary"),
        name="rglru",
    )(x, conv_w, conv_b.reshape(1, w), wr_bd, wi_bd, b_r, b_i, lam, h0)


def _block_diag(wg):
    eye = jnp.eye(LRU_BLOCKS, dtype=wg.dtype)
    dense = wg[:, :, :, None, :] * eye[None, :, None, :, None]
    return dense.reshape(2, LRU_W, LRU_W).astype(BF16)


TM_OUT = 256


def _out_body(x_ref, o_ref, hr_ref, g_ref, xg_ref, ga0_ref, ga1_ref, gb0_ref, gb1_ref, m_ref,
              hgn_ref, n2_ref, wa_ref, wb_ref, wo_ref, wq_ref, x1_ref, h2_ref, q_ref):
    o = o_ref[...]
    parts = []
    for h in range(HG_HEADS):
        parts.append(_rms(o[:, h * HG_DV:(h + 1) * HG_DV]) * hgn_ref[...])
    on = jnp.concatenate(parts, axis=1) * _silu(g_ref[...])
    y_a = jnp.dot(on.astype(BF16), wa_ref[...], preferred_element_type=F32)
    y_b = jnp.dot((hr_ref[...] * jax.nn.gelu(xg_ref[...])).astype(BF16), wb_ref[...], preferred_element_type=F32)
    ga = jnp.concatenate([ga0_ref[...], ga1_ref[...]], axis=1)
    gb = jnp.concatenate([gb0_ref[...], gb1_ref[...]], axis=1)
    merged = jax.nn.sigmoid(ga) * y_a + jax.nn.sigmoid(gb) * y_b
    mix = jnp.dot(merged.astype(BF16), wo_ref[...], preferred_element_type=F32)
    x1 = x_ref[...] + m_ref[0, 2:3, :] * mix
    x1_ref[...] = x1
    h2 = _rms(x1) * n2_ref[...] * (1.0 + m_ref[0, 4:5, :]) + m_ref[0, 3:4, :]
    h2_ref[...] = h2
    q_ref[...] = jnp.dot(h2.astype(BF16), wq_ref[...], preferred_element_type=F32).astype(BF16)


def _out_proj(x, o_hg, hr, z, mod, hg_norm, norm2, wa, wb, wo, wq, row_of):
    t, d = x.shape
    tm = TM_OUT
    nq = PK_HEADS * PK_DQ
    half = lambda i: (i, 0)
    zc = lambda c: pl.BlockSpec((tm, LRU_W), lambda i: (i, c))
    const = lambda i: (0, 0)
    return pl.pallas_call(
        _out_body,
        out_shape=(jax.ShapeDtypeStruct((t, d), F32), jax.ShapeDtypeStruct((t, d), F32),
                   jax.ShapeDtypeStruct((t, nq), BF16)),
        grid=(t // tm,),
        in_specs=[pl.BlockSpec((tm, d), half), pl.BlockSpec((tm, HG_VW), half), pl.BlockSpec((tm, LRU_W), half),
                  zc(COL_G), zc(COL_XG), zc(COL_GA), zc(COL_GA + 1), zc(COL_GB), zc(COL_GB + 1),
                  pl.BlockSpec((1, N_MOD, d), lambda i: (row_of(i), 0, 0)),
                  pl.BlockSpec((1, HG_DV), const), pl.BlockSpec((1, d), const),
                  pl.BlockSpec((HG_VW, d), const), pl.BlockSpec((LRU_W, d), const),
                  pl.BlockSpec((d, d), const), pl.BlockSpec((d, nq), const)],
        out_specs=(pl.BlockSpec((tm, d), half), pl.BlockSpec((tm, d), half), pl.BlockSpec((tm, nq), half)),
        compiler_params=_cparams("arbitrary"),
        name="out_proj",
    )(x, o_hg, hr, z, z, z, z, z, z, mod, hg_norm.reshape(1, HG_DV), norm2.reshape(1, d), wa, wb, wo, wq)


TT_TOPK = 128


def _extract_topk(s, order, k, payload=None):
    vals, ords, picks = [], [], []
    for _ in range(k):
        m = jnp.max(s, axis=0, keepdims=True)
        o = jnp.max(jnp.where(s == m, order, -1.0), axis=0, keepdims=True)
        hit = order == o
        vals.append(m)
        ords.append(o)
        if payload is not None:
            picks.append(jnp.sum(jnp.where(hit, payload, 0), axis=0, keepdims=True))
        s = jnp.where(hit, -jnp.inf, s)
    cat = lambda xs: jnp.concatenate(xs, axis=0)
    return cat(vals), cat(ords), (cat(picks) if payload is not None else None)


PAIR_ROWS = 4
PAIR_COLS = 3
assert all((i + 1) * (j + 1) > PK_TOPK for i in range(PAIR_ROWS, PK_TOPK) for j in range(PAIR_COLS, PK_TOPK))
assert all((i + 1) * (8 + 1) > PK_TOPK for i in range(1, PAIR_ROWS)) and (8 + 1) * (1 + 1) > PK_TOPK


def _pair_candidates(sv, si, tt):
    cand, order, eid = [], [], []
    hi = float(PK_TOPK * PK_TOPK - 1)
    for i in range(PAIR_ROWS):
        nj = PK_TOPK if i == 0 else 8
        j = lax.broadcasted_iota(jnp.int32, (nj, tt), 0)
        cand.append(sv[0][i:i + 1, :] + sv[1][:nj])
        order.append(hi - (i * PK_TOPK + j).astype(F32))
        eid.append(si[0][i:i + 1, :] * N_KEYS + si[1][:nj])
    for j in range(PAIR_COLS):
        ni = PK_TOPK if j == 0 else 8
        i = lax.broadcasted_iota(jnp.int32, (ni, tt), 0)
        fresh = i >= PAIR_ROWS
        cand.append(jnp.where(fresh, sv[0][:ni] + sv[1][j:j + 1, :], -jnp.inf))
        order.append(jnp.where(fresh, hi - (i * PK_TOPK + j).astype(F32), -2.0))
        eid.append(si[0][:ni] * N_KEYS + si[1][j:j + 1, :])
    cat = lambda xs: jnp.concatenate(xs, axis=0)
    return cat(cand), cat(order), cat(eid)


def _topk_body(q_ref, keys_ref, eidx_ref, gate_ref):
    tt = q_ref.shape[0]
    key_order = (N_KEYS - 1 - lax.broadcasted_iota(jnp.int32, (N_KEYS, tt), 0)).astype(F32)
    e_rows, g_rows = [], []
    for h in range(PK_HEADS):
        sv, si = [], []
        for p in range(2):
            c0 = h * PK_DQ + p * PK_DH
            s = lax.dot_general(keys_ref[p, h], q_ref[:, c0:c0 + PK_DH], (((1,), (1,)), ((), ())),
                                preferred_element_type=F32)
            v, o, _ = _extract_topk(s, key_order, PK_TOPK)
            sv.append(v)
            si.append(N_KEYS - 1 - o.astype(jnp.int32))
        cand, order, cidx = _pair_candidates(sv, si, tt)
        best, _, eid = _extract_topk(cand, order, PK_TOPK, payload=cidx)
        ex = jnp.exp(best - best[0:1, :])
        g_rows.append(ex / jnp.sum(ex, axis=0, keepdims=True))
        e_rows.append(eid)
    eidx_ref[...] = jnp.concatenate(e_rows, axis=0).T
    gate_ref[...] = jnp.concatenate(g_rows, axis=0).T


def _pk_topk(q, keys_bf):
    t = q.shape[0]
    tt = TT_TOPK
    ne = PK_HEADS * PK_TOPK
    return pl.pallas_call(
        _topk_body,
        out_shape=(jax.ShapeDtypeStruct((t, ne), jnp.int32), jax.ShapeDtypeStruct((t, ne), F32)),
        grid=(t // tt,),
        in_specs=[pl.BlockSpec((tt, PK_HEADS * PK_DQ), lambda i: (i, 0)),
                  pl.BlockSpec((2, PK_HEADS, N_KEYS, PK_DH), lambda i: (0, 0, 0, 0))],
        out_specs=(pl.BlockSpec((tt, ne), lambda i: (i, 0)), pl.BlockSpec((tt, ne), lambda i: (i, 0))),
        compiler_params=_cparams("arbitrary"),
        name="pk_topk",
    )(q, keys_bf)


SC_LANES = 16
SC_CORES = 2
SC_SUBCORES = 16
SC_WORKERS = SC_CORES * SC_SUBCORES
SC_HALF = D_MODEL // 2
SC_ROW_CHUNKS = SC_HALF // SC_LANES
SC_TOKEN_BLOCK = 8


def _pack_bf16_pairs(tab):
    lo = lax.bitcast_convert_type(tab[:, SC_HALF:].astype(BF16), jnp.uint16).astype(jnp.uint32)
    bits = lax.bitcast_convert_type(tab[:, :SC_HALF], jnp.uint32)
    sign = bits & jnp.uint32(0x80000000)
    mag = (bits & jnp.uint32(0x7FFFFFFF)) + jnp.uint32(1 << 15)
    hi = jnp.where(mag >= lo, (mag - lo) >> 16, jnp.uint32(0))
    return lax.bitcast_convert_type(sign | (hi << 16) | lo, jnp.int32)


def _unpack_pair(w):
    return lax.bitcast_convert_type(w, F32), lax.bitcast_convert_type(w << 16, F32)


def _gelu_tanh(x):
    y = 0.7978845608028654 * (x + 0.044715 * (x * x * x))
    t = 1.0 - 2.0 / (jnp.exp(2.0 * y) + 1.0)
    return x * (0.5 * (1.0 + t))


def _peer_sc_body(h_hbm, idx_hbm, gate_hbm, u_hbm, v_hbm, out_hbm,
                  h_v, idx_v, gate_v, out_v, ub0, ub1, vb0, vb1, su0, su1, sv0, sv1, *, tokens_per_worker):
    nh, k, tb, lanes = PK_HEADS, PK_TOPK, SC_TOKEN_BLOCK, SC_LANES
    wid = lax.axis_index("s") * SC_CORES + lax.axis_index("c")
    ubufs, vbufs, sus, svs = (ub0, ub1), (vb0, vb1), (su0, su1), (sv0, sv1)
    lane = lax.iota(jnp.int32, lanes)
    zero = jnp.zeros((lanes,), F32)

    def start(tok, hd, par):
        irow = idx_v.at[tok * nh + hd]
        pltpu.async_copy(u_hbm.at[irow], ubufs[par], sus[par])
        pltpu.async_copy(v_hbm.at[irow], vbufs[par], svs[par])

    def wait(par):
        irow = idx_v.at[0]
        pltpu.make_async_copy(u_hbm.at[irow], ubufs[par], sus[par]).wait()
        pltpu.make_async_copy(v_hbm.at[irow], vbufs[par], svs[par]).wait()

    @pl.loop(0, tokens_per_worker // tb)
    def _(blk):
        base = wid * tokens_per_worker + blk * tb
        pltpu.sync_copy(h_hbm.at[pl.ds(base, tb)], h_v)
        pltpu.sync_copy(idx_hbm.at[pl.ds(base * nh, tb * nh)], idx_v)
        pltpu.sync_copy(gate_hbm.at[pl.ds(base * nh, tb * nh)], gate_v)
        start(0, 0, 0)

        @pl.loop(0, tb)
        def _(tok):
            for hd in range(nh):
                par = hd % 2
                if hd + 1 < nh:
                    start(tok, hd + 1, 1 - par)
                else:
                    @pl.when(tok + 1 < tb)
                    def _():
                        start(tok + 1, 0, 1 - par)
                wait(par)
                ub, vb = ubufs[par], vbufs[par]

                @plsc.parallel_loop(0, SC_ROW_CHUNKS, carry=(zero,) * k)
                def accs(c, acc):
                    ha = h_v[tok, pl.ds(c * lanes, lanes)]
                    hb = h_v[tok, pl.ds(SC_HALF + c * lanes, lanes)]
                    out = []
                    for r in range(k):
                        a, b = _unpack_pair(ub[r, pl.ds(c * lanes, lanes)])
                        out.append(acc[r] + (ha * a + hb * b))
                    return tuple(out)

                s_vec = zero
                for r in range(k):
                    s_vec = jnp.where(lane == r, jnp.sum(accs[r]), s_vec)
                w_vec = gate_v[tok * nh + hd, :] * _gelu_tanh(s_vec)
                wb = [jnp.sum(jnp.where(lane == r, w_vec, 0.0)) for r in range(k)]

                @plsc.parallel_loop(0, SC_ROW_CHUNKS)
                def _(c):
                    pa, pb = [], []
                    for r in range(k):
                        a, b = _unpack_pair(vb[r, pl.ds(c * lanes, lanes)])
                        pa.append(wb[r] * a)
                        pb.append(wb[r] * b)
                    while len(pa) > 1:
                        pa = [pa[i] + pa[i + 1] for i in range(0, len(pa), 2)]
                        pb = [pb[i] + pb[i + 1] for i in range(0, len(pb), 2)]
                    for off, o in ((0, pa[0]), (SC_HALF, pb[0])):
                        cols = pl.ds(off + c * lanes, lanes)
                        if hd == 0:
                            out_v[tok, cols] = o
                        else:
                            out_v[tok, cols] = out_v[tok, cols] + o

        pltpu.sync_copy(out_v, out_hbm.at[pl.ds(base, tb)])


def _peer_experts(h, eidx, gate, u_tab, v_tab):
    t, d = h.shape
    assert t % (SC_WORKERS * SC_TOKEN_BLOCK) == 0 and d == D_MODEL
    rows = pltpu.VMEM((PK_TOPK, SC_HALF), jnp.int32)
    return pl.kernel(
        functools.partial(_peer_sc_body, tokens_per_worker=t // SC_WORKERS),
        out_type=jax.ShapeDtypeStruct((t, d), F32),
        mesh=plsc.VectorSubcoreMesh(core_axis_name="c", subcore_axis_name="s"),
        scratch_types=[
            pltpu.VMEM((SC_TOKEN_BLOCK, d), F32),
            pltpu.VMEM((SC_TOKEN_BLOCK * PK_HEADS, PK_TOPK), jnp.int32),
            pltpu.VMEM((SC_TOKEN_BLOCK * PK_HEADS, PK_TOPK), F32),
            pltpu.VMEM((SC_TOKEN_BLOCK, d), F32),
            rows, rows, rows, rows,
            pltpu.SemaphoreType.DMA, pltpu.SemaphoreType.DMA,
            pltpu.SemaphoreType.DMA, pltpu.SemaphoreType.DMA,
        ],
        compiler_params=pltpu.CompilerParams(needs_layout_passes=False),
        name="peer_experts_sc",
    )(h, eidx, gate, u_tab, v_tab)


TT_DENSE = 256
EB_DENSE = 512
N_EXPERTS = N_KEYS * N_KEYS
N_SEL = PK_HEADS * PK_TOPK


def _dense_body(h_ref, e_ref, g_ref, u_ref, v_ref, o_ref, g_scr, acc_scr):
    eb = pl.program_id(1)
    tt = h_ref.shape[0]
    slabs = EB_DENSE // N_KEYS

    @pl.when(eb == 0)
    def _():
        acc_scr[...] = jnp.zeros_like(acc_scr)
        sub = lax.broadcasted_iota(jnp.int32, (N_KEYS, N_SEL), 0)

        def per_token(t, carry):
            e = e_ref[pl.ds(t, 1), :]
            p_t = jnp.where(sub == (e >> 7), g_ref[pl.ds(t, 1), :], 0.0).astype(BF16)
            q_t = (sub == (e & (N_KEYS - 1))).astype(BF16)
            g_scr[pl.ds(pl.multiple_of(t * N_KEYS, N_KEYS), N_KEYS), :] = lax.dot_general(
                p_t, q_t, (((1,), (1,)), ((), ())), preferred_element_type=F32)
            return carry

        lax.fori_loop(0, tt, per_token, 0, unroll=8)

    s = lax.dot_general(h_ref[...].astype(BF16), u_ref[...], (((1,), (1,)), ((), ())), preferred_element_type=F32)
    gs = jnp.concatenate([g_scr[pl.ds(eb * slabs + k, tt, stride=N_KEYS), :] for k in range(slabs)], axis=1)
    acc_scr[...] += jnp.dot((gs * jax.nn.gelu(s)).astype(BF16), v_ref[...], preferred_element_type=F32)

    @pl.when(eb == pl.num_programs(1) - 1)
    def _():
        o_ref[...] = acc_scr[...]


def _peer_dense(h, eidx, gate, u_bf, v_bf):
    t, d = h.shape
    tt = TT_DENSE
    assert t % tt == 0 and N_KEYS == 128
    tile = lambda i, e: (i, 0)
    blk = lambda i, e: (e, 0)
    return pl.pallas_call(
        _dense_body,
        out_shape=jax.ShapeDtypeStruct((t, d), F32),
        grid=(t // tt, N_EXPERTS // EB_DENSE),
        in_specs=[pl.BlockSpec((tt, d), tile), pl.BlockSpec((tt, N_SEL), tile), pl.BlockSpec((tt, N_SEL), tile),
                  pl.BlockSpec((EB_DENSE, d), blk), pl.BlockSpec((EB_DENSE, d), blk)],
        out_specs=pl.BlockSpec((tt, d), tile),
        scratch_shapes=[pltpu.VMEM((tt * N_KEYS, N_KEYS), F32), pltpu.VMEM((tt, d), F32)],
        compiler_params=_cparams("arbitrary", "arbitrary"),
        name="peer_dense_tc",
    )(h, eidx, gate, u_bf, v_bf)


TM_FIN = 512


def _final_body(x1_ref, p_ref, m_ref, w_ref, y_ref):
    y_ref[...] = _rms(x1_ref[...] + m_ref[0, 5:6, :] * p_ref[...]) * w_ref[...]


def _final(x1, peer, mod, norm_f, rows_of):
    t, d = x1.shape
    tm = _token_tile(t, TM_FIN)
    row_of = rows_of(tm)
    tile = pl.BlockSpec((tm, d), lambda i: (i, 0))
    return pl.pallas_call(
        _final_body,
        out_shape=jax.ShapeDtypeStruct((t, d), F32),
        grid=(t // tm,),
        in_specs=[tile, tile, pl.BlockSpec((1, N_MOD, d), lambda i: (row_of(i), 0, 0)),
                  pl.BlockSpec((1, d), lambda i: (0, 0))],
        out_specs=tile,
        compiler_params=_cparams("arbitrary"),
        name="final_norm",
    )(x1, peer, mod, norm_f.reshape(1, d))


CTX_GROUPS = (1, 15, 16)
LAT_GROUP = 2
DENSE_TOKENS = 35 * 256


def _zero_of(a):
    return (lax.shift_right_logical(a[0, 0], 31) >> 1).astype(F32)


def _group(x, mod, row0, per_seq, n, hg_s0, lru_s0, col_major, p, after, dense_tokens=0):
    seqs = hg_s0.shape[0]
    t = seqs * n
    for a in after:
        mod = mod + _zero_of(a)
    rows_of = lambda tm: _mod_row_map(tm, n, row0, per_seq)
    z = _in_proj(x, mod, p['norm1'], p['w_in'], rows_of)
    o_hg, hg_fin = _hgrn2(z, p['lb'], hg_s0, n, 0)
    hr, lru_fin = _rglru(z, COL_XR, 0, n, seqs, col_major, *p['lru'], lru_s0)
    x1, h2, q = _out_proj(x, o_hg, hr, z, mod, p['hg_norm'], p['norm2'], p['w_a'], p['w_b'], p['w_o'], p['w_q'],
                          rows_of(TM_OUT))
    eidx, gate = _pk_topk(q, p['keys'])
    ts = t - dense_tokens
    parts = []
    if ts:
        parts.append(_peer_experts(h2[:ts], eidx[:ts].reshape(ts * PK_HEADS, PK_TOPK),
                                   gate[:ts].reshape(ts * PK_HEADS, PK_TOPK), p['pk_u'], p['pk_v']))
    if dense_tokens:
        parts.append(_peer_dense(h2[ts:], eidx[ts:], gate[ts:], p['u_bf'], p['v_bf']))
    peer = parts[0] if len(parts) == 1 else jnp.concatenate(parts, axis=0)
    return _final(x1, peer, mod, p['norm_f'], rows_of), hg_fin, lru_fin, eidx


def kernel(x_prompt, x_sample, state_hgrn, state_rglru, c, c_ctx, w_mod, b_mod, norm1, w_in,
           hg_lb_logits, hg_norm, lru_conv_w, lru_conv_b, lru_w_r, lru_b_r, lru_w_i, lru_b_i,
           lru_lambda, w_branch_a, w_branch_b, w_out, norm2, pk_w_q, pk_sub_keys, pk_u, pk_v,
           norm_f):
    assert w_mod.shape[0] == 1, "single trunk layer"
    d = D_MODEL
    nb_ctx, n_ctx, _ = x_prompt.shape
    nb_lat, n_lat, _ = x_sample.shape
    assert nb_lat < MOD_ROWS and nb_lat % LAT_GROUP == 0 and nb_ctx == sum(CTX_GROUPS)
    ctx_row = nb_lat

    cond = jnp.zeros((MOD_ROWS, d), F32).at[:nb_lat].set(c).at[ctx_row].set(c_ctx)
    mod = _modulation(cond, w_mod[0], b_mod[0])
    p = {
        'lb': jnp.cumsum(jax.nn.softmax(hg_lb_logits.astype(F32), axis=1), axis=1)[:, 0],
        'norm1': norm1[0], 'w_in': w_in[0].astype(BF16), 'hg_norm': hg_norm[0], 'norm2': norm2[0],
        'lru': (lru_conv_w[0], lru_conv_b[0], _block_diag(lru_w_r[0]), _block_diag(lru_w_i[0]),
                lru_b_r[0], lru_b_i[0], lru_lambda[0]),
        'w_a': w_branch_a[0].astype(BF16), 'w_b': w_branch_b[0].astype(BF16), 'w_o': w_out[0].astype(BF16),
        'w_q': pk_w_q[0].astype(BF16), 'keys': pk_sub_keys[0].astype(BF16),
        'pk_u': _pack_bf16_pairs(pk_u[0]), 'pk_v': _pack_bf16_pairs(pk_v[0]), 'norm_f': norm_f,
        'u_bf': pk_u[0].astype(BF16), 'v_bf': pk_v[0].astype(BF16),
    }
    y_ctx, hg_fin, lru_fin = [], [], []
    tables = [p['pk_u'], p['pk_v']]
    after = tables
    s0 = 0
    for g in CTX_GROUPS:
        y, hg, lr, eidx = _group(
            x_prompt[s0:s0 + g].reshape(g * n_ctx, d), mod, ctx_row, False, n_ctx,
            jnp.zeros((g, 2, HG_HEADS, HG_DK, HG_DV), F32), jnp.zeros((g, 2, LRU_W), F32), False, p, after)
        s0 += g
        after = [eidx] if after is tables else after
        y_ctx.append(y.reshape(g, n_ctx, d))
        hg_fin.append(hg)
        lru_fin.append(lr)
    y_lat = []
    tg = LAT_GROUP * n_lat
    for s0 in range(0, nb_lat, LAT_GROUP):
        sl = slice(s0, s0 + LAT_GROUP)
        tokens_after = (nb_lat - s0 - LAT_GROUP) * n_lat
        y, _, _, _ = _group(x_sample[sl].reshape(tg, d), mod, s0, True, n_lat,
                            state_hgrn[sl, 0], state_rglru[sl, 0], True, p, after,
                            dense_tokens=min(tg, max(0, DENSE_TOKENS - tokens_after)))
        y_lat.append(y.reshape(LAT_GROUP, n_lat, d))
    return (jnp.concatenate(y_ctx, axis=0), jnp.concatenate(y_lat, axis=0),
            jnp.concatenate(hg_fin, axis=0)[:, None], jnp.concatenate(lru_fin, axis=0)[:, None])
```

```python
import functools

import jax, jax.numpy as jnp
from jax import lax
from jax.experimental import pallas as pl
from jax.experimental.pallas import tpu as pltpu
from jax.experimental.pallas import tpu_sc as plsc

D_MODEL = 1024
GRID_W = 64
EPS = 1e-6
HG_HEADS = 4
HG_DK = 128
HG_DV = 128
HG_KW = HG_HEADS * HG_DK
HG_VW = HG_HEADS * HG_DV
HG_CHUNK = 32
LRU_W = D_MODEL // 2
LRU_BLOCKS = 8
LRU_BW = LRU_W // LRU_BLOCKS
CONV_W = 4
LRU_C = 8.0
PK_HEADS = 8
N_KEYS = 128
PK_TOPK = 16
PK_DQ = 256
PK_DH = PK_DQ // 2
IN_W = 3 * HG_KW + 2 * HG_VW + 2 * LRU_W + 2 * D_MODEL
COL_G, COL_XR, COL_XG, COL_GA, COL_GB = 4, 5, 6, 7, 9
N_MOD = 6
MOD_ROWS = 16
F32 = jnp.float32
BF16 = jnp.bfloat16
VMEM_LIMIT = 48 * 1024 * 1024


def _cparams(*sem):
    return pltpu.CompilerParams(dimension_semantics=sem, vmem_limit_bytes=VMEM_LIMIT)


def _silu(x):
    return x * jax.nn.sigmoid(x)


def _rms(x):
    return x * lax.rsqrt(jnp.mean(x * x, axis=-1, keepdims=True) + EPS)


def _mod_body(c_ref, w_ref, b_ref, o_ref):
    o_ref[...] = lax.dot_general(_silu(c_ref[...]), w_ref[...], (((1,), (0,)), ((), ())),
                                 precision=lax.Precision.HIGHEST, preferred_element_type=F32) + b_ref[...]


def _modulation(cond, w_mod, b_mod):
    d = D_MODEL
    out = pl.pallas_call(
        _mod_body,
        out_shape=jax.ShapeDtypeStruct((MOD_ROWS, N_MOD * d), F32),
        grid=(N_MOD,),
        in_specs=[pl.BlockSpec((MOD_ROWS, d), lambda j: (0, 0)),
                  pl.BlockSpec((d, d), lambda j: (0, j)),
                  pl.BlockSpec((1, d), lambda j: (0, j))],
        out_specs=pl.BlockSpec((MOD_ROWS, d), lambda j: (0, j)),
        compiler_params=_cparams("arbitrary"),
        name="adaln_modulation",
    )(cond, w_mod, b_mod.reshape(1, N_MOD * d))
    return out.reshape(MOD_ROWS, N_MOD, d)


def _mod_row_map(tm, n, row0, per_seq):
    per = n // tm
    return (lambda i: row0 + i // per) if per_seq else (lambda i: row0)


TM_IN = 512
TOKEN_TILE_MIN = 256
TN_IN = IN_W // 2


def _in_body(x_ref, m_ref, n1_ref, w_ref, z_ref, h_scr):
    @pl.when(pl.program_id(1) == 0)
    def _():
        y = _rms(x_ref[...]) * n1_ref[...]
        h_scr[...] = (y * (1.0 + m_ref[0, 1:2, :]) + m_ref[0, 0:1, :]).astype(BF16)

    z_ref[...] = jnp.dot(h_scr[...], w_ref[...], preferred_element_type=F32)


def _token_tile(t, preferred):
    tm = preferred if t % preferred == 0 else TOKEN_TILE_MIN
    assert t % tm == 0
    return tm


def _in_proj(x, mod, norm1, w_in_bf, rows_of):
    t, d = x.shape
    tm = _token_tile(t, TM_IN)
    row_of = rows_of(tm)
    return pl.pallas_call(
        _in_body,
        out_shape=jax.ShapeDtypeStruct((t, IN_W), F32),
        grid=(t // tm, IN_W // TN_IN),
        in_specs=[pl.BlockSpec((tm, d), lambda i, j: (i, 0)),
                  pl.BlockSpec((1, N_MOD, d), lambda i, j: (row_of(i), 0, 0)),
                  pl.BlockSpec((1, d), lambda i, j: (0, 0)),
                  pl.BlockSpec((d, TN_IN), lambda i, j: (0, j))],
        out_specs=pl.BlockSpec((tm, TN_IN), lambda i, j: (i, j)),
        scratch_shapes=[pltpu.VMEM((tm, d), BF16)],
        compiler_params=_cparams("arbitrary", "arbitrary"),
        name="in_proj",
    )(x, mod, norm1.reshape(1, d), w_in_bf)


HG_LONG_SEQ = 1024


def _heads_per_step(n):
    return HG_HEADS if n < HG_LONG_SEQ else HG_HEADS // 2


def _hg_body(q_ref, ff_ref, fb_ref, v_ref, lb_ref, s0_ref, o_ref, sfin_ref, st_scr, ob_scr, *, n, hps):
    c = HG_CHUNK
    nc = n // c
    row = lax.broadcasted_iota(jnp.int32, (c, c), 0)
    col = lax.broadcasted_iota(jnp.int32, (c, c), 1)
    lower = row >= col
    tri = (lower.astype(F32), (row <= col).astype(F32))
    masks = (lower, row <= col)
    f_refs = (ff_ref, fb_ref)
    for d in range(2):
        for hh in range(hps):
            st_scr[d * hps + hh] = s0_ref[0, d, hh].T

    def step(i, carry):
        chains = [(d, hh) for d in range(2) for hh in range(hps)]
        st1 = []
        for d, hh in chains:
            ci = i if d == 0 else nc - 1 - i
            r = pl.ds(pl.multiple_of(ci * c, c), c)
            hc = slice(hh * HG_DK, (hh + 1) * HG_DK)
            lb = lb_ref[d:d + 1, hc]
            f = lb + (1.0 - lb) * jax.nn.sigmoid(f_refs[d][r, hc])
            lf = jnp.log(f)
            cum = lax.dot_general(tri[d], lf, (((1,), (0,)), ((), ())),
                                  precision=lax.Precision.HIGHEST, preferred_element_type=F32)
            st1.append((r, hc, 1.0 - f, lf, cum))
        st2 = []
        for (d, hh), (r, hc, k, lf, cum) in zip(chains, st1):
            q = _silu(q_ref[r, hc])
            v = v_ref[r, hc].astype(BF16)
            tot = jnp.sum(lf, axis=0, keepdims=True)
            q_dec = (q * jnp.exp(cum)).astype(BF16)
            k_inv = (k * jnp.exp(-cum)).astype(BF16)
            k_end = (k * jnp.exp(tot - cum)).astype(BF16)
            st = st_scr[d * hps + hh]
            att = lax.dot_general(q_dec, k_inv, (((1,), (1,)), ((), ())), preferred_element_type=F32)
            o_inter = lax.dot_general(q_dec, st.astype(BF16), (((1,), (1,)), ((), ())), preferred_element_type=F32)
            ds_t = lax.dot_general(v, k_end, (((0,), (0,)), ((), ())), preferred_element_type=F32)
            st_scr[d * hps + hh] = st * jnp.exp(tot) + ds_t
            st2.append((v, att, o_inter))
        for (d, hh), (r, hc, _, _, _), (v, att, o_inter) in zip(chains, st1, st2):
            att = jnp.where(masks[d], att, 0.0).astype(BF16)
            o = jnp.dot(att, v, preferred_element_type=F32) + o_inter
            if d == 0:
                o_ref[r, hc] = o
            else:
                ob_scr[r, hc] = o
        return carry

    lax.fori_loop(0, nc, step, 0)
    o_ref[...] = o_ref[...] + ob_scr[...]
    for d in range(2):
        for hh in range(hps):
            sfin_ref[0, d, hh] = st_scr[d * hps + hh].T


def _hgrn2(z, lb, s0, n, row_block0):
    seqs = s0.shape[0]
    hps = _heads_per_step(n)
    wb = hps * HG_DK
    nblk = HG_KW // wb

    def zspec(seg):
        return pl.BlockSpec((n, wb), lambda s, h: (row_block0 + s, seg * nblk + h))

    st_spec = pl.BlockSpec((1, 2, hps, HG_DK, HG_DV), lambda s, h: (s, 0, h, 0, 0))
    return pl.pallas_call(
        functools.partial(_hg_body, n=n, hps=hps),
        out_shape=(jax.ShapeDtypeStruct((seqs * n, HG_VW), F32),
                   jax.ShapeDtypeStruct((seqs, 2, HG_HEADS, HG_DK, HG_DV), F32)),
        grid=(seqs, HG_HEADS // hps),
        in_specs=[zspec(0), zspec(1), zspec(2), zspec(3),
                  pl.BlockSpec((2, wb), lambda s, h: (0, h)), st_spec],
        out_specs=(pl.BlockSpec((n, wb), lambda s, h: (s, h)), st_spec),
        scratch_shapes=[pltpu.VMEM((2 * hps, HG_DV, HG_DK), F32), pltpu.VMEM((n, wb), F32)],
        compiler_params=_cparams("arbitrary", "arbitrary"),
        name="hgrn2_scan",
    )(z, z, z, z, lb, s0)


LRU_RB = 64
HALO = 8


LANES = 128


def _lru_body(x_ref, cw_ref, cb_ref, wr_ref, wi_ref, br_ref, bi_ref, lam_ref, h0_ref, hr_ref, e_ref,
              xp_scr, xc_scr, *cm_scr, n, col_major):
    rb = LRU_RB
    nb = n // rb
    w = LRU_W
    grid_rows = n // GRID_W
    slabs = w // LANES
    zeros = jnp.zeros((HALO, w), F32)
    xp_scr[0:HALO, :] = zeros
    xp_scr[HALO + n:HALO + n + HALO, :] = zeros
    if col_major:
        st_scr, hcm_scr = cm_scr
        for j in range(slabs):
            st_scr[j] = x_ref[:, j * LANES:(j + 1) * LANES]

        def to_col_major(c, carry):
            dst = pl.ds(pl.multiple_of(HALO + c * grid_rows, 8), grid_rows)
            for j in range(slabs):
                xp_scr[dst, j * LANES:(j + 1) * LANES] = st_scr[j, pl.ds(c, grid_rows, stride=GRID_W), :]
            return carry

        lax.fori_loop(0, GRID_W, to_col_major, 0)
        h_dst = hcm_scr
    else:
        xp_scr[HALO:HALO + n, :] = x_ref[...]
        h_dst = hr_ref
    cw = cw_ref[...]
    cb = cb_ref[...]

    def conv_blk(b, carry):
        base = pl.multiple_of(b * rb, rb)
        xh = xp_scr[pl.ds(base, rb + 2 * HALO), :]
        ext = rb + 2 * HALO
        acc = cb + cw[2:3, :] * xh[HALO:HALO + rb]
        acc = acc + cw[0:1, :] * pltpu.roll(xh, 2, axis=0)[HALO:HALO + rb]
        acc = acc + cw[1:2, :] * pltpu.roll(xh, 1, axis=0)[HALO:HALO + rb]
        acc = acc + cw[3:4, :] * pltpu.roll(xh, ext - 1, axis=0)[HALO:HALO + rb]
        xc_scr[pl.ds(base, rb), :] = acc
        return carry

    lax.fori_loop(0, nb, conv_blk, 0)

    rows = lax.broadcasted_iota(jnp.int32, (rb, w), 0)

    def gates(blk, d):
        r = pl.ds(pl.multiple_of(blk * rb, rb), rb)
        xc = xc_scr[r, :]
        xb = xc.astype(BF16)
        rg = jax.nn.sigmoid(jnp.dot(xb, wr_ref[d], preferred_element_type=F32) + br_ref[d:d + 1, :])
        ig = jax.nn.sigmoid(jnp.dot(xb, wi_ref[d], preferred_element_type=F32) + bi_ref[d:d + 1, :])
        log_a = (-LRU_C) * jax.nn.softplus(-lam_ref[d:d + 1, :]) * rg
        a = jnp.exp(log_a)
        u = jnp.sqrt(-jnp.tanh(log_a) * (a * a + 1.0)) * (ig * xc)
        return r, a, u

    def fwd_blk(blk, h_prev):
        r, a, u = gates(blk, 0)
        s = 1
        while s < rb:
            keep = rows >= s
            a_sh = jnp.where(keep, pltpu.roll(a, s, axis=0), 1.0)
            u_sh = jnp.where(keep, pltpu.roll(u, s, axis=0), 0.0)
            u = a * u_sh + u
            a = a * a_sh
            s *= 2
        h = u + a * h_prev
        h_dst[r, :] = h
        return h[rb - 1:rb, :]

    def bwd_blk(i, h_next):
        r, a, u = gates(nb - 1 - i, 1)
        s = 1
        while s < rb:
            keep = rows < rb - s
            a_sh = jnp.where(keep, pltpu.roll(a, rb - s, axis=0), 1.0)
            u_sh = jnp.where(keep, pltpu.roll(u, rb - s, axis=0), 0.0)
            u = a * u_sh + u
            a = a * a_sh
            s *= 2
        h = u + a * h_next
        h_dst[r, :] = h_dst[r, :] + h
        return h[0:1, :]

    e_ref[0, 0:1, :] = lax.fori_loop(0, nb, fwd_blk, h0_ref[0, 0:1, :])
    e_ref[0, 1:2, :] = lax.fori_loop(0, nb, bwd_blk, h0_ref[0, 1:2, :])
    if col_major:
        def to_row_major(c, carry):
            src = pl.ds(pl.multiple_of(c * grid_rows, 8), grid_rows)
            for j in range(slabs):
                st_scr[j, pl.ds(c, grid_rows, stride=GRID_W), :] = hcm_scr[src, j * LANES:(j + 1) * LANES]
            return carry

        lax.fori_loop(0, GRID_W, to_row_major, 0)
        for j in range(slabs):
            hr_ref[:, j * LANES:(j + 1) * LANES] = st_scr[j]


def _rglru(x, col_block, row_block0, n, seqs, col_major, conv_w, conv_b, wr_bd, wi_bd, b_r, b_i, lam, h0):
    w = LRU_W
    cm_scratch = [pltpu.VMEM((w // LANES, n, LANES), F32), pltpu.VMEM((n, w), F32)] if col_major else []
    full2 = lambda s: (0, 0)
    full3 = lambda s: (0, 0, 0)
    return pl.pallas_call(
        functools.partial(_lru_body, n=n, col_major=col_major),
        out_shape=(jax.ShapeDtypeStruct((seqs * n, w), F32), jax.ShapeDtypeStruct((seqs, 2, w), F32)),
        grid=(seqs,),
        in_specs=[pl.BlockSpec((n, w), lambda s: (row_block0 + s, col_block)),
                  pl.BlockSpec((CONV_W, w), full2), pl.BlockSpec((1, w), full2),
                  pl.BlockSpec((2, w, w), full3), pl.BlockSpec((2, w, w), full3),
                  pl.BlockSpec((2, w), full2), pl.BlockSpec((2, w), full2), pl.BlockSpec((2, w), full2),
                  pl.BlockSpec((1, 2, w), lambda s: (s, 0, 0))],
        out_specs=(pl.BlockSpec((n, w), lambda s: (s, 0)), pl.BlockSpec((1, 2, w), lambda s: (s, 0, 0))),
        scratch_shapes=[pltpu.VMEM((n + 2 * HALO, w), F32), pltpu.VMEM((n, w), F32)] + cm_scratch,
        compiler_params=_cparams("arbitrary"),
        name="rglru",
    )(x, conv_w, conv_b.reshape(1, w), wr_bd, wi_bd, b_r, b_i, lam, h0)


def _block_diag(wg):
    eye = jnp.eye(LRU_BLOCKS, dtype=wg.dtype)
    dense = wg[:, :, :, None, :] * eye[None, :, None, :, None]
    return dense.reshape(2, LRU_W, LRU_W).astype(BF16)


TM_OUT = 256


def _out_body(x_ref, o_ref, hr_ref, g_ref, xg_ref, ga0_ref, ga1_ref, gb0_ref, gb1_ref, m_ref,
              hgn_ref, n2_ref, wa_ref, wb_ref, wo_ref, wq_ref, x1_ref, h2_ref, q_ref):
    o = o_ref[...]
    parts = []
    for h in range(HG_HEADS):
        parts.append(_rms(o[:, h * HG_DV:(h + 1) * HG_DV]) * hgn_ref[...])
    on = jnp.concatenate(parts, axis=1) * _silu(g_ref[...])
    y_a = jnp.dot(on.astype(BF16), wa_ref[...], preferred_element_type=F32)
    y_b = jnp.dot((hr_ref[...] * jax.nn.gelu(xg_ref[...])).astype(BF16), wb_ref[...], preferred_element_type=F32)
    ga = jnp.concatenate([ga0_ref[...], ga1_ref[...]], axis=1)
    gb = jnp.concatenate([gb0_ref[...], gb1_ref[...]], axis=1)
    merged = jax.nn.sigmoid(ga) * y_a + jax.nn.sigmoid(gb) * y_b
    mix = jnp.dot(merged.astype(BF16), wo_ref[...], preferred_element_type=F32)
    x1 = x_ref[...] + m_ref[0, 2:3, :] * mix
    x1_ref[...] = x1
    h2 = _rms(x1) * n2_ref[...] * (1.0 + m_ref[0, 4:5, :]) + m_ref[0, 3:4, :]
    h2_ref[...] = h2
    q_ref[...] = jnp.dot(h2.astype(BF16), wq_ref[...], preferred_element_type=F32).astype(BF16)


def _out_proj(x, o_hg, hr, z, mod, hg_norm, norm2, wa, wb, wo, wq, row_of):
    t, d = x.shape
    tm = TM_OUT
    nq = PK_HEADS * PK_DQ
    half = lambda i: (i, 0)
    zc = lambda c: pl.BlockSpec((tm, LRU_W), lambda i: (i, c))
    const = lambda i: (0, 0)
    return pl.pallas_call(
        _out_body,
        out_shape=(jax.ShapeDtypeStruct((t, d), F32), jax.ShapeDtypeStruct((t, d), F32),
                   jax.ShapeDtypeStruct((t, nq), BF16)),
        grid=(t // tm,),
        in_specs=[pl.BlockSpec((tm, d), half), pl.BlockSpec((tm, HG_VW), half), pl.BlockSpec((tm, LRU_W), half),
                  zc(COL_G), zc(COL_XG), zc(COL_GA), zc(COL_GA + 1), zc(COL_GB), zc(COL_GB + 1),
                  pl.BlockSpec((1, N_MOD, d), lambda i: (row_of(i), 0, 0)),
                  pl.BlockSpec((1, HG_DV), const), pl.BlockSpec((1, d), const),
                  pl.BlockSpec((HG_VW, d), const), pl.BlockSpec((LRU_W, d), const),
                  pl.BlockSpec((d, d), const), pl.BlockSpec((d, nq), const)],
        out_specs=(pl.BlockSpec((tm, d), half), pl.BlockSpec((tm, d), half), pl.BlockSpec((tm, nq), half)),
        compiler_params=_cparams("arbitrary"),
        name="out_proj",
    )(x, o_hg, hr, z, z, z, z, z, z, mod, hg_norm.reshape(1, HG_DV), norm2.reshape(1, d), wa, wb, wo, wq)


TT_TOPK = 128


def _extract_topk(s, order, k, payload=None):
    vals, ords, picks = [], [], []
    for _ in range(k):
        m = jnp.max(s, axis=0, keepdims=True)
        o = jnp.max(jnp.where(s == m, order, -1.0), axis=0, keepdims=True)
        hit = order == o
        vals.append(m)
        ords.append(o)
        if payload is not None:
            picks.append(jnp.sum(jnp.where(hit, payload, 0), axis=0, keepdims=True))
        s = jnp.where(hit, -jnp.inf, s)
    cat = lambda xs: jnp.concatenate(xs, axis=0)
    return cat(vals), cat(ords), (cat(picks) if payload is not None else None)


PAIR_ROWS = 4
PAIR_COLS = 3
assert all((i + 1) * (j + 1) > PK_TOPK for i in range(PAIR_ROWS, PK_TOPK) for j in range(PAIR_COLS, PK_TOPK))
assert all((i + 1) * (8 + 1) > PK_TOPK for i in range(1, PAIR_ROWS)) and (8 + 1) * (1 + 1) > PK_TOPK


def _pair_candidates(sv, si, tt):
    cand, order, eid = [], [], []
    hi = float(PK_TOPK * PK_TOPK - 1)
    for i in range(PAIR_ROWS):
        nj = PK_TOPK if i == 0 else 8
        j = lax.broadcasted_iota(jnp.int32, (nj, tt), 0)
        cand.append(sv[0][i:i + 1, :] + sv[1][:nj])
        order.append(hi - (i * PK_TOPK + j).astype(F32))
        eid.append(si[0][i:i + 1, :] * N_KEYS + si[1][:nj])
    for j in range(PAIR_COLS):
        ni = PK_TOPK if j == 0 else 8
        i = lax.broadcasted_iota(jnp.int32, (ni, tt), 0)
        fresh = i >= PAIR_ROWS
        cand.append(jnp.where(fresh, sv[0][:ni] + sv[1][j:j + 1, :], -jnp.inf))
        order.append(jnp.where(fresh, hi - (i * PK_TOPK + j).astype(F32), -2.0))
        eid.append(si[0][:ni] * N_KEYS + si[1][j:j + 1, :])
    cat = lambda xs: jnp.concatenate(xs, axis=0)
    return cat(cand), cat(order), cat(eid)


def _topk_body(q_ref, keys_ref, eidx_ref, gate_ref):
    tt = q_ref.shape[0]
    key_order = (N_KEYS - 1 - lax.broadcasted_iota(jnp.int32, (N_KEYS, tt), 0)).astype(F32)
    e_rows, g_rows = [], []
    for h in range(PK_HEADS):
        sv, si = [], []
        for p in range(2):
            c0 = h * PK_DQ + p * PK_DH
            s = lax.dot_general(keys_ref[p, h], q_ref[:, c0:c0 + PK_DH], (((1,), (1,)), ((), ())),
                                preferred_element_type=F32)
            v, o, _ = _extract_topk(s, key_order, PK_TOPK)
            sv.append(v)
            si.append(N_KEYS - 1 - o.astype(jnp.int32))
        cand, order, cidx = _pair_candidates(sv, si, tt)
        best, _, eid = _extract_topk(cand, order, PK_TOPK, payload=cidx)
        ex = jnp.exp(best - best[0:1, :])
        g_rows.append(ex / jnp.sum(ex, axis=0, keepdims=True))
        e_rows.append(eid)
    eidx_ref[...] = jnp.concatenate(e_rows, axis=0).T
    gate_ref[...] = jnp.concatenate(g_rows, axis=0).T


def _pk_topk(q, keys_bf):
    t = q.shape[0]
    tt = TT_TOPK
    ne = PK_HEADS * PK_TOPK
    return pl.pallas_call(
        _topk_body,
        out_shape=(jax.ShapeDtypeStruct((t, ne), jnp.int32), jax.ShapeDtypeStruct((t, ne), F32)),
        grid=(t // tt,),
        in_specs=[pl.BlockSpec((tt, PK_HEADS * PK_DQ), lambda i: (i, 0)),
                  pl.BlockSpec((2, PK_HEADS, N_KEYS, PK_DH), lambda i: (0, 0, 0, 0))],
        out_specs=(pl.BlockSpec((tt, ne), lambda i: (i, 0)), pl.BlockSpec((tt, ne), lambda i: (i, 0))),
        compiler_params=_cparams("arbitrary"),
        name="pk_topk",
    )(q, keys_bf)


SC_LANES = 16
SC_CORES = 2
SC_SUBCORES = 16
SC_WORKERS = SC_CORES * SC_SUBCORES
SC_HALF = D_MODEL // 2
SC_ROW_CHUNKS = SC_HALF // SC_LANES
SC_TOKEN_BLOCK = 8


def _pack_bf16_pairs(tab):
    lo = lax.bitcast_convert_type(tab[:, SC_HALF:].astype(BF16), jnp.uint16).astype(jnp.uint32)
    bits = lax.bitcast_convert_type(tab[:, :SC_HALF], jnp.uint32)
    sign = bits & jnp.uint32(0x80000000)
    mag = (bits & jnp.uint32(0x7FFFFFFF)) + jnp.uint32(1 << 15)
    hi = jnp.where(mag >= lo, (mag - lo) >> 16, jnp.uint32(0))
    return lax.bitcast_convert_type(sign | (hi << 16) | lo, jnp.int32)


def _unpack_pair(w):
    return lax.bitcast_convert_type(w, F32), lax.bitcast_convert_type(w << 16, F32)


def _gelu_tanh(x):
    y = 0.7978845608028654 * (x + 0.044715 * (x * x * x))
    t = 1.0 - 2.0 / (jnp.exp(2.0 * y) + 1.0)
    return x * (0.5 * (1.0 + t))


def _peer_sc_body(h_hbm, idx_hbm, gate_hbm, u_hbm, v_hbm, out_hbm,
                  h_v, idx_v, gate_v, out_v, ub0, ub1, vb0, vb1, su0, su1, sv0, sv1, *, tokens_per_worker):
    nh, k, tb, lanes = PK_HEADS, PK_TOPK, SC_TOKEN_BLOCK, SC_LANES
    wid = lax.axis_index("s") * SC_CORES + lax.axis_index("c")
    ubufs, vbufs, sus, svs = (ub0, ub1), (vb0, vb1), (su0, su1), (sv0, sv1)
    lane = lax.iota(jnp.int32, lanes)
    zero = jnp.zeros((lanes,), F32)

    def start(tok, hd, par):
        irow = idx_v.at[tok * nh + hd]
        pltpu.async_copy(u_hbm.at[irow], ubufs[par], sus[par])
        pltpu.async_copy(v_hbm.at[irow], vbufs[par], svs[par])

    def wait(par):
        irow = idx_v.at[0]
        pltpu.make_async_copy(u_hbm.at[irow], ubufs[par], sus[par]).wait()
        pltpu.make_async_copy(v_hbm.at[irow], vbufs[par], svs[par]).wait()

    @pl.loop(0, tokens_per_worker // tb)
    def _(blk):
        base = wid * tokens_per_worker + blk * tb
        pltpu.sync_copy(h_hbm.at[pl.ds(base, tb)], h_v)
        pltpu.sync_copy(idx_hbm.at[pl.ds(base * nh, tb * nh)], idx_v)
        pltpu.sync_copy(gate_hbm.at[pl.ds(base * nh, tb * nh)], gate_v)
        start(0, 0, 0)

        @pl.loop(0, tb)
        def _(tok):
            for hd in range(nh):
                par = hd % 2
                if hd + 1 < nh:
                    start(tok, hd + 1, 1 - par)
                else:
                    @pl.when(tok + 1 < tb)
                    def _():
                        start(tok + 1, 0, 1 - par)
                wait(par)
                ub, vb = ubufs[par], vbufs[par]

                @plsc.parallel_loop(0, SC_ROW_CHUNKS, carry=(zero,) * k)
                def accs(c, acc):
                    ha = h_v[tok, pl.ds(c * lanes, lanes)]
                    hb = h_v[tok, pl.ds(SC_HALF + c * lanes, lanes)]
                    out = []
                    for r in range(k):
                        a, b = _unpack_pair(ub[r, pl.ds(c * lanes, lanes)])
                        out.append(acc[r] + (ha * a + hb * b))
                    return tuple(out)

                s_vec = zero
                for r in range(k):
                    s_vec = jnp.where(lane == r, jnp.sum(accs[r]), s_vec)
                w_vec = gate_v[tok * nh + hd, :] * _gelu_tanh(s_vec)
                wb = [jnp.sum(jnp.where(lane == r, w_vec, 0.0)) for r in range(k)]

                @plsc.parallel_loop(0, SC_ROW_CHUNKS)
                def _(c):
                    pa, pb = [], []
                    for r in range(k):
                        a, b = _unpack_pair(vb[r, pl.ds(c * lanes, lanes)])
                        pa.append(wb[r] * a)
                        pb.append(wb[r] * b)
                    while len(pa) > 1:
                        pa = [pa[i] + pa[i + 1] for i in range(0, len(pa), 2)]
                        pb = [pb[i] + pb[i + 1] for i in range(0, len(pb), 2)]
                    for off, o in ((0, pa[0]), (SC_HALF, pb[0])):
                        cols = pl.ds(off + c * lanes, lanes)
                        if hd == 0:
                            out_v[tok, cols] = o
                        else:
                            out_v[tok, cols] = out_v[tok, cols] + o

        pltpu.sync_copy(out_v, out_hbm.at[pl.ds(base, tb)])


def _peer_experts(h, eidx, gate, u_tab, v_tab):
    t, d = h.shape
    assert t % (SC_WORKERS * SC_TOKEN_BLOCK) == 0 and d == D_MODEL
    rows = pltpu.VMEM((PK_TOPK, SC_HALF), jnp.int32)
    return pl.kernel(
        functools.partial(_peer_sc_body, tokens_per_worker=t // SC_WORKERS),
        out_type=jax.ShapeDtypeStruct((t, d), F32),
        mesh=plsc.VectorSubcoreMesh(core_axis_name="c", subcore_axis_name="s"),
        scratch_types=[
            pltpu.VMEM((SC_TOKEN_BLOCK, d), F32),
            pltpu.VMEM((SC_TOKEN_BLOCK * PK_HEADS, PK_TOPK), jnp.int32),
            pltpu.VMEM((SC_TOKEN_BLOCK * PK_HEADS, PK_TOPK), F32),
            pltpu.VMEM((SC_TOKEN_BLOCK, d), F32),
            rows, rows, rows, rows,
            pltpu.SemaphoreType.DMA, pltpu.SemaphoreType.DMA,
            pltpu.SemaphoreType.DMA, pltpu.SemaphoreType.DMA,
        ],
        compiler_params=pltpu.CompilerParams(needs_layout_passes=False),
        name="peer_experts_sc",
    )(h, eidx, gate, u_tab, v_tab)


TT_DENSE = 256
EB_DENSE = 512
N_EXPERTS = N_KEYS * N_KEYS
N_SEL = PK_HEADS * PK_TOPK


def _dense_body(h_ref, e_ref, g_ref, u_ref, v_ref, o_ref, g_scr, acc_scr):
    eb = pl.program_id(1)
    tt = h_ref.shape[0]
    slabs = EB_DENSE // N_KEYS

    @pl.when(eb == 0)
    def _():
        acc_scr[...] = jnp.zeros_like(acc_scr)
        sub = lax.broadcasted_iota(jnp.int32, (N_KEYS, N_SEL), 0)

        def per_token(t, carry):
            e = e_ref[pl.ds(t, 1), :]
            p_t = jnp.where(sub == (e >> 7), g_ref[pl.ds(t, 1), :], 0.0).astype(BF16)
            q_t = (sub == (e & (N_KEYS - 1))).astype(BF16)
            g_scr[pl.ds(pl.multiple_of(t * N_KEYS, N_KEYS), N_KEYS), :] = lax.dot_general(
                p_t, q_t, (((1,), (1,)), ((), ())), preferred_element_type=F32)
            return carry

        lax.fori_loop(0, tt, per_token, 0, unroll=8)

    s = lax.dot_general(h_ref[...].astype(BF16), u_ref[...], (((1,), (1,)), ((), ())), preferred_element_type=F32)
    gs = jnp.concatenate([g_scr[pl.ds(eb * slabs + k, tt, stride=N_KEYS), :] for k in range(slabs)], axis=1)
    acc_scr[...] += jnp.dot((gs * jax.nn.gelu(s)).astype(BF16), v_ref[...], preferred_element_type=F32)

    @pl.when(eb == pl.num_programs(1) - 1)
    def _():
        o_ref[...] = acc_scr[...]


def _peer_dense(h, eidx, gate, u_bf, v_bf):
    t, d = h.shape
    tt = TT_DENSE
    assert t % tt == 0 and N_KEYS == 128
    tile = lambda i, e: (i, 0)
    blk = lambda i, e: (e, 0)
    return pl.pallas_call(
        _dense_body,
        out_shape=jax.ShapeDtypeStruct((t, d), F32),
        grid=(t // tt, N_EXPERTS // EB_DENSE),
        in_specs=[pl.BlockSpec((tt, d), tile), pl.BlockSpec((tt, N_SEL), tile), pl.BlockSpec((tt, N_SEL), tile),
                  pl.BlockSpec((EB_DENSE, d), blk), pl.BlockSpec((EB_DENSE, d), blk)],
        out_specs=pl.BlockSpec((tt, d), tile),
        scratch_shapes=[pltpu.VMEM((tt * N_KEYS, N_KEYS), F32), pltpu.VMEM((tt, d), F32)],
        compiler_params=_cparams("arbitrary", "arbitrary"),
        name="peer_dense_tc",
    )(h, eidx, gate, u_bf, v_bf)


TM_FIN = 512


def _final_body(x1_ref, p_ref, m_ref, w_ref, y_ref):
    y_ref[...] = _rms(x1_ref[...] + m_ref[0, 5:6, :] * p_ref[...]) * w_ref[...]


def _final(x1, peer, mod, norm_f, rows_of):
    t, d = x1.shape
    tm = _token_tile(t, TM_FIN)
    row_of = rows_of(tm)
    tile = pl.BlockSpec((tm, d), lambda i: (i, 0))
    return pl.pallas_call(
        _final_body,
        out_shape=jax.ShapeDtypeStruct((t, d), F32),
        grid=(t // tm,),
        in_specs=[tile, tile, pl.BlockSpec((1, N_MOD, d), lambda i: (row_of(i), 0, 0)),
                  pl.BlockSpec((1, d), lambda i: (0, 0))],
        out_specs=tile,
        compiler_params=_cparams("arbitrary"),
        name="final_norm",
    )(x1, peer, mod, norm_f.reshape(1, d))


CTX_GROUPS = (2, 14, 16)
LAT_GROUP = 2
DENSE_TOKENS = 35 * 256


def _zero_of(a):
    return (lax.shift_right_logical(a[0, 0], 31) >> 1).astype(F32)


def _group(x, mod, row0, per_seq, n, hg_s0, lru_s0, col_major, p, after, dense_tokens=0):
    seqs = hg_s0.shape[0]
    t = seqs * n
    for a in after:
        mod = mod + _zero_of(a)
    rows_of = lambda tm: _mod_row_map(tm, n, row0, per_seq)
    z = _in_proj(x, mod, p['norm1'], p['w_in'], rows_of)
    o_hg, hg_fin = _hgrn2(z, p['lb'], hg_s0, n, 0)
    hr, lru_fin = _rglru(z, COL_XR, 0, n, seqs, col_major, *p['lru'], lru_s0)
    x1, h2, q = _out_proj(x, o_hg, hr, z, mod, p['hg_norm'], p['norm2'], p['w_a'], p['w_b'], p['w_o'], p['w_q'],
                          rows_of(TM_OUT))
    eidx, gate = _pk_topk(q, p['keys'])
    ts = t - dense_tokens
    parts = []
    if ts:
        parts.append(_peer_experts(h2[:ts], eidx[:ts].reshape(ts * PK_HEADS, PK_TOPK),
                                   gate[:ts].reshape(ts * PK_HEADS, PK_TOPK), p['pk_u'], p['pk_v']))
    if dense_tokens:
        parts.append(_peer_dense(h2[ts:], eidx[ts:], gate[ts:], p['u_bf'], p['v_bf']))
    peer = parts[0] if len(parts) == 1 else jnp.concatenate(parts, axis=0)
    return _final(x1, peer, mod, p['norm_f'], rows_of), hg_fin, lru_fin, eidx


def kernel(x_prompt, x_sample, state_hgrn, state_rglru, c, c_ctx, w_mod, b_mod, norm1, w_in,
           hg_lb_logits, hg_norm, lru_conv_w, lru_conv_b, lru_w_r, lru_b_r, lru_w_i, lru_b_i,
           lru_lambda, w_branch_a, w_branch_b, w_out, norm2, pk_w_q, pk_sub_keys, pk_u, pk_v,
           norm_f):
    assert w_mod.shape[0] == 1, "single trunk layer"
    d = D_MODEL
    nb_ctx, n_ctx, _ = x_prompt.shape
    nb_lat, n_lat, _ = x_sample.shape
    assert nb_lat < MOD_ROWS and nb_lat % LAT_GROUP == 0 and nb_ctx == sum(CTX_GROUPS)
    ctx_row = nb_lat

    cond = jnp.zeros((MOD_ROWS, d), F32).at[:nb_lat].set(c).at[ctx_row].set(c_ctx)
    mod = _modulation(cond, w_mod[0], b_mod[0])
    p = {
        'lb': jnp.cumsum(jax.nn.softmax(hg_lb_logits.astype(F32), axis=1), axis=1)[:, 0],
        'norm1': norm1[0], 'w_in': w_in[0].astype(BF16), 'hg_norm': hg_norm[0], 'norm2': norm2[0],
        'lru': (lru_conv_w[0], lru_conv_b[0], _block_diag(lru_w_r[0]), _block_diag(lru_w_i[0]),
                lru_b_r[0], lru_b_i[0], lru_lambda[0]),
        'w_a': w_branch_a[0].astype(BF16), 'w_b': w_branch_b[0].astype(BF16), 'w_o': w_out[0].astype(BF16),
        'w_q': pk_w_q[0].astype(BF16), 'keys': pk_sub_keys[0].astype(BF16),
        'pk_u': _pack_bf16_pairs(pk_u[0]), 'pk_v': _pack_bf16_pairs(pk_v[0]), 'norm_f': norm_f,
        'u_bf': pk_u[0].astype(BF16), 'v_bf': pk_v[0].astype(BF16),
    }
    y_ctx, hg_fin, lru_fin = [], [], []
    tables = [p['pk_u'], p['pk_v']]
    after = tables
    s0 = 0
    for g in CTX_GROUPS:
        y, hg, lr, eidx = _group(
            x_prompt[s0:s0 + g].reshape(g * n_ctx, d), mod, ctx_row, False, n_ctx,
            jnp.zeros((g, 2, HG_HEADS, HG_DK, HG_DV), F32), jnp.zeros((g, 2, LRU_W), F32), False, p, after)
        s0 += g
        after = [eidx] if after is tables else after
        y_ctx.append(y.reshape(g, n_ctx, d))
        hg_fin.append(hg)
        lru_fin.append(lr)
    y_lat = []
    tg = LAT_GROUP * n_lat
    for s0 in range(0, nb_lat, LAT_GROUP):
        sl = slice(s0, s0 + LAT_GROUP)
        tokens_after = (nb_lat - s0 - LAT_GROUP) * n_lat
        y, _, _, _ = _group(x_sample[sl].reshape(tg, d), mod, s0, True, n_lat,
                            state_hgrn[sl, 0], state_rglru[sl, 0], True, p, after,
                            dense_tokens=min(tg, max(0, DENSE_TOKENS - tokens_after)))
        y_lat.append(y.reshape(LAT_GROUP, n_lat, d))
    return (jnp.concatenate(y_ctx, axis=0), jnp.concatenate(y_lat, axis=0),
            jnp.concatenate(hg_fin, axis=0)[:, None], jnp.concatenate(lru_fin, axis=0)[:, None])
```

```python
import functools

import jax, jax.numpy as jnp
from jax import lax
from jax.experimental import pallas as pl
from jax.experimental.pallas import tpu as pltpu
from jax.experimental.pallas import tpu_sc as plsc

D_MODEL = 1024
GRID_W = 64
EPS = 1e-6
HG_HEADS = 4
HG_DK = 128
HG_DV = 128
HG_KW = HG_HEADS * HG_DK
HG_VW = HG_HEADS * HG_DV
HG_CHUNK = 32
LRU_W = D_MODEL // 2
LRU_BLOCKS = 8
LRU_BW = LRU_W // LRU_BLOCKS
CONV_W = 4
LRU_C = 8.0
PK_HEADS = 8
N_KEYS = 128
PK_TOPK = 16
PK_DQ = 256
PK_DH = PK_DQ // 2
IN_W = 3 * HG_KW + 2 * HG_VW + 2 * LRU_W + 2 * D_MODEL
COL_G, COL_XR, COL_XG, COL_GA, COL_GB = 4, 5, 6, 7, 9
N_MOD = 6
MOD_ROWS = 16
F32 = jnp.float32
BF16 = jnp.bfloat16
VMEM_LIMIT = 48 * 1024 * 1024


def _cparams(*sem):
    return pltpu.CompilerParams(dimension_semantics=sem, vmem_limit_bytes=VMEM_LIMIT)


def _silu(x):
    return x * jax.nn.sigmoid(x)


def _rms(x):
    return x * lax.rsqrt(jnp.mean(x * x, axis=-1, keepdims=True) + EPS)


def _mod_body(c_ref, w_ref, b_ref, o_ref):
    o_ref[...] = lax.dot_general(_silu(c_ref[...]), w_ref[...], (((1,), (0,)), ((), ())),
                                 precision=lax.Precision.HIGHEST, preferred_element_type=F32) + b_ref[...]


def _modulation(cond, w_mod, b_mod):
    d = D_MODEL
    out = pl.pallas_call(
        _mod_body,
        out_shape=jax.ShapeDtypeStruct((MOD_ROWS, N_MOD * d), F32),
        grid=(N_MOD,),
        in_specs=[pl.BlockSpec((MOD_ROWS, d), lambda j: (0, 0)),
                  pl.BlockSpec((d, d), lambda j: (0, j)),
                  pl.BlockSpec((1, d), lambda j: (0, j))],
        out_specs=pl.BlockSpec((MOD_ROWS, d), lambda j: (0, j)),
        compiler_params=_cparams("arbitrary"),
        name="adaln_modulation",
    )(cond, w_mod, b_mod.reshape(1, N_MOD * d))
    return out.reshape(MOD_ROWS, N_MOD, d)


def _mod_row_map(tm, n, row0, per_seq):
    per = n // tm
    return (lambda i: row0 + i // per) if per_seq else (lambda i: row0)


TM_IN = 512
TOKEN_TILE_MIN = 256
TN_IN = IN_W // 2


def _in_body(x_ref, m_ref, n1_ref, w_ref, z_ref, h_scr):
    @pl.when(pl.program_id(1) == 0)
    def _():
        y = _rms(x_ref[...]) * n1_ref[...]
        h_scr[...] = (y * (1.0 + m_ref[0, 1:2, :]) + m_ref[0, 0:1, :]).astype(BF16)

    z_ref[...] = jnp.dot(h_scr[...], w_ref[...], preferred_element_type=F32)


def _token_tile(t, preferred):
    tm = preferred if t % preferred == 0 else TOKEN_TILE_MIN
    assert t % tm == 0
    return tm


def _in_proj(x, mod, norm1, w_in_bf, rows_of):
    t, d = x.shape
    tm = _token_tile(t, TM_IN)
    row_of = rows_of(tm)
    return pl.pallas_call(
        _in_body,
        out_shape=jax.ShapeDtypeStruct((t, IN_W), F32),
        grid=(t // tm, IN_W // TN_IN),
        in_specs=[pl.BlockSpec((tm, d), lambda i, j: (i, 0)),
                  pl.BlockSpec((1, N_MOD, d), lambda i, j: (row_of(i), 0, 0)),
                  pl.BlockSpec((1, d), lambda i, j: (0, 0)),
                  pl.BlockSpec((d, TN_IN), lambda i, j: (0, j))],
        out_specs=pl.BlockSpec((tm, TN_IN), lambda i, j: (i, j)),
        scratch_shapes=[pltpu.VMEM((tm, d), BF16)],
        compiler_params=_cparams("arbitrary", "arbitrary"),
        name="in_proj",
    )(x, mod, norm1.reshape(1, d), w_in_bf)


HG_LONG_SEQ = 1024


def _heads_per_step(n):
    return HG_HEADS if n < HG_LONG_SEQ else HG_HEADS // 2


def _hg_body(q_ref, ff_ref, fb_ref, v_ref, lb_ref, s0_ref, o_ref, sfin_ref, st_scr, ob_scr, *, n, hps):
    c = HG_CHUNK
    nc = n // c
    row = lax.broadcasted_iota(jnp.int32, (c, c), 0)
    col = lax.broadcasted_iota(jnp.int32, (c, c), 1)
    lower = row >= col
    tri = (lower.astype(F32), (row <= col).astype(F32))
    masks = (lower, row <= col)
    f_refs = (ff_ref, fb_ref)
    for d in range(2):
        for hh in range(hps):
            st_scr[d * hps + hh] = s0_ref[0, d, hh].T

    def step(i, carry):
        chains = [(d, hh) for d in range(2) for hh in range(hps)]
        st1 = []
        for d, hh in chains:
            ci = i if d == 0 else nc - 1 - i
            r = pl.ds(pl.multiple_of(ci * c, c), c)
            hc = slice(hh * HG_DK, (hh + 1) * HG_DK)
            lb = lb_ref[d:d + 1, hc]
            f = lb + (1.0 - lb) * jax.nn.sigmoid(f_refs[d][r, hc])
            lf = jnp.log(f)
            cum = lax.dot_general(tri[d], lf, (((1,), (0,)), ((), ())),
                                  precision=lax.Precision.HIGHEST, preferred_element_type=F32)
            st1.append((r, hc, 1.0 - f, lf, cum))
        st2 = []
        for (d, hh), (r, hc, k, lf, cum) in zip(chains, st1):
            q = _silu(q_ref[r, hc])
            v = v_ref[r, hc].astype(BF16)
            tot = jnp.sum(lf, axis=0, keepdims=True)
            q_dec = (q * jnp.exp(cum)).astype(BF16)
            k_inv = (k * jnp.exp(-cum)).astype(BF16)
            k_end = (k * jnp.exp(tot - cum)).astype(BF16)
            st = st_scr[d * hps + hh]
            att = lax.dot_general(q_dec, k_inv, (((1,), (1,)), ((), ())), preferred_element_type=F32)
            o_inter = lax.dot_general(q_dec, st.astype(BF16), (((1,), (1,)), ((), ())), preferred_element_type=F32)
            ds_t = lax.dot_general(v, k_end, (((0,), (0,)), ((), ())), preferred_element_type=F32)
            st_scr[d * hps + hh] = st * jnp.exp(tot) + ds_t
            st2.append((v, att, o_inter))
        for (d, hh), (r, hc, _, _, _), (v, att, o_inter) in zip(chains, st1, st2):
            att = jnp.where(masks[d], att, 0.0).astype(BF16)
            o = jnp.dot(att, v, preferred_element_type=F32) + o_inter
            if d == 0:
                o_ref[r, hc] = o
            else:
                ob_scr[r, hc] = o
        return carry

    lax.fori_loop(0, nc, step, 0)
    o_ref[...] = o_ref[...] + ob_scr[...]
    for d in range(2):
        for hh in range(hps):
            sfin_ref[0, d, hh] = st_scr[d * hps + hh].T


def _hgrn2(z, lb, s0, n, row_block0):
    seqs = s0.shape[0]
    hps = _heads_per_step(n)
    wb = hps * HG_DK
    nblk = HG_KW // wb

    def zspec(seg):
        return pl.BlockSpec((n, wb), lambda s, h: (row_block0 + s, seg * nblk + h))

    st_spec = pl.BlockSpec((1, 2, hps, HG_DK, HG_DV), lambda s, h: (s, 0, h, 0, 0))
    return pl.pallas_call(
        functools.partial(_hg_body, n=n, hps=hps),
        out_shape=(jax.ShapeDtypeStruct((seqs * n, HG_VW), F32),
                   jax.ShapeDtypeStruct((seqs, 2, HG_HEADS, HG_DK, HG_DV), F32)),
        grid=(seqs, HG_HEADS // hps),
        in_specs=[zspec(0), zspec(1), zspec(2), zspec(3),
                  pl.BlockSpec((2, wb), lambda s, h: (0, h)), st_spec],
        out_specs=(pl.BlockSpec((n, wb), lambda s, h: (s, h)), st_spec),
        scratch_shapes=[pltpu.VMEM((2 * hps, HG_DV, HG_DK), F32), pltpu.VMEM((n, wb), F32)],
        compiler_params=_cparams("arbitrary", "arbitrary"),
        name="hgrn2_scan",
    )(z, z, z, z, lb, s0)


LRU_RB = 64
HALO = 8


LANES = 128


def _lru_body(x_ref, cw_ref, cb_ref, wr_ref, wi_ref, br_ref, bi_ref, lam_ref, h0_ref, hr_ref, e_ref,
              xp_scr, xc_scr, *cm_scr, n, col_major):
    rb = LRU_RB
    nb = n // rb
    w = LRU_W
    grid_rows = n // GRID_W
    slabs = w // LANES
    zeros = jnp.zeros((HALO, w), F32)
    xp_scr[0:HALO, :] = zeros
    xp_scr[HALO + n:HALO + n + HALO, :] = zeros
    if col_major:
        st_scr, hcm_scr = cm_scr
        for j in range(slabs):
            st_scr[j] = x_ref[:, j * LANES:(j + 1) * LANES]

        def to_col_major(c, carry):
            dst = pl.ds(pl.multiple_of(HALO + c * grid_rows, 8), grid_rows)
            for j in range(slabs):
                xp_scr[dst, j * LANES:(j + 1) * LANES] = st_scr[j, pl.ds(c, grid_rows, stride=GRID_W), :]
            return carry

        lax.fori_loop(0, GRID_W, to_col_major, 0)
        h_dst = hcm_scr
    else:
        xp_scr[HALO:HALO + n, :] = x_ref[...]
        h_dst = hr_ref
    cw = cw_ref[...]
    cb = cb_ref[...]

    def conv_blk(b, carry):
        base = pl.multiple_of(b * rb, rb)
        xh = xp_scr[pl.ds(base, rb + 2 * HALO), :]
        ext = rb + 2 * HALO
        acc = cb + cw[2:3, :] * xh[HALO:HALO + rb]
        acc = acc + cw[0:1, :] * pltpu.roll(xh, 2, axis=0)[HALO:HALO + rb]
        acc = acc + cw[1:2, :] * pltpu.roll(xh, 1, axis=0)[HALO:HALO + rb]
        acc = acc + cw[3:4, :] * pltpu.roll(xh, ext - 1, axis=0)[HALO:HALO + rb]
        xc_scr[pl.ds(base, rb), :] = acc
        return carry

    lax.fori_loop(0, nb, conv_blk, 0)

    rows = lax.broadcasted_iota(jnp.int32, (rb, w), 0)

    def gates(blk, d):
        r = pl.ds(pl.multiple_of(blk * rb, rb), rb)
        xc = xc_scr[r, :]
        xb = xc.astype(BF16)
        rg = jax.nn.sigmoid(jnp.dot(xb, wr_ref[d], preferred_element_type=F32) + br_ref[d:d + 1, :])
        ig = jax.nn.sigmoid(jnp.dot(xb, wi_ref[d], preferred_element_type=F32) + bi_ref[d:d + 1, :])
        log_a = (-LRU_C) * jax.nn.softplus(-lam_ref[d:d + 1, :]) * rg
        a = jnp.exp(log_a)
        u = jnp.sqrt(-jnp.tanh(log_a) * (a * a + 1.0)) * (ig * xc)
        return r, a, u

    def fwd_blk(blk, h_prev):
        r, a, u = gates(blk, 0)
        s = 1
        while s < rb:
            keep = rows >= s
            a_sh = jnp.where(keep, pltpu.roll(a, s, axis=0), 1.0)
            u_sh = jnp.where(keep, pltpu.roll(u, s, axis=0), 0.0)
            u = a * u_sh + u
            a = a * a_sh
            s *= 2
        h = u + a * h_prev
        h_dst[r, :] = h
        return h[rb - 1:rb, :]

    def bwd_blk(i, h_next):
        r, a, u = gates(nb - 1 - i, 1)
        s = 1
        while s < rb:
            keep = rows < rb - s
            a_sh = jnp.where(keep, pltpu.roll(a, rb - s, axis=0), 1.0)
            u_sh = jnp.where(keep, pltpu.roll(u, rb - s, axis=0), 0.0)
            u = a * u_sh + u
            a = a * a_sh
            s *= 2
        h = u + a * h_next
        h_dst[r, :] = h_dst[r, :] + h
        return h[0:1, :]

    e_ref[0, 0:1, :] = lax.fori_loop(0, nb, fwd_blk, h0_ref[0, 0:1, :])
    e_ref[0, 1:2, :] = lax.fori_loop(0, nb, bwd_blk, h0_ref[0, 1:2, :])
    if col_major:
        def to_row_major(c, carry):
            src = pl.ds(pl.multiple_of(c * grid_rows, 8), grid_rows)
            for j in range(slabs):
                st_scr[j, pl.ds(c, grid_rows, stride=GRID_W), :] = hcm_scr[src, j * LANES:(j + 1) * LANES]
            return carry

        lax.fori_loop(0, GRID_W, to_row_major, 0)
        for j in range(slabs):
            hr_ref[:, j * LANES:(j + 1) * LANES] = st_scr[j]


def _rglru(x, col_block, row_block0, n, seqs, col_major, conv_w, conv_b, wr_bd, wi_bd, b_r, b_i, lam, h0):
    w = LRU_W
    cm_scratch = [pltpu.VMEM((w // LANES, n, LANES), F32), pltpu.VMEM((n, w), F32)] if col_major else []
    full2 = lambda s: (0, 0)
    full3 = lambda s: (0, 0, 0)
    return pl.pallas_call(
        functools.partial(_lru_body, n=n, col_major=col_major),
        out_shape=(jax.ShapeDtypeStruct((seqs * n, w), F32), jax.ShapeDtypeStruct((seqs, 2, w), F32)),
        grid=(seqs,),
        in_specs=[pl.BlockSpec((n, w), lambda s: (row_block0 + s, col_block)),
                  pl.BlockSpec((CONV_W, w), full2), pl.BlockSpec((1, w), full2),
                  pl.BlockSpec((2, w, w), full3), pl.BlockSpec((2, w, w), full3),
                  pl.BlockSpec((2, w), full2), pl.BlockSpec((2, w), full2), pl.BlockSpec((2, w), full2),
                  pl.BlockSpec((1, 2, w), lambda s: (s, 0, 0))],
        out_specs=(pl.BlockSpec((n, w), lambda s: (s, 0)), pl.BlockSpec((1, 2, w), lambda s: (s, 0, 0))),
        scratch_shapes=[pltpu.VMEM((n + 2 * HALO, w), F32), pltpu.VMEM((n, w), F32)] + cm_scratch,
        compiler_params=_cparams("arbitrary"),
        name="rglru",
    )(x, conv_w, conv_b.reshape(1, w), wr_bd, wi_bd, b_r, b_i, lam, h0)


def _block_diag(wg):
    eye = jnp.eye(LRU_BLOCKS, dtype=wg.dtype)
    dense = wg[:, :, :, None, :] * eye[None, :, None, :, None]
    return dense.reshape(2, LRU_W, LRU_W).astype(BF16)


TM_OUT = 256


def _out_body(x_ref, o_ref, hr_ref, g_ref, xg_ref, ga0_ref, ga1_ref, gb0_ref, gb1_ref, m_ref,
              hgn_ref, n2_ref, wa_ref, wb_ref, wo_ref, wq_ref, x1_ref, h2_ref, q_ref):
    o = o_ref[...]
    parts = []
    for h in range(HG_HEADS):
        parts.append(_rms(o[:, h * HG_DV:(h + 1) * HG_DV]) * hgn_ref[...])
    on = jnp.concatenate(parts, axis=1) * _silu(g_ref[...])
    y_a = jnp.dot(on.astype(BF16), wa_ref[...], preferred_element_type=F32)
    y_b = jnp.dot((hr_ref[...] * jax.nn.gelu(xg_ref[...])).astype(BF16), wb_ref[...], preferred_element_type=F32)
    ga = jnp.concatenate([ga0_ref[...], ga1_ref[...]], axis=1)
    gb = jnp.concatenate([gb0_ref[...], gb1_ref[...]], axis=1)
    merged = jax.nn.sigmoid(ga) * y_a + jax.nn.sigmoid(gb) * y_b
    mix = jnp.dot(merged.astype(BF16), wo_ref[...], preferred_element_type=F32)
    x1 = x_ref[...] + m_ref[0, 2:3, :] * mix
    x1_ref[...] = x1
    h2 = _rms(x1) * n2_ref[...] * (1.0 + m_ref[0, 4:5, :]) + m_ref[0, 3:4, :]
    h2_ref[...] = h2
    q_ref[...] = jnp.dot(h2.astype(BF16), wq_ref[...], preferred_element_type=F32).astype(BF16)


def _out_proj(x, o_hg, hr, z, mod, hg_norm, norm2, wa, wb, wo, wq, row_of):
    t, d = x.shape
    tm = TM_OUT
    nq = PK_HEADS * PK_DQ
    half = lambda i: (i, 0)
    zc = lambda c: pl.BlockSpec((tm, LRU_W), lambda i: (i, c))
    const = lambda i: (0, 0)
    return pl.pallas_call(
        _out_body,
        out_shape=(jax.ShapeDtypeStruct((t, d), F32), jax.ShapeDtypeStruct((t, d), F32),
                   jax.ShapeDtypeStruct((t, nq), BF16)),
        grid=(t // tm,),
        in_specs=[pl.BlockSpec((tm, d), half), pl.BlockSpec((tm, HG_VW), half), pl.BlockSpec((tm, LRU_W), half),
                  zc(COL_G), zc(COL_XG), zc(COL_GA), zc(COL_GA + 1), zc(COL_GB), zc(COL_GB + 1),
                  pl.BlockSpec((1, N_MOD, d), lambda i: (row_of(i), 0, 0)),
                  pl.BlockSpec((1, HG_DV), const), pl.BlockSpec((1, d), const),
                  pl.BlockSpec((HG_VW, d), const), pl.BlockSpec((LRU_W, d), const),
                  pl.BlockSpec((d, d), const), pl.BlockSpec((d, nq), const)],
        out_specs=(pl.BlockSpec((tm, d), half), pl.BlockSpec((tm, d), half), pl.BlockSpec((tm, nq), half)),
        compiler_params=_cparams("arbitrary"),
        name="out_proj",
    )(x, o_hg, hr, z, z, z, z, z, z, mod, hg_norm.reshape(1, HG_DV), norm2.reshape(1, d), wa, wb, wo, wq)


TT_TOPK = 128


def _extract_topk(s, order, k, payload=None):
    vals, ords, picks = [], [], []
    for _ in range(k):
        m = jnp.max(s, axis=0, keepdims=True)
        o = jnp.max(jnp.where(s == m, order, -1.0), axis=0, keepdims=True)
        hit = order == o
        vals.append(m)
        ords.append(o)
        if payload is not None:
            picks.append(jnp.sum(jnp.where(hit, payload, 0), axis=0, keepdims=True))
        s = jnp.where(hit, -jnp.inf, s)
    cat = lambda xs: jnp.concatenate(xs, axis=0)
    return cat(vals), cat(ords), (cat(picks) if payload is not None else None)


PAIR_ROWS = 4
PAIR_COLS = 3
assert all((i + 1) * (j + 1) > PK_TOPK for i in range(PAIR_ROWS, PK_TOPK) for j in range(PAIR_COLS, PK_TOPK))
assert all((i + 1) * (8 + 1) > PK_TOPK for i in range(1, PAIR_ROWS)) and (8 + 1) * (1 + 1) > PK_TOPK


def _pair_candidates(sv, si, tt):
    cand, order, eid = [], [], []
    hi = float(PK_TOPK * PK_TOPK - 1)
    for i in range(PAIR_ROWS):
        nj = PK_TOPK if i == 0 else 8
        j = lax.broadcasted_iota(jnp.int32, (nj, tt), 0)
        cand.append(sv[0][i:i + 1, :] + sv[1][:nj])
        order.append(hi - (i * PK_TOPK + j).astype(F32))
        eid.append(si[0][i:i + 1, :] * N_KEYS + si[1][:nj])
    for j in range(PAIR_COLS):
        ni = PK_TOPK if j == 0 else 8
        i = lax.broadcasted_iota(jnp.int32, (ni, tt), 0)
        fresh = i >= PAIR_ROWS
        cand.append(jnp.where(fresh, sv[0][:ni] + sv[1][j:j + 1, :], -jnp.inf))
        order.append(jnp.where(fresh, hi - (i * PK_TOPK + j).astype(F32), -2.0))
        eid.append(si[0][:ni] * N_KEYS + si[1][j:j + 1, :])
    cat = lambda xs: jnp.concatenate(xs, axis=0)
    return cat(cand), cat(order), cat(eid)


def _topk_body(q_ref, keys_ref, eidx_ref, gate_ref):
    tt = q_ref.shape[0]
    key_order = (N_KEYS - 1 - lax.broadcasted_iota(jnp.int32, (N_KEYS, tt), 0)).astype(F32)
    e_rows, g_rows = [], []
    for h in range(PK_HEADS):
        sv, si = [], []
        for p in range(2):
            c0 = h * PK_DQ + p * PK_DH
            s = lax.dot_general(keys_ref[p, h], q_ref[:, c0:c0 + PK_DH], (((1,), (1,)), ((), ())),
                                preferred_element_type=F32)
            v, o, _ = _extract_topk(s, key_order, PK_TOPK)
            sv.append(v)
            si.append(N_KEYS - 1 - o.astype(jnp.int32))
        cand, order, cidx = _pair_candidates(sv, si, tt)
        best, _, eid = _extract_topk(cand, order, PK_TOPK, payload=cidx)
        ex = jnp.exp(best - best[0:1, :])
        g_rows.append(ex / jnp.sum(ex, axis=0, keepdims=True))
        e_rows.append(eid)
    eidx_ref[...] = jnp.concatenate(e_rows, axis=0).T
    gate_ref[...] = jnp.concatenate(g_rows, axis=0).T


def _pk_topk(q, keys_bf):
    t = q.shape[0]
    tt = TT_TOPK
    ne = PK_HEADS * PK_TOPK
    return pl.pallas_call(
        _topk_body,
        out_shape=(jax.ShapeDtypeStruct((t, ne), jnp.int32), jax.ShapeDtypeStruct((t, ne), F32)),
        grid=(t // tt,),
        in_specs=[pl.BlockSpec((tt, PK_HEADS * PK_DQ), lambda i: (i, 0)),
                  pl.BlockSpec((2, PK_HEADS, N_KEYS, PK_DH), lambda i: (0, 0, 0, 0))],
        out_specs=(pl.BlockSpec((tt, ne), lambda i: (i, 0)), pl.BlockSpec((tt, ne), lambda i: (i, 0))),
        compiler_params=_cparams("arbitrary"),
        name="pk_topk",
    )(q, keys_bf)


SC_LANES = 16
SC_CORES = 2
SC_SUBCORES = 16
SC_WORKERS = SC_CORES * SC_SUBCORES
SC_HALF = D_MODEL // 2
SC_ROW_CHUNKS = SC_HALF // SC_LANES
SC_TOKEN_BLOCK = 8


def _pack_bf16_pairs(tab):
    lo = lax.bitcast_convert_type(tab[:, SC_HALF:].astype(BF16), jnp.uint16).astype(jnp.uint32)
    bits = lax.bitcast_convert_type(tab[:, :SC_HALF], jnp.uint32)
    sign = bits & jnp.uint32(0x80000000)
    mag = (bits & jnp.uint32(0x7FFFFFFF)) + jnp.uint32(1 << 15)
    hi = jnp.where(mag >= lo, (mag - lo) >> 16, jnp.uint32(0))
    return lax.bitcast_convert_type(sign | (hi << 16) | lo, jnp.int32)


def _unpack_pair(w):
    return lax.bitcast_convert_type(w, F32), lax.bitcast_convert_type(w << 16, F32)


def _gelu_tanh(x):
    y = 0.7978845608028654 * (x + 0.044715 * (x * x * x))
    t = 1.0 - 2.0 / (jnp.exp(2.0 * y) + 1.0)
    return x * (0.5 * (1.0 + t))


def _peer_sc_body(h_hbm, idx_hbm, gate_hbm, u_hbm, v_hbm, out_hbm,
                  h_v, idx_v, gate_v, out_v, ub0, ub1, vb0, vb1, su0, su1, sv0, sv1, *, tokens_per_worker):
    nh, k, tb, lanes = PK_HEADS, PK_TOPK, SC_TOKEN_BLOCK, SC_LANES
    wid = lax.axis_index("s") * SC_CORES + lax.axis_index("c")
    ubufs, vbufs, sus, svs = (ub0, ub1), (vb0, vb1), (su0, su1), (sv0, sv1)
    lane = lax.iota(jnp.int32, lanes)
    zero = jnp.zeros((lanes,), F32)

    def start(tok, hd, par):
        irow = idx_v.at[tok * nh + hd]
        pltpu.async_copy(u_hbm.at[irow], ubufs[par], sus[par])
        pltpu.async_copy(v_hbm.at[irow], vbufs[par], svs[par])

    def wait(par):
        irow = idx_v.at[0]
        pltpu.make_async_copy(u_hbm.at[irow], ubufs[par], sus[par]).wait()
        pltpu.make_async_copy(v_hbm.at[irow], vbufs[par], svs[par]).wait()

    @pl.loop(0, tokens_per_worker // tb)
    def _(blk):
        base = wid * tokens_per_worker + blk * tb
        pltpu.sync_copy(h_hbm.at[pl.ds(base, tb)], h_v)
        pltpu.sync_copy(idx_hbm.at[pl.ds(base * nh, tb * nh)], idx_v)
        pltpu.sync_copy(gate_hbm.at[pl.ds(base * nh, tb * nh)], gate_v)
        start(0, 0, 0)

        @pl.loop(0, tb)
        def _(tok):
            for hd in range(nh):
                par = hd % 2
                if hd + 1 < nh:
                    start(tok, hd + 1, 1 - par)
                else:
                    @pl.when(tok + 1 < tb)
                    def _():
                        start(tok + 1, 0, 1 - par)
                wait(par)
                ub, vb = ubufs[par], vbufs[par]

                @plsc.parallel_loop(0, SC_ROW_CHUNKS, carry=(zero,) * k)
                def accs(c, acc):
                    ha = h_v[tok, pl.ds(c * lanes, lanes)]
                    hb = h_v[tok, pl.ds(SC_HALF + c * lanes, lanes)]
                    out = []
                    for r in range(k):
                        a, b = _unpack_pair(ub[r, pl.ds(c * lanes, lanes)])
                        out.append(acc[r] + (ha * a + hb * b))
                    return tuple(out)

                s_vec = zero
                for r in range(k):
                    s_vec = jnp.where(lane == r, jnp.sum(accs[r]), s_vec)
                w_vec = gate_v[tok * nh + hd, :] * _gelu_tanh(s_vec)
                wb = [jnp.sum(jnp.where(lane == r, w_vec, 0.0)) for r in range(k)]

                @plsc.parallel_loop(0, SC_ROW_CHUNKS)
                def _(c):
                    pa, pb = [], []
                    for r in range(k):
                        a, b = _unpack_pair(vb[r, pl.ds(c * lanes, lanes)])
                        pa.append(wb[r] * a)
                        pb.append(wb[r] * b)
                    while len(pa) > 1:
                        pa = [pa[i] + pa[i + 1] for i in range(0, len(pa), 2)]
                        pb = [pb[i] + pb[i + 1] for i in range(0, len(pb), 2)]
                    for off, o in ((0, pa[0]), (SC_HALF, pb[0])):
                        cols = pl.ds(off + c * lanes, lanes)
                        if hd == 0:
                            out_v[tok, cols] = o
                        else:
                            out_v[tok, cols] = out_v[tok, cols] + o

        pltpu.sync_copy(out_v, out_hbm.at[pl.ds(base, tb)])


def _peer_experts(h, eidx, gate, u_tab, v_tab):
    t, d = h.shape
    assert t % (SC_WORKERS * SC_TOKEN_BLOCK) == 0 and d == D_MODEL
    rows = pltpu.VMEM((PK_TOPK, SC_HALF), jnp.int32)
    return pl.kernel(
        functools.partial(_peer_sc_body, tokens_per_worker=t // SC_WORKERS),
        out_type=jax.ShapeDtypeStruct((t, d), F32),
        mesh=plsc.VectorSubcoreMesh(core_axis_name="c", subcore_axis_name="s"),
        scratch_types=[
            pltpu.VMEM((SC_TOKEN_BLOCK, d), F32),
            pltpu.VMEM((SC_TOKEN_BLOCK * PK_HEADS, PK_TOPK), jnp.int32),
            pltpu.VMEM((SC_TOKEN_BLOCK * PK_HEADS, PK_TOPK), F32),
            pltpu.VMEM((SC_TOKEN_BLOCK, d), F32),
            rows, rows, rows, rows,
            pltpu.SemaphoreType.DMA, pltpu.SemaphoreType.DMA,
            pltpu.SemaphoreType.DMA, pltpu.SemaphoreType.DMA,
        ],
        compiler_params=pltpu.CompilerParams(needs_layout_passes=False),
        name="peer_experts_sc",
    )(h, eidx, gate, u_tab, v_tab)


TT_DENSE = 256
EB_DENSE = 2048
N_EXPERTS = N_KEYS * N_KEYS
N_SEL = PK_HEADS * PK_TOPK


def _dense_body(h_ref, e_ref, g_ref, u_ref, v_ref, o_ref, g_scr, acc_scr):
    eb = pl.program_id(1)
    tt = h_ref.shape[0]
    slabs = EB_DENSE // N_KEYS

    @pl.when(eb == 0)
    def _():
        acc_scr[...] = jnp.zeros_like(acc_scr)
        sub = lax.broadcasted_iota(jnp.int32, (N_KEYS, N_SEL), 0)

        def per_token(t, carry):
            e = e_ref[pl.ds(t, 1), :]
            p_t = jnp.where(sub == (e >> 7), g_ref[pl.ds(t, 1), :], 0.0).astype(BF16)
            q_t = (sub == (e & (N_KEYS - 1))).astype(BF16)
            g_scr[pl.ds(pl.multiple_of(t * N_KEYS, N_KEYS), N_KEYS), :] = lax.dot_general(
                p_t, q_t, (((1,), (1,)), ((), ())), preferred_element_type=F32)
            return carry

        lax.fori_loop(0, tt, per_token, 0, unroll=8)

    s = lax.dot_general(h_ref[...].astype(BF16), u_ref[...], (((1,), (1,)), ((), ())), preferred_element_type=F32)
    gs = jnp.concatenate([g_scr[pl.ds(eb * slabs + k, tt, stride=N_KEYS), :] for k in range(slabs)], axis=1)
    acc_scr[...] += jnp.dot((gs * jax.nn.gelu(s)).astype(BF16), v_ref[...], preferred_element_type=F32)

    @pl.when(eb == pl.num_programs(1) - 1)
    def _():
        o_ref[...] = acc_scr[...]


def _peer_dense(h, eidx, gate, u_bf, v_bf):
    t, d = h.shape
    tt = TT_DENSE
    assert t % tt == 0 and N_KEYS == 128
    tile = lambda i, e: (i, 0)
    blk = lambda i, e: (e, 0)
    return pl.pallas_call(
        _dense_body,
        out_shape=jax.ShapeDtypeStruct((t, d), F32),
        grid=(t // tt, N_EXPERTS // EB_DENSE),
        in_specs=[pl.BlockSpec((tt, d), tile), pl.BlockSpec((tt, N_SEL), tile), pl.BlockSpec((tt, N_SEL), tile),
                  pl.BlockSpec((EB_DENSE, d), blk), pl.BlockSpec((EB_DENSE, d), blk)],
        out_specs=pl.BlockSpec((tt, d), tile),
        scratch_shapes=[pltpu.VMEM((tt * N_KEYS, N_KEYS), F32), pltpu.VMEM((tt, d), F32)],
        compiler_params=_cparams("arbitrary", "arbitrary"),
        name="peer_dense_tc",
    )(h, eidx, gate, u_bf, v_bf)


TM_FIN = 512


def _final_body(x1_ref, p_ref, m_ref, w_ref, y_ref):
    y_ref[...] = _rms(x1_ref[...] + m_ref[0, 5:6, :] * p_ref[...]) * w_ref[...]


def _final(x1, peer, mod, norm_f, rows_of):
    t, d = x1.shape
    tm = _token_tile(t, TM_FIN)
    row_of = rows_of(tm)
    tile = pl.BlockSpec((tm, d), lambda i: (i, 0))
    return pl.pallas_call(
        _final_body,
        out_shape=jax.ShapeDtypeStruct((t, d), F32),
        grid=(t // tm,),
        in_specs=[tile, tile, pl.BlockSpec((1, N_MOD, d), lambda i: (row_of(i), 0, 0)),
                  pl.BlockSpec((1, d), lambda i: (0, 0))],
        out_specs=tile,
        compiler_params=_cparams("arbitrary"),
        name="final_norm",
    )(x1, peer, mod, norm_f.reshape(1, d))


CTX_GROUPS = (2, 14, 16)
LAT_GROUP = 2
DENSE_TOKENS = 41 * 256


def _zero_of(a):
    return (lax.shift_right_logical(a[0, 0], 31) >> 1).astype(F32)


def _group(x, mod, row0, per_seq, n, hg_s0, lru_s0, col_major, p, after, dense_tokens=0):
    seqs = hg_s0.shape[0]
    t = seqs * n
    for a in after:
        mod = mod + _zero_of(a)
    rows_of = lambda tm: _mod_row_map(tm, n, row0, per_seq)
    z = _in_proj(x, mod, p['norm1'], p['w_in'], rows_of)
    o_hg, hg_fin = _hgrn2(z, p['lb'], hg_s0, n, 0)
    hr, lru_fin = _rglru(z, COL_XR, 0, n, seqs, col_major, *p['lru'], lru_s0)
    x1, h2, q = _out_proj(x, o_hg, hr, z, mod, p['hg_norm'], p['norm2'], p['w_a'], p['w_b'], p['w_o'], p['w_q'],
                          rows_of(TM_OUT))
    eidx, gate = _pk_topk(q, p['keys'])
    ts = t - dense_tokens
    parts = []
    if ts:
        parts.append(_peer_experts(h2[:ts], eidx[:ts].reshape(ts * PK_HEADS, PK_TOPK),
                                   gate[:ts].reshape(ts * PK_HEADS, PK_TOPK), p['pk_u'], p['pk_v']))
    if dense_tokens:
        parts.append(_peer_dense(h2[ts:], eidx[ts:], gate[ts:], p['u_bf'], p['v_bf']))
    peer = parts[0] if len(parts) == 1 else jnp.concatenate(parts, axis=0)
    return _final(x1, peer, mod, p['norm_f'], rows_of), hg_fin, lru_fin, eidx


def kernel(x_prompt, x_sample, state_hgrn, state_rglru, c, c_ctx, w_mod, b_mod, norm1, w_in,
           hg_lb_logits, hg_norm, lru_conv_w, lru_conv_b, lru_w_r, lru_b_r, lru_w_i, lru_b_i,
           lru_lambda, w_branch_a, w_branch_b, w_out, norm2, pk_w_q, pk_sub_keys, pk_u, pk_v,
           norm_f):
    assert w_mod.shape[0] == 1, "single trunk layer"
    d = D_MODEL
    nb_ctx, n_ctx, _ = x_prompt.shape
    nb_lat, n_lat, _ = x_sample.shape
    assert nb_lat < MOD_ROWS and nb_lat % LAT_GROUP == 0 and nb_ctx == sum(CTX_GROUPS)
    ctx_row = nb_lat

    cond = jnp.zeros((MOD_ROWS, d), F32).at[:nb_lat].set(c).at[ctx_row].set(c_ctx)
    mod = _modulation(cond, w_mod[0], b_mod[0])
    p = {
        'lb': jnp.cumsum(jax.nn.softmax(hg_lb_logits.astype(F32), axis=1), axis=1)[:, 0],
        'norm1': norm1[0], 'w_in': w_in[0].astype(BF16), 'hg_norm': hg_norm[0], 'norm2': norm2[0],
        'lru': (lru_conv_w[0], lru_conv_b[0], _block_diag(lru_w_r[0]), _block_diag(lru_w_i[0]),
                lru_b_r[0], lru_b_i[0], lru_lambda[0]),
        'w_a': w_branch_a[0].astype(BF16), 'w_b': w_branch_b[0].astype(BF16), 'w_o': w_out[0].astype(BF16),
        'w_q': pk_w_q[0].astype(BF16), 'keys': pk_sub_keys[0].astype(BF16),
        'pk_u': _pack_bf16_pairs(pk_u[0]), 'pk_v': _pack_bf16_pairs(pk_v[0]), 'norm_f': norm_f,
        'u_bf': pk_u[0].astype(BF16), 'v_bf': pk_v[0].astype(BF16),
    }
    y_ctx, hg_fin, lru_fin = [], [], []
    tables = [p['pk_u'], p['pk_v']]
    after = tables
    s0 = 0
    for g in CTX_GROUPS:
        y, hg, lr, eidx = _group(
            x_prompt[s0:s0 + g].reshape(g * n_ctx, d), mod, ctx_row, False, n_ctx,
            jnp.zeros((g, 2, HG_HEADS, HG_DK, HG_DV), F32), jnp.zeros((g, 2, LRU_W), F32), False, p, after)
        s0 += g
        after = [eidx] if after is tables else after
        y_ctx.append(y.reshape(g, n_ctx, d))
        hg_fin.append(hg)
        lru_fin.append(lr)
    y_lat = []
    tg = LAT_GROUP * n_lat
    for s0 in range(0, nb_lat, LAT_GROUP):
        sl = slice(s0, s0 + LAT_GROUP)
        tokens_after = (nb_lat - s0 - LAT_GROUP) * n_lat
        y, _, _, _ = _group(x_sample[sl].reshape(tg, d), mod, s0, True, n_lat,
                            state_hgrn[sl, 0], state_rglru[sl, 0], True, p, after,
                            dense_tokens=min(tg, max(0, DENSE_TOKENS - tokens_after)))
        y_lat.append(y.reshape(LAT_GROUP, n_lat, d))
    return (jnp.concatenate(y_ctx, axis=0), jnp.concatenate(y_lat, axis=0),
            jnp.concatenate(hg_fin, axis=0)[:, None], jnp.concatenate(lru_fin, axis=0)[:, None])
```

```python
import functools

import jax, jax.numpy as jnp
from jax import lax
from jax.experimental import pallas as pl
from jax.experimental.pallas import tpu as pltpu
from jax.experimental.pallas import tpu_sc as plsc

D_MODEL = 1024
GRID_W = 64
EPS = 1e-6
HG_HEADS = 4
HG_DK = 128
HG_DV = 128
HG_KW = HG_HEADS * HG_DK
HG_VW = HG_HEADS * HG_DV
HG_CHUNK = 32
LRU_W = D_MODEL // 2
LRU_BLOCKS = 8
LRU_BW = LRU_W // LRU_BLOCKS
CONV_W = 4
LRU_C = 8.0
PK_HEADS = 8
N_KEYS = 128
PK_TOPK = 16
PK_DQ = 256
PK_DH = PK_DQ // 2
IN_W = 3 * HG_KW + 2 * HG_VW + 2 * LRU_W + 2 * D_MODEL
COL_G, COL_XR, COL_XG, COL_GA, COL_GB = 4, 5, 6, 7, 9
N_MOD = 6
MOD_ROWS = 16
F32 = jnp.float32
BF16 = jnp.bfloat16
VMEM_LIMIT = 48 * 1024 * 1024


def _cparams(*sem):
    return pltpu.CompilerParams(dimension_semantics=sem, vmem_limit_bytes=VMEM_LIMIT)


def _silu(x):
    return x * jax.nn.sigmoid(x)


def _rms(x):
    return x * lax.rsqrt(jnp.mean(x * x, axis=-1, keepdims=True) + EPS)


def _mod_body(c_ref, w_ref, b_ref, o_ref):
    o_ref[...] = lax.dot_general(_silu(c_ref[...]), w_ref[...], (((1,), (0,)), ((), ())),
                                 precision=lax.Precision.HIGHEST, preferred_element_type=F32) + b_ref[...]


def _modulation(cond, w_mod, b_mod):
    d = D_MODEL
    out = pl.pallas_call(
        _mod_body,
        out_shape=jax.ShapeDtypeStruct((MOD_ROWS, N_MOD * d), F32),
        grid=(N_MOD,),
        in_specs=[pl.BlockSpec((MOD_ROWS, d), lambda j: (0, 0)),
                  pl.BlockSpec((d, d), lambda j: (0, j)),
                  pl.BlockSpec((1, d), lambda j: (0, j))],
        out_specs=pl.BlockSpec((MOD_ROWS, d), lambda j: (0, j)),
        compiler_params=_cparams("arbitrary"),
        name="adaln_modulation",
    )(cond, w_mod, b_mod.reshape(1, N_MOD * d))
    return out.reshape(MOD_ROWS, N_MOD, d)


def _mod_row_map(tm, n, row0, per_seq):
    per = n // tm
    return (lambda i: row0 + i // per) if per_seq else (lambda i: row0)


TM_IN = 512
TOKEN_TILE_MIN = 256
TN_IN = IN_W // 2


def _in_body(x_ref, m_ref, n1_ref, w_ref, z_ref, h_scr):
    @pl.when(pl.program_id(1) == 0)
    def _():
        y = _rms(x_ref[...]) * n1_ref[...]
        h_scr[...] = (y * (1.0 + m_ref[0, 1:2, :]) + m_ref[0, 0:1, :]).astype(BF16)

    z_ref[...] = jnp.dot(h_scr[...], w_ref[...], preferred_element_type=F32)


def _token_tile(t, preferred):
    tm = preferred if t % preferred == 0 else TOKEN_TILE_MIN
    assert t % tm == 0
    return tm


def _in_proj(x, mod, norm1, w_in_bf, rows_of):
    t, d = x.shape
    tm = _token_tile(t, TM_IN)
    row_of = rows_of(tm)
    return pl.pallas_call(
        _in_body,
        out_shape=jax.ShapeDtypeStruct((t, IN_W), F32),
        grid=(t // tm, IN_W // TN_IN),
        in_specs=[pl.BlockSpec((tm, d), lambda i, j: (i, 0)),
                  pl.BlockSpec((1, N_MOD, d), lambda i, j: (row_of(i), 0, 0)),
                  pl.BlockSpec((1, d), lambda i, j: (0, 0)),
                  pl.BlockSpec((d, TN_IN), lambda i, j: (0, j))],
        out_specs=pl.BlockSpec((tm, TN_IN), lambda i, j: (i, j)),
        scratch_shapes=[pltpu.VMEM((tm, d), BF16)],
        compiler_params=_cparams("arbitrary", "arbitrary"),
        name="in_proj",
    )(x, mod, norm1.reshape(1, d), w_in_bf)


HG_LONG_SEQ = 4096


def _heads_per_step(n):
    return HG_HEADS if n < HG_LONG_SEQ else HG_HEADS // 2


def _hg_body(q_ref, ff_ref, fb_ref, v_ref, lb_ref, s0_ref, o_ref, sfin_ref, st_scr, ob_scr, *, n, hps):
    c = HG_CHUNK
    nc = n // c
    row = lax.broadcasted_iota(jnp.int32, (c, c), 0)
    col = lax.broadcasted_iota(jnp.int32, (c, c), 1)
    lower = row >= col
    tri = (lower.astype(F32), (row <= col).astype(F32))
    masks = (lower, row <= col)
    f_refs = (ff_ref, fb_ref)
    for d in range(2):
        for hh in range(hps):
            st_scr[d * hps + hh] = s0_ref[0, d, hh].T

    def step(i, carry):
        chains = [(d, hh) for d in range(2) for hh in range(hps)]
        st1 = []
        for d, hh in chains:
            ci = i if d == 0 else nc - 1 - i
            r = pl.ds(pl.multiple_of(ci * c, c), c)
            hc = slice(hh * HG_DK, (hh + 1) * HG_DK)
            lb = lb_ref[d:d + 1, hc]
            f = lb + (1.0 - lb) * jax.nn.sigmoid(f_refs[d][r, hc])
            lf = jnp.log(f)
            cum = lax.dot_general(tri[d], lf, (((1,), (0,)), ((), ())),
                                  precision=lax.Precision.HIGHEST, preferred_element_type=F32)
            st1.append((r, hc, 1.0 - f, lf, cum))
        st2 = []
        for (d, hh), (r, hc, k, lf, cum) in zip(chains, st1):
            q = _silu(q_ref[r, hc])
            v = v_ref[r, hc].astype(BF16)
            tot = jnp.sum(lf, axis=0, keepdims=True)
            q_dec = (q * jnp.exp(cum)).astype(BF16)
            k_inv = (k * jnp.exp(-cum)).astype(BF16)
            k_end = (k * jnp.exp(tot - cum)).astype(BF16)
            st = st_scr[d * hps + hh]
            att = lax.dot_general(q_dec, k_inv, (((1,), (1,)), ((), ())), preferred_element_type=F32)
            o_inter = lax.dot_general(q_dec, st.astype(BF16), (((1,), (1,)), ((), ())), preferred_element_type=F32)
            ds_t = lax.dot_general(v, k_end, (((0,), (0,)), ((), ())), preferred_element_type=F32)
            st_scr[d * hps + hh] = st * jnp.exp(tot) + ds_t
            st2.append((v, att, o_inter))
        for (d, hh), (r, hc, _, _, _), (v, att, o_inter) in zip(chains, st1, st2):
            att = jnp.where(masks[d], att, 0.0).astype(BF16)
            o = jnp.dot(att, v, preferred_element_type=F32) + o_inter
            if d == 0:
                o_ref[r, hc] = o
            else:
                ob_scr[r, hc] = o
        return carry

    lax.fori_loop(0, nc, step, 0)
    o_ref[...] = o_ref[...] + ob_scr[...]
    for d in range(2):
        for hh in range(hps):
            sfin_ref[0, d, hh] = st_scr[d * hps + hh].T


def _hgrn2(z, lb, s0, n, row_block0):
    seqs = s0.shape[0]
    hps = _heads_per_step(n)
    wb = hps * HG_DK
    nblk = HG_KW // wb

    def zspec(seg):
        return pl.BlockSpec((n, wb), lambda s, h: (row_block0 + s, seg * nblk + h))

    st_spec = pl.BlockSpec((1, 2, hps, HG_DK, HG_DV), lambda s, h: (s, 0, h, 0, 0))
    return pl.pallas_call(
        functools.partial(_hg_body, n=n, hps=hps),
        out_shape=(jax.ShapeDtypeStruct((seqs * n, HG_VW), F32),
                   jax.ShapeDtypeStruct((seqs, 2, HG_HEADS, HG_DK, HG_DV), F32)),
        grid=(seqs, HG_HEADS // hps),
        in_specs=[zspec(0), zspec(1), zspec(2), zspec(3),
                  pl.BlockSpec((2, wb), lambda s, h: (0, h)), st_spec],
        out_specs=(pl.BlockSpec((n, wb), lambda s, h: (s, h)), st_spec),
        scratch_shapes=[pltpu.VMEM((2 * hps, HG_DV, HG_DK), F32), pltpu.VMEM((n, wb), F32)],
        compiler_params=_cparams("arbitrary", "arbitrary"),
        name="hgrn2_scan",
    )(z, z, z, z, lb, s0)


LRU_RB = 64
HALO = 8


LANES = 128


def _lru_body(x_ref, cw_ref, cb_ref, wr_ref, wi_ref, br_ref, bi_ref, lam_ref, h0_ref, hr_ref, e_ref,
              xp_scr, xc_scr, *cm_scr, n, col_major):
    rb = LRU_RB
    nb = n // rb
    w = LRU_W
    grid_rows = n // GRID_W
    slabs = w // LANES
    zeros = jnp.zeros((HALO, w), F32)
    xp_scr[0:HALO, :] = zeros
    xp_scr[HALO + n:HALO + n + HALO, :] = zeros
    if col_major:
        st_scr, hcm_scr = cm_scr
        for j in range(slabs):
            st_scr[j] = x_ref[:, j * LANES:(j + 1) * LANES]

        def to_col_major(c, carry):
            dst = pl.ds(pl.multiple_of(HALO + c * grid_rows, 8), grid_rows)
            for j in range(slabs):
                xp_scr[dst, j * LANES:(j + 1) * LANES] = st_scr[j, pl.ds(c, grid_rows, stride=GRID_W), :]
            return carry

        lax.fori_loop(0, GRID_W, to_col_major, 0)
        h_dst = hcm_scr
    else:
        xp_scr[HALO:HALO + n, :] = x_ref[...]
        h_dst = hr_ref
    cw = cw_ref[...]
    cb = cb_ref[...]

    def conv_blk(b, carry):
        base = pl.multiple_of(b * rb, rb)
        xh = xp_scr[pl.ds(base, rb + 2 * HALO), :]
        ext = rb + 2 * HALO
        acc = cb + cw[2:3, :] * xh[HALO:HALO + rb]
        acc = acc + cw[0:1, :] * pltpu.roll(xh, 2, axis=0)[HALO:HALO + rb]
        acc = acc + cw[1:2, :] * pltpu.roll(xh, 1, axis=0)[HALO:HALO + rb]
        acc = acc + cw[3:4, :] * pltpu.roll(xh, ext - 1, axis=0)[HALO:HALO + rb]
        xc_scr[pl.ds(base, rb), :] = acc
        return carry

    lax.fori_loop(0, nb, conv_blk, 0)

    rows = lax.broadcasted_iota(jnp.int32, (rb, w), 0)

    def gates(blk, d):
        r = pl.ds(pl.multiple_of(blk * rb, rb), rb)
        xc = xc_scr[r, :]
        xb = xc.astype(BF16)
        rg = jax.nn.sigmoid(jnp.dot(xb, wr_ref[d], preferred_element_type=F32) + br_ref[d:d + 1, :])
        ig = jax.nn.sigmoid(jnp.dot(xb, wi_ref[d], preferred_element_type=F32) + bi_ref[d:d + 1, :])
        log_a = (-LRU_C) * jax.nn.softplus(-lam_ref[d:d + 1, :]) * rg
        a = jnp.exp(log_a)
        u = jnp.sqrt(-jnp.tanh(log_a) * (a * a + 1.0)) * (ig * xc)
        return r, a, u

    def fwd_blk(blk, h_prev):
        r, a, u = gates(blk, 0)
        s = 1
        while s < rb:
            keep = rows >= s
            a_sh = jnp.where(keep, pltpu.roll(a, s, axis=0), 1.0)
            u_sh = jnp.where(keep, pltpu.roll(u, s, axis=0), 0.0)
            u = a * u_sh + u
            a = a * a_sh
            s *= 2
        h = u + a * h_prev
        h_dst[r, :] = h
        return h[rb - 1:rb, :]

    def bwd_blk(i, h_next):
        r, a, u = gates(nb - 1 - i, 1)
        s = 1
        while s < rb:
            keep = rows < rb - s
            a_sh = jnp.where(keep, pltpu.roll(a, rb - s, axis=0), 1.0)
            u_sh = jnp.where(keep, pltpu.roll(u, rb - s, axis=0), 0.0)
            u = a * u_sh + u
            a = a * a_sh
            s *= 2
        h = u + a * h_next
        h_dst[r, :] = h_dst[r, :] + h
        return h[0:1, :]

    e_ref[0, 0:1, :] = lax.fori_loop(0, nb, fwd_blk, h0_ref[0, 0:1, :])
    e_ref[0, 1:2, :] = lax.fori_loop(0, nb, bwd_blk, h0_ref[0, 1:2, :])
    if col_major:
        def to_row_major(c, carry):
            src = pl.ds(pl.multiple_of(c * grid_rows, 8), grid_rows)
            for j in range(slabs):
                st_scr[j, pl.ds(c, grid_rows, stride=GRID_W), :] = hcm_scr[src, j * LANES:(j + 1) * LANES]
            return carry

        lax.fori_loop(0, GRID_W, to_row_major, 0)
        for j in range(slabs):
            hr_ref[:, j * LANES:(j + 1) * LANES] = st_scr[j]


def _rglru(x, col_block, row_block0, n, seqs, col_major, conv_w, conv_b, wr_bd, wi_bd, b_r, b_i, lam, h0):
    w = LRU_W
    cm_scratch = [pltpu.VMEM((w // LANES, n, LANES), F32), pltpu.VMEM((n, w), F32)] if col_major else []
    full2 = lambda s: (0, 0)
    full3 = lambda s: (0, 0, 0)
    return pl.pallas_call(
        functools.partial(_lru_body, n=n, col_major=col_major),
        out_shape=(jax.ShapeDtypeStruct((seqs * n, w), F32), jax.ShapeDtypeStruct((seqs, 2, w), F32)),
        grid=(seqs,),
        in_specs=[pl.BlockSpec((n, w), lambda s: (row_block0 + s, col_block)),
                  pl.BlockSpec((CONV_W, w), full2), pl.BlockSpec((1, w), full2),
                  pl.BlockSpec((2, w, w), full3), pl.BlockSpec((2, w, w), full3),
                  pl.BlockSpec((2, w), full2), pl.BlockSpec((2, w), full2), pl.BlockSpec((2, w), full2),
                  pl.BlockSpec((1, 2, w), lambda s: (s, 0, 0))],
        out_specs=(pl.BlockSpec((n, w), lambda s: (s, 0)), pl.BlockSpec((1, 2, w), lambda s: (s, 0, 0))),
        scratch_shapes=[pltpu.VMEM((n + 2 * HALO, w), F32), pltpu.VMEM((n, w), F32)] + cm_scratch,
        compiler_params=_cparams("arbitrary"),
        name="rglru",
    )(x, conv_w, conv_b.reshape(1, w), wr_bd, wi_bd, b_r, b_i, lam, h0)


def _block_diag(wg):
    eye = jnp.eye(LRU_BLOCKS, dtype=wg.dtype)
    dense = wg[:, :, :, None, :] * eye[None, :, None, :, None]
    return dense.reshape(2, LRU_W, LRU_W).astype(BF16)


TM_OUT = 256


def _out_body(x_ref, o_ref, hr_ref, g_ref, xg_ref, ga0_ref, ga1_ref, gb0_ref, gb1_ref, m_ref,
              hgn_ref, n2_ref, wa_ref, wb_ref, wo_ref, wq_ref, x1_ref, h2_ref, q_ref):
    o = o_ref[...]
    parts = []
    for h in range(HG_HEADS):
        parts.append(_rms(o[:, h * HG_DV:(h + 1) * HG_DV]) * hgn_ref[...])
    on = jnp.concatenate(parts, axis=1) * _silu(g_ref[...])
    y_a = jnp.dot(on.astype(BF16), wa_ref[...], preferred_element_type=F32)
    y_b = jnp.dot((hr_ref[...] * jax.nn.gelu(xg_ref[...])).astype(BF16), wb_ref[...], preferred_element_type=F32)
    ga = jnp.concatenate([ga0_ref[...], ga1_ref[...]], axis=1)
    gb = jnp.concatenate([gb0_ref[...], gb1_ref[...]], axis=1)
    merged = jax.nn.sigmoid(ga) * y_a + jax.nn.sigmoid(gb) * y_b
    mix = jnp.dot(merged.astype(BF16), wo_ref[...], preferred_element_type=F32)
    x1 = x_ref[...] + m_ref[0, 2:3, :] * mix
    x1_ref[...] = x1
    h2 = _rms(x1) * n2_ref[...] * (1.0 + m_ref[0, 4:5, :]) + m_ref[0, 3:4, :]
    h2_ref[...] = h2
    q_ref[...] = jnp.dot(h2.astype(BF16), wq_ref[...], preferred_element_type=F32).astype(BF16)


def _out_proj(x, o_hg, hr, z, mod, hg_norm, norm2, wa, wb, wo, wq, row_of):
    t, d = x.shape
    tm = TM_OUT
    nq = PK_HEADS * PK_DQ
    half = lambda i: (i, 0)
    zc = lambda c: pl.BlockSpec((tm, LRU_W), lambda i: (i, c))
    const = lambda i: (0, 0)
    return pl.pallas_call(
        _out_body,
        out_shape=(jax.ShapeDtypeStruct((t, d), F32), jax.ShapeDtypeStruct((t, d), F32),
                   jax.ShapeDtypeStruct((t, nq), BF16)),
        grid=(t // tm,),
        in_specs=[pl.BlockSpec((tm, d), half), pl.BlockSpec((tm, HG_VW), half), pl.BlockSpec((tm, LRU_W), half),
                  zc(COL_G), zc(COL_XG), zc(COL_GA), zc(COL_GA + 1), zc(COL_GB), zc(COL_GB + 1),
                  pl.BlockSpec((1, N_MOD, d), lambda i: (row_of(i), 0, 0)),
                  pl.BlockSpec((1, HG_DV), const), pl.BlockSpec((1, d), const),
                  pl.BlockSpec((HG_VW, d), const), pl.BlockSpec((LRU_W, d), const),
                  pl.BlockSpec((d, d), const), pl.BlockSpec((d, nq), const)],
        out_specs=(pl.BlockSpec((tm, d), half), pl.BlockSpec((tm, d), half), pl.BlockSpec((tm, nq), half)),
        compiler_params=_cparams("arbitrary"),
        name="out_proj",
    )(x, o_hg, hr, z, z, z, z, z, z, mod, hg_norm.reshape(1, HG_DV), norm2.reshape(1, d), wa, wb, wo, wq)


TT_TOPK = 128


def _extract_topk(s, order, k, payload=None):
    vals, ords, picks = [], [], []
    for _ in range(k):
        m = jnp.max(s, axis=0, keepdims=True)
        o = jnp.max(jnp.where(s == m, order, -1.0), axis=0, keepdims=True)
        hit = order == o
        vals.append(m)
        ords.append(o)
        if payload is not None:
            picks.append(jnp.sum(jnp.where(hit, payload, 0), axis=0, keepdims=True))
        s = jnp.where(hit, -jnp.inf, s)
    cat = lambda xs: jnp.concatenate(xs, axis=0)
    return cat(vals), cat(ords), (cat(picks) if payload is not None else None)


PAIR_ROWS = 4
PAIR_COLS = 3
assert all((i + 1) * (j + 1) > PK_TOPK for i in range(PAIR_ROWS, PK_TOPK) for j in range(PAIR_COLS, PK_TOPK))
assert all((i + 1) * (8 + 1) > PK_TOPK for i in range(1, PAIR_ROWS)) and (8 + 1) * (1 + 1) > PK_TOPK


def _pair_candidates(sv, si, tt):
    cand, order, eid = [], [], []
    hi = float(PK_TOPK * PK_TOPK - 1)
    for i in range(PAIR_ROWS):
        nj = PK_TOPK if i == 0 else 8
        j = lax.broadcasted_iota(jnp.int32, (nj, tt), 0)
        cand.append(sv[0][i:i + 1, :] + sv[1][:nj])
        order.append(hi - (i * PK_TOPK + j).astype(F32))
        eid.append(si[0][i:i + 1, :] * N_KEYS + si[1][:nj])
    for j in range(PAIR_COLS):
        ni = PK_TOPK if j == 0 else 8
        i = lax.broadcasted_iota(jnp.int32, (ni, tt), 0)
        fresh = i >= PAIR_ROWS
        cand.append(jnp.where(fresh, sv[0][:ni] + sv[1][j:j + 1, :], -jnp.inf))
        order.append(jnp.where(fresh, hi - (i * PK_TOPK + j).astype(F32), -2.0))
        eid.append(si[0][:ni] * N_KEYS + si[1][j:j + 1, :])
    cat = lambda xs: jnp.concatenate(xs, axis=0)
    return cat(cand), cat(order), cat(eid)


def _topk_body(q_ref, keys_ref, eidx_ref, gate_ref):
    tt = q_ref.shape[0]
    key_order = (N_KEYS - 1 - lax.broadcasted_iota(jnp.int32, (N_KEYS, tt), 0)).astype(F32)
    e_rows, g_rows = [], []
    for h in range(PK_HEADS):
        sv, si = [], []
        for p in range(2):
            c0 = h * PK_DQ + p * PK_DH
            s = lax.dot_general(keys_ref[p, h], q_ref[:, c0:c0 + PK_DH], (((1,), (1,)), ((), ())),
                                preferred_element_type=F32)
            v, o, _ = _extract_topk(s, key_order, PK_TOPK)
            sv.append(v)
            si.append(N_KEYS - 1 - o.astype(jnp.int32))
        cand, order, cidx = _pair_candidates(sv, si, tt)
        best, _, eid = _extract_topk(cand, order, PK_TOPK, payload=cidx)
        ex = jnp.exp(best - best[0:1, :])
        g_rows.append(ex / jnp.sum(ex, axis=0, keepdims=True))
        e_rows.append(eid)
    eidx_ref[...] = jnp.concatenate(e_rows, axis=0).T
    gate_ref[...] = jnp.concatenate(g_rows, axis=0).T


def _pk_topk(q, keys_bf):
    t = q.shape[0]
    tt = TT_TOPK
    ne = PK_HEADS * PK_TOPK
    return pl.pallas_call(
        _topk_body,
        out_shape=(jax.ShapeDtypeStruct((t, ne), jnp.int32), jax.ShapeDtypeStruct((t, ne), F32)),
        grid=(t // tt,),
        in_specs=[pl.BlockSpec((tt, PK_HEADS * PK_DQ), lambda i: (i, 0)),
                  pl.BlockSpec((2, PK_HEADS, N_KEYS, PK_DH), lambda i: (0, 0, 0, 0))],
        out_specs=(pl.BlockSpec((tt, ne), lambda i: (i, 0)), pl.BlockSpec((tt, ne), lambda i: (i, 0))),
        compiler_params=_cparams("arbitrary"),
        name="pk_topk",
    )(q, keys_bf)


SC_LANES = 16
SC_CORES = 2
SC_SUBCORES = 16
SC_WORKERS = SC_CORES * SC_SUBCORES
SC_HALF = D_MODEL // 2
SC_ROW_CHUNKS = SC_HALF // SC_LANES
SC_TOKEN_BLOCK = 8


def _pack_bf16_pairs(tab):
    lo = lax.bitcast_convert_type(tab[:, SC_HALF:].astype(BF16), jnp.uint16).astype(jnp.uint32)
    bits = lax.bitcast_convert_type(tab[:, :SC_HALF], jnp.uint32)
    sign = bits & jnp.uint32(0x80000000)
    mag = (bits & jnp.uint32(0x7FFFFFFF)) + jnp.uint32(1 << 15)
    hi = jnp.where(mag >= lo, (mag - lo) >> 16, jnp.uint32(0))
    return lax.bitcast_convert_type(sign | (hi << 16) | lo, jnp.int32)


def _unpack_pair(w):
    return lax.bitcast_convert_type(w, F32), lax.bitcast_convert_type(w << 16, F32)


def _gelu_tanh(x):
    y = 0.7978845608028654 * (x + 0.044715 * (x * x * x))
    t = 1.0 - 2.0 / (jnp.exp(2.0 * y) + 1.0)
    return x * (0.5 * (1.0 + t))


def _peer_sc_body(h_hbm, idx_hbm, gate_hbm, u_hbm, v_hbm, out_hbm,
                  h_v, idx_v, gate_v, out_v, ub0, ub1, vb0, vb1, su0, su1, sv0, sv1, *, tokens_per_worker):
    nh, k, tb, lanes = PK_HEADS, PK_TOPK, SC_TOKEN_BLOCK, SC_LANES
    wid = lax.axis_index("s") * SC_CORES + lax.axis_index("c")
    ubufs, vbufs, sus, svs = (ub0, ub1), (vb0, vb1), (su0, su1), (sv0, sv1)
    lane = lax.iota(jnp.int32, lanes)
    zero = jnp.zeros((lanes,), F32)

    def start(tok, hd, par):
        irow = idx_v.at[tok * nh + hd]
        pltpu.async_copy(u_hbm.at[irow], ubufs[par], sus[par])
        pltpu.async_copy(v_hbm.at[irow], vbufs[par], svs[par])

    def wait(par):
        irow = idx_v.at[0]
        pltpu.make_async_copy(u_hbm.at[irow], ubufs[par], sus[par]).wait()
        pltpu.make_async_copy(v_hbm.at[irow], vbufs[par], svs[par]).wait()

    @pl.loop(0, tokens_per_worker // tb)
    def _(blk):
        base = wid * tokens_per_worker + blk * tb
        pltpu.sync_copy(h_hbm.at[pl.ds(base, tb)], h_v)
        pltpu.sync_copy(idx_hbm.at[pl.ds(base * nh, tb * nh)], idx_v)
        pltpu.sync_copy(gate_hbm.at[pl.ds(base * nh, tb * nh)], gate_v)
        start(0, 0, 0)

        @pl.loop(0, tb)
        def _(tok):
            for hd in range(nh):
                par = hd % 2
                if hd + 1 < nh:
                    start(tok, hd + 1, 1 - par)
                else:
                    @pl.when(tok + 1 < tb)
                    def _():
                        start(tok + 1, 0, 1 - par)
                wait(par)
                ub, vb = ubufs[par], vbufs[par]

                @plsc.parallel_loop(0, SC_ROW_CHUNKS, carry=(zero,) * k)
                def accs(c, acc):
                    ha = h_v[tok, pl.ds(c * lanes, lanes)]
                    hb = h_v[tok, pl.ds(SC_HALF + c * lanes, lanes)]
                    out = []
                    for r in range(k):
                        a, b = _unpack_pair(ub[r, pl.ds(c * lanes, lanes)])
                        out.append(acc[r] + (ha * a + hb * b))
                    return tuple(out)

                s_vec = zero
                for r in range(k):
                    s_vec = jnp.where(lane == r, jnp.sum(accs[r]), s_vec)
                w_vec = gate_v[tok * nh + hd, :] * _gelu_tanh(s_vec)
                wb = [jnp.sum(jnp.where(lane == r, w_vec, 0.0)) for r in range(k)]

                @plsc.parallel_loop(0, SC_ROW_CHUNKS)
                def _(c):
                    pa, pb = [], []
                    for r in range(k):
                        a, b = _unpack_pair(vb[r, pl.ds(c * lanes, lanes)])
                        pa.append(wb[r] * a)
                        pb.append(wb[r] * b)
                    while len(pa) > 1:
                        pa = [pa[i] + pa[i + 1] for i in range(0, len(pa), 2)]
                        pb = [pb[i] + pb[i + 1] for i in range(0, len(pb), 2)]
                    for off, o in ((0, pa[0]), (SC_HALF, pb[0])):
                        cols = pl.ds(off + c * lanes, lanes)
                        if hd == 0:
                            out_v[tok, cols] = o
                        else:
                            out_v[tok, cols] = out_v[tok, cols] + o

        pltpu.sync_copy(out_v, out_hbm.at[pl.ds(base, tb)])


def _peer_experts(h, eidx, gate, u_tab, v_tab):
    t, d = h.shape
    assert t % (SC_WORKERS * SC_TOKEN_BLOCK) == 0 and d == D_MODEL
    rows = pltpu.VMEM((PK_TOPK, SC_HALF), jnp.int32)
    return pl.kernel(
        functools.partial(_peer_sc_body, tokens_per_worker=t // SC_WORKERS),
        out_type=jax.ShapeDtypeStruct((t, d), F32),
        mesh=plsc.VectorSubcoreMesh(core_axis_name="c", subcore_axis_name="s"),
        scratch_types=[
            pltpu.VMEM((SC_TOKEN_BLOCK, d), F32),
            pltpu.VMEM((SC_TOKEN_BLOCK * PK_HEADS, PK_TOPK), jnp.int32),
            pltpu.VMEM((SC_TOKEN_BLOCK * PK_HEADS, PK_TOPK), F32),
            pltpu.VMEM((SC_TOKEN_BLOCK, d), F32),
            rows, rows, rows, rows,
            pltpu.SemaphoreType.DMA, pltpu.SemaphoreType.DMA,
            pltpu.SemaphoreType.DMA, pltpu.SemaphoreType.DMA,
        ],
        compiler_params=pltpu.CompilerParams(needs_layout_passes=False),
        name="peer_experts_sc",
    )(h, eidx, gate, u_tab, v_tab)


TT_DENSE = 256
EB_DENSE = 2048
N_EXPERTS = N_KEYS * N_KEYS
N_SEL = PK_HEADS * PK_TOPK


def _dense_body(h_ref, e_ref, g_ref, u_ref, v_ref, o_ref, g_scr, acc_scr):
    eb = pl.program_id(1)
    tt = h_ref.shape[0]
    slabs = EB_DENSE // N_KEYS

    @pl.when(eb == 0)
    def _():
        acc_scr[...] = jnp.zeros_like(acc_scr)
        sub = lax.broadcasted_iota(jnp.int32, (N_KEYS, N_SEL), 0)

        def per_token(t, carry):
            e = e_ref[pl.ds(t, 1), :]
            p_t = jnp.where(sub == (e >> 7), g_ref[pl.ds(t, 1), :], 0.0).astype(BF16)
            q_t = (sub == (e & (N_KEYS - 1))).astype(BF16)
            g_scr[pl.ds(pl.multiple_of(t * N_KEYS, N_KEYS), N_KEYS), :] = lax.dot_general(
                p_t, q_t, (((1,), (1,)), ((), ())), preferred_element_type=F32)
            return carry

        lax.fori_loop(0, tt, per_token, 0, unroll=8)

    s = lax.dot_general(h_ref[...].astype(BF16), u_ref[...], (((1,), (1,)), ((), ())), preferred_element_type=F32)
    gs = jnp.concatenate([g_scr[pl.ds(eb * slabs + k, tt, stride=N_KEYS), :] for k in range(slabs)], axis=1)
    acc_scr[...] += jnp.dot((gs * jax.nn.gelu(s)).astype(BF16), v_ref[...], preferred_element_type=F32)

    @pl.when(eb == pl.num_programs(1) - 1)
    def _():
        o_ref[...] = acc_scr[...]


def _peer_dense(h, eidx, gate, u_bf, v_bf):
    t, d = h.shape
    tt = TT_DENSE
    assert t % tt == 0 and N_KEYS == 128
    tile = lambda i, e: (i, 0)
    blk = lambda i, e: (e, 0)
    return pl.pallas_call(
        _dense_body,
        out_shape=jax.ShapeDtypeStruct((t, d), F32),
        grid=(t // tt, N_EXPERTS // EB_DENSE),
        in_specs=[pl.BlockSpec((tt, d), tile), pl.BlockSpec((tt, N_SEL), tile), pl.BlockSpec((tt, N_SEL), tile),
                  pl.BlockSpec((EB_DENSE, d), blk), pl.BlockSpec((EB_DENSE, d), blk)],
        out_specs=pl.BlockSpec((tt, d), tile),
        scratch_shapes=[pltpu.VMEM((tt * N_KEYS, N_KEYS), F32), pltpu.VMEM((tt, d), F32)],
        compiler_params=_cparams("arbitrary", "arbitrary"),
        name="peer_dense_tc",
    )(h, eidx, gate, u_bf, v_bf)


TM_FIN = 512


def _final_body(x1_ref, p_ref, m_ref, w_ref, y_ref):
    y_ref[...] = _rms(x1_ref[...] + m_ref[0, 5:6, :] * p_ref[...]) * w_ref[...]


def _final(x1, peer, mod, norm_f, rows_of):
    t, d = x1.shape
    tm = _token_tile(t, TM_FIN)
    row_of = rows_of(tm)
    tile = pl.BlockSpec((tm, d), lambda i: (i, 0))
    return pl.pallas_call(
        _final_body,
        out_shape=jax.ShapeDtypeStruct((t, d), F32),
        grid=(t // tm,),
        in_specs=[tile, tile, pl.BlockSpec((1, N_MOD, d), lambda i: (row_of(i), 0, 0)),
                  pl.BlockSpec((1, d), lambda i: (0, 0))],
        out_specs=tile,
        compiler_params=_cparams("arbitrary"),
        name="final_norm",
    )(x1, peer, mod, norm_f.reshape(1, d))


CTX_GROUPS = (2, 14, 16)
LAT_GROUP = 2
DENSE_TOKENS = 42 * 256


def _zero_of(a):
    return (lax.shift_right_logical(a[0, 0], 31) >> 1).astype(F32)


def _group(x, mod, row0, per_seq, n, hg_s0, lru_s0, col_major, p, after, dense_tokens=0):
    seqs = hg_s0.shape[0]
    t = seqs * n
    for a in after:
        mod = mod + _zero_of(a)
    rows_of = lambda tm: _mod_row_map(tm, n, row0, per_seq)
    z = _in_proj(x, mod, p['norm1'], p['w_in'], rows_of)
    o_hg, hg_fin = _hgrn2(z, p['lb'], hg_s0, n, 0)
    hr, lru_fin = _rglru(z, COL_XR, 0, n, seqs, col_major, *p['lru'], lru_s0)
    x1, h2, q = _out_proj(x, o_hg, hr, z, mod, p['hg_norm'], p['norm2'], p['w_a'], p['w_b'], p['w_o'], p['w_q'],
                          rows_of(TM_OUT))
    eidx, gate = _pk_topk(q, p['keys'])
    ts = t - dense_tokens
    parts = []
    if ts:
        parts.append(_peer_experts(h2[:ts], eidx[:ts].reshape(ts * PK_HEADS, PK_TOPK),
                                   gate[:ts].reshape(ts * PK_HEADS, PK_TOPK), p['pk_u'], p['pk_v']))
    if dense_tokens:
        parts.append(_peer_dense(h2[ts:], eidx[ts:], gate[ts:], p['u_bf'], p['v_bf']))
    peer = parts[0] if len(parts) == 1 else jnp.concatenate(parts, axis=0)
    return _final(x1, peer, mod, p['norm_f'], rows_of), hg_fin, lru_fin, eidx


def kernel(x_prompt, x_sample, state_hgrn, state_rglru, c, c_ctx, w_mod, b_mod, norm1, w_in,
           hg_lb_logits, hg_norm, lru_conv_w, lru_conv_b, lru_w_r, lru_b_r, lru_w_i, lru_b_i,
           lru_lambda, w_branch_a, w_branch_b, w_out, norm2, pk_w_q, pk_sub_keys, pk_u, pk_v,
           norm_f):
    assert w_mod.shape[0] == 1, "single trunk layer"
    d = D_MODEL
    nb_ctx, n_ctx, _ = x_prompt.shape
    nb_lat, n_lat, _ = x_sample.shape
    assert nb_lat < MOD_ROWS and nb_lat % LAT_GROUP == 0 and nb_ctx == sum(CTX_GROUPS)
    ctx_row = nb_lat

    cond = jnp.zeros((MOD_ROWS, d), F32).at[:nb_lat].set(c).at[ctx_row].set(c_ctx)
    mod = _modulation(cond, w_mod[0], b_mod[0])
    p = {
        'lb': jnp.cumsum(jax.nn.softmax(hg_lb_logits.astype(F32), axis=1), axis=1)[:, 0],
        'norm1': norm1[0], 'w_in': w_in[0].astype(BF16), 'hg_norm': hg_norm[0], 'norm2': norm2[0],
        'lru': (lru_conv_w[0], lru_conv_b[0], _block_diag(lru_w_r[0]), _block_diag(lru_w_i[0]),
                lru_b_r[0], lru_b_i[0], lru_lambda[0]),
        'w_a': w_branch_a[0].astype(BF16), 'w_b': w_branch_b[0].astype(BF16), 'w_o': w_out[0].astype(BF16),
        'w_q': pk_w_q[0].astype(BF16), 'keys': pk_sub_keys[0].astype(BF16),
        'pk_u': _pack_bf16_pairs(pk_u[0]), 'pk_v': _pack_bf16_pairs(pk_v[0]), 'norm_f': norm_f,
        'u_bf': pk_u[0].astype(BF16), 'v_bf': pk_v[0].astype(BF16),
    }
    y_ctx, hg_fin, lru_fin = [], [], []
    tables = [p['pk_u'], p['pk_v']]
    after = tables
    s0 = 0
    for g in CTX_GROUPS:
        y, hg, lr, eidx = _group(
            x_prompt[s0:s0 + g].reshape(g * n_ctx, d), mod, ctx_row, False, n_ctx,
            jnp.zeros((g, 2, HG_HEADS, HG_DK, HG_DV), F32), jnp.zeros((g, 2, LRU_W), F32), False, p, after)
        s0 += g
        after = [eidx] if after is tables else after
        y_ctx.append(y.reshape(g, n_ctx, d))
        hg_fin.append(hg)
        lru_fin.append(lr)
    y_lat = []
    tg = LAT_GROUP * n_lat
    for s0 in range(0, nb_lat, LAT_GROUP):
        sl = slice(s0, s0 + LAT_GROUP)
        tokens_after = (nb_lat - s0 - LAT_GROUP) * n_lat
        y, _, _, _ = _group(x_sample[sl].reshape(tg, d), mod, s0, True, n_lat,
                            state_hgrn[sl, 0], state_rglru[sl, 0], True, p, after,
                            dense_tokens=min(tg, max(0, DENSE_TOKENS - tokens_after)))
        y_lat.append(y.reshape(LAT_GROUP, n_lat, d))
    return (jnp.concatenate(y_ctx, axis=0), jnp.concatenate(y_lat, axis=0),
            jnp.concatenate(hg_fin, axis=0)[:, None], jnp.concatenate(lru_fin, axis=0)[:, None])
```

```python
import functools

import jax, jax.numpy as jnp
from jax import lax
from jax.experimental import pallas as pl
from jax.experimental.pallas import tpu as pltpu
from jax.experimental.pallas import tpu_sc as plsc

D_MODEL = 1024
GRID_W = 64
EPS = 1e-6
HG_HEADS = 4
HG_DK = 128
HG_DV = 128
HG_KW = HG_HEADS * HG_DK
HG_VW = HG_HEADS * HG_DV
HG_CHUNK = 32
LRU_W = D_MODEL // 2
LRU_BLOCKS = 8
LRU_BW = LRU_W // LRU_BLOCKS
CONV_W = 4
LRU_C = 8.0
PK_HEADS = 8
N_KEYS = 128
PK_TOPK = 16
PK_DQ = 256
PK_DH = PK_DQ // 2
IN_W = 3 * HG_KW + 2 * HG_VW + 2 * LRU_W + 2 * D_MODEL
COL_G, COL_XR, COL_XG, COL_GA, COL_GB = 4, 5, 6, 7, 9
N_MOD = 6
MOD_ROWS = 16
F32 = jnp.float32
BF16 = jnp.bfloat16
VMEM_LIMIT = 48 * 1024 * 1024


def _cparams(*sem):
    return pltpu.CompilerParams(dimension_semantics=sem, vmem_limit_bytes=VMEM_LIMIT)


def _silu(x):
    return x * jax.nn.sigmoid(x)


def _rms(x):
    return x * lax.rsqrt(jnp.mean(x * x, axis=-1, keepdims=True) + EPS)


def _mod_body(c_ref, w_ref, b_ref, o_ref):
    o_ref[...] = lax.dot_general(_silu(c_ref[...]), w_ref[...], (((1,), (0,)), ((), ())),
                                 precision=lax.Precision.HIGHEST, preferred_element_type=F32) + b_ref[...]


def _modulation(cond, w_mod, b_mod):
    d = D_MODEL
    out = pl.pallas_call(
        _mod_body,
        out_shape=jax.ShapeDtypeStruct((MOD_ROWS, N_MOD * d), F32),
        grid=(N_MOD,),
        in_specs=[pl.BlockSpec((MOD_ROWS, d), lambda j: (0, 0)),
                  pl.BlockSpec((d, d), lambda j: (0, j)),
                  pl.BlockSpec((1, d), lambda j: (0, j))],
        out_specs=pl.BlockSpec((MOD_ROWS, d), lambda j: (0, j)),
        compiler_params=_cparams("arbitrary"),
        name="adaln_modulation",
    )(cond, w_mod, b_mod.reshape(1, N_MOD * d))
    return out.reshape(MOD_ROWS, N_MOD, d)


def _mod_row_map(tm, n, row0, per_seq):
    per = n // tm
    return (lambda i: row0 + i // per) if per_seq else (lambda i: row0)


TM_IN = 512
TOKEN_TILE_MIN = 256
TN_IN = IN_W // 2


def _in_body(x_ref, m_ref, n1_ref, w_ref, z_ref):
    y = _rms(x_ref[...]) * n1_ref[...]
    h = (y * (1.0 + m_ref[0, 1:2, :]) + m_ref[0, 0:1, :]).astype(BF16)
    z_ref[...] = jnp.dot(h, w_ref[...], preferred_element_type=F32)


def _token_tile(t, preferred):
    tm = preferred if t % preferred == 0 else TOKEN_TILE_MIN
    assert t % tm == 0
    return tm


def _in_proj(x, mod, norm1, w_in_bf, rows_of):
    t, d = x.shape
    tm = _token_tile(t, TM_IN)
    row_of = rows_of(tm)
    return pl.pallas_call(
        _in_body,
        out_shape=jax.ShapeDtypeStruct((t, IN_W), F32),
        grid=(IN_W // TN_IN, t // tm),
        in_specs=[pl.BlockSpec((tm, d), lambda j, i: (i, 0)),
                  pl.BlockSpec((1, N_MOD, d), lambda j, i: (row_of(i), 0, 0)),
                  pl.BlockSpec((1, d), lambda j, i: (0, 0)),
                  pl.BlockSpec((d, TN_IN), lambda j, i: (0, j))],
        out_specs=pl.BlockSpec((tm, TN_IN), lambda j, i: (i, j)),
        compiler_params=_cparams("arbitrary", "arbitrary"),
        name="in_proj",
    )(x, mod, norm1.reshape(1, d), w_in_bf)


HG_LONG_SEQ = 4096


def _heads_per_step(n):
    return HG_HEADS if n < HG_LONG_SEQ else HG_HEADS // 2


def _hg_body(q_ref, ff_ref, fb_ref, v_ref, lb_ref, s0_ref, o_ref, sfin_ref, st_scr, ob_scr, *, n, hps):
    c = HG_CHUNK
    nc = n // c
    row = lax.broadcasted_iota(jnp.int32, (c, c), 0)
    col = lax.broadcasted_iota(jnp.int32, (c, c), 1)
    lower = row >= col
    tri = (lower.astype(F32), (row <= col).astype(F32))
    masks = (lower, row <= col)
    f_refs = (ff_ref, fb_ref)
    for d in range(2):
        for hh in range(hps):
            st_scr[d * hps + hh] = s0_ref[0, d, hh].T

    def step(i, carry):
        chains = [(d, hh) for d in range(2) for hh in range(hps)]
        st1 = []
        for d, hh in chains:
            ci = i if d == 0 else nc - 1 - i
            r = pl.ds(pl.multiple_of(ci * c, c), c)
            hc = slice(hh * HG_DK, (hh + 1) * HG_DK)
            lb = lb_ref[d:d + 1, hc]
            f = lb + (1.0 - lb) * jax.nn.sigmoid(f_refs[d][r, hc])
            lf = jnp.log(f)
            cum = lax.dot_general(tri[d], lf, (((1,), (0,)), ((), ())),
                                  precision=lax.Precision.HIGHEST, preferred_element_type=F32)
            st1.append((r, hc, 1.0 - f, lf, cum))
        st2 = []
        for (d, hh), (r, hc, k, lf, cum) in zip(chains, st1):
            q = _silu(q_ref[r, hc])
            v = v_ref[r, hc].astype(BF16)
            tot = jnp.sum(lf, axis=0, keepdims=True)
            q_dec = (q * jnp.exp(cum)).astype(BF16)
            k_inv = (k * jnp.exp(-cum)).astype(BF16)
            k_end = (k * jnp.exp(tot - cum)).astype(BF16)
            st = st_scr[d * hps + hh]
            att = lax.dot_general(q_dec, k_inv, (((1,), (1,)), ((), ())), preferred_element_type=F32)
            o_inter = lax.dot_general(q_dec, st.astype(BF16), (((1,), (1,)), ((), ())), preferred_element_type=F32)
            ds_t = lax.dot_general(v, k_end, (((0,), (0,)), ((), ())), preferred_element_type=F32)
            st_scr[d * hps + hh] = st * jnp.exp(tot) + ds_t
            st2.append((v, att, o_inter))
        for (d, hh), (r, hc, _, _, _), (v, att, o_inter) in zip(chains, st1, st2):
            att = jnp.where(masks[d], att, 0.0).astype(BF16)
            o = jnp.dot(att, v, preferred_element_type=F32) + o_inter
            if d == 0:
                o_ref[r, hc] = o
            else:
                ob_scr[r, hc] = o
        return carry

    lax.fori_loop(0, nc, step, 0)
    o_ref[...] = o_ref[...] + ob_scr[...]
    for d in range(2):
        for hh in range(hps):
            sfin_ref[0, d, hh] = st_scr[d * hps + hh].T


def _hgrn2(z, lb, s0, n, row_block0):
    seqs = s0.shape[0]
    hps = _heads_per_step(n)
    wb = hps * HG_DK
    nblk = HG_KW // wb

    def zspec(seg):
        return pl.BlockSpec((n, wb), lambda s, h: (row_block0 + s, seg * nblk + h))

    st_spec = pl.BlockSpec((1, 2, hps, HG_DK, HG_DV), lambda s, h: (s, 0, h, 0, 0))
    return pl.pallas_call(
        functools.partial(_hg_body, n=n, hps=hps),
        out_shape=(jax.ShapeDtypeStruct((seqs * n, HG_VW), F32),
                   jax.ShapeDtypeStruct((seqs, 2, HG_HEADS, HG_DK, HG_DV), F32)),
        grid=(seqs, HG_HEADS // hps),
        in_specs=[zspec(0), zspec(1), zspec(2), zspec(3),
                  pl.BlockSpec((2, wb), lambda s, h: (0, h)), st_spec],
        out_specs=(pl.BlockSpec((n, wb), lambda s, h: (s, h)), st_spec),
        scratch_shapes=[pltpu.VMEM((2 * hps, HG_DV, HG_DK), F32), pltpu.VMEM((n, wb), F32)],
        compiler_params=_cparams("arbitrary", "arbitrary"),
        name="hgrn2_scan",
    )(z, z, z, z, lb, s0)


LRU_RB = 64
HALO = 8


LANES = 128


def _lru_body(x_ref, cw_ref, cb_ref, wr_ref, wi_ref, br_ref, bi_ref, lam_ref, h0_ref, hr_ref, e_ref,
              xp_scr, xc_scr, *cm_scr, n, col_major):
    rb = LRU_RB
    nb = n // rb
    w = LRU_W
    grid_rows = n // GRID_W
    slabs = w // LANES
    zeros = jnp.zeros((HALO, w), F32)
    xp_scr[0:HALO, :] = zeros
    xp_scr[HALO + n:HALO + n + HALO, :] = zeros
    if col_major:
        st_scr, hcm_scr = cm_scr
        for j in range(slabs):
            st_scr[j] = x_ref[:, j * LANES:(j + 1) * LANES]

        def to_col_major(c, carry):
            dst = pl.ds(pl.multiple_of(HALO + c * grid_rows, 8), grid_rows)
            for j in range(slabs):
                xp_scr[dst, j * LANES:(j + 1) * LANES] = st_scr[j, pl.ds(c, grid_rows, stride=GRID_W), :]
            return carry

        lax.fori_loop(0, GRID_W, to_col_major, 0)
        h_dst = hcm_scr
    else:
        xp_scr[HALO:HALO + n, :] = x_ref[...]
        h_dst = hr_ref
    cw = cw_ref[...]
    cb = cb_ref[...]

    def conv_blk(b, carry):
        base = pl.multiple_of(b * rb, rb)
        xh = xp_scr[pl.ds(base, rb + 2 * HALO), :]
        ext = rb + 2 * HALO
        acc = cb + cw[2:3, :] * xh[HALO:HALO + rb]
        acc = acc + cw[0:1, :] * pltpu.roll(xh, 2, axis=0)[HALO:HALO + rb]
        acc = acc + cw[1:2, :] * pltpu.roll(xh, 1, axis=0)[HALO:HALO + rb]
        acc = acc + cw[3:4, :] * pltpu.roll(xh, ext - 1, axis=0)[HALO:HALO + rb]
        xc_scr[pl.ds(base, rb), :] = acc
        return carry

    lax.fori_loop(0, nb, conv_blk, 0)

    rows = lax.broadcasted_iota(jnp.int32, (rb, w), 0)

    def gates(blk, d):
        r = pl.ds(pl.multiple_of(blk * rb, rb), rb)
        xc = xc_scr[r, :]
        xb = xc.astype(BF16)
        rg = jax.nn.sigmoid(jnp.dot(xb, wr_ref[d], preferred_element_type=F32) + br_ref[d:d + 1, :])
        ig = jax.nn.sigmoid(jnp.dot(xb, wi_ref[d], preferred_element_type=F32) + bi_ref[d:d + 1, :])
        log_a = (-LRU_C) * jax.nn.softplus(-lam_ref[d:d + 1, :]) * rg
        a = jnp.exp(log_a)
        u = jnp.sqrt(-jnp.tanh(log_a) * (a * a + 1.0)) * (ig * xc)
        return r, a, u

    def fwd_blk(blk, h_prev):
        r, a, u = gates(blk, 0)
        s = 1
        while s < rb:
            keep = rows >= s
            a_sh = jnp.where(keep, pltpu.roll(a, s, axis=0), 1.0)
            u_sh = jnp.where(keep, pltpu.roll(u, s, axis=0), 0.0)
            u = a * u_sh + u
            a = a * a_sh
            s *= 2
        h = u + a * h_prev
        h_dst[r, :] = h
        return h[rb - 1:rb, :]

    def bwd_blk(i, h_next):
        r, a, u = gates(nb - 1 - i, 1)
        s = 1
        while s < rb:
            keep = rows < rb - s
            a_sh = jnp.where(keep, pltpu.roll(a, rb - s, axis=0), 1.0)
            u_sh = jnp.where(keep, pltpu.roll(u, rb - s, axis=0), 0.0)
            u = a * u_sh + u
            a = a * a_sh
            s *= 2
        h = u + a * h_next
        h_dst[r, :] = h_dst[r, :] + h
        return h[0:1, :]

    e_ref[0, 0:1, :] = lax.fori_loop(0, nb, fwd_blk, h0_ref[0, 0:1, :])
    e_ref[0, 1:2, :] = lax.fori_loop(0, nb, bwd_blk, h0_ref[0, 1:2, :])
    if col_major:
        def to_row_major(c, carry):
            src = pl.ds(pl.multiple_of(c * grid_rows, 8), grid_rows)
            for j in range(slabs):
                st_scr[j, pl.ds(c, grid_rows, stride=GRID_W), :] = hcm_scr[src, j * LANES:(j + 1) * LANES]
            return carry

        lax.fori_loop(0, GRID_W, to_row_major, 0)
        for j in range(slabs):
            hr_ref[:, j * LANES:(j + 1) * LANES] = st_scr[j]


def _rglru(x, col_block, row_block0, n, seqs, col_major, conv_w, conv_b, wr_bd, wi_bd, b_r, b_i, lam, h0):
    w = LRU_W
    cm_scratch = [pltpu.VMEM((w // LANES, n, LANES), F32), pltpu.VMEM((n, w), F32)] if col_major else []
    full2 = lambda s: (0, 0)
    full3 = lambda s: (0, 0, 0)
    return pl.pallas_call(
        functools.partial(_lru_body, n=n, col_major=col_major),
        out_shape=(jax.ShapeDtypeStruct((seqs * n, w), F32), jax.ShapeDtypeStruct((seqs, 2, w), F32)),
        grid=(seqs,),
        in_specs=[pl.BlockSpec((n, w), lambda s: (row_block0 + s, col_block)),
                  pl.BlockSpec((CONV_W, w), full2), pl.BlockSpec((1, w), full2),
                  pl.BlockSpec((2, w, w), full3), pl.BlockSpec((2, w, w), full3),
                  pl.BlockSpec((2, w), full2), pl.BlockSpec((2, w), full2), pl.BlockSpec((2, w), full2),
                  pl.BlockSpec((1, 2, w), lambda s: (s, 0, 0))],
        out_specs=(pl.BlockSpec((n, w), lambda s: (s, 0)), pl.BlockSpec((1, 2, w), lambda s: (s, 0, 0))),
        scratch_shapes=[pltpu.VMEM((n + 2 * HALO, w), F32), pltpu.VMEM((n, w), F32)] + cm_scratch,
        compiler_params=_cparams("arbitrary"),
        name="rglru",
    )(x, conv_w, conv_b.reshape(1, w), wr_bd, wi_bd, b_r, b_i, lam, h0)


def _block_diag(wg):
    eye = jnp.eye(LRU_BLOCKS, dtype=wg.dtype)
    dense = wg[:, :, :, None, :] * eye[None, :, None, :, None]
    return dense.reshape(2, LRU_W, LRU_W).astype(BF16)


TM_OUT = 256


def _out_body(x_ref, o_ref, hr_ref, g_ref, xg_ref, ga0_ref, ga1_ref, gb0_ref, gb1_ref, m_ref,
              hgn_ref, n2_ref, wa_ref, wb_ref, wo_ref, wq_ref, x1_ref, h2_ref, q_ref):
    o = o_ref[...]
    parts = []
    for h in range(HG_HEADS):
        parts.append(_rms(o[:, h * HG_DV:(h + 1) * HG_DV]) * hgn_ref[...])
    on = jnp.concatenate(parts, axis=1) * _silu(g_ref[...])
    y_a = jnp.dot(on.astype(BF16), wa_ref[...], preferred_element_type=F32)
    y_b = jnp.dot((hr_ref[...] * jax.nn.gelu(xg_ref[...])).astype(BF16), wb_ref[...], preferred_element_type=F32)
    ga = jnp.concatenate([ga0_ref[...], ga1_ref[...]], axis=1)
    gb = jnp.concatenate([gb0_ref[...], gb1_ref[...]], axis=1)
    merged = jax.nn.sigmoid(ga) * y_a + jax.nn.sigmoid(gb) * y_b
    mix = jnp.dot(merged.astype(BF16), wo_ref[...], preferred_element_type=F32)
    x1 = x_ref[...] + m_ref[0, 2:3, :] * mix
    x1_ref[...] = x1
    h2 = _rms(x1) * n2_ref[...] * (1.0 + m_ref[0, 4:5, :]) + m_ref[0, 3:4, :]
    h2_ref[...] = h2
    q_ref[...] = jnp.dot(h2.astype(BF16), wq_ref[...], preferred_element_type=F32).astype(BF16)


def _out_proj(x, o_hg, hr, z, mod, hg_norm, norm2, wa, wb, wo, wq, row_of):
    t, d = x.shape
    tm = TM_OUT
    nq = PK_HEADS * PK_DQ
    half = lambda i: (i, 0)
    zc = lambda c: pl.BlockSpec((tm, LRU_W), lambda i: (i, c))
    const = lambda i: (0, 0)
    return pl.pallas_call(
        _out_body,
        out_shape=(jax.ShapeDtypeStruct((t, d), F32), jax.ShapeDtypeStruct((t, d), F32),
                   jax.ShapeDtypeStruct((t, nq), BF16)),
        grid=(t // tm,),
        in_specs=[pl.BlockSpec((tm, d), half), pl.BlockSpec((tm, HG_VW), half), pl.BlockSpec((tm, LRU_W), half),
                  zc(COL_G), zc(COL_XG), zc(COL_GA), zc(COL_GA + 1), zc(COL_GB), zc(COL_GB + 1),
                  pl.BlockSpec((1, N_MOD, d), lambda i: (row_of(i), 0, 0)),
                  pl.BlockSpec((1, HG_DV), const), pl.BlockSpec((1, d), const),
                  pl.BlockSpec((HG_VW, d), const), pl.BlockSpec((LRU_W, d), const),
                  pl.BlockSpec((d, d), const), pl.BlockSpec((d, nq), const)],
        out_specs=(pl.BlockSpec((tm, d), half), pl.BlockSpec((tm, d), half), pl.BlockSpec((tm, nq), half)),
        compiler_params=_cparams("arbitrary"),
        name="out_proj",
    )(x, o_hg, hr, z, z, z, z, z, z, mod, hg_norm.reshape(1, HG_DV), norm2.reshape(1, d), wa, wb, wo, wq)


TT_TOPK = 128


def _extract_topk(s, order, k, payload=None):
    vals, ords, picks = [], [], []
    for _ in range(k):
        m = jnp.max(s, axis=0, keepdims=True)
        o = jnp.max(jnp.where(s == m, order, -1.0), axis=0, keepdims=True)
        hit = order == o
        vals.append(m)
        ords.append(o)
        if payload is not None:
            picks.append(jnp.sum(jnp.where(hit, payload, 0), axis=0, keepdims=True))
        s = jnp.where(hit, -jnp.inf, s)
    cat = lambda xs: jnp.concatenate(xs, axis=0)
    return cat(vals), cat(ords), (cat(picks) if payload is not None else None)


PAIR_ROWS = 4
PAIR_COLS = 3
assert all((i + 1) * (j + 1) > PK_TOPK for i in range(PAIR_ROWS, PK_TOPK) for j in range(PAIR_COLS, PK_TOPK))
assert all((i + 1) * (8 + 1) > PK_TOPK for i in range(1, PAIR_ROWS)) and (8 + 1) * (1 + 1) > PK_TOPK


def _pair_candidates(sv, si, tt):
    cand, order, eid = [], [], []
    hi = float(PK_TOPK * PK_TOPK - 1)
    for i in range(PAIR_ROWS):
        nj = PK_TOPK if i == 0 else 8
        j = lax.broadcasted_iota(jnp.int32, (nj, tt), 0)
        cand.append(sv[0][i:i + 1, :] + sv[1][:nj])
        order.append(hi - (i * PK_TOPK + j).astype(F32))
        eid.append(si[0][i:i + 1, :] * N_KEYS + si[1][:nj])
    for j in range(PAIR_COLS):
        ni = PK_TOPK if j == 0 else 8
        i = lax.broadcasted_iota(jnp.int32, (ni, tt), 0)
        fresh = i >= PAIR_ROWS
        cand.append(jnp.where(fresh, sv[0][:ni] + sv[1][j:j + 1, :], -jnp.inf))
        order.append(jnp.where(fresh, hi - (i * PK_TOPK + j).astype(F32), -2.0))
        eid.append(si[0][:ni] * N_KEYS + si[1][j:j + 1, :])
    cat = lambda xs: jnp.concatenate(xs, axis=0)
    return cat(cand), cat(order), cat(eid)


def _topk_body(q_ref, keys_ref, eidx_ref, gate_ref):
    tt = q_ref.shape[0]
    key_order = (N_KEYS - 1 - lax.broadcasted_iota(jnp.int32, (N_KEYS, tt), 0)).astype(F32)
    e_rows, g_rows = [], []
    for h in range(PK_HEADS):
        sv, si = [], []
        for p in range(2):
            c0 = h * PK_DQ + p * PK_DH
            s = lax.dot_general(keys_ref[p, h], q_ref[:, c0:c0 + PK_DH], (((1,), (1,)), ((), ())),
                                preferred_element_type=F32)
            v, o, _ = _extract_topk(s, key_order, PK_TOPK)
            sv.append(v)
            si.append(N_KEYS - 1 - o.astype(jnp.int32))
        cand, order, cidx = _pair_candidates(sv, si, tt)
        best, _, eid = _extract_topk(cand, order, PK_TOPK, payload=cidx)
        ex = jnp.exp(best - best[0:1, :])
        g_rows.append(ex / jnp.sum(ex, axis=0, keepdims=True))
        e_rows.append(eid)
    eidx_ref[...] = jnp.concatenate(e_rows, axis=0).T
    gate_ref[...] = jnp.concatenate(g_rows, axis=0).T


def _pk_topk(q, keys_bf):
    t = q.shape[0]
    tt = TT_TOPK
    ne = PK_HEADS * PK_TOPK
    return pl.pallas_call(
        _topk_body,
        out_shape=(jax.ShapeDtypeStruct((t, ne), jnp.int32), jax.ShapeDtypeStruct((t, ne), F32)),
        grid=(t // tt,),
        in_specs=[pl.BlockSpec((tt, PK_HEADS * PK_DQ), lambda i: (i, 0)),
                  pl.BlockSpec((2, PK_HEADS, N_KEYS, PK_DH), lambda i: (0, 0, 0, 0))],
        out_specs=(pl.BlockSpec((tt, ne), lambda i: (i, 0)), pl.BlockSpec((tt, ne), lambda i: (i, 0))),
        compiler_params=_cparams("arbitrary"),
        name="pk_topk",
    )(q, keys_bf)


SC_LANES = 16
SC_CORES = 2
SC_SUBCORES = 16
SC_WORKERS = SC_CORES * SC_SUBCORES
SC_HALF = D_MODEL // 2
SC_ROW_CHUNKS = SC_HALF // SC_LANES
SC_TOKEN_BLOCK = 8


def _pack_bf16_pairs(tab):
    lo = lax.bitcast_convert_type(tab[:, SC_HALF:].astype(BF16), jnp.uint16).astype(jnp.uint32)
    bits = lax.bitcast_convert_type(tab[:, :SC_HALF], jnp.uint32)
    sign = bits & jnp.uint32(0x80000000)
    mag = (bits & jnp.uint32(0x7FFFFFFF)) + jnp.uint32(1 << 15)
    hi = jnp.where(mag >= lo, (mag - lo) >> 16, jnp.uint32(0))
    return lax.bitcast_convert_type(sign | (hi << 16) | lo, jnp.int32)


def _unpack_pair(w):
    return lax.bitcast_convert_type(w, F32), lax.bitcast_convert_type(w << 16, F32)


def _gelu_tanh(x):
    y = 0.7978845608028654 * (x + 0.044715 * (x * x * x))
    t = 1.0 - 2.0 / (jnp.exp(2.0 * y) + 1.0)
    return x * (0.5 * (1.0 + t))


def _peer_sc_body(h_hbm, idx_hbm, gate_hbm, u_hbm, v_hbm, out_hbm,
                  h_v, idx_v, gate_v, out_v, ub0, ub1, vb0, vb1, su0, su1, sv0, sv1, *, tokens_per_worker):
    nh, k, tb, lanes = PK_HEADS, PK_TOPK, SC_TOKEN_BLOCK, SC_LANES
    wid = lax.axis_index("s") * SC_CORES + lax.axis_index("c")
    ubufs, vbufs, sus, svs = (ub0, ub1), (vb0, vb1), (su0, su1), (sv0, sv1)
    lane = lax.iota(jnp.int32, lanes)
    zero = jnp.zeros((lanes,), F32)

    def start(tok, hd, par):
        irow = idx_v.at[tok * nh + hd]
        pltpu.async_copy(u_hbm.at[irow], ubufs[par], sus[par])
        pltpu.async_copy(v_hbm.at[irow], vbufs[par], svs[par])

    def wait(par):
        irow = idx_v.at[0]
        pltpu.make_async_copy(u_hbm.at[irow], ubufs[par], sus[par]).wait()
        pltpu.make_async_copy(v_hbm.at[irow], vbufs[par], svs[par]).wait()

    @pl.loop(0, tokens_per_worker // tb)
    def _(blk):
        base = wid * tokens_per_worker + blk * tb
        pltpu.sync_copy(h_hbm.at[pl.ds(base, tb)], h_v)
        pltpu.sync_copy(idx_hbm.at[pl.ds(base * nh, tb * nh)], idx_v)
        pltpu.sync_copy(gate_hbm.at[pl.ds(base * nh, tb * nh)], gate_v)
        start(0, 0, 0)

        @pl.loop(0, tb)
        def _(tok):
            for hd in range(nh):
                par = hd % 2
                if hd + 1 < nh:
                    start(tok, hd + 1, 1 - par)
                else:
                    @pl.when(tok + 1 < tb)
                    def _():
                        start(tok + 1, 0, 1 - par)
                wait(par)
                ub, vb = ubufs[par], vbufs[par]

                @plsc.parallel_loop(0, SC_ROW_CHUNKS, carry=(zero,) * k)
                def accs(c, acc):
                    ha = h_v[tok, pl.ds(c * lanes, lanes)]
                    hb = h_v[tok, pl.ds(SC_HALF + c * lanes, lanes)]
                    out = []
                    for r in range(k):
                        a, b = _unpack_pair(ub[r, pl.ds(c * lanes, lanes)])
                        out.append(acc[r] + (ha * a + hb * b))
                    return tuple(out)

                s_vec = zero
                for r in range(k):
                    s_vec = jnp.where(lane == r, jnp.sum(accs[r]), s_vec)
                w_vec = gate_v[tok * nh + hd, :] * _gelu_tanh(s_vec)
                wb = [jnp.sum(jnp.where(lane == r, w_vec, 0.0)) for r in range(k)]

                @plsc.parallel_loop(0, SC_ROW_CHUNKS)
                def _(c):
                    pa, pb = [], []
                    for r in range(k):
                        a, b = _unpack_pair(vb[r, pl.ds(c * lanes, lanes)])
                        pa.append(wb[r] * a)
                        pb.append(wb[r] * b)
                    while len(pa) > 1:
                        pa = [pa[i] + pa[i + 1] for i in range(0, len(pa), 2)]
                        pb = [pb[i] + pb[i + 1] for i in range(0, len(pb), 2)]
                    for off, o in ((0, pa[0]), (SC_HALF, pb[0])):
                        cols = pl.ds(off + c * lanes, lanes)
                        if hd == 0:
                            out_v[tok, cols] = o
                        else:
                            out_v[tok, cols] = out_v[tok, cols] + o

        pltpu.sync_copy(out_v, out_hbm.at[pl.ds(base, tb)])


def _peer_experts(h, eidx, gate, u_tab, v_tab):
    t, d = h.shape
    assert t % (SC_WORKERS * SC_TOKEN_BLOCK) == 0 and d == D_MODEL
    rows = pltpu.VMEM((PK_TOPK, SC_HALF), jnp.int32)
    return pl.kernel(
        functools.partial(_peer_sc_body, tokens_per_worker=t // SC_WORKERS),
        out_type=jax.ShapeDtypeStruct((t, d), F32),
        mesh=plsc.VectorSubcoreMesh(core_axis_name="c", subcore_axis_name="s"),
        scratch_types=[
            pltpu.VMEM((SC_TOKEN_BLOCK, d), F32),
            pltpu.VMEM((SC_TOKEN_BLOCK * PK_HEADS, PK_TOPK), jnp.int32),
            pltpu.VMEM((SC_TOKEN_BLOCK * PK_HEADS, PK_TOPK), F32),
            pltpu.VMEM((SC_TOKEN_BLOCK, d), F32),
            rows, rows, rows, rows,
            pltpu.SemaphoreType.DMA, pltpu.SemaphoreType.DMA,
            pltpu.SemaphoreType.DMA, pltpu.SemaphoreType.DMA,
        ],
        compiler_params=pltpu.CompilerParams(needs_layout_passes=False),
        name="peer_experts_sc",
    )(h, eidx, gate, u_tab, v_tab)


TT_DENSE = 256
EB_DENSE = 2048
N_EXPERTS = N_KEYS * N_KEYS
N_SEL = PK_HEADS * PK_TOPK


def _dense_body(h_ref, e_ref, g_ref, u_ref, v_ref, o_ref, g_scr, acc_scr):
    eb = pl.program_id(1)
    tt = h_ref.shape[0]
    slabs = EB_DENSE // N_KEYS

    @pl.when(eb == 0)
    def _():
        acc_scr[...] = jnp.zeros_like(acc_scr)
        sub = lax.broadcasted_iota(jnp.int32, (N_KEYS, N_SEL), 0)

        def per_token(t, carry):
            e = e_ref[pl.ds(t, 1), :]
            p_t = jnp.where(sub == (e >> 7), g_ref[pl.ds(t, 1), :], 0.0).astype(BF16)
            q_t = (sub == (e & (N_KEYS - 1))).astype(BF16)
            g_scr[pl.ds(pl.multiple_of(t * N_KEYS, N_KEYS), N_KEYS), :] = lax.dot_general(
                p_t, q_t, (((1,), (1,)), ((), ())), preferred_element_type=F32)
            return carry

        lax.fori_loop(0, tt, per_token, 0, unroll=8)

    s = lax.dot_general(h_ref[...].astype(BF16), u_ref[...], (((1,), (1,)), ((), ())), preferred_element_type=F32)
    gs = jnp.concatenate([g_scr[pl.ds(eb * slabs + k, tt, stride=N_KEYS), :] for k in range(slabs)], axis=1)
    acc_scr[...] += jnp.dot((gs * jax.nn.gelu(s)).astype(BF16), v_ref[...], preferred_element_type=F32)

    @pl.when(eb == pl.num_programs(1) - 1)
    def _():
        o_ref[...] = acc_scr[...]


def _peer_dense(h, eidx, gate, u_bf, v_bf):
    t, d = h.shape
    tt = TT_DENSE
    assert t % tt == 0 and N_KEYS == 128
    tile = lambda i, e: (i, 0)
    blk = lambda i, e: (e, 0)
    return pl.pallas_call(
        _dense_body,
        out_shape=jax.ShapeDtypeStruct((t, d), F32),
        grid=(t // tt, N_EXPERTS // EB_DENSE),
        in_specs=[pl.BlockSpec((tt, d), tile), pl.BlockSpec((tt, N_SEL), tile), pl.BlockSpec((tt, N_SEL), tile),
                  pl.BlockSpec((EB_DENSE, d), blk), pl.BlockSpec((EB_DENSE, d), blk)],
        out_specs=pl.BlockSpec((tt, d), tile),
        scratch_shapes=[pltpu.VMEM((tt * N_KEYS, N_KEYS), F32), pltpu.VMEM((tt, d), F32)],
        compiler_params=_cparams("arbitrary", "arbitrary"),
        name="peer_dense_tc",
    )(h, eidx, gate, u_bf, v_bf)


TM_FIN = 512


def _final_body(x1_ref, p_ref, m_ref, w_ref, y_ref):
    y_ref[...] = _rms(x1_ref[...] + m_ref[0, 5:6, :] * p_ref[...]) * w_ref[...]


def _final(x1, peer, mod, norm_f, rows_of):
    t, d = x1.shape
    tm = _token_tile(t, TM_FIN)
    row_of = rows_of(tm)
    tile = pl.BlockSpec((tm, d), lambda i: (i, 0))
    return pl.pallas_call(
        _final_body,
        out_shape=jax.ShapeDtypeStruct((t, d), F32),
        grid=(t // tm,),
        in_specs=[tile, tile, pl.BlockSpec((1, N_MOD, d), lambda i: (row_of(i), 0, 0)),
                  pl.BlockSpec((1, d), lambda i: (0, 0))],
        out_specs=tile,
        compiler_params=_cparams("arbitrary"),
        name="final_norm",
    )(x1, peer, mod, norm_f.reshape(1, d))


CTX_GROUPS = (2, 14, 16)
LAT_GROUP = 2
DENSE_TOKENS = 42 * 256


def _zero_of(a):
    return (lax.shift_right_logical(a[0, 0], 31) >> 1).astype(F32)


def _group(x, mod, row0, per_seq, n, hg_s0, lru_s0, col_major, p, after, dense_tokens=0):
    seqs = hg_s0.shape[0]
    t = seqs * n
    for a in after:
        mod = mod + _zero_of(a)
    rows_of = lambda tm: _mod_row_map(tm, n, row0, per_seq)
    z = _in_proj(x, mod, p['norm1'], p['w_in'], rows_of)
    o_hg, hg_fin = _hgrn2(z, p['lb'], hg_s0, n, 0)
    hr, lru_fin = _rglru(z, COL_XR, 0, n, seqs, col_major, *p['lru'], lru_s0)
    x1, h2, q = _out_proj(x, o_hg, hr, z, mod, p['hg_norm'], p['norm2'], p['w_a'], p['w_b'], p['w_o'], p['w_q'],
                          rows_of(TM_OUT))
    eidx, gate = _pk_topk(q, p['keys'])
    ts = t - dense_tokens
    parts = []
    if ts:
        parts.append(_peer_experts(h2[:ts], eidx[:ts].reshape(ts * PK_HEADS, PK_TOPK),
                                   gate[:ts].reshape(ts * PK_HEADS, PK_TOPK), p['pk_u'], p['pk_v']))
    if dense_tokens:
        parts.append(_peer_dense(h2[ts:], eidx[ts:], gate[ts:], p['u_bf'], p['v_bf']))
    peer = parts[0] if len(parts) == 1 else jnp.concatenate(parts, axis=0)
    return _final(x1, peer, mod, p['norm_f'], rows_of), hg_fin, lru_fin, eidx


def kernel(x_prompt, x_sample, state_hgrn, state_rglru, c, c_ctx, w_mod, b_mod, norm1, w_in,
           hg_lb_logits, hg_norm, lru_conv_w, lru_conv_b, lru_w_r, lru_b_r, lru_w_i, lru_b_i,
           lru_lambda, w_branch_a, w_branch_b, w_out, norm2, pk_w_q, pk_sub_keys, pk_u, pk_v,
           norm_f):
    assert w_mod.shape[0] == 1, "single trunk layer"
    d = D_MODEL
    nb_ctx, n_ctx, _ = x_prompt.shape
    nb_lat, n_lat, _ = x_sample.shape
    assert nb_lat < MOD_ROWS and nb_lat % LAT_GROUP == 0 and nb_ctx == sum(CTX_GROUPS)
    ctx_row = nb_lat

    cond = jnp.zeros((MOD_ROWS, d), F32).at[:nb_lat].set(c).at[ctx_row].set(c_ctx)
    mod = _modulation(cond, w_mod[0], b_mod[0])
    p = {
        'lb': jnp.cumsum(jax.nn.softmax(hg_lb_logits.astype(F32), axis=1), axis=1)[:, 0],
        'norm1': norm1[0], 'w_in': w_in[0].astype(BF16), 'hg_norm': hg_norm[0], 'norm2': norm2[0],
        'lru': (lru_conv_w[0], lru_conv_b[0], _block_diag(lru_w_r[0]), _block_diag(lru_w_i[0]),
                lru_b_r[0], lru_b_i[0], lru_lambda[0]),
        'w_a': w_branch_a[0].astype(BF16), 'w_b': w_branch_b[0].astype(BF16), 'w_o': w_out[0].astype(BF16),
        'w_q': pk_w_q[0].astype(BF16), 'keys': pk_sub_keys[0].astype(BF16),
        'pk_u': _pack_bf16_pairs(pk_u[0]), 'pk_v': _pack_bf16_pairs(pk_v[0]), 'norm_f': norm_f,
        'u_bf': pk_u[0].astype(BF16), 'v_bf': pk_v[0].astype(BF16),
    }
    y_ctx, hg_fin, lru_fin = [], [], []
    tables = [p['pk_u'], p['pk_v']]
    after = tables
    s0 = 0
    for g in CTX_GROUPS:
        y, hg, lr, eidx = _group(
            x_prompt[s0:s0 + g].reshape(g * n_ctx, d), mod, ctx_row, False, n_ctx,
            jnp.zeros((g, 2, HG_HEADS, HG_DK, HG_DV), F32), jnp.zeros((g, 2, LRU_W), F32), False, p, after)
        s0 += g
        after = [eidx] if after is tables else after
        y_ctx.append(y.reshape(g, n_ctx, d))
        hg_fin.append(hg)
        lru_fin.append(lr)
    y_lat = []
    tg = LAT_GROUP * n_lat
    for s0 in range(0, nb_lat, LAT_GROUP):
        sl = slice(s0, s0 + LAT_GROUP)
        tokens_after = (nb_lat - s0 - LAT_GROUP) * n_lat
        y, _, _, _ = _group(x_sample[sl].reshape(tg, d), mod, s0, True, n_lat,
                            state_hgrn[sl, 0], state_rglru[sl, 0], True, p, after,
                            dense_tokens=min(tg, max(0, DENSE_TOKENS - tokens_after)))
        y_lat.append(y.reshape(LAT_GROUP, n_lat, d))
    return (jnp.concatenate(y_ctx, axis=0), jnp.concatenate(y_lat, axis=0),
            jnp.concatenate(hg_fin, axis=0)[:, None], jnp.concatenate(lru_fin, axis=0)[:, None])
```

```python
import functools

import jax, jax.numpy as jnp
from jax import lax
from jax.experimental import pallas as pl
from jax.experimental.pallas import tpu as pltpu
from jax.experimental.pallas import tpu_sc as plsc

D_MODEL = 1024
GRID_W = 64
EPS = 1e-6
HG_HEADS = 4
HG_DK = 128
HG_DV = 128
HG_KW = HG_HEADS * HG_DK
HG_VW = HG_HEADS * HG_DV
HG_CHUNK = 32
LRU_W = D_MODEL // 2
LRU_BLOCKS = 8
LRU_BW = LRU_W // LRU_BLOCKS
CONV_W = 4
LRU_C = 8.0
PK_HEADS = 8
N_KEYS = 128
PK_TOPK = 16
PK_DQ = 256
PK_DH = PK_DQ // 2
IN_W = 3 * HG_KW + 2 * HG_VW + 2 * LRU_W + 2 * D_MODEL
COL_G, COL_XR, COL_XG, COL_GA, COL_GB = 4, 5, 6, 7, 9
N_MOD = 6
MOD_ROWS = 16
F32 = jnp.float32
BF16 = jnp.bfloat16
VMEM_LIMIT = 48 * 1024 * 1024


def _cparams(*sem):
    return pltpu.CompilerParams(dimension_semantics=sem, vmem_limit_bytes=VMEM_LIMIT)


def _silu(x):
    return x * jax.nn.sigmoid(x)


def _rms(x):
    return x * lax.rsqrt(jnp.mean(x * x, axis=-1, keepdims=True) + EPS)


def _mod_body(c_ref, w_ref, b_ref, o_ref):
    o_ref[...] = lax.dot_general(_silu(c_ref[...]), w_ref[...], (((1,), (0,)), ((), ())),
                                 precision=lax.Precision.HIGHEST, preferred_element_type=F32) + b_ref[...]


def _modulation(cond, w_mod, b_mod):
    d = D_MODEL
    out = pl.pallas_call(
        _mod_body,
        out_shape=jax.ShapeDtypeStruct((MOD_ROWS, N_MOD * d), F32),
        grid=(N_MOD,),
        in_specs=[pl.BlockSpec((MOD_ROWS, d), lambda j: (0, 0)),
                  pl.BlockSpec((d, d), lambda j: (0, j)),
                  pl.BlockSpec((1, d), lambda j: (0, j))],
        out_specs=pl.BlockSpec((MOD_ROWS, d), lambda j: (0, j)),
        compiler_params=_cparams("arbitrary"),
        name="adaln_modulation",
    )(cond, w_mod, b_mod.reshape(1, N_MOD * d))
    return out.reshape(MOD_ROWS, N_MOD, d)


def _mod_row_map(tm, n, row0, per_seq):
    per = n // tm
    return (lambda i: row0 + i // per) if per_seq else (lambda i: row0)


TM_IN = 512
TOKEN_TILE_MIN = 256
TN_IN = IN_W // 2


def _in_body(x_ref, m_ref, n1_ref, w_ref, z_ref):
    y = _rms(x_ref[...]) * n1_ref[...]
    h = (y * (1.0 + m_ref[0, 1:2, :]) + m_ref[0, 0:1, :]).astype(BF16)
    z_ref[...] = jnp.dot(h, w_ref[...], preferred_element_type=F32)


def _token_tile(t, preferred):
    tm = preferred if t % preferred == 0 else TOKEN_TILE_MIN
    assert t % tm == 0
    return tm


def _in_proj(x, mod, norm1, w_in_bf, rows_of):
    t, d = x.shape
    tm = _token_tile(t, TM_IN)
    row_of = rows_of(tm)
    return pl.pallas_call(
        _in_body,
        out_shape=jax.ShapeDtypeStruct((t, IN_W), F32),
        grid=(IN_W // TN_IN, t // tm),
        in_specs=[pl.BlockSpec((tm, d), lambda j, i: (i, 0)),
                  pl.BlockSpec((1, N_MOD, d), lambda j, i: (row_of(i), 0, 0)),
                  pl.BlockSpec((1, d), lambda j, i: (0, 0)),
                  pl.BlockSpec((d, TN_IN), lambda j, i: (0, j))],
        out_specs=pl.BlockSpec((tm, TN_IN), lambda j, i: (i, j)),
        compiler_params=_cparams("arbitrary", "arbitrary"),
        name="in_proj",
    )(x, mod, norm1.reshape(1, d), w_in_bf)


HG_LONG_SEQ = 4096


def _heads_per_step(n):
    return HG_HEADS if n < HG_LONG_SEQ else HG_HEADS // 2


def _hg_body(q_ref, ff_ref, fb_ref, v_ref, lb_ref, s0_ref, o_ref, sfin_ref, st_scr, ob_scr, *, n, hps):
    c = HG_CHUNK
    nc = n // c
    row = lax.broadcasted_iota(jnp.int32, (c, c), 0)
    col = lax.broadcasted_iota(jnp.int32, (c, c), 1)
    lower = row >= col
    tri = (lower.astype(F32), (row <= col).astype(F32))
    masks = (lower, row <= col)
    f_refs = (ff_ref, fb_ref)
    for d in range(2):
        for hh in range(hps):
            st_scr[d * hps + hh] = s0_ref[0, d, hh].T

    def step(i, carry):
        chains = [(d, hh) for d in range(2) for hh in range(hps)]
        st1 = []
        for d, hh in chains:
            ci = i if d == 0 else nc - 1 - i
            r = pl.ds(pl.multiple_of(ci * c, c), c)
            hc = slice(hh * HG_DK, (hh + 1) * HG_DK)
            lb = lb_ref[d:d + 1, hc]
            f = lb + (1.0 - lb) * jax.nn.sigmoid(f_refs[d][r, hc])
            lf = jnp.log(f)
            cum = lax.dot_general(tri[d], lf, (((1,), (0,)), ((), ())),
                                  precision=lax.Precision.HIGHEST, preferred_element_type=F32)
            st1.append((r, hc, 1.0 - f, lf, cum))
        st2 = []
        for (d, hh), (r, hc, k, lf, cum) in zip(chains, st1):
            q = _silu(q_ref[r, hc])
            v = v_ref[r, hc].astype(BF16)
            tot = jnp.sum(lf, axis=0, keepdims=True)
            q_dec = (q * jnp.exp(cum)).astype(BF16)
            k_inv = (k * jnp.exp(-cum)).astype(BF16)
            k_end = (k * jnp.exp(tot - cum)).astype(BF16)
            st = st_scr[d * hps + hh]
            att = lax.dot_general(q_dec, k_inv, (((1,), (1,)), ((), ())), preferred_element_type=F32)
            o_inter = lax.dot_general(q_dec, st.astype(BF16), (((1,), (1,)), ((), ())), preferred_element_type=F32)
            ds_t = lax.dot_general(v, k_end, (((0,), (0,)), ((), ())), preferred_element_type=F32)
            st_scr[d * hps + hh] = st * jnp.exp(tot) + ds_t
            st2.append((v, att, o_inter))
        for (d, hh), (r, hc, _, _, _), (v, att, o_inter) in zip(chains, st1, st2):
            att = jnp.where(masks[d], att, 0.0).astype(BF16)
            o = jnp.dot(att, v, preferred_element_type=F32) + o_inter
            if d == 0:
                o_ref[r, hc] = o
            else:
                ob_scr[r, hc] = o
        return carry

    lax.fori_loop(0, nc, step, 0)
    o_ref[...] = o_ref[...] + ob_scr[...]
    for d in range(2):
        for hh in range(hps):
            sfin_ref[0, d, hh] = st_scr[d * hps + hh].T


def _hgrn2(z, lb, s0, n, row_block0):
    seqs = s0.shape[0]
    hps = _heads_per_step(n)
    wb = hps * HG_DK
    nblk = HG_KW // wb

    def zspec(seg):
        return pl.BlockSpec((n, wb), lambda s, h: (row_block0 + s, seg * nblk + h))

    st_spec = pl.BlockSpec((1, 2, hps, HG_DK, HG_DV), lambda s, h: (s, 0, h, 0, 0))
    return pl.pallas_call(
        functools.partial(_hg_body, n=n, hps=hps),
        out_shape=(jax.ShapeDtypeStruct((seqs * n, HG_VW), F32),
                   jax.ShapeDtypeStruct((seqs, 2, HG_HEADS, HG_DK, HG_DV), F32)),
        grid=(seqs, HG_HEADS // hps),
        in_specs=[zspec(0), zspec(1), zspec(2), zspec(3),
                  pl.BlockSpec((2, wb), lambda s, h: (0, h)), st_spec],
        out_specs=(pl.BlockSpec((n, wb), lambda s, h: (s, h)), st_spec),
        scratch_shapes=[pltpu.VMEM((2 * hps, HG_DV, HG_DK), F32), pltpu.VMEM((n, wb), F32)],
        compiler_params=_cparams("arbitrary", "arbitrary"),
        name="hgrn2_scan",
    )(z, z, z, z, lb, s0)


LRU_RB = 64
HALO = 8


LANES = 128


def _lru_body(x_ref, cw_ref, cb_ref, wr_ref, wi_ref, br_ref, bi_ref, lam_ref, h0_ref, hr_ref, e_ref,
              xp_scr, xc_scr, *cm_scr, n, col_major):
    rb = LRU_RB
    nb = n // rb
    w = LRU_W
    grid_rows = n // GRID_W
    slabs = w // LANES
    zeros = jnp.zeros((HALO, w), F32)
    xp_scr[0:HALO, :] = zeros
    xp_scr[HALO + n:HALO + n + HALO, :] = zeros
    if col_major:
        st_scr, hcm_scr = cm_scr
        for j in range(slabs):
            st_scr[j] = x_ref[:, j * LANES:(j + 1) * LANES]

        def to_col_major(c, carry):
            dst = pl.ds(pl.multiple_of(HALO + c * grid_rows, 8), grid_rows)
            for j in range(slabs):
                xp_scr[dst, j * LANES:(j + 1) * LANES] = st_scr[j, pl.ds(c, grid_rows, stride=GRID_W), :]
            return carry

        lax.fori_loop(0, GRID_W, to_col_major, 0)
        h_dst = hcm_scr
    else:
        xp_scr[HALO:HALO + n, :] = x_ref[...]
        h_dst = hr_ref
    cw = cw_ref[...]
    cb = cb_ref[...]

    def conv_blk(b, carry):
        base = pl.multiple_of(b * rb, rb)
        xh = xp_scr[pl.ds(base, rb + 2 * HALO), :]
        ext = rb + 2 * HALO
        acc = cb + cw[2:3, :] * xh[HALO:HALO + rb]
        acc = acc + cw[0:1, :] * pltpu.roll(xh, 2, axis=0)[HALO:HALO + rb]
        acc = acc + cw[1:2, :] * pltpu.roll(xh, 1, axis=0)[HALO:HALO + rb]
        acc = acc + cw[3:4, :] * pltpu.roll(xh, ext - 1, axis=0)[HALO:HALO + rb]
        xc_scr[pl.ds(base, rb), :] = acc
        return carry

    lax.fori_loop(0, nb, conv_blk, 0)

    rows = lax.broadcasted_iota(jnp.int32, (rb, w), 0)

    def gates(blk, d):
        r = pl.ds(pl.multiple_of(blk * rb, rb), rb)
        xc = xc_scr[r, :]
        xb = xc.astype(BF16)
        rg = jax.nn.sigmoid(jnp.dot(xb, wr_ref[d], preferred_element_type=F32) + br_ref[d:d + 1, :])
        ig = jax.nn.sigmoid(jnp.dot(xb, wi_ref[d], preferred_element_type=F32) + bi_ref[d:d + 1, :])
        log_a = (-LRU_C) * jax.nn.softplus(-lam_ref[d:d + 1, :]) * rg
        a = jnp.exp(log_a)
        u = jnp.sqrt(-jnp.tanh(log_a) * (a * a + 1.0)) * (ig * xc)
        return r, a, u

    def fwd_blk(blk, h_prev):
        r, a, u = gates(blk, 0)
        s = 1
        while s < rb:
            keep = rows >= s
            a_sh = jnp.where(keep, pltpu.roll(a, s, axis=0), 1.0)
            u_sh = jnp.where(keep, pltpu.roll(u, s, axis=0), 0.0)
            u = a * u_sh + u
            a = a * a_sh
            s *= 2
        h = u + a * h_prev
        h_dst[r, :] = h
        return h[rb - 1:rb, :]

    def bwd_blk(i, h_next):
        r, a, u = gates(nb - 1 - i, 1)
        s = 1
        while s < rb:
            keep = rows < rb - s
            a_sh = jnp.where(keep, pltpu.roll(a, rb - s, axis=0), 1.0)
            u_sh = jnp.where(keep, pltpu.roll(u, rb - s, axis=0), 0.0)
            u = a * u_sh + u
            a = a * a_sh
            s *= 2
        h = u + a * h_next
        h_dst[r, :] = h_dst[r, :] + h
        return h[0:1, :]

    e_ref[0, 0:1, :] = lax.fori_loop(0, nb, fwd_blk, h0_ref[0, 0:1, :])
    e_ref[0, 1:2, :] = lax.fori_loop(0, nb, bwd_blk, h0_ref[0, 1:2, :])
    if col_major:
        def to_row_major(c, carry):
            src = pl.ds(pl.multiple_of(c * grid_rows, 8), grid_rows)
            for j in range(slabs):
                st_scr[j, pl.ds(c, grid_rows, stride=GRID_W), :] = hcm_scr[src, j * LANES:(j + 1) * LANES]
            return carry

        lax.fori_loop(0, GRID_W, to_row_major, 0)
        for j in range(slabs):
            hr_ref[:, j * LANES:(j + 1) * LANES] = st_scr[j]


def _rglru(x, col_block, row_block0, n, seqs, col_major, conv_w, conv_b, wr_bd, wi_bd, b_r, b_i, lam, h0):
    w = LRU_W
    cm_scratch = [pltpu.VMEM((w // LANES, n, LANES), F32), pltpu.VMEM((n, w), F32)] if col_major else []
    full2 = lambda s: (0, 0)
    full3 = lambda s: (0, 0, 0)
    return pl.pallas_call(
        functools.partial(_lru_body, n=n, col_major=col_major),
        out_shape=(jax.ShapeDtypeStruct((seqs * n, w), F32), jax.ShapeDtypeStruct((seqs, 2, w), F32)),
        grid=(seqs,),
        in_specs=[pl.BlockSpec((n, w), lambda s: (row_block0 + s, col_block)),
                  pl.BlockSpec((CONV_W, w), full2), pl.BlockSpec((1, w), full2),
                  pl.BlockSpec((2, w, w), full3), pl.BlockSpec((2, w, w), full3),
                  pl.BlockSpec((2, w), full2), pl.BlockSpec((2, w), full2), pl.BlockSpec((2, w), full2),
                  pl.BlockSpec((1, 2, w), lambda s: (s, 0, 0))],
        out_specs=(pl.BlockSpec((n, w), lambda s: (s, 0)), pl.BlockSpec((1, 2, w), lambda s: (s, 0, 0))),
        scratch_shapes=[pltpu.VMEM((n + 2 * HALO, w), F32), pltpu.VMEM((n, w), F32)] + cm_scratch,
        compiler_params=_cparams("arbitrary"),
        name="rglru",
    )(x, conv_w, conv_b.reshape(1, w), wr_bd, wi_bd, b_r, b_i, lam, h0)


def _block_diag(wg):
    eye = jnp.eye(LRU_BLOCKS, dtype=wg.dtype)
    dense = wg[:, :, :, None, :] * eye[None, :, None, :, None]
    return dense.reshape(2, LRU_W, LRU_W).astype(BF16)


TM_OUT = 256


def _out_body(x_ref, o_ref, hr_ref, g_ref, xg_ref, ga0_ref, ga1_ref, gb0_ref, gb1_ref, m_ref,
              hgn_ref, n2_ref, wa_ref, wb_ref, wo_ref, wq_ref, x1_ref, h2_ref, q_ref):
    o = o_ref[...]
    parts = []
    for h in range(HG_HEADS):
        parts.append(_rms(o[:, h * HG_DV:(h + 1) * HG_DV]) * hgn_ref[...])
    on = jnp.concatenate(parts, axis=1) * _silu(g_ref[...])
    y_a = jnp.dot(on.astype(BF16), wa_ref[...], preferred_element_type=F32)
    y_b = jnp.dot((hr_ref[...] * jax.nn.gelu(xg_ref[...])).astype(BF16), wb_ref[...], preferred_element_type=F32)
    ga = jnp.concatenate([ga0_ref[...], ga1_ref[...]], axis=1)
    gb = jnp.concatenate([gb0_ref[...], gb1_ref[...]], axis=1)
    merged = jax.nn.sigmoid(ga) * y_a + jax.nn.sigmoid(gb) * y_b
    mix = jnp.dot(merged.astype(BF16), wo_ref[...], preferred_element_type=F32)
    x1 = x_ref[...] + m_ref[0, 2:3, :] * mix
    x1_ref[...] = x1
    h2 = _rms(x1) * n2_ref[...] * (1.0 + m_ref[0, 4:5, :]) + m_ref[0, 3:4, :]
    h2_ref[...] = h2
    q_ref[...] = jnp.dot(h2.astype(BF16), wq_ref[...], preferred_element_type=F32).astype(BF16)


def _out_proj(x, o_hg, hr, z, mod, hg_norm, norm2, wa, wb, wo, wq, row_of):
    t, d = x.shape
    tm = TM_OUT
    nq = PK_HEADS * PK_DQ
    half = lambda i: (i, 0)
    zc = lambda c: pl.BlockSpec((tm, LRU_W), lambda i: (i, c))
    const = lambda i: (0, 0)
    return pl.pallas_call(
        _out_body,
        out_shape=(jax.ShapeDtypeStruct((t, d), F32), jax.ShapeDtypeStruct((t, d), F32),
                   jax.ShapeDtypeStruct((t, nq), BF16)),
        grid=(t // tm,),
        in_specs=[pl.BlockSpec((tm, d), half), pl.BlockSpec((tm, HG_VW), half), pl.BlockSpec((tm, LRU_W), half),
                  zc(COL_G), zc(COL_XG), zc(COL_GA), zc(COL_GA + 1), zc(COL_GB), zc(COL_GB + 1),
                  pl.BlockSpec((1, N_MOD, d), lambda i: (row_of(i), 0, 0)),
                  pl.BlockSpec((1, HG_DV), const), pl.BlockSpec((1, d), const),
                  pl.BlockSpec((HG_VW, d), const), pl.BlockSpec((LRU_W, d), const),
                  pl.BlockSpec((d, d), const), pl.BlockSpec((d, nq), const)],
        out_specs=(pl.BlockSpec((tm, d), half), pl.BlockSpec((tm, d), half), pl.BlockSpec((tm, nq), half)),
        compiler_params=_cparams("arbitrary"),
        name="out_proj",
    )(x, o_hg, hr, z, z, z, z, z, z, mod, hg_norm.reshape(1, HG_DV), norm2.reshape(1, d), wa, wb, wo, wq)


TT_TOPK = 128


def _extract_topk(s, order, k, payload=None):
    vals, ords, picks = [], [], []
    for _ in range(k):
        m = jnp.max(s, axis=0, keepdims=True)
        o = jnp.max(jnp.where(s == m, order, -1.0), axis=0, keepdims=True)
        hit = order == o
        vals.append(m)
        ords.append(o)
        if payload is not None:
            picks.append(jnp.sum(jnp.where(hit, payload, 0), axis=0, keepdims=True))
        s = jnp.where(hit, -jnp.inf, s)
    cat = lambda xs: jnp.concatenate(xs, axis=0)
    return cat(vals), cat(ords), (cat(picks) if payload is not None else None)


PAIR_ROWS = 4
PAIR_COLS = 3
assert all((i + 1) * (j + 1) > PK_TOPK for i in range(PAIR_ROWS, PK_TOPK) for j in range(PAIR_COLS, PK_TOPK))
assert all((i + 1) * (8 + 1) > PK_TOPK for i in range(1, PAIR_ROWS)) and (8 + 1) * (1 + 1) > PK_TOPK


def _pair_candidates(sv, si, tt):
    cand, order, eid = [], [], []
    hi = float(PK_TOPK * PK_TOPK - 1)
    for i in range(PAIR_ROWS):
        nj = PK_TOPK if i == 0 else 8
        j = lax.broadcasted_iota(jnp.int32, (nj, tt), 0)
        cand.append(sv[0][i:i + 1, :] + sv[1][:nj])
        order.append(hi - (i * PK_TOPK + j).astype(F32))
        eid.append(si[0][i:i + 1, :] * N_KEYS + si[1][:nj])
    for j in range(PAIR_COLS):
        ni = PK_TOPK if j == 0 else 8
        i = lax.broadcasted_iota(jnp.int32, (ni, tt), 0)
        fresh = i >= PAIR_ROWS
        cand.append(jnp.where(fresh, sv[0][:ni] + sv[1][j:j + 1, :], -jnp.inf))
        order.append(jnp.where(fresh, hi - (i * PK_TOPK + j).astype(F32), -2.0))
        eid.append(si[0][:ni] * N_KEYS + si[1][j:j + 1, :])
    cat = lambda xs: jnp.concatenate(xs, axis=0)
    return cat(cand), cat(order), cat(eid)


def _topk_body(q_ref, keys_ref, eidx_ref, gate_ref):
    tt = q_ref.shape[0]
    key_order = (N_KEYS - 1 - lax.broadcasted_iota(jnp.int32, (N_KEYS, tt), 0)).astype(F32)
    e_rows, g_rows = [], []
    for h in range(PK_HEADS):
        sv, si = [], []
        for p in range(2):
            c0 = h * PK_DQ + p * PK_DH
            s = lax.dot_general(keys_ref[p, h], q_ref[:, c0:c0 + PK_DH], (((1,), (1,)), ((), ())),
                                preferred_element_type=F32)
            v, o, _ = _extract_topk(s, key_order, PK_TOPK)
            sv.append(v)
            si.append(N_KEYS - 1 - o.astype(jnp.int32))
        cand, order, cidx = _pair_candidates(sv, si, tt)
        best, _, eid = _extract_topk(cand, order, PK_TOPK, payload=cidx)
        ex = jnp.exp(best - best[0:1, :])
        g_rows.append(ex / jnp.sum(ex, axis=0, keepdims=True))
        e_rows.append(eid)
    eidx_ref[...] = jnp.concatenate(e_rows, axis=0).T
    gate_ref[...] = jnp.concatenate(g_rows, axis=0).T


def _pk_topk(q, keys_bf):
    t = q.shape[0]
    tt = TT_TOPK
    ne = PK_HEADS * PK_TOPK
    return pl.pallas_call(
        _topk_body,
        out_shape=(jax.ShapeDtypeStruct((t, ne), jnp.int32), jax.ShapeDtypeStruct((t, ne), F32)),
        grid=(t // tt,),
        in_specs=[pl.BlockSpec((tt, PK_HEADS * PK_DQ), lambda i: (i, 0)),
                  pl.BlockSpec((2, PK_HEADS, N_KEYS, PK_DH), lambda i: (0, 0, 0, 0))],
        out_specs=(pl.BlockSpec((tt, ne), lambda i: (i, 0)), pl.BlockSpec((tt, ne), lambda i: (i, 0))),
        compiler_params=_cparams("arbitrary"),
        name="pk_topk",
    )(q, keys_bf)


SC_LANES = 16
SC_CORES = 2
SC_SUBCORES = 16
SC_WORKERS = SC_CORES * SC_SUBCORES
SC_HALF = D_MODEL // 2
SC_ROW_CHUNKS = SC_HALF // SC_LANES
SC_TOKEN_BLOCK = 8


def _pack_bf16_pairs(tab):
    lo = lax.bitcast_convert_type(tab[:, SC_HALF:].astype(BF16), jnp.uint16).astype(jnp.uint32)
    bits = lax.bitcast_convert_type(tab[:, :SC_HALF], jnp.uint32)
    sign = bits & jnp.uint32(0x80000000)
    mag = (bits & jnp.uint32(0x7FFFFFFF)) + jnp.uint32(1 << 15)
    hi = jnp.where(mag >= lo, (mag - lo) >> 16, jnp.uint32(0))
    return lax.bitcast_convert_type(sign | (hi << 16) | lo, jnp.int32)


def _unpack_pair(w):
    return lax.bitcast_convert_type(w, F32), lax.bitcast_convert_type(w << 16, F32)


def _gelu_tanh(x):
    y = 0.7978845608028654 * (x + 0.044715 * (x * x * x))
    t = 1.0 - 2.0 / (jnp.exp(2.0 * y) + 1.0)
    return x * (0.5 * (1.0 + t))


def _peer_sc_body(h_hbm, idx_hbm, gate_hbm, u_hbm, v_hbm, out_hbm,
                  h_v, idx_v, gate_v, out_v, ub0, ub1, vb0, vb1, su0, su1, sv0, sv1, *, tokens_per_worker):
    nh, k, tb, lanes = PK_HEADS, PK_TOPK, SC_TOKEN_BLOCK, SC_LANES
    wid = lax.axis_index("s") * SC_CORES + lax.axis_index("c")
    ubufs, vbufs, sus, svs = (ub0, ub1), (vb0, vb1), (su0, su1), (sv0, sv1)
    lane = lax.iota(jnp.int32, lanes)
    zero = jnp.zeros((lanes,), F32)

    def start(tok, hd, par):
        irow = idx_v.at[tok * nh + hd]
        pltpu.async_copy(u_hbm.at[irow], ubufs[par], sus[par])
        pltpu.async_copy(v_hbm.at[irow], vbufs[par], svs[par])

    def wait(par):
        irow = idx_v.at[0]
        pltpu.make_async_copy(u_hbm.at[irow], ubufs[par], sus[par]).wait()
        pltpu.make_async_copy(v_hbm.at[irow], vbufs[par], svs[par]).wait()

    @pl.loop(0, tokens_per_worker // tb)
    def _(blk):
        base = wid * tokens_per_worker + blk * tb
        pltpu.sync_copy(h_hbm.at[pl.ds(base, tb)], h_v)
        pltpu.sync_copy(idx_hbm.at[pl.ds(base * nh, tb * nh)], idx_v)
        pltpu.sync_copy(gate_hbm.at[pl.ds(base * nh, tb * nh)], gate_v)
        start(0, 0, 0)

        @pl.loop(0, tb)
        def _(tok):
            for hd in range(nh):
                par = hd % 2
                if hd + 1 < nh:
                    start(tok, hd + 1, 1 - par)
                else:
                    @pl.when(tok + 1 < tb)
                    def _():
                        start(tok + 1, 0, 1 - par)
                wait(par)
                ub, vb = ubufs[par], vbufs[par]

                @plsc.parallel_loop(0, SC_ROW_CHUNKS, carry=(zero,) * k)
                def accs(c, acc):
                    ha = h_v[tok, pl.ds(c * lanes, lanes)]
                    hb = h_v[tok, pl.ds(SC_HALF + c * lanes, lanes)]
                    out = []
                    for r in range(k):
                        a, b = _unpack_pair(ub[r, pl.ds(c * lanes, lanes)])
                        out.append(acc[r] + (ha * a + hb * b))
                    return tuple(out)

                s_vec = zero
                for r in range(k):
                    s_vec = jnp.where(lane == r, jnp.sum(accs[r]), s_vec)
                w_vec = gate_v[tok * nh + hd, :] * _gelu_tanh(s_vec)
                wb = [jnp.sum(jnp.where(lane == r, w_vec, 0.0)) for r in range(k)]

                @plsc.parallel_loop(0, SC_ROW_CHUNKS)
                def _(c):
                    pa, pb = [], []
                    for r in range(k):
                        a, b = _unpack_pair(vb[r, pl.ds(c * lanes, lanes)])
                        pa.append(wb[r] * a)
                        pb.append(wb[r] * b)
                    while len(pa) > 1:
                        pa = [pa[i] + pa[i + 1] for i in range(0, len(pa), 2)]
                        pb = [pb[i] + pb[i + 1] for i in range(0, len(pb), 2)]
                    for off, o in ((0, pa[0]), (SC_HALF, pb[0])):
                        cols = pl.ds(off + c * lanes, lanes)
                        if hd == 0:
                            out_v[tok, cols] = o
                        else:
                            out_v[tok, cols] = out_v[tok, cols] + o

        pltpu.sync_copy(out_v, out_hbm.at[pl.ds(base, tb)])


def _peer_experts(h, eidx, gate, u_tab, v_tab):
    t, d = h.shape
    assert t % (SC_WORKERS * SC_TOKEN_BLOCK) == 0 and d == D_MODEL
    rows = pltpu.VMEM((PK_TOPK, SC_HALF), jnp.int32)
    return pl.kernel(
        functools.partial(_peer_sc_body, tokens_per_worker=t // SC_WORKERS),
        out_type=jax.ShapeDtypeStruct((t, d), F32),
        mesh=plsc.VectorSubcoreMesh(core_axis_name="c", subcore_axis_name="s"),
        scratch_types=[
            pltpu.VMEM((SC_TOKEN_BLOCK, d), F32),
            pltpu.VMEM((SC_TOKEN_BLOCK * PK_HEADS, PK_TOPK), jnp.int32),
            pltpu.VMEM((SC_TOKEN_BLOCK * PK_HEADS, PK_TOPK), F32),
            pltpu.VMEM((SC_TOKEN_BLOCK, d), F32),
            rows, rows, rows, rows,
            pltpu.SemaphoreType.DMA, pltpu.SemaphoreType.DMA,
            pltpu.SemaphoreType.DMA, pltpu.SemaphoreType.DMA,
        ],
        compiler_params=pltpu.CompilerParams(needs_layout_passes=False),
        name="peer_experts_sc",
    )(h, eidx, gate, u_tab, v_tab)


TT_DENSE = 256
EB_DENSE = 2048
N_EXPERTS = N_KEYS * N_KEYS
N_SEL = PK_HEADS * PK_TOPK


def _dense_body(h_ref, e_ref, g_ref, u_ref, v_ref, o_ref, g_scr, acc_scr):
    eb = pl.program_id(1)
    tt = h_ref.shape[0]
    slabs = EB_DENSE // N_KEYS

    @pl.when(eb == 0)
    def _():
        acc_scr[...] = jnp.zeros_like(acc_scr)
        sub = lax.broadcasted_iota(jnp.int32, (N_KEYS, N_SEL), 0)

        def per_token(t, carry):
            e = e_ref[pl.ds(t, 1), :]
            p_t = jnp.where(sub == (e >> 7), g_ref[pl.ds(t, 1), :], 0.0).astype(BF16)
            q_t = (sub == (e & (N_KEYS - 1))).astype(BF16)
            g_scr[pl.ds(pl.multiple_of(t * N_KEYS, N_KEYS), N_KEYS), :] = lax.dot_general(
                p_t, q_t, (((1,), (1,)), ((), ())), preferred_element_type=F32)
            return carry

        lax.fori_loop(0, tt, per_token, 0, unroll=8)

    s = lax.dot_general(h_ref[...].astype(BF16), u_ref[...], (((1,), (1,)), ((), ())), preferred_element_type=F32)
    gs = jnp.concatenate([g_scr[pl.ds(eb * slabs + k, tt, stride=N_KEYS), :] for k in range(slabs)], axis=1)
    acc_scr[...] += jnp.dot((gs * jax.nn.gelu(s)).astype(BF16), v_ref[...], preferred_element_type=F32)

    @pl.when(eb == pl.num_programs(1) - 1)
    def _():
        o_ref[...] = acc_scr[...]


def _peer_dense(h, eidx, gate, u_bf, v_bf):
    t, d = h.shape
    tt = TT_DENSE
    assert t % tt == 0 and N_KEYS == 128
    tile = lambda i, e: (i, 0)
    blk = lambda i, e: (e, 0)
    return pl.pallas_call(
        _dense_body,
        out_shape=jax.ShapeDtypeStruct((t, d), F32),
        grid=(t // tt, N_EXPERTS // EB_DENSE),
        in_specs=[pl.BlockSpec((tt, d), tile), pl.BlockSpec((tt, N_SEL), tile), pl.BlockSpec((tt, N_SEL), tile),
                  pl.BlockSpec((EB_DENSE, d), blk), pl.BlockSpec((EB_DENSE, d), blk)],
        out_specs=pl.BlockSpec((tt, d), tile),
        scratch_shapes=[pltpu.VMEM((tt * N_KEYS, N_KEYS), F32), pltpu.VMEM((tt, d), F32)],
        compiler_params=_cparams("arbitrary", "arbitrary"),
        name="peer_dense_tc",
    )(h, eidx, gate, u_bf, v_bf)


TM_FIN = 512


def _final_body(x1_ref, p_ref, m_ref, w_ref, y_ref):
    y_ref[...] = _rms(x1_ref[...] + m_ref[0, 5:6, :] * p_ref[...]) * w_ref[...]


def _final(x1, peer, mod, norm_f, rows_of):
    t, d = x1.shape
    tm = _token_tile(t, TM_FIN)
    row_of = rows_of(tm)
    tile = pl.BlockSpec((tm, d), lambda i: (i, 0))
    return pl.pallas_call(
        _final_body,
        out_shape=jax.ShapeDtypeStruct((t, d), F32),
        grid=(t // tm,),
        in_specs=[tile, tile, pl.BlockSpec((1, N_MOD, d), lambda i: (row_of(i), 0, 0)),
                  pl.BlockSpec((1, d), lambda i: (0, 0))],
        out_specs=tile,
        compiler_params=_cparams("arbitrary"),
        name="final_norm",
    )(x1, peer, mod, norm_f.reshape(1, d))


CTX_GROUPS = (2, 14, 16)
LAT_GROUPS = (2, 2, 2, 2)
DENSE_TOKENS = 42 * 256


def _zero_of(a):
    return (lax.shift_right_logical(a[0, 0], 31) >> 1).astype(F32)


def _group(x, mod, row0, per_seq, n, hg_s0, lru_s0, col_major, p, after, dense_tokens=0):
    seqs = hg_s0.shape[0]
    t = seqs * n
    for a in after:
        mod = mod + _zero_of(a)
    rows_of = lambda tm: _mod_row_map(tm, n, row0, per_seq)
    z = _in_proj(x, mod, p['norm1'], p['w_in'], rows_of)
    o_hg, hg_fin = _hgrn2(z, p['lb'], hg_s0, n, 0)
    hr, lru_fin = _rglru(z, COL_XR, 0, n, seqs, col_major, *p['lru'], lru_s0)
    x1, h2, q = _out_proj(x, o_hg, hr, z, mod, p['hg_norm'], p['norm2'], p['w_a'], p['w_b'], p['w_o'], p['w_q'],
                          rows_of(TM_OUT))
    eidx, gate = _pk_topk(q, p['keys'])
    ts = t - dense_tokens
    parts = []
    if ts:
        parts.append(_peer_experts(h2[:ts], eidx[:ts].reshape(ts * PK_HEADS, PK_TOPK),
                                   gate[:ts].reshape(ts * PK_HEADS, PK_TOPK), p['pk_u'], p['pk_v']))
    if dense_tokens:
        parts.append(_peer_dense(h2[ts:], eidx[ts:], gate[ts:], p['u_bf'], p['v_bf']))
    peer = parts[0] if len(parts) == 1 else jnp.concatenate(parts, axis=0)
    return _final(x1, peer, mod, p['norm_f'], rows_of), hg_fin, lru_fin, eidx


def kernel(x_prompt, x_sample, state_hgrn, state_rglru, c, c_ctx, w_mod, b_mod, norm1, w_in,
           hg_lb_logits, hg_norm, lru_conv_w, lru_conv_b, lru_w_r, lru_b_r, lru_w_i, lru_b_i,
           lru_lambda, w_branch_a, w_branch_b, w_out, norm2, pk_w_q, pk_sub_keys, pk_u, pk_v,
           norm_f):
    assert w_mod.shape[0] == 1, "single trunk layer"
    d = D_MODEL
    nb_ctx, n_ctx, _ = x_prompt.shape
    nb_lat, n_lat, _ = x_sample.shape
    assert nb_lat < MOD_ROWS and nb_lat == sum(LAT_GROUPS) and nb_ctx == sum(CTX_GROUPS)
    ctx_row = nb_lat

    cond = jnp.zeros((MOD_ROWS, d), F32).at[:nb_lat].set(c).at[ctx_row].set(c_ctx)
    mod = _modulation(cond, w_mod[0], b_mod[0])
    p = {
        'lb': jnp.cumsum(jax.nn.softmax(hg_lb_logits.astype(F32), axis=1), axis=1)[:, 0],
        'norm1': norm1[0], 'w_in': w_in[0].astype(BF16), 'hg_norm': hg_norm[0], 'norm2': norm2[0],
        'lru': (lru_conv_w[0], lru_conv_b[0], _block_diag(lru_w_r[0]), _block_diag(lru_w_i[0]),
                lru_b_r[0], lru_b_i[0], lru_lambda[0]),
        'w_a': w_branch_a[0].astype(BF16), 'w_b': w_branch_b[0].astype(BF16), 'w_o': w_out[0].astype(BF16),
        'w_q': pk_w_q[0].astype(BF16), 'keys': pk_sub_keys[0].astype(BF16),
        'pk_u': _pack_bf16_pairs(pk_u[0]), 'pk_v': _pack_bf16_pairs(pk_v[0]), 'norm_f': norm_f,
        'u_bf': pk_u[0].astype(BF16), 'v_bf': pk_v[0].astype(BF16),
    }
    y_ctx, hg_fin, lru_fin = [], [], []
    tables = [p['pk_u'], p['pk_v']]
    after = tables
    s0 = 0
    for g in CTX_GROUPS:
        y, hg, lr, eidx = _group(
            x_prompt[s0:s0 + g].reshape(g * n_ctx, d), mod, ctx_row, False, n_ctx,
            jnp.zeros((g, 2, HG_HEADS, HG_DK, HG_DV), F32), jnp.zeros((g, 2, LRU_W), F32), False, p, after)
        s0 += g
        after = [eidx] if after is tables else after
        y_ctx.append(y.reshape(g, n_ctx, d))
        hg_fin.append(hg)
        lru_fin.append(lr)
    y_lat = []
    s0 = 0
    for g in LAT_GROUPS:
        sl = slice(s0, s0 + g)
        s0 += g
        tokens_after = (nb_lat - s0) * n_lat
        y, _, _, _ = _group(x_sample[sl].reshape(g * n_lat, d), mod, s0 - g, True, n_lat,
                            state_hgrn[sl, 0], state_rglru[sl, 0], True, p, after,
                            dense_tokens=min(g * n_lat, max(0, DENSE_TOKENS - tokens_after)))
        y_lat.append(y.reshape(g, n_lat, d))
    return (jnp.concatenate(y_ctx, axis=0), jnp.concatenate(y_lat, axis=0),
            jnp.concatenate(hg_fin, axis=0)[:, None], jnp.concatenate(lru_fin, axis=0)[:, None])
```

```python
import functools

import jax, jax.numpy as jnp
from jax import lax
from jax.experimental import pallas as pl
from jax.experimental.pallas import tpu as pltpu
from jax.experimental.pallas import tpu_sc as plsc

D_MODEL = 1024
GRID_W = 64
EPS = 1e-6
HG_HEADS = 4
HG_DK = 128
HG_DV = 128
HG_KW = HG_HEADS * HG_DK
HG_VW = HG_HEADS * HG_DV
HG_CHUNK = 32
LRU_W = D_MODEL // 2
LRU_BLOCKS = 8
LRU_BW = LRU_W // LRU_BLOCKS
CONV_W = 4
LRU_C = 8.0
PK_HEADS = 8
N_KEYS = 128
PK_TOPK = 16
PK_DQ = 256
PK_DH = PK_DQ // 2
IN_W = 3 * HG_KW + 2 * HG_VW + 2 * LRU_W + 2 * D_MODEL
COL_G, COL_XR, COL_XG, COL_GA, COL_GB = 4, 5, 6, 7, 9
N_MOD = 6
MOD_ROWS = 16
F32 = jnp.float32
BF16 = jnp.bfloat16
VMEM_LIMIT = 48 * 1024 * 1024


def _cparams(*sem):
    return pltpu.CompilerParams(dimension_semantics=sem, vmem_limit_bytes=VMEM_LIMIT)


def _silu(x):
    return x * jax.nn.sigmoid(x)


def _rms(x):
    return x * lax.rsqrt(jnp.mean(x * x, axis=-1, keepdims=True) + EPS)


def _mod_body(c_ref, w_ref, b_ref, o_ref):
    o_ref[...] = lax.dot_general(_silu(c_ref[...]), w_ref[...], (((1,), (0,)), ((), ())),
                                 precision=lax.Precision.HIGHEST, preferred_element_type=F32) + b_ref[...]


def _modulation(cond, w_mod, b_mod):
    d = D_MODEL
    out = pl.pallas_call(
        _mod_body,
        out_shape=jax.ShapeDtypeStruct((MOD_ROWS, N_MOD * d), F32),
        grid=(N_MOD,),
        in_specs=[pl.BlockSpec((MOD_ROWS, d), lambda j: (0, 0)),
                  pl.BlockSpec((d, d), lambda j: (0, j)),
                  pl.BlockSpec((1, d), lambda j: (0, j))],
        out_specs=pl.BlockSpec((MOD_ROWS, d), lambda j: (0, j)),
        compiler_params=_cparams("arbitrary"),
        name="adaln_modulation",
    )(cond, w_mod, b_mod.reshape(1, N_MOD * d))
    return out.reshape(MOD_ROWS, N_MOD, d)


def _mod_row_map(tm, n, row0, per_seq):
    per = n // tm
    return (lambda i: row0 + i // per) if per_seq else (lambda i: row0)


TM_IN = 512
TOKEN_TILE_MIN = 256
TN_IN = IN_W // 2


def _in_body(x_ref, m_ref, n1_ref, w_ref, z_ref):
    y = _rms(x_ref[...]) * n1_ref[...]
    h = (y * (1.0 + m_ref[0, 1:2, :]) + m_ref[0, 0:1, :]).astype(BF16)
    z_ref[...] = jnp.dot(h, w_ref[...], preferred_element_type=F32)


def _token_tile(t, preferred):
    tm = preferred if t % preferred == 0 else TOKEN_TILE_MIN
    assert t % tm == 0
    return tm


def _in_proj(x, mod, norm1, w_in_bf, rows_of):
    t, d = x.shape
    tm = _token_tile(t, TM_IN)
    row_of = rows_of(tm)
    return pl.pallas_call(
        _in_body,
        out_shape=jax.ShapeDtypeStruct((t, IN_W), F32),
        grid=(IN_W // TN_IN, t // tm),
        in_specs=[pl.BlockSpec((tm, d), lambda j, i: (i, 0)),
                  pl.BlockSpec((1, N_MOD, d), lambda j, i: (row_of(i), 0, 0)),
                  pl.BlockSpec((1, d), lambda j, i: (0, 0)),
                  pl.BlockSpec((d, TN_IN), lambda j, i: (0, j))],
        out_specs=pl.BlockSpec((tm, TN_IN), lambda j, i: (i, j)),
        compiler_params=_cparams("arbitrary", "arbitrary"),
        name="in_proj",
    )(x, mod, norm1.reshape(1, d), w_in_bf)


HG_LONG_SEQ = 4096


def _heads_per_step(n):
    return HG_HEADS if n < HG_LONG_SEQ else HG_HEADS // 2


def _hg_body(q_ref, ff_ref, fb_ref, v_ref, lb_ref, s0_ref, o_ref, sfin_ref, st_scr, ob_scr, *, n, hps):
    c = HG_CHUNK
    nc = n // c
    row = lax.broadcasted_iota(jnp.int32, (c, c), 0)
    col = lax.broadcasted_iota(jnp.int32, (c, c), 1)
    lower = row >= col
    tri = (lower.astype(F32), (row <= col).astype(F32))
    masks = (lower, row <= col)
    f_refs = (ff_ref, fb_ref)
    for d in range(2):
        for hh in range(hps):
            st_scr[d * hps + hh] = s0_ref[0, d, hh].T

    def step(i, carry):
        chains = [(d, hh) for d in range(2) for hh in range(hps)]
        st1 = []
        for d, hh in chains:
            ci = i if d == 0 else nc - 1 - i
            r = pl.ds(pl.multiple_of(ci * c, c), c)
            hc = slice(hh * HG_DK, (hh + 1) * HG_DK)
            lb = lb_ref[d:d + 1, hc]
            f = lb + (1.0 - lb) * jax.nn.sigmoid(f_refs[d][r, hc])
            lf = jnp.log(f)
            cum = lax.dot_general(tri[d], lf, (((1,), (0,)), ((), ())),
                                  precision=lax.Precision.HIGHEST, preferred_element_type=F32)
            st1.append((r, hc, 1.0 - f, lf, cum))
        st2 = []
        for (d, hh), (r, hc, k, lf, cum) in zip(chains, st1):
            q = _silu(q_ref[r, hc])
            v = v_ref[r, hc].astype(BF16)
            tot = jnp.sum(lf, axis=0, keepdims=True)
            q_dec = (q * jnp.exp(cum)).astype(BF16)
            k_inv = (k * jnp.exp(-cum)).astype(BF16)
            k_end = (k * jnp.exp(tot - cum)).astype(BF16)
            st = st_scr[d * hps + hh]
            att = lax.dot_general(q_dec, k_inv, (((1,), (1,)), ((), ())), preferred_element_type=F32)
            o_inter = lax.dot_general(q_dec, st.astype(BF16), (((1,), (1,)), ((), ())), preferred_element_type=F32)
            ds_t = lax.dot_general(v, k_end, (((0,), (0,)), ((), ())), preferred_element_type=F32)
            st_scr[d * hps + hh] = st * jnp.exp(tot) + ds_t
            st2.append((v, att, o_inter))
        for (d, hh), (r, hc, _, _, _), (v, att, o_inter) in zip(chains, st1, st2):
            att = jnp.where(masks[d], att, 0.0).astype(BF16)
            o = jnp.dot(att, v, preferred_element_type=F32) + o_inter
            if d == 0:
                o_ref[r, hc] = o
            else:
                ob_scr[r, hc] = o
        return carry

    lax.fori_loop(0, nc, step, 0)
    o_ref[...] = o_ref[...] + ob_scr[...]
    for d in range(2):
        for hh in range(hps):
            sfin_ref[0, d, hh] = st_scr[d * hps + hh].T


def _hgrn2(z, lb, s0, n, row_block0):
    seqs = s0.shape[0]
    hps = _heads_per_step(n)
    wb = hps * HG_DK
    nblk = HG_KW // wb

    def zspec(seg):
        return pl.BlockSpec((n, wb), lambda s, h: (row_block0 + s, seg * nblk + h))

    st_spec = pl.BlockSpec((1, 2, hps, HG_DK, HG_DV), lambda s, h: (s, 0, h, 0, 0))
    return pl.pallas_call(
        functools.partial(_hg_body, n=n, hps=hps),
        out_shape=(jax.ShapeDtypeStruct((seqs * n, HG_VW), F32),
                   jax.ShapeDtypeStruct((seqs, 2, HG_HEADS, HG_DK, HG_DV), F32)),
        grid=(seqs, HG_HEADS // hps),
        in_specs=[zspec(0), zspec(1), zspec(2), zspec(3),
                  pl.BlockSpec((2, wb), lambda s, h: (0, h)), st_spec],
        out_specs=(pl.BlockSpec((n, wb), lambda s, h: (s, h)), st_spec),
        scratch_shapes=[pltpu.VMEM((2 * hps, HG_DV, HG_DK), F32), pltpu.VMEM((n, wb), F32)],
        compiler_params=_cparams("arbitrary", "arbitrary"),
        name="hgrn2_scan",
    )(z, z, z, z, lb, s0)


LRU_RB = 64
HALO = 8


LANES = 128


def _lru_body(x_ref, cw_ref, cb_ref, wr_ref, wi_ref, br_ref, bi_ref, lam_ref, h0_ref, hr_ref, e_ref,
              xp_scr, xc_scr, *cm_scr, n, col_major):
    rb = LRU_RB
    nb = n // rb
    w = LRU_W
    grid_rows = n // GRID_W
    slabs = w // LANES
    zeros = jnp.zeros((HALO, w), F32)
    xp_scr[0:HALO, :] = zeros
    xp_scr[HALO + n:HALO + n + HALO, :] = zeros
    if col_major:
        st_scr, hcm_scr = cm_scr
        for j in range(slabs):
            st_scr[j] = x_ref[:, j * LANES:(j + 1) * LANES]

        def to_col_major(c, carry):
            dst = pl.ds(pl.multiple_of(HALO + c * grid_rows, 8), grid_rows)
            for j in range(slabs):
                xp_scr[dst, j * LANES:(j + 1) * LANES] = st_scr[j, pl.ds(c, grid_rows, stride=GRID_W), :]
            return carry

        lax.fori_loop(0, GRID_W, to_col_major, 0)
        h_dst = hcm_scr
    else:
        xp_scr[HALO:HALO + n, :] = x_ref[...]
        h_dst = hr_ref
    cw = cw_ref[...]
    cb = cb_ref[...]

    def conv_blk(b, carry):
        base = pl.multiple_of(b * rb, rb)
        xh = xp_scr[pl.ds(base, rb + 2 * HALO), :]
        ext = rb + 2 * HALO
        acc = cb + cw[2:3, :] * xh[HALO:HALO + rb]
        acc = acc + cw[0:1, :] * pltpu.roll(xh, 2, axis=0)[HALO:HALO + rb]
        acc = acc + cw[1:2, :] * pltpu.roll(xh, 1, axis=0)[HALO:HALO + rb]
        acc = acc + cw[3:4, :] * pltpu.roll(xh, ext - 1, axis=0)[HALO:HALO + rb]
        xc_scr[pl.ds(base, rb), :] = acc
        return carry

    lax.fori_loop(0, nb, conv_blk, 0)

    rows = lax.broadcasted_iota(jnp.int32, (rb, w), 0)

    def gates(blk, d):
        r = pl.ds(pl.multiple_of(blk * rb, rb), rb)
        xc = xc_scr[r, :]
        xb = xc.astype(BF16)
        rg = jax.nn.sigmoid(jnp.dot(xb, wr_ref[d], preferred_element_type=F32) + br_ref[d:d + 1, :])
        ig = jax.nn.sigmoid(jnp.dot(xb, wi_ref[d], preferred_element_type=F32) + bi_ref[d:d + 1, :])
        log_a = (-LRU_C) * jax.nn.softplus(-lam_ref[d:d + 1, :]) * rg
        a = jnp.exp(log_a)
        u = jnp.sqrt(-jnp.tanh(log_a) * (a * a + 1.0)) * (ig * xc)
        return r, a, u

    def fwd_blk(blk, h_prev):
        r, a, u = gates(blk, 0)
        s = 1
        while s < rb:
            keep = rows >= s
            a_sh = jnp.where(keep, pltpu.roll(a, s, axis=0), 1.0)
            u_sh = jnp.where(keep, pltpu.roll(u, s, axis=0), 0.0)
            u = a * u_sh + u
            a = a * a_sh
            s *= 2
        h = u + a * h_prev
        h_dst[r, :] = h
        return h[rb - 1:rb, :]

    def bwd_blk(i, h_next):
        r, a, u = gates(nb - 1 - i, 1)
        s = 1
        while s < rb:
            keep = rows < rb - s
            a_sh = jnp.where(keep, pltpu.roll(a, rb - s, axis=0), 1.0)
            u_sh = jnp.where(keep, pltpu.roll(u, rb - s, axis=0), 0.0)
            u = a * u_sh + u
            a = a * a_sh
            s *= 2
        h = u + a * h_next
        h_dst[r, :] = h_dst[r, :] + h
        return h[0:1, :]

    e_ref[0, 0:1, :] = lax.fori_loop(0, nb, fwd_blk, h0_ref[0, 0:1, :])
    e_ref[0, 1:2, :] = lax.fori_loop(0, nb, bwd_blk, h0_ref[0, 1:2, :])
    if col_major:
        def to_row_major(c, carry):
            src = pl.ds(pl.multiple_of(c * grid_rows, 8), grid_rows)
            for j in range(slabs):
                st_scr[j, pl.ds(c, grid_rows, stride=GRID_W), :] = hcm_scr[src, j * LANES:(j + 1) * LANES]
            return carry

        lax.fori_loop(0, GRID_W, to_row_major, 0)
        for j in range(slabs):
            hr_ref[:, j * LANES:(j + 1) * LANES] = st_scr[j]


def _rglru(x, col_block, row_block0, n, seqs, col_major, conv_w, conv_b, wr_bd, wi_bd, b_r, b_i, lam, h0):
    w = LRU_W
    cm_scratch = [pltpu.VMEM((w // LANES, n, LANES), F32), pltpu.VMEM((n, w), F32)] if col_major else []
    full2 = lambda s: (0, 0)
    full3 = lambda s: (0, 0, 0)
    return pl.pallas_call(
        functools.partial(_lru_body, n=n, col_major=col_major),
        out_shape=(jax.ShapeDtypeStruct((seqs * n, w), F32), jax.ShapeDtypeStruct((seqs, 2, w), F32)),
        grid=(seqs,),
        in_specs=[pl.BlockSpec((n, w), lambda s: (row_block0 + s, col_block)),
                  pl.BlockSpec((CONV_W, w), full2), pl.BlockSpec((1, w), full2),
                  pl.BlockSpec((2, w, w), full3), pl.BlockSpec((2, w, w), full3),
                  pl.BlockSpec((2, w), full2), pl.BlockSpec((2, w), full2), pl.BlockSpec((2, w), full2),
                  pl.BlockSpec((1, 2, w), lambda s: (s, 0, 0))],
        out_specs=(pl.BlockSpec((n, w), lambda s: (s, 0)), pl.BlockSpec((1, 2, w), lambda s: (s, 0, 0))),
        scratch_shapes=[pltpu.VMEM((n + 2 * HALO, w), F32), pltpu.VMEM((n, w), F32)] + cm_scratch,
        compiler_params=_cparams("arbitrary"),
        name="rglru",
    )(x, conv_w, conv_b.reshape(1, w), wr_bd, wi_bd, b_r, b_i, lam, h0)


def _block_diag(wg):
    eye = jnp.eye(LRU_BLOCKS, dtype=wg.dtype)
    dense = wg[:, :, :, None, :] * eye[None, :, None, :, None]
    return dense.reshape(2, LRU_W, LRU_W).astype(BF16)


TM_OUT = 256


def _out_body(x_ref, o_ref, hr_ref, g_ref, xg_ref, ga0_ref, ga1_ref, gb0_ref, gb1_ref, m_ref,
              hgn_ref, n2_ref, wa_ref, wb_ref, wo_ref, wq_ref, x1_ref, h2_ref, q_ref):
    o = o_ref[...]
    parts = []
    for h in range(HG_HEADS):
        parts.append(_rms(o[:, h * HG_DV:(h + 1) * HG_DV]) * hgn_ref[...])
    on = jnp.concatenate(parts, axis=1) * _silu(g_ref[...])
    y_a = jnp.dot(on.astype(BF16), wa_ref[...], preferred_element_type=F32)
    y_b = jnp.dot((hr_ref[...] * jax.nn.gelu(xg_ref[...])).astype(BF16), wb_ref[...], preferred_element_type=F32)
    ga = jnp.concatenate([ga0_ref[...], ga1_ref[...]], axis=1)
    gb = jnp.concatenate([gb0_ref[...], gb1_ref[...]], axis=1)
    merged = jax.nn.sigmoid(ga) * y_a + jax.nn.sigmoid(gb) * y_b
    mix = jnp.dot(merged.astype(BF16), wo_ref[...], preferred_element_type=F32)
    x1 = x_ref[...] + m_ref[0, 2:3, :] * mix
    x1_ref[...] = x1
    h2 = _rms(x1) * n2_ref[...] * (1.0 + m_ref[0, 4:5, :]) + m_ref[0, 3:4, :]
    h2_ref[...] = h2
    q_ref[...] = jnp.dot(h2.astype(BF16), wq_ref[...], preferred_element_type=F32).astype(BF16)


def _out_proj(x, o_hg, hr, z, mod, hg_norm, norm2, wa, wb, wo, wq, row_of):
    t, d = x.shape
    tm = TM_OUT
    nq = PK_HEADS * PK_DQ
    half = lambda i: (i, 0)
    zc = lambda c: pl.BlockSpec((tm, LRU_W), lambda i: (i, c))
    const = lambda i: (0, 0)
    return pl.pallas_call(
        _out_body,
        out_shape=(jax.ShapeDtypeStruct((t, d), F32), jax.ShapeDtypeStruct((t, d), F32),
                   jax.ShapeDtypeStruct((t, nq), BF16)),
        grid=(t // tm,),
        in_specs=[pl.BlockSpec((tm, d), half), pl.BlockSpec((tm, HG_VW), half), pl.BlockSpec((tm, LRU_W), half),
                  zc(COL_G), zc(COL_XG), zc(COL_GA), zc(COL_GA + 1), zc(COL_GB), zc(COL_GB + 1),
                  pl.BlockSpec((1, N_MOD, d), lambda i: (row_of(i), 0, 0)),
                  pl.BlockSpec((1, HG_DV), const), pl.BlockSpec((1, d), const),
                  pl.BlockSpec((HG_VW, d), const), pl.BlockSpec((LRU_W, d), const),
                  pl.BlockSpec((d, d), const), pl.BlockSpec((d, nq), const)],
        out_specs=(pl.BlockSpec((tm, d), half), pl.BlockSpec((tm, d), half), pl.BlockSpec((tm, nq), half)),
        compiler_params=_cparams("arbitrary"),
        name="out_proj",
    )(x, o_hg, hr, z, z, z, z, z, z, mod, hg_norm.reshape(1, HG_DV), norm2.reshape(1, d), wa, wb, wo, wq)


TT_TOPK = 256


def _extract_topk(s, order, k, payload=None):
    vals, ords, picks = [], [], []
    for _ in range(k):
        m = jnp.max(s, axis=0, keepdims=True)
        o = jnp.max(jnp.where(s == m, order, -1.0), axis=0, keepdims=True)
        hit = order == o
        vals.append(m)
        ords.append(o)
        if payload is not None:
            picks.append(jnp.sum(jnp.where(hit, payload, 0), axis=0, keepdims=True))
        s = jnp.where(hit, -jnp.inf, s)
    cat = lambda xs: jnp.concatenate(xs, axis=0)
    return cat(vals), cat(ords), (cat(picks) if payload is not None else None)


PAIR_ROWS = 4
PAIR_COLS = 3
assert all((i + 1) * (j + 1) > PK_TOPK for i in range(PAIR_ROWS, PK_TOPK) for j in range(PAIR_COLS, PK_TOPK))
assert all((i + 1) * (8 + 1) > PK_TOPK for i in range(1, PAIR_ROWS)) and (8 + 1) * (1 + 1) > PK_TOPK


def _pair_candidates(sv, si, tt):
    cand, order, eid = [], [], []
    hi = float(PK_TOPK * PK_TOPK - 1)
    for i in range(PAIR_ROWS):
        nj = PK_TOPK if i == 0 else 8
        j = lax.broadcasted_iota(jnp.int32, (nj, tt), 0)
        cand.append(sv[0][i:i + 1, :] + sv[1][:nj])
        order.append(hi - (i * PK_TOPK + j).astype(F32))
        eid.append(si[0][i:i + 1, :] * N_KEYS + si[1][:nj])
    for j in range(PAIR_COLS):
        ni = PK_TOPK if j == 0 else 8
        i = lax.broadcasted_iota(jnp.int32, (ni, tt), 0)
        fresh = i >= PAIR_ROWS
        cand.append(jnp.where(fresh, sv[0][:ni] + sv[1][j:j + 1, :], -jnp.inf))
        order.append(jnp.where(fresh, hi - (i * PK_TOPK + j).astype(F32), -2.0))
        eid.append(si[0][:ni] * N_KEYS + si[1][j:j + 1, :])
    cat = lambda xs: jnp.concatenate(xs, axis=0)
    return cat(cand), cat(order), cat(eid)


def _topk_body(q_ref, keys_ref, eidx_ref, gate_ref):
    tt = q_ref.shape[0]
    key_order = (N_KEYS - 1 - lax.broadcasted_iota(jnp.int32, (N_KEYS, tt), 0)).astype(F32)
    e_rows, g_rows = [], []
    for h in range(PK_HEADS):
        sv, si = [], []
        for p in range(2):
            c0 = h * PK_DQ + p * PK_DH
            s = lax.dot_general(keys_ref[p, h], q_ref[:, c0:c0 + PK_DH], (((1,), (1,)), ((), ())),
                                preferred_element_type=F32)
            v, o, _ = _extract_topk(s, key_order, PK_TOPK)
            sv.append(v)
            si.append(N_KEYS - 1 - o.astype(jnp.int32))
        cand, order, cidx = _pair_candidates(sv, si, tt)
        best, _, eid = _extract_topk(cand, order, PK_TOPK, payload=cidx)
        ex = jnp.exp(best - best[0:1, :])
        g_rows.append(ex / jnp.sum(ex, axis=0, keepdims=True))
        e_rows.append(eid)
    eidx_ref[...] = jnp.concatenate(e_rows, axis=0).T
    gate_ref[...] = jnp.concatenate(g_rows, axis=0).T


def _pk_topk(q, keys_bf):
    t = q.shape[0]
    tt = TT_TOPK
    ne = PK_HEADS * PK_TOPK
    return pl.pallas_call(
        _topk_body,
        out_shape=(jax.ShapeDtypeStruct((t, ne), jnp.int32), jax.ShapeDtypeStruct((t, ne), F32)),
        grid=(t // tt,),
        in_specs=[pl.BlockSpec((tt, PK_HEADS * PK_DQ), lambda i: (i, 0)),
                  pl.BlockSpec((2, PK_HEADS, N_KEYS, PK_DH), lambda i: (0, 0, 0, 0))],
        out_specs=(pl.BlockSpec((tt, ne), lambda i: (i, 0)), pl.BlockSpec((tt, ne), lambda i: (i, 0))),
        compiler_params=_cparams("arbitrary"),
        name="pk_topk",
    )(q, keys_bf)


SC_LANES = 16
SC_CORES = 2
SC_SUBCORES = 16
SC_WORKERS = SC_CORES * SC_SUBCORES
SC_HALF = D_MODEL // 2
SC_ROW_CHUNKS = SC_HALF // SC_LANES
SC_TOKEN_BLOCK = 8


def _pack_bf16_pairs(tab):
    lo = lax.bitcast_convert_type(tab[:, SC_HALF:].astype(BF16), jnp.uint16).astype(jnp.uint32)
    bits = lax.bitcast_convert_type(tab[:, :SC_HALF], jnp.uint32)
    sign = bits & jnp.uint32(0x80000000)
    mag = (bits & jnp.uint32(0x7FFFFFFF)) + jnp.uint32(1 << 15)
    hi = jnp.where(mag >= lo, (mag - lo) >> 16, jnp.uint32(0))
    return lax.bitcast_convert_type(sign | (hi << 16) | lo, jnp.int32)


def _unpack_pair(w):
    return lax.bitcast_convert_type(w, F32), lax.bitcast_convert_type(w << 16, F32)


def _gelu_tanh(x):
    y = 0.7978845608028654 * (x + 0.044715 * (x * x * x))
    t = 1.0 - 2.0 / (jnp.exp(2.0 * y) + 1.0)
    return x * (0.5 * (1.0 + t))


def _peer_sc_body(h_hbm, idx_hbm, gate_hbm, u_hbm, v_hbm, out_hbm,
                  h_v, idx_v, gate_v, out_v, ub0, ub1, vb0, vb1, su0, su1, sv0, sv1, *, tokens_per_worker):
    nh, k, tb, lanes = PK_HEADS, PK_TOPK, SC_TOKEN_BLOCK, SC_LANES
    wid = lax.axis_index("s") * SC_CORES + lax.axis_index("c")
    ubufs, vbufs, sus, svs = (ub0, ub1), (vb0, vb1), (su0, su1), (sv0, sv1)
    lane = lax.iota(jnp.int32, lanes)
    zero = jnp.zeros((lanes,), F32)

    def start(tok, hd, par):
        irow = idx_v.at[tok * nh + hd]
        pltpu.async_copy(u_hbm.at[irow], ubufs[par], sus[par])
        pltpu.async_copy(v_hbm.at[irow], vbufs[par], svs[par])

    def wait(par):
        irow = idx_v.at[0]
        pltpu.make_async_copy(u_hbm.at[irow], ubufs[par], sus[par]).wait()
        pltpu.make_async_copy(v_hbm.at[irow], vbufs[par], svs[par]).wait()

    @pl.loop(0, tokens_per_worker // tb)
    def _(blk):
        base = wid * tokens_per_worker + blk * tb
        pltpu.sync_copy(h_hbm.at[pl.ds(base, tb)], h_v)
        pltpu.sync_copy(idx_hbm.at[pl.ds(base * nh, tb * nh)], idx_v)
        pltpu.sync_copy(gate_hbm.at[pl.ds(base * nh, tb * nh)], gate_v)
        start(0, 0, 0)

        @pl.loop(0, tb)
        def _(tok):
            for hd in range(nh):
                par = hd % 2
                if hd + 1 < nh:
                    start(tok, hd + 1, 1 - par)
                else:
                    @pl.when(tok + 1 < tb)
                    def _():
                        start(tok + 1, 0, 1 - par)
                wait(par)
                ub, vb = ubufs[par], vbufs[par]

                @plsc.parallel_loop(0, SC_ROW_CHUNKS, carry=(zero,) * k)
                def accs(c, acc):
                    ha = h_v[tok, pl.ds(c * lanes, lanes)]
                    hb = h_v[tok, pl.ds(SC_HALF + c * lanes, lanes)]
                    out = []
                    for r in range(k):
                        a, b = _unpack_pair(ub[r, pl.ds(c * lanes, lanes)])
                        out.append(acc[r] + (ha * a + hb * b))
                    return tuple(out)

                s_vec = zero
                for r in range(k):
                    s_vec = jnp.where(lane == r, jnp.sum(accs[r]), s_vec)
                w_vec = gate_v[tok * nh + hd, :] * _gelu_tanh(s_vec)
                wb = [jnp.sum(jnp.where(lane == r, w_vec, 0.0)) for r in range(k)]

                @plsc.parallel_loop(0, SC_ROW_CHUNKS)
                def _(c):
                    pa, pb = [], []
                    for r in range(k):
                        a, b = _unpack_pair(vb[r, pl.ds(c * lanes, lanes)])
                        pa.append(wb[r] * a)
                        pb.append(wb[r] * b)
                    while len(pa) > 1:
                        pa = [pa[i] + pa[i + 1] for i in range(0, len(pa), 2)]
                        pb = [pb[i] + pb[i + 1] for i in range(0, len(pb), 2)]
                    for off, o in ((0, pa[0]), (SC_HALF, pb[0])):
                        cols = pl.ds(off + c * lanes, lanes)
                        if hd == 0:
                            out_v[tok, cols] = o
                        else:
                            out_v[tok, cols] = out_v[tok, cols] + o

        pltpu.sync_copy(out_v, out_hbm.at[pl.ds(base, tb)])


def _peer_experts(h, eidx, gate, u_tab, v_tab):
    t, d = h.shape
    assert t % (SC_WORKERS * SC_TOKEN_BLOCK) == 0 and d == D_MODEL
    rows = pltpu.VMEM((PK_TOPK, SC_HALF), jnp.int32)
    return pl.kernel(
        functools.partial(_peer_sc_body, tokens_per_worker=t // SC_WORKERS),
        out_type=jax.ShapeDtypeStruct((t, d), F32),
        mesh=plsc.VectorSubcoreMesh(core_axis_name="c", subcore_axis_name="s"),
        scratch_types=[
            pltpu.VMEM((SC_TOKEN_BLOCK, d), F32),
            pltpu.VMEM((SC_TOKEN_BLOCK * PK_HEADS, PK_TOPK), jnp.int32),
            pltpu.VMEM((SC_TOKEN_BLOCK * PK_HEADS, PK_TOPK), F32),
            pltpu.VMEM((SC_TOKEN_BLOCK, d), F32),
            rows, rows, rows, rows,
            pltpu.SemaphoreType.DMA, pltpu.SemaphoreType.DMA,
            pltpu.SemaphoreType.DMA, pltpu.SemaphoreType.DMA,
        ],
        compiler_params=pltpu.CompilerParams(needs_layout_passes=False),
        name="peer_experts_sc",
    )(h, eidx, gate, u_tab, v_tab)


TT_DENSE = 256
EB_DENSE = 2048
N_EXPERTS = N_KEYS * N_KEYS
N_SEL = PK_HEADS * PK_TOPK


def _dense_body(h_ref, e_ref, g_ref, u_ref, v_ref, o_ref, g_scr, acc_scr):
    eb = pl.program_id(1)
    tt = h_ref.shape[0]
    slabs = EB_DENSE // N_KEYS

    @pl.when(eb == 0)
    def _():
        acc_scr[...] = jnp.zeros_like(acc_scr)
        sub = lax.broadcasted_iota(jnp.int32, (N_KEYS, N_SEL), 0)

        def per_token(t, carry):
            e = e_ref[pl.ds(t, 1), :]
            p_t = jnp.where(sub == (e >> 7), g_ref[pl.ds(t, 1), :], 0.0).astype(BF16)
            q_t = (sub == (e & (N_KEYS - 1))).astype(BF16)
            g_scr[pl.ds(pl.multiple_of(t * N_KEYS, N_KEYS), N_KEYS), :] = lax.dot_general(
                p_t, q_t, (((1,), (1,)), ((), ())), preferred_element_type=F32)
            return carry

        lax.fori_loop(0, tt, per_token, 0, unroll=8)

    s = lax.dot_general(h_ref[...].astype(BF16), u_ref[...], (((1,), (1,)), ((), ())), preferred_element_type=F32)
    gs = jnp.concatenate([g_scr[pl.ds(eb * slabs + k, tt, stride=N_KEYS), :] for k in range(slabs)], axis=1)
    acc_scr[...] += jnp.dot((gs * jax.nn.gelu(s)).astype(BF16), v_ref[...], preferred_element_type=F32)

    @pl.when(eb == pl.num_programs(1) - 1)
    def _():
        o_ref[...] = acc_scr[...]


def _peer_dense(h, eidx, gate, u_bf, v_bf):
    t, d = h.shape
    tt = TT_DENSE
    assert t % tt == 0 and N_KEYS == 128
    tile = lambda i, e: (i, 0)
    blk = lambda i, e: (e, 0)
    return pl.pallas_call(
        _dense_body,
        out_shape=jax.ShapeDtypeStruct((t, d), F32),
        grid=(t // tt, N_EXPERTS // EB_DENSE),
        in_specs=[pl.BlockSpec((tt, d), tile), pl.BlockSpec((tt, N_SEL), tile), pl.BlockSpec((tt, N_SEL), tile),
                  pl.BlockSpec((EB_DENSE, d), blk), pl.BlockSpec((EB_DENSE, d), blk)],
        out_specs=pl.BlockSpec((tt, d), tile),
        scratch_shapes=[pltpu.VMEM((tt * N_KEYS, N_KEYS), F32), pltpu.VMEM((tt, d), F32)],
        compiler_params=_cparams("arbitrary", "arbitrary"),
        name="peer_dense_tc",
    )(h, eidx, gate, u_bf, v_bf)


TM_FIN = 512


def _final_body(x1_ref, p_ref, m_ref, w_ref, y_ref):
    y_ref[...] = _rms(x1_ref[...] + m_ref[0, 5:6, :] * p_ref[...]) * w_ref[...]


def _final(x1, peer, mod, norm_f, rows_of):
    t, d = x1.shape
    tm = _token_tile(t, TM_FIN)
    row_of = rows_of(tm)
    tile = pl.BlockSpec((tm, d), lambda i: (i, 0))
    return pl.pallas_call(
        _final_body,
        out_shape=jax.ShapeDtypeStruct((t, d), F32),
        grid=(t // tm,),
        in_specs=[tile, tile, pl.BlockSpec((1, N_MOD, d), lambda i: (row_of(i), 0, 0)),
                  pl.BlockSpec((1, d), lambda i: (0, 0))],
        out_specs=tile,
        compiler_params=_cparams("arbitrary"),
        name="final_norm",
    )(x1, peer, mod, norm_f.reshape(1, d))


CTX_GROUPS = (2, 14, 16)
LAT_GROUPS = (2, 2, 2, 2)
DENSE_TOKENS = 43 * 256


def _zero_of(a):
    return (lax.shift_right_logical(a[0, 0], 31) >> 1).astype(F32)


def _group(x, mod, row0, per_seq, n, hg_s0, lru_s0, col_major, p, after, dense_tokens=0):
    seqs = hg_s0.shape[0]
    t = seqs * n
    for a in after:
        mod = mod + _zero_of(a)
    rows_of = lambda tm: _mod_row_map(tm, n, row0, per_seq)
    z = _in_proj(x, mod, p['norm1'], p['w_in'], rows_of)
    o_hg, hg_fin = _hgrn2(z, p['lb'], hg_s0, n, 0)
    hr, lru_fin = _rglru(z, COL_XR, 0, n, seqs, col_major, *p['lru'], lru_s0)
    x1, h2, q = _out_proj(x, o_hg, hr, z, mod, p['hg_norm'], p['norm2'], p['w_a'], p['w_b'], p['w_o'], p['w_q'],
                          rows_of(TM_OUT))
    eidx, gate = _pk_topk(q, p['keys'])
    ts = t - dense_tokens
    parts = []
    if ts:
        parts.append(_peer_experts(h2[:ts], eidx[:ts].reshape(ts * PK_HEADS, PK_TOPK),
                                   gate[:ts].reshape(ts * PK_HEADS, PK_TOPK), p['pk_u'], p['pk_v']))
    if dense_tokens:
        parts.append(_peer_dense(h2[ts:], eidx[ts:], gate[ts:], p['u_bf'], p['v_bf']))
    peer = parts[0] if len(parts) == 1 else jnp.concatenate(parts, axis=0)
    return _final(x1, peer, mod, p['norm_f'], rows_of), hg_fin, lru_fin, eidx


def kernel(x_prompt, x_sample, state_hgrn, state_rglru, c, c_ctx, w_mod, b_mod, norm1, w_in,
           hg_lb_logits, hg_norm, lru_conv_w, lru_conv_b, lru_w_r, lru_b_r, lru_w_i, lru_b_i,
           lru_lambda, w_branch_a, w_branch_b, w_out, norm2, pk_w_q, pk_sub_keys, pk_u, pk_v,
           norm_f):
    assert w_mod.shape[0] == 1, "single trunk layer"
    d = D_MODEL
    nb_ctx, n_ctx, _ = x_prompt.shape
    nb_lat, n_lat, _ = x_sample.shape
    assert nb_lat < MOD_ROWS and nb_lat == sum(LAT_GROUPS) and nb_ctx == sum(CTX_GROUPS)
    ctx_row = nb_lat

    cond = jnp.zeros((MOD_ROWS, d), F32).at[:nb_lat].set(c).at[ctx_row].set(c_ctx)
    mod = _modulation(cond, w_mod[0], b_mod[0])
    p = {
        'lb': jnp.cumsum(jax.nn.softmax(hg_lb_logits.astype(F32), axis=1), axis=1)[:, 0],
        'norm1': norm1[0], 'w_in': w_in[0].astype(BF16), 'hg_norm': hg_norm[0], 'norm2': norm2[0],
        'lru': (lru_conv_w[0], lru_conv_b[0], _block_diag(lru_w_r[0]), _block_diag(lru_w_i[0]),
                lru_b_r[0], lru_b_i[0], lru_lambda[0]),
        'w_a': w_branch_a[0].astype(BF16), 'w_b': w_branch_b[0].astype(BF16), 'w_o': w_out[0].astype(BF16),
        'w_q': pk_w_q[0].astype(BF16), 'keys': pk_sub_keys[0].astype(BF16),
        'pk_u': _pack_bf16_pairs(pk_u[0]), 'pk_v': _pack_bf16_pairs(pk_v[0]), 'norm_f': norm_f,
        'u_bf': pk_u[0].astype(BF16), 'v_bf': pk_v[0].astype(BF16),
    }
    y_ctx, hg_fin, lru_fin = [], [], []
    tables = [p['pk_u'], p['pk_v']]
    after = tables
    s0 = 0
    for g in CTX_GROUPS:
        y, hg, lr, eidx = _group(
            x_prompt[s0:s0 + g].reshape(g * n_ctx, d), mod, ctx_row, False, n_ctx,
            jnp.zeros((g, 2, HG_HEADS, HG_DK, HG_DV), F32), jnp.zeros((g, 2, LRU_W), F32), False, p, after)
        s0 += g
        after = [eidx] if after is tables else after
        y_ctx.append(y.reshape(g, n_ctx, d))
        hg_fin.append(hg)
        lru_fin.append(lr)
    y_lat = []
    s0 = 0
    for g in LAT_GROUPS:
        sl = slice(s0, s0 + g)
        s0 += g
        tokens_after = (nb_lat - s0) * n_lat
        y, _, _, _ = _group(x_sample[sl].reshape(g * n_lat, d), mod, s0 - g, True, n_lat,
                            state_hgrn[sl, 0], state_rglru[sl, 0], True, p, after,
                            dense_tokens=min(g * n_lat, max(0, DENSE_TOKENS - tokens_after)))
        y_lat.append(y.reshape(g, n_lat, d))
    return (jnp.concatenate(y_ctx, axis=0), jnp.concatenate(y_lat, axis=0),
            jnp.concatenate(hg_fin, axis=0)[:, None], jnp.concatenate(lru_fin, axis=0)[:, None])
```

```python
import functools

import jax, jax.numpy as jnp
from jax import lax
from jax.experimental import pallas as pl
from jax.experimental.pallas import tpu as pltpu
from jax.experimental.pallas import tpu_sc as plsc

D_MODEL = 1024
GRID_W = 64
EPS = 1e-6
HG_HEADS = 4
HG_DK = 128
HG_DV = 128
HG_KW = HG_HEADS * HG_DK
HG_VW = HG_HEADS * HG_DV
HG_CHUNK = 32
LRU_W = D_MODEL // 2
LRU_BLOCKS = 8
LRU_BW = LRU_W // LRU_BLOCKS
CONV_W = 4
LRU_C = 8.0
PK_HEADS = 8
N_KEYS = 128
PK_TOPK = 16
PK_DQ = 256
PK_DH = PK_DQ // 2
IN_W = 3 * HG_KW + 2 * HG_VW + 2 * LRU_W + 2 * D_MODEL
COL_G, COL_XR, COL_XG, COL_GA, COL_GB = 4, 5, 6, 7, 9
N_MOD = 6
MOD_ROWS = 16
F32 = jnp.float32
BF16 = jnp.bfloat16
VMEM_LIMIT = 48 * 1024 * 1024


def _cparams(*sem):
    return pltpu.CompilerParams(dimension_semantics=sem, vmem_limit_bytes=VMEM_LIMIT)


def _silu(x):
    return x * jax.nn.sigmoid(x)


def _rms(x):
    return x * lax.rsqrt(jnp.mean(x * x, axis=-1, keepdims=True) + EPS)


def _mod_body(c_ref, w_ref, b_ref, o_ref):
    o_ref[...] = lax.dot_general(_silu(c_ref[...]), w_ref[...], (((1,), (0,)), ((), ())),
                                 precision=lax.Precision.HIGHEST, preferred_element_type=F32) + b_ref[...]


def _modulation(cond, w_mod, b_mod):
    d = D_MODEL
    out = pl.pallas_call(
        _mod_body,
        out_shape=jax.ShapeDtypeStruct((MOD_ROWS, N_MOD * d), F32),
        grid=(N_MOD,),
        in_specs=[pl.BlockSpec((MOD_ROWS, d), lambda j: (0, 0)),
                  pl.BlockSpec((d, d), lambda j: (0, j)),
                  pl.BlockSpec((1, d), lambda j: (0, j))],
        out_specs=pl.BlockSpec((MOD_ROWS, d), lambda j: (0, j)),
        compiler_params=_cparams("arbitrary"),
        name="adaln_modulation",
    )(cond, w_mod, b_mod.reshape(1, N_MOD * d))
    return out.reshape(MOD_ROWS, N_MOD, d)


def _mod_row_map(tm, n, row0, per_seq):
    per = n // tm
    return (lambda i: row0 + i // per) if per_seq else (lambda i: row0)


TM_IN = 512
TOKEN_TILE_MIN = 256
TN_IN = IN_W // 2


def _in_body(x_ref, m_ref, n1_ref, w_ref, z_ref):
    y = _rms(x_ref[...]) * n1_ref[...]
    h = (y * (1.0 + m_ref[0, 1:2, :]) + m_ref[0, 0:1, :]).astype(BF16)
    z_ref[...] = jnp.dot(h, w_ref[...], preferred_element_type=F32)


def _token_tile(t, preferred):
    tm = preferred if t % preferred == 0 else TOKEN_TILE_MIN
    assert t % tm == 0
    return tm


def _in_proj(x, mod, norm1, w_in_bf, rows_of):
    t, d = x.shape
    tm = _token_tile(t, TM_IN)
    row_of = rows_of(tm)
    return pl.pallas_call(
        _in_body,
        out_shape=jax.ShapeDtypeStruct((t, IN_W), F32),
        grid=(IN_W // TN_IN, t // tm),
        in_specs=[pl.BlockSpec((tm, d), lambda j, i: (i, 0)),
                  pl.BlockSpec((1, N_MOD, d), lambda j, i: (row_of(i), 0, 0)),
                  pl.BlockSpec((1, d), lambda j, i: (0, 0)),
                  pl.BlockSpec((d, TN_IN), lambda j, i: (0, j))],
        out_specs=pl.BlockSpec((tm, TN_IN), lambda j, i: (i, j)),
        compiler_params=_cparams("arbitrary", "arbitrary"),
        name="in_proj",
    )(x, mod, norm1.reshape(1, d), w_in_bf)


HG_LONG_SEQ = 4096


def _heads_per_step(n):
    return HG_HEADS if n < HG_LONG_SEQ else HG_HEADS // 2


def _hg_body(q_ref, ff_ref, fb_ref, v_ref, lb_ref, s0_ref, o_ref, sfin_ref, st_scr, ob_scr, *, n, hps):
    c = HG_CHUNK
    nc = n // c
    row = lax.broadcasted_iota(jnp.int32, (c, c), 0)
    col = lax.broadcasted_iota(jnp.int32, (c, c), 1)
    lower = row >= col
    tri = (lower.astype(F32), (row <= col).astype(F32))
    masks = (lower, row <= col)
    f_refs = (ff_ref, fb_ref)
    for d in range(2):
        for hh in range(hps):
            st_scr[d * hps + hh] = s0_ref[0, d, hh].T

    def step(i, carry):
        chains = [(d, hh) for d in range(2) for hh in range(hps)]
        st1 = []
        for d, hh in chains:
            ci = i if d == 0 else nc - 1 - i
            r = pl.ds(pl.multiple_of(ci * c, c), c)
            hc = slice(hh * HG_DK, (hh + 1) * HG_DK)
            lb = lb_ref[d:d + 1, hc]
            f = lb + (1.0 - lb) * jax.nn.sigmoid(f_refs[d][r, hc])
            lf = jnp.log(f)
            cum = lax.dot_general(tri[d], lf, (((1,), (0,)), ((), ())),
                                  precision=lax.Precision.HIGHEST, preferred_element_type=F32)
            st1.append((r, hc, 1.0 - f, lf, cum))
        st2 = []
        for (d, hh), (r, hc, k, lf, cum) in zip(chains, st1):
            q = _silu(q_ref[r, hc])
            v = v_ref[r, hc].astype(BF16)
            tot = jnp.sum(lf, axis=0, keepdims=True)
            q_dec = (q * jnp.exp(cum)).astype(BF16)
            k_inv = (k * jnp.exp(-cum)).astype(BF16)
            k_end = (k * jnp.exp(tot - cum)).astype(BF16)
            st = st_scr[d * hps + hh]
            att = lax.dot_general(q_dec, k_inv, (((1,), (1,)), ((), ())), preferred_element_type=F32)
            o_inter = lax.dot_general(q_dec, st.astype(BF16), (((1,), (1,)), ((), ())), preferred_element_type=F32)
            ds_t = lax.dot_general(v, k_end, (((0,), (0,)), ((), ())), preferred_element_type=F32)
            st_scr[d * hps + hh] = st * jnp.exp(tot) + ds_t
            st2.append((v, att, o_inter))
        for (d, hh), (r, hc, _, _, _), (v, att, o_inter) in zip(chains, st1, st2):
            att = jnp.where(masks[d], att, 0.0).astype(BF16)
            o = jnp.dot(att, v, preferred_element_type=F32) + o_inter
            if d == 0:
                o_ref[r, hc] = o
            else:
                ob_scr[r, hc] = o
        return carry

    lax.fori_loop(0, nc, step, 0)
    o_ref[...] = o_ref[...] + ob_scr[...]
    for d in range(2):
        for hh in range(hps):
            sfin_ref[0, d, hh] = st_scr[d * hps + hh].T


def _hgrn2(z, lb, s0, n, row_block0):
    seqs = s0.shape[0]
    hps = _heads_per_step(n)
    wb = hps * HG_DK
    nblk = HG_KW // wb

    def zspec(seg):
        return pl.BlockSpec((n, wb), lambda s, h: (row_block0 + s, seg * nblk + h))

    st_spec = pl.BlockSpec((1, 2, hps, HG_DK, HG_DV), lambda s, h: (s, 0, h, 0, 0))
    return pl.pallas_call(
        functools.partial(_hg_body, n=n, hps=hps),
        out_shape=(jax.ShapeDtypeStruct((seqs * n, HG_VW), F32),
                   jax.ShapeDtypeStruct((seqs, 2, HG_HEADS, HG_DK, HG_DV), F32)),
        grid=(seqs, HG_HEADS // hps),
        in_specs=[zspec(0), zspec(1), zspec(2), zspec(3),
                  pl.BlockSpec((2, wb), lambda s, h: (0, h)), st_spec],
        out_specs=(pl.BlockSpec((n, wb), lambda s, h: (s, h)), st_spec),
        scratch_shapes=[pltpu.VMEM((2 * hps, HG_DV, HG_DK), F32), pltpu.VMEM((n, wb), F32)],
        compiler_params=_cparams("arbitrary", "arbitrary"),
        name="hgrn2_scan",
    )(z, z, z, z, lb, s0)


LRU_RB = 64
HALO = 8


LANES = 128


def _lru_body(x_ref, cw_ref, cb_ref, wr_ref, wi_ref, br_ref, bi_ref, lam_ref, h0_ref, hr_ref, e_ref,
              xp_scr, xc_scr, *cm_scr, n, col_major):
    rb = LRU_RB
    nb = n // rb
    w = LRU_W
    grid_rows = n // GRID_W
    slabs = w // LANES
    zeros = jnp.zeros((HALO, w), F32)
    xp_scr[0:HALO, :] = zeros
    xp_scr[HALO + n:HALO + n + HALO, :] = zeros
    if col_major:
        st_scr, hcm_scr = cm_scr
        for j in range(slabs):
            st_scr[j] = x_ref[:, j * LANES:(j + 1) * LANES]

        def to_col_major(c, carry):
            dst = pl.ds(pl.multiple_of(HALO + c * grid_rows, 8), grid_rows)
            for j in range(slabs):
                xp_scr[dst, j * LANES:(j + 1) * LANES] = st_scr[j, pl.ds(c, grid_rows, stride=GRID_W), :]
            return carry

        lax.fori_loop(0, GRID_W, to_col_major, 0)
        h_dst = hcm_scr
    else:
        xp_scr[HALO:HALO + n, :] = x_ref[...]
        h_dst = hr_ref
    cw = cw_ref[...]
    cb = cb_ref[...]

    def conv_blk(b, carry):
        base = pl.multiple_of(b * rb, rb)
        xh = xp_scr[pl.ds(base, rb + 2 * HALO), :]
        ext = rb + 2 * HALO
        acc = cb + cw[2:3, :] * xh[HALO:HALO + rb]
        acc = acc + cw[0:1, :] * pltpu.roll(xh, 2, axis=0)[HALO:HALO + rb]
        acc = acc + cw[1:2, :] * pltpu.roll(xh, 1, axis=0)[HALO:HALO + rb]
        acc = acc + cw[3:4, :] * pltpu.roll(xh, ext - 1, axis=0)[HALO:HALO + rb]
        xc_scr[pl.ds(base, rb), :] = acc
        return carry

    lax.fori_loop(0, nb, conv_blk, 0)

    rows = lax.broadcasted_iota(jnp.int32, (rb, w), 0)

    def gates(blk, d):
        r = pl.ds(pl.multiple_of(blk * rb, rb), rb)
        xc = xc_scr[r, :]
        xb = xc.astype(BF16)
        rg = jax.nn.sigmoid(jnp.dot(xb, wr_ref[d], preferred_element_type=F32) + br_ref[d:d + 1, :])
        ig = jax.nn.sigmoid(jnp.dot(xb, wi_ref[d], preferred_element_type=F32) + bi_ref[d:d + 1, :])
        log_a = (-LRU_C) * jax.nn.softplus(-lam_ref[d:d + 1, :]) * rg
        a = jnp.exp(log_a)
        u = jnp.sqrt(-jnp.tanh(log_a) * (a * a + 1.0)) * (ig * xc)
        return r, a, u

    def fwd_blk(blk, h_prev):
        r, a, u = gates(blk, 0)
        s = 1
        while s < rb:
            keep = rows >= s
            a_sh = jnp.where(keep, pltpu.roll(a, s, axis=0), 1.0)
            u_sh = jnp.where(keep, pltpu.roll(u, s, axis=0), 0.0)
            u = a * u_sh + u
            a = a * a_sh
            s *= 2
        h = u + a * h_prev
        h_dst[r, :] = h
        return h[rb - 1:rb, :]

    def bwd_blk(i, h_next):
        r, a, u = gates(nb - 1 - i, 1)
        s = 1
        while s < rb:
            keep = rows < rb - s
            a_sh = jnp.where(keep, pltpu.roll(a, rb - s, axis=0), 1.0)
            u_sh = jnp.where(keep, pltpu.roll(u, rb - s, axis=0), 0.0)
            u = a * u_sh + u
            a = a * a_sh
            s *= 2
        h = u + a * h_next
        h_dst[r, :] = h_dst[r, :] + h
        return h[0:1, :]

    e_ref[0, 0:1, :] = lax.fori_loop(0, nb, fwd_blk, h0_ref[0, 0:1, :])
    e_ref[0, 1:2, :] = lax.fori_loop(0, nb, bwd_blk, h0_ref[0, 1:2, :])
    if col_major:
        def to_row_major(c, carry):
            src = pl.ds(pl.multiple_of(c * grid_rows, 8), grid_rows)
            for j in range(slabs):
                st_scr[j, pl.ds(c, grid_rows, stride=GRID_W), :] = hcm_scr[src, j * LANES:(j + 1) * LANES]
            return carry

        lax.fori_loop(0, GRID_W, to_row_major, 0)
        for j in range(slabs):
            hr_ref[:, j * LANES:(j + 1) * LANES] = st_scr[j]


def _rglru(x, col_block, row_block0, n, seqs, col_major, conv_w, conv_b, wr_bd, wi_bd, b_r, b_i, lam, h0):
    w = LRU_W
    cm_scratch = [pltpu.VMEM((w // LANES, n, LANES), F32), pltpu.VMEM((n, w), F32)] if col_major else []
    full2 = lambda s: (0, 0)
    full3 = lambda s: (0, 0, 0)
    return pl.pallas_call(
        functools.partial(_lru_body, n=n, col_major=col_major),
        out_shape=(jax.ShapeDtypeStruct((seqs * n, w), F32), jax.ShapeDtypeStruct((seqs, 2, w), F32)),
        grid=(seqs,),
        in_specs=[pl.BlockSpec((n, w), lambda s: (row_block0 + s, col_block)),
                  pl.BlockSpec((CONV_W, w), full2), pl.BlockSpec((1, w), full2),
                  pl.BlockSpec((2, w, w), full3), pl.BlockSpec((2, w, w), full3),
                  pl.BlockSpec((2, w), full2), pl.BlockSpec((2, w), full2), pl.BlockSpec((2, w), full2),
                  pl.BlockSpec((1, 2, w), lambda s: (s, 0, 0))],
        out_specs=(pl.BlockSpec((n, w), lambda s: (s, 0)), pl.BlockSpec((1, 2, w), lambda s: (s, 0, 0))),
        scratch_shapes=[pltpu.VMEM((n + 2 * HALO, w), F32), pltpu.VMEM((n, w), F32)] + cm_scratch,
        compiler_params=_cparams("arbitrary"),
        name="rglru",
    )(x, conv_w, conv_b.reshape(1, w), wr_bd, wi_bd, b_r, b_i, lam, h0)


def _block_diag(wg):
    eye = jnp.eye(LRU_BLOCKS, dtype=wg.dtype)
    dense = wg[:, :, :, None, :] * eye[None, :, None, :, None]
    return dense.reshape(2, LRU_W, LRU_W).astype(BF16)


TM_OUT = 256


def _out_body(x_ref, o_ref, hr_ref, g_ref, xg_ref, ga0_ref, ga1_ref, gb0_ref, gb1_ref, m_ref,
              hgn_ref, n2_ref, wa_ref, wb_ref, wo_ref, wq_ref, x1_ref, h2_ref, q_ref):
    o = o_ref[...]
    parts = []
    for h in range(HG_HEADS):
        parts.append(_rms(o[:, h * HG_DV:(h + 1) * HG_DV]) * hgn_ref[...])
    on = jnp.concatenate(parts, axis=1) * _silu(g_ref[...])
    y_a = jnp.dot(on.astype(BF16), wa_ref[...], preferred_element_type=F32)
    y_b = jnp.dot((hr_ref[...] * jax.nn.gelu(xg_ref[...])).astype(BF16), wb_ref[...], preferred_element_type=F32)
    ga = jnp.concatenate([ga0_ref[...], ga1_ref[...]], axis=1)
    gb = jnp.concatenate([gb0_ref[...], gb1_ref[...]], axis=1)
    merged = jax.nn.sigmoid(ga) * y_a + jax.nn.sigmoid(gb) * y_b
    mix = jnp.dot(merged.astype(BF16), wo_ref[...], preferred_element_type=F32)
    x1 = x_ref[...] + m_ref[0, 2:3, :] * mix
    x1_ref[...] = x1
    h2 = _rms(x1) * n2_ref[...] * (1.0 + m_ref[0, 4:5, :]) + m_ref[0, 3:4, :]
    h2_ref[...] = h2
    q_ref[...] = jnp.dot(h2.astype(BF16), wq_ref[...], preferred_element_type=F32).astype(BF16)


def _out_proj(x, o_hg, hr, z, mod, hg_norm, norm2, wa, wb, wo, wq, row_of):
    t, d = x.shape
    tm = TM_OUT
    nq = PK_HEADS * PK_DQ
    half = lambda i: (i, 0)
    zc = lambda c: pl.BlockSpec((tm, LRU_W), lambda i: (i, c))
    const = lambda i: (0, 0)
    return pl.pallas_call(
        _out_body,
        out_shape=(jax.ShapeDtypeStruct((t, d), F32), jax.ShapeDtypeStruct((t, d), F32),
                   jax.ShapeDtypeStruct((t, nq), BF16)),
        grid=(t // tm,),
        in_specs=[pl.BlockSpec((tm, d), half), pl.BlockSpec((tm, HG_VW), half), pl.BlockSpec((tm, LRU_W), half),
                  zc(COL_G), zc(COL_XG), zc(COL_GA), zc(COL_GA + 1), zc(COL_GB), zc(COL_GB + 1),
                  pl.BlockSpec((1, N_MOD, d), lambda i: (row_of(i), 0, 0)),
                  pl.BlockSpec((1, HG_DV), const), pl.BlockSpec((1, d), const),
                  pl.BlockSpec((HG_VW, d), const), pl.BlockSpec((LRU_W, d), const),
                  pl.BlockSpec((d, d), const), pl.BlockSpec((d, nq), const)],
        out_specs=(pl.BlockSpec((tm, d), half), pl.BlockSpec((tm, d), half), pl.BlockSpec((tm, nq), half)),
        compiler_params=_cparams("arbitrary"),
        name="out_proj",
    )(x, o_hg, hr, z, z, z, z, z, z, mod, hg_norm.reshape(1, HG_DV), norm2.reshape(1, d), wa, wb, wo, wq)


TT_TOPK = 256


def _extract_topk(s, order, k, payload=None):
    vals, ords, picks = [], [], []
    for _ in range(k):
        m = jnp.max(s, axis=0, keepdims=True)
        o = jnp.max(jnp.where(s == m, order, -1.0), axis=0, keepdims=True)
        hit = order == o
        vals.append(m)
        ords.append(o)
        if payload is not None:
            picks.append(jnp.sum(jnp.where(hit, payload, 0), axis=0, keepdims=True))
        s = jnp.where(hit, -jnp.inf, s)
    cat = lambda xs: jnp.concatenate(xs, axis=0)
    return cat(vals), cat(ords), (cat(picks) if payload is not None else None)


PAIR_ROWS = 4
PAIR_COLS = 3
assert all((i + 1) * (j + 1) > PK_TOPK for i in range(PAIR_ROWS, PK_TOPK) for j in range(PAIR_COLS, PK_TOPK))
assert all((i + 1) * (8 + 1) > PK_TOPK for i in range(1, PAIR_ROWS)) and (8 + 1) * (1 + 1) > PK_TOPK


def _pair_candidates(sv, si, tt):
    cand, order, eid = [], [], []
    hi = float(PK_TOPK * PK_TOPK - 1)
    for i in range(PAIR_ROWS):
        nj = PK_TOPK if i == 0 else 8
        j = lax.broadcasted_iota(jnp.int32, (nj, tt), 0)
        cand.append(sv[0][i:i + 1, :] + sv[1][:nj])
        order.append(hi - (i * PK_TOPK + j).astype(F32))
        eid.append(si[0][i:i + 1, :] * N_KEYS + si[1][:nj])
    for j in range(PAIR_COLS):
        ni = PK_TOPK if j == 0 else 8
        i = lax.broadcasted_iota(jnp.int32, (ni, tt), 0)
        fresh = i >= PAIR_ROWS
        cand.append(jnp.where(fresh, sv[0][:ni] + sv[1][j:j + 1, :], -jnp.inf))
        order.append(jnp.where(fresh, hi - (i * PK_TOPK + j).astype(F32), -2.0))
        eid.append(si[0][:ni] * N_KEYS + si[1][j:j + 1, :])
    cat = lambda xs: jnp.concatenate(xs, axis=0)
    return cat(cand), cat(order), cat(eid)


def _topk_body(q_ref, keys_ref, eidx_ref, gate_ref):
    tt = q_ref.shape[0]
    key_order = (N_KEYS - 1 - lax.broadcasted_iota(jnp.int32, (N_KEYS, tt), 0)).astype(F32)
    e_rows, g_rows = [], []
    for h in range(PK_HEADS):
        sv, si = [], []
        for p in range(2):
            c0 = h * PK_DQ + p * PK_DH
            s = lax.dot_general(keys_ref[p, h], q_ref[:, c0:c0 + PK_DH], (((1,), (1,)), ((), ())),
                                preferred_element_type=F32)
            v, o, _ = _extract_topk(s, key_order, PK_TOPK)
            sv.append(v)
            si.append(N_KEYS - 1 - o.astype(jnp.int32))
        cand, order, cidx = _pair_candidates(sv, si, tt)
        best, _, eid = _extract_topk(cand, order, PK_TOPK, payload=cidx)
        ex = jnp.exp(best - best[0:1, :])
        g_rows.append(ex / jnp.sum(ex, axis=0, keepdims=True))
        e_rows.append(eid)
    eidx_ref[...] = jnp.concatenate(e_rows, axis=0).T
    gate_ref[...] = jnp.concatenate(g_rows, axis=0).T


def _pk_topk(q, keys_bf):
    t = q.shape[0]
    tt = TT_TOPK
    ne = PK_HEADS * PK_TOPK
    return pl.pallas_call(
        _topk_body,
        out_shape=(jax.ShapeDtypeStruct((t, ne), jnp.int32), jax.ShapeDtypeStruct((t, ne), F32)),
        grid=(t // tt,),
        in_specs=[pl.BlockSpec((tt, PK_HEADS * PK_DQ), lambda i: (i, 0)),
                  pl.BlockSpec((2, PK_HEADS, N_KEYS, PK_DH), lambda i: (0, 0, 0, 0))],
        out_specs=(pl.BlockSpec((tt, ne), lambda i: (i, 0)), pl.BlockSpec((tt, ne), lambda i: (i, 0))),
        compiler_params=_cparams("arbitrary"),
        name="pk_topk",
    )(q, keys_bf)


SC_LANES = 16
SC_CORES = 2
SC_SUBCORES = 16
SC_WORKERS = SC_CORES * SC_SUBCORES
SC_HALF = D_MODEL // 2
SC_ROW_CHUNKS = SC_HALF // SC_LANES
SC_TOKEN_BLOCK = 8


def _pack_bf16_pairs(tab):
    lo = lax.bitcast_convert_type(tab[:, SC_HALF:].astype(BF16), jnp.uint16).astype(jnp.uint32)
    bits = lax.bitcast_convert_type(tab[:, :SC_HALF], jnp.uint32)
    sign = bits & jnp.uint32(0x80000000)
    mag = (bits & jnp.uint32(0x7FFFFFFF)) + jnp.uint32(1 << 15)
    hi = jnp.where(mag >= lo, (mag - lo) >> 16, jnp.uint32(0))
    return lax.bitcast_convert_type(sign | (hi << 16) | lo, jnp.int32)


def _unpack_pair(w):
    return lax.bitcast_convert_type(w, F32), lax.bitcast_convert_type(w << 16, F32)


def _gelu_tanh(x):
    y = 0.7978845608028654 * (x + 0.044715 * (x * x * x))
    t = 1.0 - 2.0 / (jnp.exp(2.0 * y) + 1.0)
    return x * (0.5 * (1.0 + t))


def _peer_sc_body(h_hbm, idx_hbm, gate_hbm, u_hbm, v_hbm, out_hbm,
                  h_v, idx_v, gate_v, out_v, ub0, ub1, vb0, vb1, su0, su1, sv0, sv1, *, tokens_per_worker):
    nh, k, tb, lanes = PK_HEADS, PK_TOPK, SC_TOKEN_BLOCK, SC_LANES
    wid = lax.axis_index("s") * SC_CORES + lax.axis_index("c")
    ubufs, vbufs, sus, svs = (ub0, ub1), (vb0, vb1), (su0, su1), (sv0, sv1)
    lane = lax.iota(jnp.int32, lanes)
    zero = jnp.zeros((lanes,), F32)

    def start(tok, hd, par):
        irow = idx_v.at[tok * nh + hd]
        pltpu.async_copy(u_hbm.at[irow], ubufs[par], sus[par])
        pltpu.async_copy(v_hbm.at[irow], vbufs[par], svs[par])

    def wait(par):
        irow = idx_v.at[0]
        pltpu.make_async_copy(u_hbm.at[irow], ubufs[par], sus[par]).wait()
        pltpu.make_async_copy(v_hbm.at[irow], vbufs[par], svs[par]).wait()

    @pl.loop(0, tokens_per_worker // tb)
    def _(blk):
        base = wid * tokens_per_worker + blk * tb
        pltpu.sync_copy(h_hbm.at[pl.ds(base, tb)], h_v)
        pltpu.sync_copy(idx_hbm.at[pl.ds(base * nh, tb * nh)], idx_v)
        pltpu.sync_copy(gate_hbm.at[pl.ds(base * nh, tb * nh)], gate_v)
        start(0, 0, 0)

        @pl.loop(0, tb)
        def _(tok):
            for hd in range(nh):
                par = hd % 2
                if hd + 1 < nh:
                    start(tok, hd + 1, 1 - par)
                else:
                    @pl.when(tok + 1 < tb)
                    def _():
                        start(tok + 1, 0, 1 - par)
                wait(par)
                ub, vb = ubufs[par], vbufs[par]

                @plsc.parallel_loop(0, SC_ROW_CHUNKS, carry=(zero,) * k)
                def accs(c, acc):
                    ha = h_v[tok, pl.ds(c * lanes, lanes)]
                    hb = h_v[tok, pl.ds(SC_HALF + c * lanes, lanes)]
                    out = []
                    for r in range(k):
                        a, b = _unpack_pair(ub[r, pl.ds(c * lanes, lanes)])
                        out.append(acc[r] + (ha * a + hb * b))
                    return tuple(out)

                s_vec = zero
                for r in range(k):
                    s_vec = jnp.where(lane == r, jnp.sum(accs[r]), s_vec)
                w_vec = gate_v[tok * nh + hd, :] * _gelu_tanh(s_vec)
                wb = [jnp.sum(jnp.where(lane == r, w_vec, 0.0)) for r in range(k)]

                @plsc.parallel_loop(0, SC_ROW_CHUNKS)
                def _(c):
                    pa, pb = [], []
                    for r in range(k):
                        a, b = _unpack_pair(vb[r, pl.ds(c * lanes, lanes)])
                        pa.append(wb[r] * a)
                        pb.append(wb[r] * b)
                    while len(pa) > 1:
                        pa = [pa[i] + pa[i + 1] for i in range(0, len(pa), 2)]
                        pb = [pb[i] + pb[i + 1] for i in range(0, len(pb), 2)]
                    for off, o in ((0, pa[0]), (SC_HALF, pb[0])):
                        cols = pl.ds(off + c * lanes, lanes)
                        if hd == 0:
                            out_v[tok, cols] = o
                        else:
                            out_v[tok, cols] = out_v[tok, cols] + o

        pltpu.sync_copy(out_v, out_hbm.at[pl.ds(base, tb)])


def _peer_experts(h, eidx, gate, u_tab, v_tab):
    t, d = h.shape
    assert t % (SC_WORKERS * SC_TOKEN_BLOCK) == 0 and d == D_MODEL
    rows = pltpu.VMEM((PK_TOPK, SC_HALF), jnp.int32)
    return pl.kernel(
        functools.partial(_peer_sc_body, tokens_per_worker=t // SC_WORKERS),
        out_type=jax.ShapeDtypeStruct((t, d), F32),
        mesh=plsc.VectorSubcoreMesh(core_axis_name="c", subcore_axis_name="s"),
        scratch_types=[
            pltpu.VMEM((SC_TOKEN_BLOCK, d), F32),
            pltpu.VMEM((SC_TOKEN_BLOCK * PK_HEADS, PK_TOPK), jnp.int32),
            pltpu.VMEM((SC_TOKEN_BLOCK * PK_HEADS, PK_TOPK), F32),
            pltpu.VMEM((SC_TOKEN_BLOCK, d), F32),
            rows, rows, rows, rows,
            pltpu.SemaphoreType.DMA, pltpu.SemaphoreType.DMA,
            pltpu.SemaphoreType.DMA, pltpu.SemaphoreType.DMA,
        ],
        compiler_params=pltpu.CompilerParams(needs_layout_passes=False),
        name="peer_experts_sc",
    )(h, eidx, gate, u_tab, v_tab)


TT_DENSE = 512
EB_DENSE = 1024
N_EXPERTS = N_KEYS * N_KEYS
N_SEL = PK_HEADS * PK_TOPK
DENSE_VMEM_LIMIT = 56 * 1024 * 1024


def _dense_body(h_ref, e_ref, g_ref, u_ref, v_ref, o_ref, g_scr, acc_scr):
    eb = pl.program_id(1)
    tt = h_ref.shape[0]
    slabs = EB_DENSE // N_KEYS

    @pl.when(eb == 0)
    def _():
        acc_scr[...] = jnp.zeros_like(acc_scr)
        sub = lax.broadcasted_iota(jnp.int32, (N_KEYS, N_SEL), 0)

        def per_token(t, carry):
            e = e_ref[pl.ds(t, 1), :]
            p_t = jnp.where(sub == (e >> 7), g_ref[pl.ds(t, 1), :], 0.0).astype(BF16)
            q_t = (sub == (e & (N_KEYS - 1))).astype(BF16)
            g_scr[pl.ds(pl.multiple_of(t * N_KEYS, N_KEYS), N_KEYS), :] = lax.dot_general(
                p_t, q_t, (((1,), (1,)), ((), ())), preferred_element_type=F32)
            return carry

        lax.fori_loop(0, tt, per_token, 0, unroll=8)

    s = lax.dot_general(h_ref[...].astype(BF16), u_ref[...], (((1,), (1,)), ((), ())), preferred_element_type=F32)
    gs = jnp.concatenate([g_scr[pl.ds(eb * slabs + k, tt, stride=N_KEYS), :] for k in range(slabs)], axis=1)
    acc_scr[...] += jnp.dot((gs * jax.nn.gelu(s)).astype(BF16), v_ref[...], preferred_element_type=F32)

    @pl.when(eb == pl.num_programs(1) - 1)
    def _():
        o_ref[...] = acc_scr[...]


def _peer_dense(h, eidx, gate, u_bf, v_bf):
    t, d = h.shape
    tt = TT_DENSE
    assert t % tt == 0 and N_KEYS == 128
    tile = lambda i, e: (i, 0)
    blk = lambda i, e: (e, 0)
    return pl.pallas_call(
        _dense_body,
        out_shape=jax.ShapeDtypeStruct((t, d), F32),
        grid=(t // tt, N_EXPERTS // EB_DENSE),
        in_specs=[pl.BlockSpec((tt, d), tile), pl.BlockSpec((tt, N_SEL), tile), pl.BlockSpec((tt, N_SEL), tile),
                  pl.BlockSpec((EB_DENSE, d), blk), pl.BlockSpec((EB_DENSE, d), blk)],
        out_specs=pl.BlockSpec((tt, d), tile),
        scratch_shapes=[pltpu.VMEM((tt * N_KEYS, N_KEYS), F32), pltpu.VMEM((tt, d), F32)],
        compiler_params=pltpu.CompilerParams(dimension_semantics=("arbitrary", "arbitrary"),
                                             vmem_limit_bytes=DENSE_VMEM_LIMIT),
        name="peer_dense_tc",
    )(h, eidx, gate, u_bf, v_bf)


TM_FIN = 512


def _final_body(x1_ref, p_ref, m_ref, w_ref, y_ref):
    y_ref[...] = _rms(x1_ref[...] + m_ref[0, 5:6, :] * p_ref[...]) * w_ref[...]


def _final(x1, peer, mod, norm_f, rows_of):
    t, d = x1.shape
    tm = _token_tile(t, TM_FIN)
    row_of = rows_of(tm)
    tile = pl.BlockSpec((tm, d), lambda i: (i, 0))
    return pl.pallas_call(
        _final_body,
        out_shape=jax.ShapeDtypeStruct((t, d), F32),
        grid=(t // tm,),
        in_specs=[tile, tile, pl.BlockSpec((1, N_MOD, d), lambda i: (row_of(i), 0, 0)),
                  pl.BlockSpec((1, d), lambda i: (0, 0))],
        out_specs=tile,
        compiler_params=_cparams("arbitrary"),
        name="final_norm",
    )(x1, peer, mod, norm_f.reshape(1, d))


CTX_GROUPS = (2, 14, 16)
LAT_GROUPS = (2, 2, 2, 2)
DENSE_TOKENS = 46 * 256


def _zero_of(a):
    return (lax.shift_right_logical(a[0, 0], 31) >> 1).astype(F32)


def _group(x, mod, row0, per_seq, n, hg_s0, lru_s0, col_major, p, after, dense_tokens=0):
    seqs = hg_s0.shape[0]
    t = seqs * n
    for a in after:
        mod = mod + _zero_of(a)
    rows_of = lambda tm: _mod_row_map(tm, n, row0, per_seq)
    z = _in_proj(x, mod, p['norm1'], p['w_in'], rows_of)
    o_hg, hg_fin = _hgrn2(z, p['lb'], hg_s0, n, 0)
    hr, lru_fin = _rglru(z, COL_XR, 0, n, seqs, col_major, *p['lru'], lru_s0)
    x1, h2, q = _out_proj(x, o_hg, hr, z, mod, p['hg_norm'], p['norm2'], p['w_a'], p['w_b'], p['w_o'], p['w_q'],
                          rows_of(TM_OUT))
    eidx, gate = _pk_topk(q, p['keys'])
    ts = t - dense_tokens
    parts = []
    if ts:
        parts.append(_peer_experts(h2[:ts], eidx[:ts].reshape(ts * PK_HEADS, PK_TOPK),
                                   gate[:ts].reshape(ts * PK_HEADS, PK_TOPK), p['pk_u'], p['pk_v']))
    if dense_tokens:
        parts.append(_peer_dense(h2[ts:], eidx[ts:], gate[ts:], p['u_bf'], p['v_bf']))
    peer = parts[0] if len(parts) == 1 else jnp.concatenate(parts, axis=0)
    return _final(x1, peer, mod, p['norm_f'], rows_of), hg_fin, lru_fin, eidx


def kernel(x_prompt, x_sample, state_hgrn, state_rglru, c, c_ctx, w_mod, b_mod, norm1, w_in,
           hg_lb_logits, hg_norm, lru_conv_w, lru_conv_b, lru_w_r, lru_b_r, lru_w_i, lru_b_i,
           lru_lambda, w_branch_a, w_branch_b, w_out, norm2, pk_w_q, pk_sub_keys, pk_u, pk_v,
           norm_f):
    assert w_mod.shape[0] == 1, "single trunk layer"
    d = D_MODEL
    nb_ctx, n_ctx, _ = x_prompt.shape
    nb_lat, n_lat, _ = x_sample.shape
    assert nb_lat < MOD_ROWS and nb_lat == sum(LAT_GROUPS) and nb_ctx == sum(CTX_GROUPS)
    ctx_row = nb_lat

    cond = jnp.zeros((MOD_ROWS, d), F32).at[:nb_lat].set(c).at[ctx_row].set(c_ctx)
    mod = _modulation(cond, w_mod[0], b_mod[0])
    p = {
        'lb': jnp.cumsum(jax.nn.softmax(hg_lb_logits.astype(F32), axis=1), axis=1)[:, 0],
        'norm1': norm1[0], 'w_in': w_in[0].astype(BF16), 'hg_norm': hg_norm[0], 'norm2': norm2[0],
        'lru': (lru_conv_w[0], lru_conv_b[0], _block_diag(lru_w_r[0]), _block_diag(lru_w_i[0]),
                lru_b_r[0], lru_b_i[0], lru_lambda[0]),
        'w_a': w_branch_a[0].astype(BF16), 'w_b': w_branch_b[0].astype(BF16), 'w_o': w_out[0].astype(BF16),
        'w_q': pk_w_q[0].astype(BF16), 'keys': pk_sub_keys[0].astype(BF16),
        'pk_u': _pack_bf16_pairs(pk_u[0]), 'pk_v': _pack_bf16_pairs(pk_v[0]), 'norm_f': norm_f,
        'u_bf': pk_u[0].astype(BF16), 'v_bf': pk_v[0].astype(BF16),
    }
    y_ctx, hg_fin, lru_fin = [], [], []
    tables = [p['pk_u'], p['pk_v']]
    after = tables
    s0 = 0
    for g in CTX_GROUPS:
        y, hg, lr, eidx = _group(
            x_prompt[s0:s0 + g].reshape(g * n_ctx, d), mod, ctx_row, False, n_ctx,
            jnp.zeros((g, 2, HG_HEADS, HG_DK, HG_DV), F32), jnp.zeros((g, 2, LRU_W), F32), False, p, after)
        s0 += g
        after = [eidx] if after is tables else after
        y_ctx.append(y.reshape(g, n_ctx, d))
        hg_fin.append(hg)
        lru_fin.append(lr)
    y_lat = []
    s0 = 0
    for g in LAT_GROUPS:
        sl = slice(s0, s0 + g)
        s0 += g
        tokens_after = (nb_lat - s0) * n_lat
        y, _, _, _ = _group(x_sample[sl].reshape(g * n_lat, d), mod, s0 - g, True, n_lat,
                            state_hgrn[sl, 0], state_rglru[sl, 0], True, p, after,
                            dense_tokens=min(g * n_lat, max(0, DENSE_TOKENS - tokens_after)))
        y_lat.append(y.reshape(g, n_lat, d))
    return (jnp.concatenate(y_ctx, axis=0), jnp.concatenate(y_lat, axis=0),
            jnp.concatenate(hg_fin, axis=0)[:, None], jnp.concatenate(lru_fin, axis=0)[:, None])
```

```python
import functools

import jax, jax.numpy as jnp
from jax import lax
from jax.experimental import pallas as pl
from jax.experimental.pallas import tpu as pltpu
from jax.experimental.pallas import tpu_sc as plsc

D_MODEL = 1024
GRID_W = 64
EPS = 1e-6
HG_HEADS = 4
HG_DK = 128
HG_DV = 128
HG_KW = HG_HEADS * HG_DK
HG_VW = HG_HEADS * HG_DV
HG_CHUNK = 32
LRU_W = D_MODEL // 2
LRU_BLOCKS = 8
LRU_BW = LRU_W // LRU_BLOCKS
CONV_W = 4
LRU_C = 8.0
PK_HEADS = 8
N_KEYS = 128
PK_TOPK = 16
PK_DQ = 256
PK_DH = PK_DQ // 2
IN_W = 3 * HG_KW + 2 * HG_VW + 2 * LRU_W + 2 * D_MODEL
COL_G, COL_XR, COL_XG, COL_GA, COL_GB = 4, 5, 6, 7, 9
N_MOD = 6
MOD_ROWS = 16
F32 = jnp.float32
BF16 = jnp.bfloat16
VMEM_LIMIT = 48 * 1024 * 1024


def _cparams(*sem):
    return pltpu.CompilerParams(dimension_semantics=sem, vmem_limit_bytes=VMEM_LIMIT)


def _silu(x):
    return x * jax.nn.sigmoid(x)


def _rms(x):
    return x * lax.rsqrt(jnp.mean(x * x, axis=-1, keepdims=True) + EPS)


def _mod_body(c_ref, w_ref, b_ref, o_ref):
    o_ref[...] = lax.dot_general(_silu(c_ref[...]), w_ref[...], (((1,), (0,)), ((), ())),
                                 precision=lax.Precision.HIGHEST, preferred_element_type=F32) + b_ref[...]


def _modulation(cond, w_mod, b_mod):
    d = D_MODEL
    out = pl.pallas_call(
        _mod_body,
        out_shape=jax.ShapeDtypeStruct((MOD_ROWS, N_MOD * d), F32),
        grid=(N_MOD,),
        in_specs=[pl.BlockSpec((MOD_ROWS, d), lambda j: (0, 0)),
                  pl.BlockSpec((d, d), lambda j: (0, j)),
                  pl.BlockSpec((1, d), lambda j: (0, j))],
        out_specs=pl.BlockSpec((MOD_ROWS, d), lambda j: (0, j)),
        compiler_params=_cparams("arbitrary"),
        name="adaln_modulation",
    )(cond, w_mod, b_mod.reshape(1, N_MOD * d))
    return out.reshape(MOD_ROWS, N_MOD, d)


def _mod_row_map(tm, n, row0, per_seq):
    per = n // tm
    return (lambda i: row0 + i // per) if per_seq else (lambda i: row0)


TM_IN = 512
TOKEN_TILE_MIN = 256
TN_IN = IN_W // 2


def _in_body(x_ref, m_ref, n1_ref, w_ref, z_ref):
    y = _rms(x_ref[...]) * n1_ref[...]
    h = (y * (1.0 + m_ref[0, 1:2, :]) + m_ref[0, 0:1, :]).astype(BF16)
    z_ref[...] = jnp.dot(h, w_ref[...], preferred_element_type=F32)


def _token_tile(t, preferred):
    tm = preferred if t % preferred == 0 else TOKEN_TILE_MIN
    assert t % tm == 0
    return tm


def _in_proj(x, mod, norm1, w_in_bf, rows_of):
    t, d = x.shape
    tm = _token_tile(t, TM_IN)
    row_of = rows_of(tm)
    return pl.pallas_call(
        _in_body,
        out_shape=jax.ShapeDtypeStruct((t, IN_W), F32),
        grid=(IN_W // TN_IN, t // tm),
        in_specs=[pl.BlockSpec((tm, d), lambda j, i: (i, 0)),
                  pl.BlockSpec((1, N_MOD, d), lambda j, i: (row_of(i), 0, 0)),
                  pl.BlockSpec((1, d), lambda j, i: (0, 0)),
                  pl.BlockSpec((d, TN_IN), lambda j, i: (0, j))],
        out_specs=pl.BlockSpec((tm, TN_IN), lambda j, i: (i, j)),
        compiler_params=_cparams("arbitrary", "arbitrary"),
        name="in_proj",
    )(x, mod, norm1.reshape(1, d), w_in_bf)


HG_LONG_SEQ = 4096


def _heads_per_step(n):
    return HG_HEADS if n < HG_LONG_SEQ else HG_HEADS // 2


def _hg_body(q_ref, ff_ref, fb_ref, v_ref, lb_ref, s0_ref, o_ref, sfin_ref, st_scr, ob_scr, *, n, hps):
    c = HG_CHUNK
    nc = n // c
    row = lax.broadcasted_iota(jnp.int32, (c, c), 0)
    col = lax.broadcasted_iota(jnp.int32, (c, c), 1)
    lower = row >= col
    tri = (lower.astype(F32), (row <= col).astype(F32))
    masks = (lower, row <= col)
    f_refs = (ff_ref, fb_ref)
    for d in range(2):
        for hh in range(hps):
            st_scr[d * hps + hh] = s0_ref[0, d, hh].T

    def step(i, carry):
        chains = [(d, hh) for d in range(2) for hh in range(hps)]
        st1 = []
        for d, hh in chains:
            ci = i if d == 0 else nc - 1 - i
            r = pl.ds(pl.multiple_of(ci * c, c), c)
            hc = slice(hh * HG_DK, (hh + 1) * HG_DK)
            lb = lb_ref[d:d + 1, hc]
            f = lb + (1.0 - lb) * jax.nn.sigmoid(f_refs[d][r, hc])
            lf = jnp.log(f)
            cum = lax.dot_general(tri[d], lf, (((1,), (0,)), ((), ())),
                                  precision=lax.Precision.HIGHEST, preferred_element_type=F32)
            st1.append((r, hc, 1.0 - f, lf, cum))
        st2 = []
        for (d, hh), (r, hc, k, lf, cum) in zip(chains, st1):
            q = _silu(q_ref[r, hc])
            v = v_ref[r, hc].astype(BF16)
            tot = jnp.sum(lf, axis=0, keepdims=True)
            q_dec = (q * jnp.exp(cum)).astype(BF16)
            k_inv = (k * jnp.exp(-cum)).astype(BF16)
            k_end = (k * jnp.exp(tot - cum)).astype(BF16)
            st = st_scr[d * hps + hh]
            att = lax.dot_general(q_dec, k_inv, (((1,), (1,)), ((), ())), preferred_element_type=F32)
            o_inter = lax.dot_general(q_dec, st.astype(BF16), (((1,), (1,)), ((), ())), preferred_element_type=F32)
            ds_t = lax.dot_general(v, k_end, (((0,), (0,)), ((), ())), preferred_element_type=F32)
            st_scr[d * hps + hh] = st * jnp.exp(tot) + ds_t
            st2.append((v, att, o_inter))
        for (d, hh), (r, hc, _, _, _), (v, att, o_inter) in zip(chains, st1, st2):
            att = jnp.where(masks[d], att, 0.0).astype(BF16)
            o = jnp.dot(att, v, preferred_element_type=F32) + o_inter
            if d == 0:
                o_ref[r, hc] = o
            else:
                ob_scr[r, hc] = o
        return carry

    lax.fori_loop(0, nc, step, 0)
    o_ref[...] = o_ref[...] + ob_scr[...]
    for d in range(2):
        for hh in range(hps):
            sfin_ref[0, d, hh] = st_scr[d * hps + hh].T


def _hgrn2(z, lb, s0, n, row_block0):
    seqs = s0.shape[0]
    hps = _heads_per_step(n)
    wb = hps * HG_DK
    nblk = HG_KW // wb

    def zspec(seg):
        return pl.BlockSpec((n, wb), lambda s, h: (row_block0 + s, seg * nblk + h))

    st_spec = pl.BlockSpec((1, 2, hps, HG_DK, HG_DV), lambda s, h: (s, 0, h, 0, 0))
    return pl.pallas_call(
        functools.partial(_hg_body, n=n, hps=hps),
        out_shape=(jax.ShapeDtypeStruct((seqs * n, HG_VW), F32),
                   jax.ShapeDtypeStruct((seqs, 2, HG_HEADS, HG_DK, HG_DV), F32)),
        grid=(seqs, HG_HEADS // hps),
        in_specs=[zspec(0), zspec(1), zspec(2), zspec(3),
                  pl.BlockSpec((2, wb), lambda s, h: (0, h)), st_spec],
        out_specs=(pl.BlockSpec((n, wb), lambda s, h: (s, h)), st_spec),
        scratch_shapes=[pltpu.VMEM((2 * hps, HG_DV, HG_DK), F32), pltpu.VMEM((n, wb), F32)],
        compiler_params=_cparams("arbitrary", "arbitrary"),
        name="hgrn2_scan",
    )(z, z, z, z, lb, s0)


LRU_RB = 128
HALO = 8


LANES = 128


def _lru_body(x_ref, cw_ref, cb_ref, wr_ref, wi_ref, br_ref, bi_ref, lam_ref, h0_ref, hr_ref, e_ref,
              xp_scr, xc_scr, *cm_scr, n, col_major):
    rb = LRU_RB
    nb = n // rb
    w = LRU_W
    grid_rows = n // GRID_W
    slabs = w // LANES
    zeros = jnp.zeros((HALO, w), F32)
    xp_scr[0:HALO, :] = zeros
    xp_scr[HALO + n:HALO + n + HALO, :] = zeros
    if col_major:
        st_scr, hcm_scr = cm_scr
        for j in range(slabs):
            st_scr[j] = x_ref[:, j * LANES:(j + 1) * LANES]

        def to_col_major(c, carry):
            dst = pl.ds(pl.multiple_of(HALO + c * grid_rows, 8), grid_rows)
            for j in range(slabs):
                xp_scr[dst, j * LANES:(j + 1) * LANES] = st_scr[j, pl.ds(c, grid_rows, stride=GRID_W), :]
            return carry

        lax.fori_loop(0, GRID_W, to_col_major, 0)
        h_dst = hcm_scr
    else:
        xp_scr[HALO:HALO + n, :] = x_ref[...]
        h_dst = hr_ref
    cw = cw_ref[...]
    cb = cb_ref[...]

    def conv_blk(b, carry):
        base = pl.multiple_of(b * rb, rb)
        xh = xp_scr[pl.ds(base, rb + 2 * HALO), :]
        ext = rb + 2 * HALO
        acc = cb + cw[2:3, :] * xh[HALO:HALO + rb]
        acc = acc + cw[0:1, :] * pltpu.roll(xh, 2, axis=0)[HALO:HALO + rb]
        acc = acc + cw[1:2, :] * pltpu.roll(xh, 1, axis=0)[HALO:HALO + rb]
        acc = acc + cw[3:4, :] * pltpu.roll(xh, ext - 1, axis=0)[HALO:HALO + rb]
        xc_scr[pl.ds(base, rb), :] = acc
        return carry

    lax.fori_loop(0, nb, conv_blk, 0)

    rows = lax.broadcasted_iota(jnp.int32, (rb, w), 0)

    def gates(blk, d):
        r = pl.ds(pl.multiple_of(blk * rb, rb), rb)
        xc = xc_scr[r, :]
        xb = xc.astype(BF16)
        rg = jax.nn.sigmoid(jnp.dot(xb, wr_ref[d], preferred_element_type=F32) + br_ref[d:d + 1, :])
        ig = jax.nn.sigmoid(jnp.dot(xb, wi_ref[d], preferred_element_type=F32) + bi_ref[d:d + 1, :])
        log_a = (-LRU_C) * jax.nn.softplus(-lam_ref[d:d + 1, :]) * rg
        a = jnp.exp(log_a)
        u = jnp.sqrt(-jnp.tanh(log_a) * (a * a + 1.0)) * (ig * xc)
        return r, a, u

    def fwd_blk(blk, h_prev):
        r, a, u = gates(blk, 0)
        s = 1
        while s < rb:
            keep = rows >= s
            a_sh = jnp.where(keep, pltpu.roll(a, s, axis=0), 1.0)
            u_sh = jnp.where(keep, pltpu.roll(u, s, axis=0), 0.0)
            u = a * u_sh + u
            a = a * a_sh
            s *= 2
        h = u + a * h_prev
        h_dst[r, :] = h
        return h[rb - 1:rb, :]

    def bwd_blk(i, h_next):
        r, a, u = gates(nb - 1 - i, 1)
        s = 1
        while s < rb:
            keep = rows < rb - s
            a_sh = jnp.where(keep, pltpu.roll(a, rb - s, axis=0), 1.0)
            u_sh = jnp.where(keep, pltpu.roll(u, rb - s, axis=0), 0.0)
            u = a * u_sh + u
            a = a * a_sh
            s *= 2
        h = u + a * h_next
        h_dst[r, :] = h_dst[r, :] + h
        return h[0:1, :]

    e_ref[0, 0:1, :] = lax.fori_loop(0, nb, fwd_blk, h0_ref[0, 0:1, :])
    e_ref[0, 1:2, :] = lax.fori_loop(0, nb, bwd_blk, h0_ref[0, 1:2, :])
    if col_major:
        def to_row_major(c, carry):
            src = pl.ds(pl.multiple_of(c * grid_rows, 8), grid_rows)
            for j in range(slabs):
                st_scr[j, pl.ds(c, grid_rows, stride=GRID_W), :] = hcm_scr[src, j * LANES:(j + 1) * LANES]
            return carry

        lax.fori_loop(0, GRID_W, to_row_major, 0)
        for j in range(slabs):
            hr_ref[:, j * LANES:(j + 1) * LANES] = st_scr[j]


def _rglru(x, col_block, row_block0, n, seqs, col_major, conv_w, conv_b, wr_bd, wi_bd, b_r, b_i, lam, h0):
    w = LRU_W
    cm_scratch = [pltpu.VMEM((w // LANES, n, LANES), F32), pltpu.VMEM((n, w), F32)] if col_major else []
    full2 = lambda s: (0, 0)
    full3 = lambda s: (0, 0, 0)
    return pl.pallas_call(
        functools.partial(_lru_body, n=n, col_major=col_major),
        out_shape=(jax.ShapeDtypeStruct((seqs * n, w), F32), jax.ShapeDtypeStruct((seqs, 2, w), F32)),
        grid=(seqs,),
        in_specs=[pl.BlockSpec((n, w), lambda s: (row_block0 + s, col_block)),
                  pl.BlockSpec((CONV_W, w), full2), pl.BlockSpec((1, w), full2),
                  pl.BlockSpec((2, w, w), full3), pl.BlockSpec((2, w, w), full3),
                  pl.BlockSpec((2, w), full2), pl.BlockSpec((2, w), full2), pl.BlockSpec((2, w), full2),
                  pl.BlockSpec((1, 2, w), lambda s: (s, 0, 0))],
        out_specs=(pl.BlockSpec((n, w), lambda s: (s, 0)), pl.BlockSpec((1, 2, w), lambda s: (s, 0, 0))),
        scratch_shapes=[pltpu.VMEM((n + 2 * HALO, w), F32), pltpu.VMEM((n, w), F32)] + cm_scratch,
        compiler_params=_cparams("arbitrary"),
        name="rglru",
    )(x, conv_w, conv_b.reshape(1, w), wr_bd, wi_bd, b_r, b_i, lam, h0)


def _block_diag(wg):
    eye = jnp.eye(LRU_BLOCKS, dtype=wg.dtype)
    dense = wg[:, :, :, None, :] * eye[None, :, None, :, None]
    return dense.reshape(2, LRU_W, LRU_W).astype(BF16)


TM_OUT = 256


def _out_body(x_ref, o_ref, hr_ref, g_ref, xg_ref, ga0_ref, ga1_ref, gb0_ref, gb1_ref, m_ref,
              hgn_ref, n2_ref, wa_ref, wb_ref, wo_ref, wq_ref, x1_ref, h2_ref, q_ref):
    o = o_ref[...]
    parts = []
    for h in range(HG_HEADS):
        parts.append(_rms(o[:, h * HG_DV:(h + 1) * HG_DV]) * hgn_ref[...])
    on = jnp.concatenate(parts, axis=1) * _silu(g_ref[...])
    y_a = jnp.dot(on.astype(BF16), wa_ref[...], preferred_element_type=F32)
    y_b = jnp.dot((hr_ref[...] * jax.nn.gelu(xg_ref[...])).astype(BF16), wb_ref[...], preferred_element_type=F32)
    ga = jnp.concatenate([ga0_ref[...], ga1_ref[...]], axis=1)
    gb = jnp.concatenate([gb0_ref[...], gb1_ref[...]], axis=1)
    merged = jax.nn.sigmoid(ga) * y_a + jax.nn.sigmoid(gb) * y_b
    mix = jnp.dot(merged.astype(BF16), wo_ref[...], preferred_element_type=F32)
    x1 = x_ref[...] + m_ref[0, 2:3, :] * mix
    x1_ref[...] = x1
    h2 = _rms(x1) * n2_ref[...] * (1.0 + m_ref[0, 4:5, :]) + m_ref[0, 3:4, :]
    h2_ref[...] = h2
    q_ref[...] = jnp.dot(h2.astype(BF16), wq_ref[...], preferred_element_type=F32).astype(BF16)


def _out_proj(x, o_hg, hr, z, mod, hg_norm, norm2, wa, wb, wo, wq, row_of):
    t, d = x.shape
    tm = TM_OUT
    nq = PK_HEADS * PK_DQ
    half = lambda i: (i, 0)
    zc = lambda c: pl.BlockSpec((tm, LRU_W), lambda i: (i, c))
    const = lambda i: (0, 0)
    return pl.pallas_call(
        _out_body,
        out_shape=(jax.ShapeDtypeStruct((t, d), F32), jax.ShapeDtypeStruct((t, d), F32),
                   jax.ShapeDtypeStruct((t, nq), BF16)),
        grid=(t // tm,),
        in_specs=[pl.BlockSpec((tm, d), half), pl.BlockSpec((tm, HG_VW), half), pl.BlockSpec((tm, LRU_W), half),
                  zc(COL_G), zc(COL_XG), zc(COL_GA), zc(COL_GA + 1), zc(COL_GB), zc(COL_GB + 1),
                  pl.BlockSpec((1, N_MOD, d), lambda i: (row_of(i), 0, 0)),
                  pl.BlockSpec((1, HG_DV), const), pl.BlockSpec((1, d), const),
                  pl.BlockSpec((HG_VW, d), const), pl.BlockSpec((LRU_W, d), const),
                  pl.BlockSpec((d, d), const), pl.BlockSpec((d, nq), const)],
        out_specs=(pl.BlockSpec((tm, d), half), pl.BlockSpec((tm, d), half), pl.BlockSpec((tm, nq), half)),
        compiler_params=_cparams("arbitrary"),
        name="out_proj",
    )(x, o_hg, hr, z, z, z, z, z, z, mod, hg_norm.reshape(1, HG_DV), norm2.reshape(1, d), wa, wb, wo, wq)


TT_TOPK = 256


def _extract_topk(s, order, k, payload=None):
    vals, ords, picks = [], [], []
    for _ in range(k):
        m = jnp.max(s, axis=0, keepdims=True)
        o = jnp.max(jnp.where(s == m, order, -1.0), axis=0, keepdims=True)
        hit = order == o
        vals.append(m)
        ords.append(o)
        if payload is not None:
            picks.append(jnp.sum(jnp.where(hit, payload, 0), axis=0, keepdims=True))
        s = jnp.where(hit, -jnp.inf, s)
    cat = lambda xs: jnp.concatenate(xs, axis=0)
    return cat(vals), cat(ords), (cat(picks) if payload is not None else None)


PAIR_ROWS = 4
PAIR_COLS = 3
assert all((i + 1) * (j + 1) > PK_TOPK for i in range(PAIR_ROWS, PK_TOPK) for j in range(PAIR_COLS, PK_TOPK))
assert all((i + 1) * (8 + 1) > PK_TOPK for i in range(1, PAIR_ROWS)) and (8 + 1) * (1 + 1) > PK_TOPK


def _pair_candidates(sv, si, tt):
    cand, order, eid = [], [], []
    hi = float(PK_TOPK * PK_TOPK - 1)
    for i in range(PAIR_ROWS):
        nj = PK_TOPK if i == 0 else 8
        j = lax.broadcasted_iota(jnp.int32, (nj, tt), 0)
        cand.append(sv[0][i:i + 1, :] + sv[1][:nj])
        order.append(hi - (i * PK_TOPK + j).astype(F32))
        eid.append(si[0][i:i + 1, :] * N_KEYS + si[1][:nj])
    for j in range(PAIR_COLS):
        ni = PK_TOPK if j == 0 else 8
        i = lax.broadcasted_iota(jnp.int32, (ni, tt), 0)
        fresh = i >= PAIR_ROWS
        cand.append(jnp.where(fresh, sv[0][:ni] + sv[1][j:j + 1, :], -jnp.inf))
        order.append(jnp.where(fresh, hi - (i * PK_TOPK + j).astype(F32), -2.0))
        eid.append(si[0][:ni] * N_KEYS + si[1][j:j + 1, :])
    cat = lambda xs: jnp.concatenate(xs, axis=0)
    return cat(cand), cat(order), cat(eid)


def _topk_body(q_ref, keys_ref, eidx_ref, gate_ref):
    tt = q_ref.shape[0]
    key_order = (N_KEYS - 1 - lax.broadcasted_iota(jnp.int32, (N_KEYS, tt), 0)).astype(F32)
    e_rows, g_rows = [], []
    for h in range(PK_HEADS):
        sv, si = [], []
        for p in range(2):
            c0 = h * PK_DQ + p * PK_DH
            s = lax.dot_general(keys_ref[p, h], q_ref[:, c0:c0 + PK_DH], (((1,), (1,)), ((), ())),
                                preferred_element_type=F32)
            v, o, _ = _extract_topk(s, key_order, PK_TOPK)
            sv.append(v)
            si.append(N_KEYS - 1 - o.astype(jnp.int32))
        cand, order, cidx = _pair_candidates(sv, si, tt)
        best, _, eid = _extract_topk(cand, order, PK_TOPK, payload=cidx)
        ex = jnp.exp(best - best[0:1, :])
        g_rows.append(ex / jnp.sum(ex, axis=0, keepdims=True))
        e_rows.append(eid)
    eidx_ref[...] = jnp.concatenate(e_rows, axis=0).T
    gate_ref[...] = jnp.concatenate(g_rows, axis=0).T


def _pk_topk(q, keys_bf):
    t = q.shape[0]
    tt = TT_TOPK
    ne = PK_HEADS * PK_TOPK
    return pl.pallas_call(
        _topk_body,
        out_shape=(jax.ShapeDtypeStruct((t, ne), jnp.int32), jax.ShapeDtypeStruct((t, ne), F32)),
        grid=(t // tt,),
        in_specs=[pl.BlockSpec((tt, PK_HEADS * PK_DQ), lambda i: (i, 0)),
                  pl.BlockSpec((2, PK_HEADS, N_KEYS, PK_DH), lambda i: (0, 0, 0, 0))],
        out_specs=(pl.BlockSpec((tt, ne), lambda i: (i, 0)), pl.BlockSpec((tt, ne), lambda i: (i, 0))),
        compiler_params=_cparams("arbitrary"),
        name="pk_topk",
    )(q, keys_bf)


SC_LANES = 16
SC_CORES = 2
SC_SUBCORES = 16
SC_WORKERS = SC_CORES * SC_SUBCORES
SC_HALF = D_MODEL // 2
SC_ROW_CHUNKS = SC_HALF // SC_LANES
SC_TOKEN_BLOCK = 8


def _pack_bf16_pairs(tab):
    lo = lax.bitcast_convert_type(tab[:, SC_HALF:].astype(BF16), jnp.uint16).astype(jnp.uint32)
    bits = lax.bitcast_convert_type(tab[:, :SC_HALF], jnp.uint32)
    sign = bits & jnp.uint32(0x80000000)
    mag = (bits & jnp.uint32(0x7FFFFFFF)) + jnp.uint32(1 << 15)
    hi = jnp.where(mag >= lo, (mag - lo) >> 16, jnp.uint32(0))
    return lax.bitcast_convert_type(sign | (hi << 16) | lo, jnp.int32)


def _unpack_pair(w):
    return lax.bitcast_convert_type(w, F32), lax.bitcast_convert_type(w << 16, F32)


def _gelu_tanh(x):
    y = 0.7978845608028654 * (x + 0.044715 * (x * x * x))
    t = 1.0 - 2.0 / (jnp.exp(2.0 * y) + 1.0)
    return x * (0.5 * (1.0 + t))


def _peer_sc_body(h_hbm, idx_hbm, gate_hbm, u_hbm, v_hbm, out_hbm,
                  h_v, idx_v, gate_v, out_v, ub0, ub1, vb0, vb1, su0, su1, sv0, sv1, *, tokens_per_worker):
    nh, k, tb, lanes = PK_HEADS, PK_TOPK, SC_TOKEN_BLOCK, SC_LANES
    wid = lax.axis_index("s") * SC_CORES + lax.axis_index("c")
    ubufs, vbufs, sus, svs = (ub0, ub1), (vb0, vb1), (su0, su1), (sv0, sv1)
    lane = lax.iota(jnp.int32, lanes)
    zero = jnp.zeros((lanes,), F32)

    def start(tok, hd, par):
        irow = idx_v.at[tok * nh + hd]
        pltpu.async_copy(u_hbm.at[irow], ubufs[par], sus[par])
        pltpu.async_copy(v_hbm.at[irow], vbufs[par], svs[par])

    def wait(par):
        irow = idx_v.at[0]
        pltpu.make_async_copy(u_hbm.at[irow], ubufs[par], sus[par]).wait()
        pltpu.make_async_copy(v_hbm.at[irow], vbufs[par], svs[par]).wait()

    @pl.loop(0, tokens_per_worker // tb)
    def _(blk):
        base = wid * tokens_per_worker + blk * tb
        pltpu.sync_copy(h_hbm.at[pl.ds(base, tb)], h_v)
        pltpu.sync_copy(idx_hbm.at[pl.ds(base * nh, tb * nh)], idx_v)
        pltpu.sync_copy(gate_hbm.at[pl.ds(base * nh, tb * nh)], gate_v)
        start(0, 0, 0)

        @pl.loop(0, tb)
        def _(tok):
            for hd in range(nh):
                par = hd % 2
                if hd + 1 < nh:
                    start(tok, hd + 1, 1 - par)
                else:
                    @pl.when(tok + 1 < tb)
                    def _():
                        start(tok + 1, 0, 1 - par)
                wait(par)
                ub, vb = ubufs[par], vbufs[par]

                @plsc.parallel_loop(0, SC_ROW_CHUNKS, carry=(zero,) * k)
                def accs(c, acc):
                    ha = h_v[tok, pl.ds(c * lanes, lanes)]
                    hb = h_v[tok, pl.ds(SC_HALF + c * lanes, lanes)]
                    out = []
                    for r in range(k):
                        a, b = _unpack_pair(ub[r, pl.ds(c * lanes, lanes)])
                        out.append(acc[r] + (ha * a + hb * b))
                    return tuple(out)

                s_vec = zero
                for r in range(k):
                    s_vec = jnp.where(lane == r, jnp.sum(accs[r]), s_vec)
                w_vec = gate_v[tok * nh + hd, :] * _gelu_tanh(s_vec)
                wb = [jnp.sum(jnp.where(lane == r, w_vec, 0.0)) for r in range(k)]

                @plsc.parallel_loop(0, SC_ROW_CHUNKS)
                def _(c):
                    pa, pb = [], []
                    for r in range(k):
                        a, b = _unpack_pair(vb[r, pl.ds(c * lanes, lanes)])
                        pa.append(wb[r] * a)
                        pb.append(wb[r] * b)
                    while len(pa) > 1:
                        pa = [pa[i] + pa[i + 1] for i in range(0, len(pa), 2)]
                        pb = [pb[i] + pb[i + 1] for i in range(0, len(pb), 2)]
                    for off, o in ((0, pa[0]), (SC_HALF, pb[0])):
                        cols = pl.ds(off + c * lanes, lanes)
                        if hd == 0:
                            out_v[tok, cols] = o
                        else:
                            out_v[tok, cols] = out_v[tok, cols] + o

        pltpu.sync_copy(out_v, out_hbm.at[pl.ds(base, tb)])


def _peer_experts(h, eidx, gate, u_tab, v_tab):
    t, d = h.shape
    assert t % (SC_WORKERS * SC_TOKEN_BLOCK) == 0 and d == D_MODEL
    rows = pltpu.VMEM((PK_TOPK, SC_HALF), jnp.int32)
    return pl.kernel(
        functools.partial(_peer_sc_body, tokens_per_worker=t // SC_WORKERS),
        out_type=jax.ShapeDtypeStruct((t, d), F32),
        mesh=plsc.VectorSubcoreMesh(core_axis_name="c", subcore_axis_name="s"),
        scratch_types=[
            pltpu.VMEM((SC_TOKEN_BLOCK, d), F32),
            pltpu.VMEM((SC_TOKEN_BLOCK * PK_HEADS, PK_TOPK), jnp.int32),
            pltpu.VMEM((SC_TOKEN_BLOCK * PK_HEADS, PK_TOPK), F32),
            pltpu.VMEM((SC_TOKEN_BLOCK, d), F32),
            rows, rows, rows, rows,
            pltpu.SemaphoreType.DMA, pltpu.SemaphoreType.DMA,
            pltpu.SemaphoreType.DMA, pltpu.SemaphoreType.DMA,
        ],
        compiler_params=pltpu.CompilerParams(needs_layout_passes=False),
        name="peer_experts_sc",
    )(h, eidx, gate, u_tab, v_tab)


TT_DENSE = 512
EB_DENSE = 1024
N_EXPERTS = N_KEYS * N_KEYS
N_SEL = PK_HEADS * PK_TOPK
DENSE_VMEM_LIMIT = 56 * 1024 * 1024


def _dense_body(h_ref, e_ref, g_ref, u_ref, v_ref, o_ref, g_scr, acc_scr):
    eb = pl.program_id(1)
    tt = h_ref.shape[0]
    slabs = EB_DENSE // N_KEYS

    @pl.when(eb == 0)
    def _():
        acc_scr[...] = jnp.zeros_like(acc_scr)
        sub = lax.broadcasted_iota(jnp.int32, (N_KEYS, N_SEL), 0)

        def per_token(t, carry):
            e = e_ref[pl.ds(t, 1), :]
            p_t = jnp.where(sub == (e >> 7), g_ref[pl.ds(t, 1), :], 0.0).astype(BF16)
            q_t = (sub == (e & (N_KEYS - 1))).astype(BF16)
            g_scr[pl.ds(pl.multiple_of(t * N_KEYS, N_KEYS), N_KEYS), :] = lax.dot_general(
                p_t, q_t, (((1,), (1,)), ((), ())), preferred_element_type=F32)
            return carry

        lax.fori_loop(0, tt, per_token, 0, unroll=8)

    s = lax.dot_general(h_ref[...].astype(BF16), u_ref[...], (((1,), (1,)), ((), ())), preferred_element_type=F32)
    gs = jnp.concatenate([g_scr[pl.ds(eb * slabs + k, tt, stride=N_KEYS), :] for k in range(slabs)], axis=1)
    acc_scr[...] += jnp.dot((gs * jax.nn.gelu(s)).astype(BF16), v_ref[...], preferred_element_type=F32)

    @pl.when(eb == pl.num_programs(1) - 1)
    def _():
        o_ref[...] = acc_scr[...]


def _peer_dense(h, eidx, gate, u_bf, v_bf):
    t, d = h.shape
    tt = TT_DENSE
    assert t % tt == 0 and N_KEYS == 128
    tile = lambda i, e: (i, 0)
    blk = lambda i, e: (e, 0)
    return pl.pallas_call(
        _dense_body,
        out_shape=jax.ShapeDtypeStruct((t, d), F32),
        grid=(t // tt, N_EXPERTS // EB_DENSE),
        in_specs=[pl.BlockSpec((tt, d), tile), pl.BlockSpec((tt, N_SEL), tile), pl.BlockSpec((tt, N_SEL), tile),
                  pl.BlockSpec((EB_DENSE, d), blk), pl.BlockSpec((EB_DENSE, d), blk)],
        out_specs=pl.BlockSpec((tt, d), tile),
        scratch_shapes=[pltpu.VMEM((tt * N_KEYS, N_KEYS), F32), pltpu.VMEM((tt, d), F32)],
        compiler_params=pltpu.CompilerParams(dimension_semantics=("arbitrary", "arbitrary"),
                                             vmem_limit_bytes=DENSE_VMEM_LIMIT),
        name="peer_dense_tc",
    )(h, eidx, gate, u_bf, v_bf)


TM_FIN = 512


def _final_body(x1_ref, p_ref, m_ref, w_ref, y_ref):
    y_ref[...] = _rms(x1_ref[...] + m_ref[0, 5:6, :] * p_ref[...]) * w_ref[...]


def _final(x1, peer, mod, norm_f, rows_of):
    t, d = x1.shape
    tm = _token_tile(t, TM_FIN)
    row_of = rows_of(tm)
    tile = pl.BlockSpec((tm, d), lambda i: (i, 0))
    return pl.pallas_call(
        _final_body,
        out_shape=jax.ShapeDtypeStruct((t, d), F32),
        grid=(t // tm,),
        in_specs=[tile, tile, pl.BlockSpec((1, N_MOD, d), lambda i: (row_of(i), 0, 0)),
                  pl.BlockSpec((1, d), lambda i: (0, 0))],
        out_specs=tile,
        compiler_params=_cparams("arbitrary"),
        name="final_norm",
    )(x1, peer, mod, norm_f.reshape(1, d))


CTX_GROUPS = (2, 14, 16)
LAT_GROUPS = (2, 2, 2, 2)
DENSE_TOKENS = 46 * 256


def _zero_of(a):
    return (lax.shift_right_logical(a[0, 0], 31) >> 1).astype(F32)


def _group(x, mod, row0, per_seq, n, hg_s0, lru_s0, col_major, p, after, dense_tokens=0):
    seqs = hg_s0.shape[0]
    t = seqs * n
    for a in after:
        mod = mod + _zero_of(a)
    rows_of = lambda tm: _mod_row_map(tm, n, row0, per_seq)
    z = _in_proj(x, mod, p['norm1'], p['w_in'], rows_of)
    o_hg, hg_fin = _hgrn2(z, p['lb'], hg_s0, n, 0)
    hr, lru_fin = _rglru(z, COL_XR, 0, n, seqs, col_major, *p['lru'], lru_s0)
    x1, h2, q = _out_proj(x, o_hg, hr, z, mod, p['hg_norm'], p['norm2'], p['w_a'], p['w_b'], p['w_o'], p['w_q'],
                          rows_of(TM_OUT))
    eidx, gate = _pk_topk(q, p['keys'])
    ts = t - dense_tokens
    parts = []
    if ts:
        parts.append(_peer_experts(h2[:ts], eidx[:ts].reshape(ts * PK_HEADS, PK_TOPK),
                                   gate[:ts].reshape(ts * PK_HEADS, PK_TOPK), p['pk_u'], p['pk_v']))
    if dense_tokens:
        parts.append(_peer_dense(h2[ts:], eidx[ts:], gate[ts:], p['u_bf'], p['v_bf']))
    peer = parts[0] if len(parts) == 1 else jnp.concatenate(parts, axis=0)
    return _final(x1, peer, mod, p['norm_f'], rows_of), hg_fin, lru_fin, eidx


def kernel(x_prompt, x_sample, state_hgrn, state_rglru, c, c_ctx, w_mod, b_mod, norm1, w_in,
           hg_lb_logits, hg_norm, lru_conv_w, lru_conv_b, lru_w_r, lru_b_r, lru_w_i, lru_b_i,
           lru_lambda, w_branch_a, w_branch_b, w_out, norm2, pk_w_q, pk_sub_keys, pk_u, pk_v,
           norm_f):
    assert w_mod.shape[0] == 1, "single trunk layer"
    d = D_MODEL
    nb_ctx, n_ctx, _ = x_prompt.shape
    nb_lat, n_lat, _ = x_sample.shape
    assert nb_lat < MOD_ROWS and nb_lat == sum(LAT_GROUPS) and nb_ctx == sum(CTX_GROUPS)
    ctx_row = nb_lat

    cond = jnp.zeros((MOD_ROWS, d), F32).at[:nb_lat].set(c).at[ctx_row].set(c_ctx)
    mod = _modulation(cond, w_mod[0], b_mod[0])
    p = {
        'lb': jnp.cumsum(jax.nn.softmax(hg_lb_logits.astype(F32), axis=1), axis=1)[:, 0],
        'norm1': norm1[0], 'w_in': w_in[0].astype(BF16), 'hg_norm': hg_norm[0], 'norm2': norm2[0],
        'lru': (lru_conv_w[0], lru_conv_b[0], _block_diag(lru_w_r[0]), _block_diag(lru_w_i[0]),
                lru_b_r[0], lru_b_i[0], lru_lambda[0]),
        'w_a': w_branch_a[0].astype(BF16), 'w_b': w_branch_b[0].astype(BF16), 'w_o': w_out[0].astype(BF16),
        'w_q': pk_w_q[0].astype(BF16), 'keys': pk_sub_keys[0].astype(BF16),
        'pk_u': _pack_bf16_pairs(pk_u[0]), 'pk_v': _pack_bf16_pairs(pk_v[0]), 'norm_f': norm_f,
        'u_bf': pk_u[0].astype(BF16), 'v_bf': pk_v[0].astype(BF16),
    }
    y_ctx, hg_fin, lru_fin = [], [], []
    tables = [p['pk_u'], p['pk_v']]
    after = tables
    s0 = 0
    for g in CTX_GROUPS:
        y, hg, lr, eidx = _group(
            x_prompt[s0:s0 + g].reshape(g * n_ctx, d), mod, ctx_row, False, n_ctx,
            jnp.zeros((g, 2, HG_HEADS, HG_DK, HG_DV), F32), jnp.zeros((g, 2, LRU_W), F32), False, p, after)
        s0 += g
        after = [eidx] if after is tables else after
        y_ctx.append(y.reshape(g, n_ctx, d))
        hg_fin.append(hg)
        lru_fin.append(lr)
    y_lat = []
    s0 = 0
    for g in LAT_GROUPS:
        sl = slice(s0, s0 + g)
        s0 += g
        tokens_after = (nb_lat - s0) * n_lat
        y, _, _, _ = _group(x_sample[sl].reshape(g * n_lat, d), mod, s0 - g, True, n_lat,
                            state_hgrn[sl, 0], state_rglru[sl, 0], True, p, after,
                            dense_tokens=min(g * n_lat, max(0, DENSE_TOKENS - tokens_after)))
        y_lat.append(y.reshape(g, n_lat, d))
    return (jnp.concatenate(y_ctx, axis=0), jnp.concatenate(y_lat, axis=0),
            jnp.concatenate(hg_fin, axis=0)[:, None], jnp.concatenate(lru_fin, axis=0)[:, None])
```
